```python
import math
import jax, jax.numpy as jnp
from jax import lax
import numpy as np

D_MODEL = 2048
BATCH = 8
SEQ = 8192
DEPTH = 4

N_A_LAYERS = DEPTH // 2
N_B_LAYERS = DEPTH - N_A_LAYERS
EPS = 1e-6
D_FF = ((8 * D_MODEL // 3 + 255) // 256) * 256
HEAD_DIM_A = 128
N_HEADS_A = D_MODEL // HEAD_DIM_A
DILATED_BRANCHES = ((128, 1), (512, 4), (2048, 16))
N_HEADS_B = D_MODEL // 128
QK_NOPE_DIM = 128
QK_ROPE_DIM = 64
V_HEAD_DIM = 128
KV_LORA_RANK = D_MODEL // 4
Q_LORA_RANK = D_MODEL // 4
ROPE_THETA = 10000.0
Q_BLOCK = 128

kernel_name = "yoco_dilated_swa_mla_macaron"


def rms_norm(x, g):
    xf = x.astype(jnp.float32)
    y = xf * lax.rsqrt(jnp.mean(xf * xf, axis=-1, keepdims=True) + EPS)
    return (y * g.astype(jnp.float32)).astype(x.dtype)


def swiglu(h, w_gate, w_up, w_down):
    return (jax.nn.silu(h @ w_gate) * (h @ w_up)) @ w_down


def alibi_slopes(n_heads):
    return jnp.asarray(2.0 ** (-8.0 * (np.arange(n_heads) + 1) / n_heads), dtype=jnp.float32)


def rope_tables(seq):
    inv = 1.0 / (ROPE_THETA ** (jnp.arange(0, QK_ROPE_DIM, 2, dtype=jnp.float32) / QK_ROPE_DIM))
    ang = jnp.arange(seq, dtype=jnp.float32)[:, None] * inv[None, :]
    return jnp.cos(ang), jnp.sin(ang)


def apply_rope(t, cos, sin):
    tf = t.astype(jnp.float32)
    t1, t2 = jnp.split(tf, 2, axis=-1)
    return jnp.concatenate([t1 * cos - t2 * sin, t1 * sin + t2 * cos], axis=-1).astype(t.dtype)


def dilated_branch(q, k, v, window, dilation, slopes):
    B, S, H, Dh = q.shape
    n = window // dilation
    span = n * dilation
    Sp = -(-S // span) * span
    nb = Sp // span

    def to_blocks(t):
        t = jnp.pad(t, ((0, 0), (0, Sp - S), (0, 0), (0, 0)))
        t = t.reshape(B, Sp // dilation, dilation, H, Dh).transpose(0, 2, 1, 3, 4)
        return t.reshape(B, dilation, nb, n, H, Dh)

    def with_prev(t):
        prev = jnp.pad(t[:, :, :-1], ((0, 0), (0, 0), (1, 0), (0, 0), (0, 0), (0, 0)))
        return jnp.concatenate([prev, t], axis=3)

    qb = to_blocks(q)
    kw = with_prev(to_blocks(k))
    vw = with_prev(to_blocks(v))
    s = jnp.einsum('brcihd,brcjhd->brchij', qb, kw, preferred_element_type=jnp.float32) * (Dh ** -0.5)
    i = jnp.arange(n)[:, None]
    j = jnp.arange(2 * n)[None, :]
    steps = n + i - j
    band = (steps >= 0) & (steps <= n)
    valid = band[None] & ((jnp.arange(nb)[:, None, None] > 0) | (j >= n)[None])
    bias = -slopes[:, None, None] * (dilation * steps).astype(jnp.float32)[None]
    s = jnp.where(valid[None, None, :, None], s + bias[None, None, None], -jnp.inf)
    m = jnp.max(s, axis=-1, keepdims=True)
    p = jnp.exp(s - m)
    l = jnp.sum(p, axis=-1, keepdims=True)
    o = jnp.einsum('brchij,brcjhd->brcihd', (p / l).astype(v.dtype), vw)
    lse = (m + jnp.log(l))[..., 0]
    o = o.reshape(B, dilation, Sp // dilation, H, Dh).transpose(0, 2, 1, 3, 4).reshape(B, Sp, H, Dh)[:, :S]
    lse = lse.transpose(0, 1, 2, 4, 3).reshape(B, dilation, Sp // dilation, H)
    lse = lse.transpose(0, 2, 1, 3).reshape(B, Sp, H)[:, :S]
    return o, lse


def dilated_attention(h, w_qkv, w_o, slopes):
    B, S, _ = h.shape
    qkv = (h @ w_qkv).reshape(B, S, 3, N_HEADS_A, HEAD_DIM_A)
    q, k, v = qkv[:, :, 0], qkv[:, :, 1], qkv[:, :, 2]
    outs, lses = [], []
    for window, dilation in DILATED_BRANCHES:
        o, lse = dilated_branch(q, k, v, window, dilation, slopes)
        outs.append(o)
        lses.append(lse)
    wts = jax.nn.softmax(jnp.stack(lses, axis=0), axis=0)
    o = jnp.einsum('gbsh,gbshd->bshd', wts.astype(q.dtype), jnp.stack(outs, axis=0))
    return o.reshape(B, S, N_HEADS_A * HEAD_DIM_A) @ w_o


def mla_shared_kv(x, kv_norm, b_wdkv, b_ckv_norm, b_wkr, b_wuk, b_wuv, cos, sin):
    h = rms_norm(x, kv_norm)
    c_kv = rms_norm(h @ b_wdkv, b_ckv_norm)
    k_nope = jnp.einsum('bsc,chd->bshd', c_kv, b_wuk)
    v = jnp.einsum('bsc,chd->bshd', c_kv, b_wuv)
    k_rope = apply_rope(h @ b_wkr, cos, sin)
    return k_nope, k_rope, v


def mla_attention(h, k_nope, k_rope, v, w_dq, cq_norm, w_uq, w_o, cos, sin):
    B, S, _ = h.shape
    c_q = rms_norm(h @ w_dq, cq_norm)
    q = jnp.einsum('bsc,chd->bshd', c_q, w_uq)
    q_nope = q[..., :QK_NOPE_DIM]
    q_rope = apply_rope(q[..., QK_NOPE_DIM:], cos[:, None, :], sin[:, None, :])
    nb = S // Q_BLOCK
    qn_b = q_nope.reshape(B, nb, Q_BLOCK, N_HEADS_B, QK_NOPE_DIM).transpose(1, 0, 2, 3, 4)
    qr_b = q_rope.reshape(B, nb, Q_BLOCK, N_HEADS_B, QK_ROPE_DIM).transpose(1, 0, 2, 3, 4)
    starts = jnp.arange(nb, dtype=jnp.int32) * Q_BLOCK
    scale = (QK_NOPE_DIM + QK_ROPE_DIM) ** -0.5
    kpos = jnp.arange(S, dtype=jnp.int32)

    def attend(args):
        qn, qr, start = args
        s = (jnp.einsum('bihd,bjhd->bhij', qn, k_nope, preferred_element_type=jnp.float32)
             + jnp.einsum('bihr,bjr->bhij', qr, k_rope, preferred_element_type=jnp.float32)) * scale
        qpos = start + jnp.arange(Q_BLOCK, dtype=jnp.int32)
        s = jnp.where(kpos[None, :] <= qpos[:, None], s, -jnp.inf)
        p = jax.nn.softmax(s, axis=-1)
        return jnp.einsum('bhij,bjhd->bihd', p.astype(v.dtype), v)

    o = lax.map(attend, (qn_b, qr_b, starts))
    o = o.transpose(1, 0, 2, 3, 4).reshape(B, S, N_HEADS_B * V_HEAD_DIM)
    return o @ w_o


def _fwd_setup_inputs(seed: int = 0) -> dict:
    key = jax.random.key(seed)
    ks = jax.random.split(key, 24)
    f32 = jnp.float32

    def w(k, shape, fan_in):
        return jax.random.normal(k, shape, f32) * (fan_in ** -0.5)

    def gain(k, shape):
        return 1.0 + 0.01 * jax.random.normal(k, shape, f32)

    D, F = D_MODEL, D_FF
    return {
        "x": jax.random.normal(ks[0], (BATCH, SEQ, D), f32),
        "ffn_norm1": gain(ks[1], (DEPTH, D)),
        "ffn1_wg": w(ks[2], (DEPTH, D, F), D),
        "ffn1_wu": w(ks[3], (DEPTH, D, F), D),
        "ffn1_wd": w(ks[4], (DEPTH, F, D), F),
        "mix_norm": gain(ks[5], (DEPTH, D)),
        "ffn_norm2": gain(ks[6], (DEPTH, D)),
        "ffn2_wg": w(ks[7], (DEPTH, D, F), D),
        "ffn2_wu": w(ks[8], (DEPTH, D, F), D),
        "ffn2_wd": w(ks[9], (DEPTH, F, D), F),
        "a_wqkv": w(ks[10], (N_A_LAYERS, D, 3 * N_HEADS_A * HEAD_DIM_A), D),
        "a_wo": w(ks[11], (N_A_LAYERS, N_HEADS_A * HEAD_DIM_A, D), N_HEADS_A * HEAD_DIM_A),
        "kv_norm": gain(ks[12], (D,)),
        "b_wdkv": w(ks[13], (D, KV_LORA_RANK), D),
        "b_ckv_norm": gain(ks[14], (KV_LORA_RANK,)),
        "b_wkr": w(ks[15], (D, QK_ROPE_DIM), D),
        "b_wuk": w(ks[16], (KV_LORA_RANK, N_HEADS_B, QK_NOPE_DIM), KV_LORA_RANK),
        "b_wuv": w(ks[17], (KV_LORA_RANK, N_HEADS_B, V_HEAD_DIM), KV_LORA_RANK),
        "b_wdq": w(ks[18], (N_B_LAYERS, D, Q_LORA_RANK), D),
        "b_cq_norm": gain(ks[19], (N_B_LAYERS, Q_LORA_RANK)),
        "b_wuq": w(ks[20], (N_B_LAYERS, Q_LORA_RANK, N_HEADS_B, QK_NOPE_DIM + QK_ROPE_DIM), Q_LORA_RANK),
        "b_wo": w(ks[21], (N_B_LAYERS, N_HEADS_B * V_HEAD_DIM, D), N_HEADS_B * V_HEAD_DIM),
        "final_norm": gain(ks[22], (D,)),
    }


def _fwd_reference(x, ffn_norm1, ffn1_wg, ffn1_wu, ffn1_wd, mix_norm, ffn_norm2, ffn2_wg, ffn2_wu, ffn2_wd,
              a_wqkv, a_wo, kv_norm, b_wdkv, b_ckv_norm, b_wkr, b_wuk, b_wuv,
              b_wdq, b_cq_norm, b_wuq, b_wo, final_norm):
    S = x.shape[1]
    slopes = alibi_slopes(N_HEADS_A)
    cos, sin = rope_tables(S)
    k_nope = k_rope = v_shared = None
    for layer in range(DEPTH):
        if layer == N_A_LAYERS:
            k_nope, k_rope, v_shared = mla_shared_kv(x, kv_norm, b_wdkv, b_ckv_norm, b_wkr, b_wuk, b_wuv, cos, sin)
        x = x + 0.5 * swiglu(rms_norm(x, ffn_norm1[layer]), ffn1_wg[layer], ffn1_wu[layer], ffn1_wd[layer])
        h = rms_norm(x, mix_norm[layer])
        if layer < N_A_LAYERS:
            x = x + dilated_attention(h, a_wqkv[layer], a_wo[layer], slopes)
        else:
            jb = layer - N_A_LAYERS
            x = x + mla_attention(h, k_nope, k_rope, v_shared, b_wdq[jb], b_cq_norm[jb], b_wuq[jb], b_wo[jb], cos, sin)
        x = x + 0.5 * swiglu(rms_norm(x, ffn_norm2[layer]), ffn2_wg[layer], ffn2_wu[layer], ffn2_wd[layer])
    return rms_norm(x, final_norm)


import jax as _jax
import jax.numpy as _jnp

TWIN_FORMAT = 'train_step'
FWD_PARAMS = ['x', 'ffn_norm1', 'ffn1_wg', 'ffn1_wu', 'ffn1_wd', 'mix_norm', 'ffn_norm2', 'ffn2_wg', 'ffn2_wu', 'ffn2_wd', 'a_wqkv', 'a_wo', 'kv_norm', 'b_wdkv', 'b_ckv_norm', 'b_wkr', 'b_wuk', 'b_wuv', 'b_wdq', 'b_cq_norm', 'b_wuq', 'b_wo', 'final_norm']
TWIN_WEIGHTS = ['ffn_norm1', 'ffn1_wg', 'ffn1_wu', 'ffn1_wd', 'mix_norm', 'ffn_norm2', 'ffn2_wg', 'ffn2_wu', 'ffn2_wd', 'a_wqkv', 'a_wo', 'kv_norm', 'b_wdkv', 'b_ckv_norm', 'b_wkr', 'b_wuk', 'b_wuv', 'b_wdq', 'b_cq_norm', 'b_wuq', 'b_wo', 'final_norm']
TWIN_DIFF_INPUT = 'x'
TWIN_INPUTS = ['x', 'ffn_norm1', 'ffn1_wg', 'ffn1_wu', 'ffn1_wd', 'mix_norm', 'ffn_norm2', 'ffn2_wg', 'ffn2_wu', 'ffn2_wd', 'a_wqkv', 'a_wo', 'kv_norm', 'b_wdkv', 'b_ckv_norm', 'b_wkr', 'b_wuk', 'b_wuv', 'b_wdq', 'b_cq_norm', 'b_wuq', 'b_wo', 'final_norm', 'loss_target', 'm_ffn_norm1', 'm_ffn1_wg', 'm_ffn1_wu', 'm_ffn1_wd', 'm_mix_norm', 'm_ffn_norm2', 'm_ffn2_wg', 'm_ffn2_wu', 'm_ffn2_wd', 'm_a_wqkv', 'm_a_wo', 'm_kv_norm', 'm_b_wdkv', 'm_b_ckv_norm', 'm_b_wkr', 'm_b_wuk', 'm_b_wuv', 'm_b_wdq', 'm_b_cq_norm', 'm_b_wuq', 'm_b_wo', 'm_final_norm', 'v_ffn_norm1', 'v_ffn1_wg', 'v_ffn1_wu', 'v_ffn1_wd', 'v_mix_norm', 'v_ffn_norm2', 'v_ffn2_wg', 'v_ffn2_wu', 'v_ffn2_wd', 'v_a_wqkv', 'v_a_wo', 'v_kv_norm', 'v_b_wdkv', 'v_b_ckv_norm', 'v_b_wkr', 'v_b_wuk', 'v_b_wuv', 'v_b_wdq', 'v_b_cq_norm', 'v_b_wuq', 'v_b_wo', 'v_final_norm']
TWIN_OUTPUTS = ['loss', 'grad_x', 'grad_ffn_norm1', 'grad_ffn1_wg', 'grad_ffn1_wu', 'grad_ffn1_wd', 'grad_mix_norm', 'grad_ffn_norm2', 'grad_ffn2_wg', 'grad_ffn2_wu', 'grad_ffn2_wd', 'grad_a_wqkv', 'grad_a_wo', 'grad_kv_norm', 'grad_b_wdkv', 'grad_b_ckv_norm', 'grad_b_wkr', 'grad_b_wuk', 'grad_b_wuv', 'grad_b_wdq', 'grad_b_cq_norm', 'grad_b_wuq', 'grad_b_wo', 'grad_final_norm', 'delta_ffn_norm1', 'delta_ffn1_wg', 'delta_ffn1_wu', 'delta_ffn1_wd', 'delta_mix_norm', 'delta_ffn_norm2', 'delta_ffn2_wg', 'delta_ffn2_wu', 'delta_ffn2_wd', 'delta_a_wqkv', 'delta_a_wo', 'delta_kv_norm', 'delta_b_wdkv', 'delta_b_ckv_norm', 'delta_b_wkr', 'delta_b_wuk', 'delta_b_wuv', 'delta_b_wdq', 'delta_b_cq_norm', 'delta_b_wuq', 'delta_b_wo', 'delta_final_norm', 'new_m_ffn_norm1', 'new_m_ffn1_wg', 'new_m_ffn1_wu', 'new_m_ffn1_wd', 'new_m_mix_norm', 'new_m_ffn_norm2', 'new_m_ffn2_wg', 'new_m_ffn2_wu', 'new_m_ffn2_wd', 'new_m_a_wqkv', 'new_m_a_wo', 'new_m_kv_norm', 'new_m_b_wdkv', 'new_m_b_ckv_norm', 'new_m_b_wkr', 'new_m_b_wuk', 'new_m_b_wuv', 'new_m_b_wdq', 'new_m_b_cq_norm', 'new_m_b_wuq', 'new_m_b_wo', 'new_m_final_norm', 'new_v_ffn_norm1', 'new_v_ffn1_wg', 'new_v_ffn1_wu', 'new_v_ffn1_wd', 'new_v_mix_norm', 'new_v_ffn_norm2', 'new_v_ffn2_wg', 'new_v_ffn2_wu', 'new_v_ffn2_wd', 'new_v_a_wqkv', 'new_v_a_wo', 'new_v_kv_norm', 'new_v_b_wdkv', 'new_v_b_ckv_norm', 'new_v_b_wkr', 'new_v_b_wuk', 'new_v_b_wuv', 'new_v_b_wdq', 'new_v_b_cq_norm', 'new_v_b_wuq', 'new_v_b_wo', 'new_v_final_norm']
TWIN_LEAF_KINDS = {'loss': 'loss', 'grad_x': 'grad_x', 'grad_ffn_norm1': 'grad_w', 'grad_ffn1_wg': 'grad_w', 'grad_ffn1_wu': 'grad_w', 'grad_ffn1_wd': 'grad_w', 'grad_mix_norm': 'grad_w', 'grad_ffn_norm2': 'grad_w', 'grad_ffn2_wg': 'grad_w', 'grad_ffn2_wu': 'grad_w', 'grad_ffn2_wd': 'grad_w', 'grad_a_wqkv': 'grad_w', 'grad_a_wo': 'grad_w', 'grad_kv_norm': 'grad_w', 'grad_b_wdkv': 'grad_w', 'grad_b_ckv_norm': 'grad_w', 'grad_b_wkr': 'grad_w', 'grad_b_wuk': 'grad_w', 'grad_b_wuv': 'grad_w', 'grad_b_wdq': 'grad_w', 'grad_b_cq_norm': 'grad_w', 'grad_b_wuq': 'grad_w', 'grad_b_wo': 'grad_w', 'grad_final_norm': 'grad_w', 'delta_ffn_norm1': 'delta_w', 'delta_ffn1_wg': 'delta_w', 'delta_ffn1_wu': 'delta_w', 'delta_ffn1_wd': 'delta_w', 'delta_mix_norm': 'delta_w', 'delta_ffn_norm2': 'delta_w', 'delta_ffn2_wg': 'delta_w', 'delta_ffn2_wu': 'delta_w', 'delta_ffn2_wd': 'delta_w', 'delta_a_wqkv': 'delta_w', 'delta_a_wo': 'delta_w', 'delta_kv_norm': 'delta_w', 'delta_b_wdkv': 'delta_w', 'delta_b_ckv_norm': 'delta_w', 'delta_b_wkr': 'delta_w', 'delta_b_wuk': 'delta_w', 'delta_b_wuv': 'delta_w', 'delta_b_wdq': 'delta_w', 'delta_b_cq_norm': 'delta_w', 'delta_b_wuq': 'delta_w', 'delta_b_wo': 'delta_w', 'delta_final_norm': 'delta_w', 'new_m_ffn_norm1': 'new_m', 'new_m_ffn1_wg': 'new_m', 'new_m_ffn1_wu': 'new_m', 'new_m_ffn1_wd': 'new_m', 'new_m_mix_norm': 'new_m', 'new_m_ffn_norm2': 'new_m', 'new_m_ffn2_wg': 'new_m', 'new_m_ffn2_wu': 'new_m', 'new_m_ffn2_wd': 'new_m', 'new_m_a_wqkv': 'new_m', 'new_m_a_wo': 'new_m', 'new_m_kv_norm': 'new_m', 'new_m_b_wdkv': 'new_m', 'new_m_b_ckv_norm': 'new_m', 'new_m_b_wkr': 'new_m', 'new_m_b_wuk': 'new_m', 'new_m_b_wuv': 'new_m', 'new_m_b_wdq': 'new_m', 'new_m_b_cq_norm': 'new_m', 'new_m_b_wuq': 'new_m', 'new_m_b_wo': 'new_m', 'new_m_final_norm': 'new_m', 'new_v_ffn_norm1': 'new_v', 'new_v_ffn1_wg': 'new_v', 'new_v_ffn1_wu': 'new_v', 'new_v_ffn1_wd': 'new_v', 'new_v_mix_norm': 'new_v', 'new_v_ffn_norm2': 'new_v', 'new_v_ffn2_wg': 'new_v', 'new_v_ffn2_wu': 'new_v', 'new_v_ffn2_wd': 'new_v', 'new_v_a_wqkv': 'new_v', 'new_v_a_wo': 'new_v', 'new_v_kv_norm': 'new_v', 'new_v_b_wdkv': 'new_v', 'new_v_b_ckv_norm': 'new_v', 'new_v_b_wkr': 'new_v', 'new_v_b_wuk': 'new_v', 'new_v_b_wuv': 'new_v', 'new_v_b_wdq': 'new_v', 'new_v_b_cq_norm': 'new_v', 'new_v_b_wuq': 'new_v', 'new_v_b_wo': 'new_v', 'new_v_final_norm': 'new_v'}


def _forward(args):
    return _fwd_reference(*[args[k] for k in FWD_PARAMS])


def _output_shape():
    def fwd():
        inp = _fwd_setup_inputs(0)
        return _fwd_reference(*[inp[k] for k in FWD_PARAMS])
    out = _jax.eval_shape(fwd)
    return out.shape, out.dtype

N_MICROBATCH = 1
ADAM_LR = 0.001
ADAM_B1 = 0.9
ADAM_B2 = 0.999
ADAM_EPS = 1e-08
ADAM_WD = 0.01
ADAM_STEP = 10
PER_EXAMPLE_BATCH_AXIS = {'x': 0, 'loss_target': 0}
SHARED_INPUTS = []
_WEIGHT_DTYPES = {'ffn_norm1': _jnp.float32, 'ffn1_wg': _jnp.float32, 'ffn1_wu': _jnp.float32, 'ffn1_wd': _jnp.float32, 'mix_norm': _jnp.float32, 'ffn_norm2': _jnp.float32, 'ffn2_wg': _jnp.float32, 'ffn2_wu': _jnp.float32, 'ffn2_wd': _jnp.float32, 'a_wqkv': _jnp.float32, 'a_wo': _jnp.float32, 'kv_norm': _jnp.float32, 'b_wdkv': _jnp.float32, 'b_ckv_norm': _jnp.float32, 'b_wkr': _jnp.float32, 'b_wuk': _jnp.float32, 'b_wuv': _jnp.float32, 'b_wdq': _jnp.float32, 'b_cq_norm': _jnp.float32, 'b_wuq': _jnp.float32, 'b_wo': _jnp.float32, 'final_norm': _jnp.float32}
MOMENT_SCALE = {'ffn_norm1': 5.502092e-02, 'ffn1_wg': 2.269829e-02, 'ffn1_wu': 2.198315e-02, 'ffn1_wd': 3.641915e-02, 'mix_norm': 5.244049e-02, 'ffn_norm2': 4.824514e-02, 'ffn2_wg': 2.019313e-02, 'ffn2_wu': 1.956939e-02, 'ffn2_wd': 3.243287e-02, 'a_wqkv': 4.232360e-02, 'a_wo': 5.616217e-02, 'kv_norm': 3.892107e-02, 'b_wdkv': 7.283447e-02, 'b_ckv_norm': 7.363302e-02, 'b_wkr': 6.902022e-02, 'b_wuk': 1.771447e-02, 'b_wuv': 3.244569e-02, 'b_wdq': 3.032014e-02, 'b_cq_norm': 2.946229e-02, 'b_wuq': 1.237811e-02, 'b_wo': 2.307698e-02, 'final_norm': 3.194504e+01}


def _to_microbatches(a, axis):
    t = _jnp.moveaxis(a, axis, 0)
    t = t.reshape((N_MICROBATCH, t.shape[0] // N_MICROBATCH) + t.shape[1:])
    return _jnp.moveaxis(t, 1, axis + 1)


def setup_inputs(seed: int = 0) -> dict:
    inp = _fwd_setup_inputs(seed)
    key = _jax.random.fold_in(_jax.random.key(seed), 7919)
    shape, _ = _output_shape()
    out = dict(inp)
    out["loss_target"] = _jax.random.normal(_jax.random.fold_in(key, 0), shape, _jnp.float32)
    for i, name in enumerate(TWIN_WEIGHTS):
        w = inp[name].astype(_jnp.float32)
        if MOMENT_SCALE is None:
            s = _jnp.sqrt(_jnp.mean(_jnp.square(w)) + 1e-30)
        else:
            s = MOMENT_SCALE[name]
        km, kv = _jax.random.split(_jax.random.fold_in(key, i + 1))
        out[name] = w
        out["m_" + name] = s * _jax.random.normal(km, w.shape, _jnp.float32)
        out["v_" + name] = (s * s) * _jax.random.uniform(kv, w.shape, _jnp.float32, 0.5, 1.5)
    if N_MICROBATCH > 1:
        for name, axis in PER_EXAMPLE_BATCH_AXIS.items():
            out[name] = _to_microbatches(out[name], axis)
    return {'x': out['x'], 'ffn_norm1': out['ffn_norm1'], 'ffn1_wg': out['ffn1_wg'], 'ffn1_wu': out['ffn1_wu'], 'ffn1_wd': out['ffn1_wd'], 'mix_norm': out['mix_norm'], 'ffn_norm2': out['ffn_norm2'], 'ffn2_wg': out['ffn2_wg'], 'ffn2_wu': out['ffn2_wu'], 'ffn2_wd': out['ffn2_wd'], 'a_wqkv': out['a_wqkv'], 'a_wo': out['a_wo'], 'kv_norm': out['kv_norm'], 'b_wdkv': out['b_wdkv'], 'b_ckv_norm': out['b_ckv_norm'], 'b_wkr': out['b_wkr'], 'b_wuk': out['b_wuk'], 'b_wuv': out['b_wuv'], 'b_wdq': out['b_wdq'], 'b_cq_norm': out['b_cq_norm'], 'b_wuq': out['b_wuq'], 'b_wo': out['b_wo'], 'final_norm': out['final_norm'], 'loss_target': out['loss_target'], 'm_ffn_norm1': out['m_ffn_norm1'], 'm_ffn1_wg': out['m_ffn1_wg'], 'm_ffn1_wu': out['m_ffn1_wu'], 'm_ffn1_wd': out['m_ffn1_wd'], 'm_mix_norm': out['m_mix_norm'], 'm_ffn_norm2': out['m_ffn_norm2'], 'm_ffn2_wg': out['m_ffn2_wg'], 'm_ffn2_wu': out['m_ffn2_wu'], 'm_ffn2_wd': out['m_ffn2_wd'], 'm_a_wqkv': out['m_a_wqkv'], 'm_a_wo': out['m_a_wo'], 'm_kv_norm': out['m_kv_norm'], 'm_b_wdkv': out['m_b_wdkv'], 'm_b_ckv_norm': out['m_b_ckv_norm'], 'm_b_wkr': out['m_b_wkr'], 'm_b_wuk': out['m_b_wuk'], 'm_b_wuv': out['m_b_wuv'], 'm_b_wdq': out['m_b_wdq'], 'm_b_cq_norm': out['m_b_cq_norm'], 'm_b_wuq': out['m_b_wuq'], 'm_b_wo': out['m_b_wo'], 'm_final_norm': out['m_final_norm'], 'v_ffn_norm1': out['v_ffn_norm1'], 'v_ffn1_wg': out['v_ffn1_wg'], 'v_ffn1_wu': out['v_ffn1_wu'], 'v_ffn1_wd': out['v_ffn1_wd'], 'v_mix_norm': out['v_mix_norm'], 'v_ffn_norm2': out['v_ffn_norm2'], 'v_ffn2_wg': out['v_ffn2_wg'], 'v_ffn2_wu': out['v_ffn2_wu'], 'v_ffn2_wd': out['v_ffn2_wd'], 'v_a_wqkv': out['v_a_wqkv'], 'v_a_wo': out['v_a_wo'], 'v_kv_norm': out['v_kv_norm'], 'v_b_wdkv': out['v_b_wdkv'], 'v_b_ckv_norm': out['v_b_ckv_norm'], 'v_b_wkr': out['v_b_wkr'], 'v_b_wuk': out['v_b_wuk'], 'v_b_wuv': out['v_b_wuv'], 'v_b_wdq': out['v_b_wdq'], 'v_b_cq_norm': out['v_b_cq_norm'], 'v_b_wuq': out['v_b_wuq'], 'v_b_wo': out['v_b_wo'], 'v_final_norm': out['v_final_norm']}


def _loss(weights, diff, rest, loss_target):
    with _jax.named_scope("forward"):
        args = {**rest, TWIN_DIFF_INPUT: diff, **{k: w.astype(_WEIGHT_DTYPES[k]) for k, w in weights.items()}}
        y = _forward(args)
    with _jax.named_scope("loss_head"):
        err = _jnp.square(y.astype(_jnp.float32) - loss_target)
        return 0.5 * _jnp.sum(_jnp.mean(err, axis=-1)) if err.ndim else 0.5 * err


def _adamw(w, g, m, v):
    m = ADAM_B1 * m + (1.0 - ADAM_B1) * g
    v = ADAM_B2 * v + (1.0 - ADAM_B2) * _jnp.square(g)
    m_hat = m / (1.0 - ADAM_B1 ** ADAM_STEP)
    v_hat = v / (1.0 - ADAM_B2 ** ADAM_STEP)
    delta = -ADAM_LR * (m_hat / (_jnp.sqrt(v_hat) + ADAM_EPS) + ADAM_WD * w)
    return delta, m, v


def reference(x, ffn_norm1, ffn1_wg, ffn1_wu, ffn1_wd, mix_norm, ffn_norm2, ffn2_wg, ffn2_wu, ffn2_wd, a_wqkv, a_wo, kv_norm, b_wdkv, b_ckv_norm, b_wkr, b_wuk, b_wuv, b_wdq, b_cq_norm, b_wuq, b_wo, final_norm, loss_target, m_ffn_norm1, m_ffn1_wg, m_ffn1_wu, m_ffn1_wd, m_mix_norm, m_ffn_norm2, m_ffn2_wg, m_ffn2_wu, m_ffn2_wd, m_a_wqkv, m_a_wo, m_kv_norm, m_b_wdkv, m_b_ckv_norm, m_b_wkr, m_b_wuk, m_b_wuv, m_b_wdq, m_b_cq_norm, m_b_wuq, m_b_wo, m_final_norm, v_ffn_norm1, v_ffn1_wg, v_ffn1_wu, v_ffn1_wd, v_mix_norm, v_ffn_norm2, v_ffn2_wg, v_ffn2_wu, v_ffn2_wd, v_a_wqkv, v_a_wo, v_kv_norm, v_b_wdkv, v_b_ckv_norm, v_b_wkr, v_b_wuk, v_b_wuv, v_b_wdq, v_b_cq_norm, v_b_wuq, v_b_wo, v_final_norm):
    given = dict(x=x, ffn_norm1=ffn_norm1, ffn1_wg=ffn1_wg, ffn1_wu=ffn1_wu, ffn1_wd=ffn1_wd, mix_norm=mix_norm, ffn_norm2=ffn_norm2, ffn2_wg=ffn2_wg, ffn2_wu=ffn2_wu, ffn2_wd=ffn2_wd, a_wqkv=a_wqkv, a_wo=a_wo, kv_norm=kv_norm, b_wdkv=b_wdkv, b_ckv_norm=b_ckv_norm, b_wkr=b_wkr, b_wuk=b_wuk, b_wuv=b_wuv, b_wdq=b_wdq, b_cq_norm=b_cq_norm, b_wuq=b_wuq, b_wo=b_wo, final_norm=final_norm, loss_target=loss_target, m_ffn_norm1=m_ffn_norm1, m_ffn1_wg=m_ffn1_wg, m_ffn1_wu=m_ffn1_wu, m_ffn1_wd=m_ffn1_wd, m_mix_norm=m_mix_norm, m_ffn_norm2=m_ffn_norm2, m_ffn2_wg=m_ffn2_wg, m_ffn2_wu=m_ffn2_wu, m_ffn2_wd=m_ffn2_wd, m_a_wqkv=m_a_wqkv, m_a_wo=m_a_wo, m_kv_norm=m_kv_norm, m_b_wdkv=m_b_wdkv, m_b_ckv_norm=m_b_ckv_norm, m_b_wkr=m_b_wkr, m_b_wuk=m_b_wuk, m_b_wuv=m_b_wuv, m_b_wdq=m_b_wdq, m_b_cq_norm=m_b_cq_norm, m_b_wuq=m_b_wuq, m_b_wo=m_b_wo, m_final_norm=m_final_norm, v_ffn_norm1=v_ffn_norm1, v_ffn1_wg=v_ffn1_wg, v_ffn1_wu=v_ffn1_wu, v_ffn1_wd=v_ffn1_wd, v_mix_norm=v_mix_norm, v_ffn_norm2=v_ffn_norm2, v_ffn2_wg=v_ffn2_wg, v_ffn2_wu=v_ffn2_wu, v_ffn2_wd=v_ffn2_wd, v_a_wqkv=v_a_wqkv, v_a_wo=v_a_wo, v_kv_norm=v_kv_norm, v_b_wdkv=v_b_wdkv, v_b_ckv_norm=v_b_ckv_norm, v_b_wkr=v_b_wkr, v_b_wuk=v_b_wuk, v_b_wuv=v_b_wuv, v_b_wdq=v_b_wdq, v_b_cq_norm=v_b_cq_norm, v_b_wuq=v_b_wuq, v_b_wo=v_b_wo, v_final_norm=v_final_norm)
    weights = {n: given[n] for n in TWIN_WEIGHTS}
    shared = {n: given[n] for n in SHARED_INPUTS}
    per_example = {n: given[n] for n in ['x']}
    grad_fn = _jax.value_and_grad(_loss, argnums=(0, 1))

    def one_microbatch(ex, loss_target):
        ex = dict(ex)
        diff = ex.pop(TWIN_DIFF_INPUT)
        return grad_fn(weights, diff, {**shared, **ex}, loss_target)

    if N_MICROBATCH == 1:
        loss, (grad_w, grad_x) = one_microbatch(per_example, given["loss_target"])
    else:
        def body(carry, xs):
            loss_sum, grad_sum = carry
            l_k, (gw_k, gx_k) = one_microbatch(xs[0], xs[1])
            with _jax.named_scope("update"):
                return (loss_sum + l_k, _jax.tree.map(_jnp.add, grad_sum, gw_k)), gx_k

        init = (_jnp.zeros((), _jnp.float32), _jax.tree.map(_jnp.zeros_like, weights))
        (loss, grad_w), grad_x = _jax.lax.scan(body, init, (per_example, given["loss_target"]))
    with _jax.named_scope("update"):
        delta_w, new_m, new_v = {}, {}, {}
        for n in TWIN_WEIGHTS:
            delta_w[n], new_m[n], new_v[n] = _adamw(weights[n], grad_w[n], given["m_" + n], given["v_" + n])
    return (loss, grad_x, *[grad_w[n] for n in TWIN_WEIGHTS], *[delta_w[n] for n in TWIN_WEIGHTS],
            *[new_m[n] for n in TWIN_WEIGHTS], *[new_v[n] for n in TWIN_WEIGHTS])
```

```python
import functools
import math

import numpy as np
import jax
import jax.numpy as jnp
from jax import lax
from jax.experimental import pallas as pl
from jax.experimental.pallas import tpu as pltpu

BF = jnp.bfloat16
F32 = jnp.float32
MESH = pl.DeviceIdType.MESH
ANY = pl.BlockSpec(memory_space=pl.ANY)

N_DEV = 8
LANES = 128
HEAD = 128
ROPE_HALF = 32
DIL_N = 128
BRANCH_DILATIONS = (1, 4, 16)
ROPE_THETA = 10000.0
EPS = 1e-6
NEG = -1e30
VMEM_LIMIT = 58 * 1024 * 1024

ADAM_LR, ADAM_B1, ADAM_B2, ADAM_EPS, ADAM_WD, ADAM_STEP = 0.001, 0.9, 0.999, 1e-08, 0.01, 10

W_NAMES = ['ffn_norm1', 'ffn1_wg', 'ffn1_wu', 'ffn1_wd', 'mix_norm', 'ffn_norm2', 'ffn2_wg', 'ffn2_wu', 'ffn2_wd',
           'a_wqkv', 'a_wo', 'kv_norm', 'b_wdkv', 'b_ckv_norm', 'b_wkr', 'b_wuk', 'b_wuv', 'b_wdq', 'b_cq_norm',
           'b_wuq', 'b_wo', 'final_norm']
FFN_COL = ['ffn1_wg', 'ffn1_wu', 'ffn2_wg', 'ffn2_wu']
FFN_ROW = ['ffn1_wd', 'ffn2_wd']
MIX_W = ['a_wqkv', 'a_wo', 'b_wdkv', 'b_wkr', 'b_wuk', 'b_wuv', 'b_wdq', 'b_wuq', 'b_wo']
VEC_W = ['ffn_norm1', 'mix_norm', 'ffn_norm2', 'kv_norm', 'b_ckv_norm', 'b_cq_norm', 'final_norm']


def _tile(n, pref):
    t = min(n, pref)
    assert n % t == 0, (n, pref)
    return t


def _params(sem):
    return pltpu.CompilerParams(dimension_semantics=sem, vmem_limit_bytes=VMEM_LIMIT)


def _dot(a, b):
    return jnp.dot(a, b, preferred_element_type=F32)


def _dot_nt(a, b):
    return lax.dot_general(a, b, (((1,), (1,)), ((), ())), preferred_element_type=F32)


def _dot_tn(a, b):
    return lax.dot_general(a, b, (((0,), (0,)), ((), ())), preferred_element_type=F32)


def _rms(x, g):
    r = lax.rsqrt(jnp.mean(x * x, axis=-1, keepdims=True) + EPS)
    return x * r * g, r


def _rms_bwd(x, g, dh):
    r = lax.rsqrt(jnp.mean(x * x, axis=-1, keepdims=True) + EPS)
    xhat = x * r
    gd = dh * g
    dx = r * (gd - xhat * jnp.mean(gd * xhat, axis=-1, keepdims=True))
    return dx, jnp.sum(dh * xhat, axis=0, keepdims=True)


def _rope(t, cos, sin):
    return t * cos + pltpu.roll(t, 2 * ROPE_HALF, 1) * sin


def _rope_bwd(g, cos, sin):
    return g * cos + pltpu.roll(g * sin, 2 * ROPE_HALF, 1)


def _coords():
    return lax.axis_index("x"), lax.axis_index("y"), lax.axis_index("c")


def _lin(px, py, pc):
    return 4 * px + 2 * py + pc


def all_gather(xs):
    n = len(xs)

    def body(*refs):
        ins, outs = refs[:n], refs[n:2 * n]
        ssem, rsem, lsem = refs[2 * n:]
        x, y, c = _coords()
        me, sibling = (x, y, c), (x, y, 1 - c)
        chips = [(1 - x, y), (x, 1 - y), (1 - x, 1 - y)]

        def copy(t, k, block, to, src=None):
            slot = outs[t].at[_lin(*block)]
            return pltpu.make_async_remote_copy(
                src_ref=slot if src is None else src, dst_ref=slot,
                send_sem=ssem.at[7 * t + k], recv_sem=rsem.at[7 * t + k],
                device_id=to, device_id_type=MESH)

        mine, first, passed = [], [], []
        for t in range(n):
            m = pltpu.make_async_copy(ins[t], outs[t].at[_lin(*me)], lsem.at[t])
            m.start()
            mine.append(m)
            f = [copy(t, 0, me, sibling, src=ins[t])]
            f += [copy(t, 1 + j, me, (*chip, c), src=ins[t]) for j, chip in enumerate(chips)]
            for cp in f:
                cp.start()
            first += f
        for t in range(n):
            for j, chip in enumerate(chips):
                copy(t, 1 + j, (*chip, c), me).wait_recv()
                cp = copy(t, 4 + j, (*chip, c), sibling)
                cp.start()
                passed.append(cp)
        for t in range(n):
            copy(t, 0, sibling, me).wait_recv()
            for j, chip in enumerate(chips):
                copy(t, 4 + j, (*chip, 1 - c), me).wait_recv()
        for cp in first + passed:
            cp.wait_send()
        for m in mine:
            m.wait()

    return pl.pallas_call(
        body, name="all_gather",
        out_shape=[jax.ShapeDtypeStruct((N_DEV,) + a.shape, a.dtype) for a in xs],
        in_specs=[ANY] * n, out_specs=[ANY] * n,
        scratch_shapes=[pltpu.SemaphoreType.DMA((7 * n,)), pltpu.SemaphoreType.DMA((7 * n,)),
                        pltpu.SemaphoreType.DMA((n,))],
    )(*xs)


def exchange(xs, bcast):
    n = len(xs)

    def body(*refs):
        ins, outs = refs[:n], refs[n:2 * n]
        ssem, rsem, lsem = refs[2 * n:]
        x, y, c = _coords()
        me = _lin(x, y, c)
        started = []
        for t in range(n):
            own = ins[t] if bcast[t] else ins[t].at[me]
            m = pltpu.make_async_copy(own, outs[t].at[me], lsem.at[t])
            m.start()
            started.append(m)
        peers = []
        for k in range(1, N_DEV):
            kx, ky, kc = (k >> 2) & 1, (k >> 1) & 1, k & 1
            px = 1 - x if kx else x
            py = 1 - y if ky else y
            pc = 1 - c if kc else c
            peers.append((k, (px, py, pc)))
        sends = []
        for t in range(n):
            for k, peer in peers:
                src = ins[t] if bcast[t] else ins[t].at[_lin(*peer)]
                cp = pltpu.make_async_remote_copy(
                    src_ref=src, dst_ref=outs[t].at[me],
                    send_sem=ssem.at[7 * t + k - 1], recv_sem=rsem.at[7 * t + k - 1],
                    device_id=peer, device_id_type=MESH)
                cp.start()
                sends.append(cp)
        for t in range(n):
            for k, peer in peers:
                slot = outs[t].at[_lin(*peer)]
                pltpu.make_async_remote_copy(
                    src_ref=slot, dst_ref=slot,
                    send_sem=ssem.at[7 * t + k - 1], recv_sem=rsem.at[7 * t + k - 1],
                    device_id=peer, device_id_type=MESH).wait_recv()
        for cp in sends:
            cp.wait_send()
        for m in started:
            m.wait()

    out_shape = [jax.ShapeDtypeStruct(((N_DEV,) + a.shape) if b else a.shape, a.dtype) for a, b in zip(xs, bcast)]
    return pl.pallas_call(
        body, name="grad_exchange", out_shape=out_shape,
        in_specs=[ANY] * n, out_specs=[ANY] * n,
        scratch_shapes=[pltpu.SemaphoreType.DMA((7 * n,)), pltpu.SemaphoreType.DMA((7 * n,)),
                        pltpu.SemaphoreType.DMA((n,))],
    )(*xs)


def mm(a, b, *, ta=False, tb=False, add=None, out_dtype=F32, tm=512, tn=512, tk=512, name="mm"):
    K, M = a.shape if ta else a.shape[::-1]
    N = b.shape[0] if tb else b.shape[1]
    assert (b.shape[1] if tb else b.shape[0]) == K and not (ta and tb)
    tm, tn, tk = _tile(M, tm), _tile(N, tn), _tile(K, tk)
    nk = K // tk
    has_add = add is not None

    def body(*refs):
        if has_add:
            a_ref, b_ref, add_ref, o_ref, acc = refs
        else:
            a_ref, b_ref, o_ref, acc = refs
        k = pl.program_id(2)

        @pl.when(k == 0)
        def _():
            acc[...] = jnp.zeros_like(acc)

        av = a_ref[...].astype(BF)
        bv = b_ref[...].astype(BF)
        if ta:
            acc[...] += _dot_tn(av, bv)
        elif tb:
            acc[...] += _dot_nt(av, bv)
        else:
            acc[...] += _dot(av, bv)

        @pl.when(k == nk - 1)
        def _():
            r = acc[...]
            if has_add:
                r = r + add_ref[...]
            o_ref[...] = r.astype(out_dtype)

    a_spec = pl.BlockSpec((tk, tm), lambda i, j, k: (k, i)) if ta else pl.BlockSpec((tm, tk), lambda i, j, k: (i, k))
    b_spec = pl.BlockSpec((tn, tk), lambda i, j, k: (j, k)) if tb else pl.BlockSpec((tk, tn), lambda i, j, k: (k, j))
    o_spec = pl.BlockSpec((tm, tn), lambda i, j, k: (i, j))
    in_specs = [a_spec, b_spec] + ([o_spec] if has_add else [])
    args = [a, b] + ([add] if has_add else [])
    return pl.pallas_call(
        body, name=name, grid=(M // tm, N // tn, nk),
        in_specs=in_specs, out_specs=o_spec,
        out_shape=jax.ShapeDtypeStruct((M, N), out_dtype),
        scratch_shapes=[pltpu.VMEM((tm, tn), F32)],
        compiler_params=_params(("parallel", "parallel", "arbitrary")),
    )(*args)


def norm_mm(x, gain, w, *, out_dtype, tn=512, rope=None, write_h=True, tm=512, name="norm_mm"):
    S, K = x.shape
    N = w.shape[1]
    tm, tn = _tile(S, tm), _tile(N, tn)
    if rope is not None:
        assert tn == LANES
    gain = gain.reshape(1, K)

    def body(*refs):
        refs = list(refs)
        x_ref, g_ref, w_ref = refs[:3]
        refs = refs[3:]
        if rope is not None:
            cos_ref, sin_ref = refs[:2]
            refs = refs[2:]
        y_ref = refs[0]
        h_ref = refs[1] if write_h else None
        hs = refs[-1]
        j = pl.program_id(1)

        @pl.when(j == 0)
        def _():
            h, _ = _rms(x_ref[...], g_ref[...])
            hb = h.astype(BF)
            hs[...] = hb
            if write_h:
                h_ref[...] = hb

        y = _dot(hs[...], w_ref[...])
        if rope is not None:
            y = _rope(y, cos_ref[...], sin_ref[...])
        y_ref[...] = y.astype(out_dtype)

    in_specs = [pl.BlockSpec((tm, K), lambda i, j: (i, 0)), pl.BlockSpec((1, K), lambda i, j: (0, 0)),
                pl.BlockSpec((K, tn), lambda i, j: (0, j))]
    args = [x, gain, w]
    if rope is not None:
        in_specs += [pl.BlockSpec((tm, LANES), lambda i, j: (i, 0))] * 2
        args += list(rope)
    out_specs = [pl.BlockSpec((tm, tn), lambda i, j: (i, j))]
    out_shape = [jax.ShapeDtypeStruct((S, N), out_dtype)]
    if write_h:
        out_specs.append(pl.BlockSpec((tm, K), lambda i, j: (i, 0)))
        out_shape.append(jax.ShapeDtypeStruct((S, K), BF))
    res = pl.pallas_call(
        body, name=name, grid=(S // tm, N // tn), in_specs=in_specs, out_specs=out_specs, out_shape=out_shape,
        scratch_shapes=[pltpu.VMEM((tm, K), BF)],
        compiler_params=_params(("parallel", "arbitrary")),
    )(*args)
    return res if write_h else res[0]


def norm_bwd(x, gain, dh, dres=None, *, tm=512, name="norm_bwd"):
    S, K = x.shape
    tm = _tile(S, tm)
    gain = gain.reshape(1, K)
    has_res = dres is not None

    def body(*refs):
        if has_res:
            x_ref, g_ref, dh_ref, dr_ref, dx_ref, dg_ref = refs
        else:
            x_ref, g_ref, dh_ref, dx_ref, dg_ref = refs

        @pl.when(pl.program_id(0) == 0)
        def _():
            dg_ref[...] = jnp.zeros_like(dg_ref)

        dx, dg = _rms_bwd(x_ref[...], g_ref[...], dh_ref[...].astype(F32))
        if has_res:
            dx = dx + dr_ref[...]
        dx_ref[...] = dx
        dg_ref[...] += jnp.broadcast_to(dg, dg_ref.shape)

    row = pl.BlockSpec((tm, K), lambda i: (i, 0))
    in_specs = [row, pl.BlockSpec((1, K), lambda i: (0, 0)), row] + ([row] if has_res else [])
    args = [x, gain, dh] + ([dres] if has_res else [])
    dx, dg = pl.pallas_call(
        body, name=name, grid=(S // tm,), in_specs=in_specs,
        out_specs=[row, pl.BlockSpec((8, K), lambda i: (0, 0))],
        out_shape=[jax.ShapeDtypeStruct((S, K), F32), jax.ShapeDtypeStruct((8, K), F32)],
        compiler_params=_params(("arbitrary",)),
    )(*args)
    return dx, dg[0]


def loss_head(x, gain, target, *, tm=512):
    S, K = x.shape
    tm = _tile(S, tm)
    gain = gain.reshape(1, K)

    def body(x_ref, g_ref, t_ref, dx_ref, dg_ref, ls_ref):
        @pl.when(pl.program_id(0) == 0)
        def _():
            dg_ref[...] = jnp.zeros_like(dg_ref)
            ls_ref[...] = jnp.zeros_like(ls_ref)

        xv, g = x_ref[...], g_ref[...]
        y, _ = _rms(xv, g)
        e = y - t_ref[...]
        part = jnp.sum(jnp.mean(e * e, axis=-1, keepdims=True), axis=0, keepdims=True)
        ls_ref[...] += jnp.broadcast_to(0.5 * part, ls_ref.shape)
        dx, dg = _rms_bwd(xv, g, e / K)
        dx_ref[...] = dx
        dg_ref[...] += jnp.broadcast_to(dg, dg_ref.shape)

    row = pl.BlockSpec((tm, K), lambda i: (i, 0))
    dx, dg, ls = pl.pallas_call(
        body, name="loss_head", grid=(S // tm,),
        in_specs=[row, pl.BlockSpec((1, K), lambda i: (0, 0)), row],
        out_specs=[row, pl.BlockSpec((8, K), lambda i: (0, 0)), pl.BlockSpec((8, LANES), lambda i: (0, 0))],
        out_shape=[jax.ShapeDtypeStruct((S, K), F32), jax.ShapeDtypeStruct((8, K), F32),
                   jax.ShapeDtypeStruct((8, LANES), F32)],
        compiler_params=_params(("arbitrary",)),
    )(x, gain, target)
    return ls[0, 0], dx, dg[0]


def _once(shape, index_map):
    return pl.BlockSpec(shape, index_map, pipeline_mode=pl.Buffered(1))


def ffn_fwd(x, gain, wg, wu, wd, *, tm=512):
    S, D = x.shape
    NB, _, Fs = wg.shape
    tm = _tile(S, tm)
    gain = gain.reshape(1, D)

    def body(x_ref, g_ref, wg_ref, wu_ref, wd_ref, xo_ref, h_ref, G_ref, U_ref, hs, acc):
        j = pl.program_id(1)

        @pl.when(j == 0)
        def _():
            h, _ = _rms(x_ref[...], g_ref[...])
            hb = h.astype(BF)
            hs[...] = hb
            h_ref[...] = hb
            acc[...] = jnp.zeros_like(acc)

        h = hs[...]
        g = _dot(h, wg_ref[...])
        u = _dot(h, wu_ref[...])
        G_ref[...] = g.astype(BF)
        U_ref[...] = u.astype(BF)
        a = (g * jax.nn.sigmoid(g) * u).astype(BF)
        acc[...] += _dot(a, wd_ref[...])

        @pl.when(j == NB - 1)
        def _():
            xo_ref[...] = x_ref[...] + 0.5 * acc[...]

    row = lambda i, j: (i, 0)
    blk = lambda i, j: (j, 0, 0)
    hid = pl.BlockSpec((None, tm, Fs), lambda i, j: (j, i, 0))
    return pl.pallas_call(
        body, name="ffn_fwd", grid=(S // tm, NB),
        in_specs=[_once((tm, D), row), pl.BlockSpec((1, D), lambda i, j: (0, 0)),
                  pl.BlockSpec((None, D, Fs), blk), pl.BlockSpec((None, D, Fs), blk), pl.BlockSpec((None, Fs, D), blk)],
        out_specs=[_once((tm, D), row), _once((tm, D), row), hid, hid],
        out_shape=[jax.ShapeDtypeStruct((S, D), F32), jax.ShapeDtypeStruct((S, D), BF),
                   jax.ShapeDtypeStruct((NB, S, Fs), BF), jax.ShapeDtypeStruct((NB, S, Fs), BF)],
        scratch_shapes=[pltpu.VMEM((tm, D), BF), pltpu.VMEM((tm, D), F32)],
        compiler_params=_params(("parallel", "arbitrary")),
    )(x, gain, wg, wu, wd)


def ffn_bwd_x(dy, x, gain, G, U, wg, wu, wd, *, tm=256):
    S, D = x.shape
    NB, _, Fs = wg.shape
    tm = _tile(S, tm)
    gain = gain.reshape(1, D)

    def body(dy_ref, x_ref, g_ref, G_ref, U_ref, wg_ref, wu_ref, wd_ref,
             dx_ref, do_ref, dG_ref, dU_ref, dg_ref, dob, acc):
        i, j = pl.program_id(0), pl.program_id(1)

        @pl.when(j == 0)
        def _():
            d = (0.5 * dy_ref[...]).astype(BF)
            dob[...] = d
            do_ref[...] = d
            acc[...] = jnp.zeros_like(acc)

        @pl.when((i == 0) & (j == 0))
        def _():
            dg_ref[...] = jnp.zeros_like(dg_ref)

        dA = _dot_nt(dob[...], wd_ref[...])
        g = G_ref[...].astype(F32)
        u = U_ref[...].astype(F32)
        sig = jax.nn.sigmoid(g)
        dG = (dA * u * (sig * (1.0 + g * (1.0 - sig)))).astype(BF)
        dU = (dA * (g * sig)).astype(BF)
        dG_ref[...] = dG
        dU_ref[...] = dU
        acc[...] += _dot_nt(dG, wg_ref[...]) + _dot_nt(dU, wu_ref[...])

        @pl.when(j == NB - 1)
        def _():
            dxn, dg = _rms_bwd(x_ref[...], g_ref[...], acc[...])
            dx_ref[...] = dy_ref[...] + dxn
            dg_ref[...] += jnp.broadcast_to(dg, dg_ref.shape)

    row = lambda i, j: (i, 0)
    blk = lambda i, j: (j, 0, 0)
    hid = pl.BlockSpec((None, tm, Fs), lambda i, j: (j, i, 0))
    dx, dout, dG, dU, dg = pl.pallas_call(
        body, name="ffn_bwd_x", grid=(S // tm, NB),
        in_specs=[_once((tm, D), row), _once((tm, D), row), pl.BlockSpec((1, D), lambda i, j: (0, 0)), hid, hid,
                  pl.BlockSpec((None, D, Fs), blk), pl.BlockSpec((None, D, Fs), blk), pl.BlockSpec((None, Fs, D), blk)],
        out_specs=[_once((tm, D), row), _once((tm, D), row), hid, hid, pl.BlockSpec((8, D), lambda i, j: (0, 0))],
        out_shape=[jax.ShapeDtypeStruct((S, D), F32), jax.ShapeDtypeStruct((S, D), BF),
                   jax.ShapeDtypeStruct((NB, S, Fs), BF), jax.ShapeDtypeStruct((NB, S, Fs), BF),
                   jax.ShapeDtypeStruct((8, D), F32)],
        scratch_shapes=[pltpu.VMEM((tm, D), BF), pltpu.VMEM((tm, D), F32)],
        compiler_params=_params(("arbitrary", "arbitrary")),
    )(dy, x, gain, G, U, wg, wu, wd)
    return dx, dout, dG, dU, dg[0]


def ffn_bwd_w(h, dout, G, U, dG, dU, *, tm=256):
    S, D = h.shape
    NB, _, Fs = G.shape
    tm = _tile(S, tm)
    ni = S // tm

    def body(h_ref, do_ref, G_ref, U_ref, dG_ref, dU_ref, wg_ref, wu_ref, wd_ref, ag, au, ad):
        i = pl.program_id(1)

        @pl.when(i == 0)
        def _():
            ag[...] = jnp.zeros_like(ag)
            au[...] = jnp.zeros_like(au)
            ad[...] = jnp.zeros_like(ad)

        h = h_ref[...]
        ag[...] += _dot_tn(h, dG_ref[...])
        au[...] += _dot_tn(h, dU_ref[...])
        g = G_ref[...].astype(F32)
        a = (g * jax.nn.sigmoid(g) * U_ref[...].astype(F32)).astype(BF)
        ad[...] += _dot_tn(a, do_ref[...])

        @pl.when(i == ni - 1)
        def _():
            wg_ref[...] = ag[...].astype(BF)
            wu_ref[...] = au[...].astype(BF)
            wd_ref[...] = ad[...].astype(BF)

    row = pl.BlockSpec((tm, D), lambda j, i: (i, 0))
    hid = pl.BlockSpec((None, tm, Fs), lambda j, i: (j, i, 0))
    blk = lambda j, i: (j, 0, 0)
    return pl.pallas_call(
        body, name="ffn_bwd_w", grid=(NB, ni),
        in_specs=[row, row, hid, hid, hid, hid],
        out_specs=[pl.BlockSpec((None, D, Fs), blk), pl.BlockSpec((None, D, Fs), blk), pl.BlockSpec((None, Fs, D), blk)],
        out_shape=[jax.ShapeDtypeStruct((NB, D, Fs), BF), jax.ShapeDtypeStruct((NB, D, Fs), BF),
                   jax.ShapeDtypeStruct((NB, Fs, D), BF)],
        scratch_shapes=[pltpu.VMEM((D, Fs), F32), pltpu.VMEM((D, Fs), F32), pltpu.VMEM((Fs, D), F32)],
        compiler_params=_params(("parallel", "arbitrary")),
    )(h, dout, G, U, dG, dU)


def dilated_bias(H, d):
    n = DIL_N
    slopes = 2.0 ** (-8.0 * (np.arange(H) + 1) / H)
    i = np.arange(n)[:, None]
    j = np.arange(2 * n)[None, :]
    steps = n + i - j
    band = (steps >= 0) & (steps <= n)
    first = band & (j >= n)
    bias = -slopes[:, None, None] * (d * steps).astype(np.float64)[None]
    out = np.stack([np.where(band[None], bias, NEG), np.where(first[None], bias, NEG)], axis=1)
    return jnp.asarray(out, dtype=F32)


def dil_fwd(qkv, bias, d, H):
    S = qkv.shape[0]
    Sd = S // d
    TQ = _tile(Sd, 512)
    nsub, nc = TQ // DIL_N, Sd // TQ
    scale = HEAD ** -0.5
    qv = qkv.reshape(Sd, d * 3 * H * HEAD)

    def body(q_ref, kc_ref, kp_ref, vc_ref, vp_ref, b_ref, o_ref, l_ref):
        c = pl.program_id(2)
        for i in range(nsub):
            sl = slice(i * DIL_N, (i + 1) * DIL_N)
            if i == 0:
                k2 = jnp.concatenate([kp_ref[...], kc_ref[0:DIL_N, :]], axis=0)
                v2 = jnp.concatenate([vp_ref[...], vc_ref[0:DIL_N, :]], axis=0)
                b = jnp.where(c == 0, b_ref[1], b_ref[0])
            else:
                k2 = kc_ref[(i - 1) * DIL_N:(i + 1) * DIL_N, :]
                v2 = vc_ref[(i - 1) * DIL_N:(i + 1) * DIL_N, :]
                b = b_ref[0]
            s = _dot_nt(q_ref[sl, :], k2) * scale + b
            m = jnp.max(s, axis=1, keepdims=True)
            p = jnp.exp(s - m)
            l = jnp.sum(p, axis=1, keepdims=True)
            o_ref[sl, :] = _dot(p.astype(BF), v2) / l
            l_ref[sl, :] = jnp.broadcast_to(m + jnp.log(l), (DIL_N, LANES))

    W3 = 3 * H
    chunk = lambda off: pl.BlockSpec((TQ, HEAD), lambda h, r, c: (c, r * W3 + off * H + h))
    halo = lambda off: pl.BlockSpec((DIL_N, HEAD), lambda h, r, c: (jnp.maximum(c * nsub - 1, 0), r * W3 + off * H + h))
    out = pl.BlockSpec((TQ, HEAD), lambda h, r, c: (c, r * H + h))
    o, l = pl.pallas_call(
        body, name=f"dil_fwd_d{d}", grid=(H, d, nc),
        in_specs=[chunk(0), chunk(1), halo(1), chunk(2), halo(2),
                  pl.BlockSpec((None, 2, DIL_N, 2 * DIL_N), lambda h, r, c: (h, 0, 0, 0))],
        out_specs=[out, out],
        out_shape=[jax.ShapeDtypeStruct((Sd, d * H * HEAD), F32)] * 2,
        compiler_params=_params(("parallel", "parallel", "arbitrary")),
    )(qv, qv, qv, qv, qv, bias)
    return o.reshape(S, H * HEAD), l.reshape(S, H * HEAD)


def dil_combine(os, ls, *, tm=256):
    S, W = os[0].shape
    tm = _tile(S, tm)

    def body(o1, o2, o3, l1, l2, l3, o_ref, L_ref):
        a, b, c = l1[...], l2[...], l3[...]
        m = jnp.maximum(jnp.maximum(a, b), c)
        ea, eb, ec = jnp.exp(a - m), jnp.exp(b - m), jnp.exp(c - m)
        z = ea + eb + ec
        o_ref[...] = ((ea * o1[...] + eb * o2[...] + ec * o3[...]) / z).astype(BF)
        L_ref[...] = m + jnp.log(z)

    row = pl.BlockSpec((tm, W), lambda i: (i, 0))
    return pl.pallas_call(
        body, name="dil_combine", grid=(S // tm,), in_specs=[row] * 6, out_specs=[row, row],
        out_shape=[jax.ShapeDtypeStruct((S, W), BF), jax.ShapeDtypeStruct((S, W), F32)],
        compiler_params=_params(("parallel",)),
    )(*os, *ls)


def head_delta(do, o, *, tm=512):
    S, W = o.shape
    tm = _tile(S, tm)

    def body(do_ref, o_ref, d_ref):
        s = jnp.sum(do_ref[...].astype(F32) * o_ref[...].astype(F32), axis=1, keepdims=True)
        d_ref[...] = jnp.broadcast_to(s, d_ref.shape)

    blk = pl.BlockSpec((tm, HEAD), lambda i, h: (i, h))
    return pl.pallas_call(
        body, name="head_delta", grid=(S // tm, W // HEAD), in_specs=[blk, blk], out_specs=blk,
        out_shape=jax.ShapeDtypeStruct((S, W), F32),
        compiler_params=_params(("parallel", "parallel")),
    )(do, o)


def dil_bwd(qkv, do, L, delta, bias, d, H, prev=None):
    S = qkv.shape[0]
    Sd = S // d
    TQ = _tile(Sd, 512)
    nsub, nc = TQ // DIL_N, Sd // TQ
    nblk = Sd // DIL_N
    scale = HEAD ** -0.5
    W3 = 3 * H
    qv = qkv.reshape(Sd, d * W3 * HEAD)
    dov, Lv, dlv = (t.reshape(Sd, d * H * HEAD) for t in (do, L, delta))
    has_prev = prev is not None

    def body(*refs):
        (qc, qn, kc, kp, vc, vp, doc, don, Lc, Ln, dlc, dln, b_ref) = refs[:13]
        refs = refs[13:]
        if has_prev:
            dqi, dki, dvi = refs[:3]
            refs = refs[3:]
        dqo, dko, dvo, dk_acc, dv_acc = refs
        c = pl.program_id(2)
        dk_acc[...] = jnp.zeros_like(dk_acc)
        dv_acc[...] = jnp.zeros_like(dv_acc)
        for i in range(nsub):
            sl = slice(i * DIL_N, (i + 1) * DIL_N)
            if i == 0:
                k2 = jnp.concatenate([kp[...], kc[0:DIL_N, :]], axis=0)
                v2 = jnp.concatenate([vp[...], vc[0:DIL_N, :]], axis=0)
                b = jnp.where(c == 0, b_ref[1], b_ref[0])
            else:
                k2 = kc[(i - 1) * DIL_N:(i + 1) * DIL_N, :]
                v2 = vc[(i - 1) * DIL_N:(i + 1) * DIL_N, :]
                b = b_ref[0]
            q, dov_ = qc[sl, :], doc[sl, :]
            Lq, dl = Lc[sl, :], dlc[sl, :]
            s = _dot_nt(q, k2) * scale + b
            p = jnp.exp(s - jnp.concatenate([Lq, Lq], axis=1))
            dp = _dot_nt(dov_, v2)
            ds = p * (dp - jnp.concatenate([dl, dl], axis=1))
            dsb, pb = ds.astype(BF), p.astype(BF)
            dq = _dot(dsb, k2) * scale
            dqo[sl, :] = dq + dqi[sl, :] if has_prev else dq
            dk2 = _dot_tn(dsb, q) * scale
            dv2 = _dot_tn(pb, dov_)
            if i >= 1:
                pv = slice((i - 1) * DIL_N, i * DIL_N)
                dk_acc[pv, :] += dk2[:DIL_N]
                dv_acc[pv, :] += dv2[:DIL_N]
            dk_acc[sl, :] += dk2[DIL_N:]
            dv_acc[sl, :] += dv2[DIL_N:]

        @pl.when(c < nc - 1)
        def _():
            last = slice((nsub - 1) * DIL_N, nsub * DIL_N)
            q, dov_ = qn[...], don[...]
            k1, v1 = kc[last, :], vc[last, :]
            s = _dot_nt(q, k1) * scale + b_ref[0][:, :DIL_N]
            p = jnp.exp(s - Ln[...])
            ds = p * (_dot_nt(dov_, v1) - dln[...])
            dk_acc[last, :] += _dot_tn(ds.astype(BF), q) * scale
            dv_acc[last, :] += _dot_tn(p.astype(BF), dov_)

        if has_prev:
            dko[...] = dk_acc[...] + dki[...]
            dvo[...] = dv_acc[...] + dvi[...]
        else:
            dko[...] = dk_acc[...]
            dvo[...] = dv_acc[...]

    nxt = lambda c: jnp.minimum((c + 1) * nsub, nblk - 1)
    prv = lambda c: jnp.maximum(c * nsub - 1, 0)
    chunk3 = lambda off: pl.BlockSpec((TQ, HEAD), lambda h, r, c: (c, r * W3 + off * H + h))
    halo3 = lambda off, f: pl.BlockSpec((DIL_N, HEAD), lambda h, r, c: (f(c), r * W3 + off * H + h))
    chunk1 = pl.BlockSpec((TQ, HEAD), lambda h, r, c: (c, r * H + h))
    next1 = pl.BlockSpec((DIL_N, HEAD), lambda h, r, c: (nxt(c), r * H + h))
    in_specs = [chunk3(0), halo3(0, nxt), chunk3(1), halo3(1, prv), chunk3(2), halo3(2, prv),
                chunk1, next1, chunk1, next1, chunk1, next1,
                pl.BlockSpec((None, 2, DIL_N, 2 * DIL_N), lambda h, r, c: (h, 0, 0, 0))]
    args = [qv] * 6 + [dov, dov, Lv, Lv, dlv, dlv, bias]
    if has_prev:
        in_specs += [chunk1] * 3
        args += [t.reshape(Sd, d * H * HEAD) for t in prev]
    outs = pl.pallas_call(
        body, name=f"dil_bwd_d{d}", grid=(H, d, nc), in_specs=in_specs, out_specs=[chunk1] * 3,
        out_shape=[jax.ShapeDtypeStruct((Sd, d * H * HEAD), F32)] * 3,
        scratch_shapes=[pltpu.VMEM((TQ, HEAD), F32), pltpu.VMEM((TQ, HEAD), F32)],
        compiler_params=_params(("parallel", "parallel", "arbitrary")),
    )(*args)
    return tuple(t.reshape(S, H * HEAD) for t in outs)


def _causal_mask(s, qi, ki, tq, tk):
    qpos = qi * tq + lax.broadcasted_iota(jnp.int32, s.shape, 0)
    kpos = ki * tk + lax.broadcasted_iota(jnp.int32, s.shape, 1)
    return jnp.where(kpos <= qpos, s, NEG)


def mla_fwd(qn, qr, kv, kr, H, *, tq=1024, tk=512):
    S = qn.shape[0]
    tq, tk = _tile(S, tq), _tile(S, tk)
    nk = S // tk
    scale = (HEAD + 2 * ROPE_HALF) ** -0.5

    def body(qn_ref, qr_ref, kn_ref, kr_ref, v_ref, o_ref, L_ref, m_s, l_s, acc):
        qi, ki = pl.program_id(1), pl.program_id(2)

        @pl.when(ki == 0)
        def _():
            m_s[...] = jnp.full_like(m_s, NEG)
            l_s[...] = jnp.zeros_like(l_s)
            acc[...] = jnp.zeros_like(acc)

        def step(masked):
            q = jnp.concatenate([qn_ref[...], qr_ref[...]], axis=1)
            k = jnp.concatenate([kn_ref[...], kr_ref[...]], axis=1)
            s = _dot_nt(q, k) * scale
            if masked:
                s = _causal_mask(s, qi, ki, tq, tk)
            m_prev = m_s[...]
            m_new = jnp.maximum(m_prev, jnp.max(s, axis=1, keepdims=True))
            alpha = jnp.exp(m_prev - m_new)
            p = jnp.exp(s - m_new)
            l_s[...] = alpha * l_s[...] + jnp.sum(p, axis=1, keepdims=True)
            acc[...] = alpha * acc[...] + _dot(p.astype(BF), v_ref[...])
            m_s[...] = m_new

        full = ki * tk + tk - 1 <= qi * tq
        live = ki * tk <= qi * tq + tq - 1
        pl.when(full)(lambda: step(False))
        pl.when(live & jnp.logical_not(full))(lambda: step(True))

        @pl.when(ki == nk - 1)
        def _():
            o_ref[...] = (acc[...] / l_s[...]).astype(BF)
            L_ref[...] = jnp.broadcast_to(m_s[...] + jnp.log(l_s[...]), L_ref.shape)

    kcl = lambda qi, ki: jnp.minimum(ki, (qi * tq + tq - 1) // tk)
    qs = pl.BlockSpec((tq, HEAD), lambda h, qi, ki: (qi, h))
    return pl.pallas_call(
        body, name="mla_fwd", grid=(H, S // tq, nk),
        in_specs=[qs, qs, pl.BlockSpec((tk, HEAD), lambda h, qi, ki: (kcl(qi, ki), h)),
                  pl.BlockSpec((tk, LANES), lambda h, qi, ki: (kcl(qi, ki), 0)),
                  pl.BlockSpec((tk, HEAD), lambda h, qi, ki: (kcl(qi, ki), H + h))],
        out_specs=[qs, qs],
        out_shape=[jax.ShapeDtypeStruct((S, H * HEAD), BF), jax.ShapeDtypeStruct((S, H * HEAD), F32)],
        scratch_shapes=[pltpu.VMEM((tq, 1), F32), pltpu.VMEM((tq, 1), F32), pltpu.VMEM((tq, HEAD), F32)],
        compiler_params=_params(("parallel", "parallel", "arbitrary")),
    )(qn, qr, kv, kr, kv)


def mla_bwd_q(qn, qr, kv, kr, do, L, delta, rope, H, *, tq=1024, tk=512):
    S = qn.shape[0]
    tq, tk = _tile(S, tq), _tile(S, tk)
    nk = S // tk
    scale = (HEAD + 2 * ROPE_HALF) ** -0.5

    def body(qn_ref, qr_ref, kn_ref, kr_ref, v_ref, do_ref, L_ref, dl_ref, cos_ref, sin_ref, dqn_ref, dqr_ref, acc):
        qi, ki = pl.program_id(1), pl.program_id(2)

        @pl.when(ki == 0)
        def _():
            acc[...] = jnp.zeros_like(acc)

        def step(masked):
            q = jnp.concatenate([qn_ref[...], qr_ref[...]], axis=1)
            k = jnp.concatenate([kn_ref[...], kr_ref[...]], axis=1)
            s = _dot_nt(q, k) * scale
            if masked:
                s = _causal_mask(s, qi, ki, tq, tk)
            p = jnp.exp(s - L_ref[:, 0:1])
            dp = _dot_nt(do_ref[...], v_ref[...])
            ds = (p * (dp - dl_ref[:, 0:1])).astype(BF)
            acc[...] += _dot(ds, k)

        full = ki * tk + tk - 1 <= qi * tq
        live = ki * tk <= qi * tq + tq - 1
        pl.when(full)(lambda: step(False))
        pl.when(live & jnp.logical_not(full))(lambda: step(True))

        @pl.when(ki == nk - 1)
        def _():
            dq = acc[...] * scale
            dqn_ref[...] = dq[:, :HEAD].astype(BF)
            dqr_ref[...] = _rope_bwd(dq[:, HEAD:], cos_ref[...], sin_ref[...]).astype(BF)

    kcl = lambda qi, ki: jnp.minimum(ki, (qi * tq + tq - 1) // tk)
    qs = pl.BlockSpec((tq, HEAD), lambda h, qi, ki: (qi, h))
    tab = pl.BlockSpec((tq, LANES), lambda h, qi, ki: (qi, 0))
    return pl.pallas_call(
        body, name="mla_bwd_q", grid=(H, S // tq, nk),
        in_specs=[qs, qs, pl.BlockSpec((tk, HEAD), lambda h, qi, ki: (kcl(qi, ki), h)),
                  pl.BlockSpec((tk, LANES), lambda h, qi, ki: (kcl(qi, ki), 0)),
                  pl.BlockSpec((tk, HEAD), lambda h, qi, ki: (kcl(qi, ki), H + h)),
                  qs, qs, qs, tab, tab],
        out_specs=[qs, qs],
        out_shape=[jax.ShapeDtypeStruct((S, H * HEAD), BF)] * 2,
        scratch_shapes=[pltpu.VMEM((tq, 2 * HEAD), F32)],
        compiler_params=_params(("parallel", "parallel", "arbitrary")),
    )(qn, qr, kv, kr, kv, do, L, delta, *rope)


def mla_bwd_kv(qn, qr, kv, kr, do, L, delta, rope, H, prev=None, *, tq=512, tk=1024):
    S = qn.shape[0]
    tq, tk = _tile(S, tq), _tile(S, tk)
    nq = S // tq
    scale = (HEAD + 2 * ROPE_HALF) ** -0.5
    has_prev = prev is not None

    def body(*refs):
        (qn_ref, qr_ref, kn_ref, kr_ref, v_ref, do_ref, L_ref, dl_ref, cos_ref, sin_ref) = refs[:10]
        refs = refs[10:]
        if has_prev:
            pk_ref, pv_ref, pr_ref = refs[:3]
            refs = refs[3:]
        dk_ref, dv_ref, dr_ref, dk_acc, dv_acc, dr_acc = refs
        ki, h, qi = pl.program_id(0), pl.program_id(1), pl.program_id(2)

        @pl.when(qi == 0)
        def _():
            dk_acc[...] = jnp.zeros_like(dk_acc)
            dv_acc[...] = jnp.zeros_like(dv_acc)

        @pl.when((qi == 0) & (h == 0))
        def _():
            dr_acc[...] = jnp.zeros_like(dr_acc)

        def step(masked):
            q = jnp.concatenate([qn_ref[...], qr_ref[...]], axis=1)
            k = jnp.concatenate([kn_ref[...], kr_ref[...]], axis=1)
            s = _dot_nt(q, k) * scale
            if masked:
                s = _causal_mask(s, qi, ki, tq, tk)
            p = jnp.exp(s - L_ref[:, 0:1])
            dov = do_ref[...]
            dv_acc[...] += _dot_tn(p.astype(BF), dov)
            dp = _dot_nt(dov, v_ref[...])
            ds = (p * (dp - dl_ref[:, 0:1])).astype(BF)
            dk_acc[...] += _dot_tn(ds, q)

        full = ki * tk + tk - 1 <= qi * tq
        live = ki * tk <= qi * tq + tq - 1
        pl.when(full)(lambda: step(False))
        pl.when(live & jnp.logical_not(full))(lambda: step(True))

        @pl.when(qi == nq - 1)
        def _():
            dk = dk_acc[...] * scale
            dkn, dv = dk[:, :HEAD], dv_acc[...]
            if has_prev:
                dkn, dv = dkn + pk_ref[...], dv + pv_ref[...]
            dk_ref[...] = dkn
            dv_ref[...] = dv
            dr_acc[...] += dk[:, HEAD:]

        @pl.when((qi == nq - 1) & (h == H - 1))
        def _():
            dr = _rope_bwd(dr_acc[...], cos_ref[...], sin_ref[...])
            dr_ref[...] = dr + pr_ref[...] if has_prev else dr

    qcl = lambda ki, qi: jnp.maximum(qi, (ki * tk) // tq)
    qs = pl.BlockSpec((tq, HEAD), lambda ki, h, qi: (qcl(ki, qi), h))
    kn = pl.BlockSpec((tk, HEAD), lambda ki, h, qi: (ki, h))
    vs = pl.BlockSpec((tk, HEAD), lambda ki, h, qi: (ki, H + h))
    k1 = pl.BlockSpec((tk, LANES), lambda ki, h, qi: (ki, 0))
    in_specs = [qs, qs, kn, k1, vs, qs, qs, qs, k1, k1]
    args = [qn, qr, kv, kr, kv, do, L, delta, *rope]
    if has_prev:
        in_specs += [kn, vs, k1]
        args += [prev[0], prev[0], prev[1]]
    dkn, dv, dr = pl.pallas_call(
        body, name="mla_bwd_kv", grid=(S // tk, H, nq), in_specs=in_specs,
        out_specs=[kn, kn, k1],
        out_shape=[jax.ShapeDtypeStruct((S, H * HEAD), F32), jax.ShapeDtypeStruct((S, H * HEAD), F32),
                   jax.ShapeDtypeStruct((S, LANES), F32)],
        scratch_shapes=[pltpu.VMEM((tk, 2 * HEAD), F32), pltpu.VMEM((tk, HEAD), F32), pltpu.VMEM((tk, LANES), F32)],
        compiler_params=_params(("parallel", "arbitrary", "arbitrary")),
    )(*args)
    return dkn, dv, dr


def adamw(recv, w, m, v, row_off=0, *, name="adamw"):
    R, C = w.shape
    tr = 16
    while tr * 2 * C * 44 <= 6 * 1024 * 1024 and R % (tr * 2) == 0 and row_off % (tr * 2) == 0:
        tr *= 2
    tr = min(tr, R)
    assert R % tr == 0 and row_off % tr == 0
    off = row_off // tr

    def body(r_ref, w_ref, m_ref, v_ref, g_ref, d_ref, mo_ref, vo_ref):
        g = r_ref[0].astype(F32)
        for s in range(1, N_DEV):
            g = g + r_ref[s].astype(F32)
        m2 = ADAM_B1 * m_ref[...] + (1.0 - ADAM_B1) * g
        v2 = ADAM_B2 * v_ref[...] + (1.0 - ADAM_B2) * (g * g)
        m_hat = m2 / (1.0 - ADAM_B1 ** ADAM_STEP)
        v_hat = v2 / (1.0 - ADAM_B2 ** ADAM_STEP)
        g_ref[...] = g
        d_ref[...] = -ADAM_LR * (m_hat / (jnp.sqrt(v_hat) + ADAM_EPS) + ADAM_WD * w_ref[...])
        mo_ref[...] = m2
        vo_ref[...] = v2

    row = pl.BlockSpec((tr, C), lambda i: (i, 0))
    return pl.pallas_call(
        body, name=name, grid=(R // tr,),
        in_specs=[pl.BlockSpec((N_DEV, tr, C), lambda i: (0, i + off, 0)), row, row, row],
        out_specs=[row] * 4, out_shape=[jax.ShapeDtypeStruct((R, C), F32)] * 4,
        compiler_params=_params(("parallel",)),
    )(recv, w, m, v)


def _pack(ts, width, dtype, row_mult=16):
    flat = jnp.concatenate([t.astype(dtype).reshape(-1) for t in ts])
    n = flat.shape[0]
    rows = -(-n // (width * row_mult)) * row_mult
    return jnp.pad(flat, (0, rows * width - n)).reshape(rows, width)


def _unpack(buf, shapes):
    lead = buf.shape[:-2]
    flat = buf.reshape(lead + (-1,))
    out, off = [], 0
    for s in shapes:
        n = int(np.prod(s))
        out.append(flat[..., off:off + n].reshape(lead + tuple(s)))
        off += n
    return out


def _rope_pad(r):
    z = jnp.zeros(r.shape[:-1] + (ROPE_HALF,), r.dtype)
    return jnp.concatenate([r[..., :ROPE_HALF], z, r[..., ROPE_HALF:], z], axis=-1)


def _rope_unpad(r):
    return jnp.concatenate([r[..., :ROPE_HALF], r[..., 2 * ROPE_HALF:3 * ROPE_HALF]], axis=-1)


def _step(P):
    x0 = P['x'][0]
    target = P['loss_target'][0]
    S, D = x0.shape
    NL = P['ffn_norm1'].shape[0]
    NA = P['a_wqkv'].shape[0]
    Fs = P['ffn1_wg'].shape[2]
    H = D // HEAD
    KV = P['b_wdkv'].shape[1]
    QL = P['b_wdq'].shape[2]
    HW = H * HEAD

    wa_loc = jnp.stack([P[n] for n in FFN_COL]).astype(BF).reshape(len(FFN_COL) * NL, D, Fs)
    wb_loc = jnp.stack([P[n] for n in FFN_ROW]).astype(BF).reshape(len(FFN_ROW) * NL, Fs, D)
    mix_shapes = [P[n].shape for n in MIX_W]
    wc_loc = _pack([P[n] for n in MIX_W], D, BF)
    WA, WB, WC = all_gather([wa_loc, wb_loc, wc_loc])
    g = dict(zip(MIX_W, _unpack(WC, mix_shapes)))
    cols = lambda t: jnp.moveaxis(t, 0, -2).reshape(t.shape[1:-1] + (N_DEV * t.shape[-1],))
    rows = lambda t, lead: jnp.moveaxis(t, 0, lead).reshape(t.shape[1:1 + lead] + (N_DEV * t.shape[1 + lead],) + t.shape[2 + lead:])
    a_wqkv = cols(g['a_wqkv'])
    a_wo = rows(g['a_wo'], 1)
    b_wo = rows(g['b_wo'], 1)
    wdkv = rows(g['b_wdkv'], 0)
    wkr = _rope_pad(rows(g['b_wkr'], 0))
    wuk = rows(g['b_wuk'], 0).reshape(KV, HW)
    wuv = rows(g['b_wuv'], 0).reshape(KV, HW)
    wkv = jnp.concatenate([wuk, wuv], axis=1)
    wdq = rows(g['b_wdq'], 1)
    wuq = rows(g['b_wuq'], 1)
    wuq_n = wuq[..., :HEAD].reshape(-1, QL, HW)
    wuq_r = _rope_pad(wuq[..., HEAD:]).reshape(-1, QL, HW)

    def ffn_w(kind_col, kind_row, l):
        return WA[:, (2 * kind_col) * NL + l], WA[:, (2 * kind_col + 1) * NL + l], WB[:, kind_row * NL + l]

    inv = 1.0 / (ROPE_THETA ** (jnp.arange(0, 2 * ROPE_HALF, 2, dtype=F32) / (2 * ROPE_HALF)))
    ang = jnp.arange(S, dtype=F32)[:, None] * inv[None, :]
    z = jnp.zeros((S, ROPE_HALF), F32)
    rope = (jnp.concatenate([jnp.cos(ang), z, jnp.cos(ang), z], axis=1),
            jnp.concatenate([-jnp.sin(ang), z, jnp.sin(ang), z], axis=1))
    biases = [dilated_bias(H, d) for d in BRANCH_DILATIONS]

    saved = []
    kvs = None
    x = x0
    for l in range(NL):
        st = {'x_in': x}
        if l == NA:
            ckv_pre, hkv = norm_mm(x, P['kv_norm'], wdkv, out_dtype=F32, name="kv_down")
            kr = norm_mm(x, P['kv_norm'], wkr, out_dtype=BF, tn=LANES, rope=rope, write_h=False, name="kv_rope")
            kvm, ckv = norm_mm(ckv_pre, P['b_ckv_norm'], wkv, out_dtype=BF, name="kv_up")
            kvs = dict(x=x, ckv_pre=ckv_pre, hkv=hkv, kr=kr, kv=kvm, ckv=ckv)
        w1 = ffn_w(0, 0, l)
        xa, h1, G1, U1 = ffn_fwd(x, P['ffn_norm1'][l], *w1)
        st.update(h1=h1, G1=G1, U1=U1, xa=xa)
        if l < NA:
            qkv, hm = norm_mm(xa, P['mix_norm'][l], a_wqkv[l], out_dtype=BF, name="a_qkv")
            os, ls = zip(*[dil_fwd(qkv, b, d, H) for b, d in zip(biases, BRANCH_DILATIONS)])
            o, Lj = dil_combine(os, ls)
            xb = mm(o, a_wo[l], add=xa, name="mix_out")
            st.update(qkv=qkv, hm=hm, o=o, L=Lj)
        else:
            jb = l - NA
            cq_pre, hm = norm_mm(xa, P['mix_norm'][l], wdq[jb], out_dtype=F32, name="q_down")
            qn, cq = norm_mm(cq_pre, P['b_cq_norm'][jb], wuq_n[jb], out_dtype=BF, name="q_up")
            qr = norm_mm(cq_pre, P['b_cq_norm'][jb], wuq_r[jb], out_dtype=BF, tn=LANES, rope=rope, write_h=False,
                         name="q_rope")
            o, Lj = mla_fwd(qn, qr, kvs['kv'], kvs['kr'], H)
            xb = mm(o, b_wo[jb], add=xa, name="mix_out")
            st.update(cq_pre=cq_pre, hm=hm, qn=qn, qr=qr, cq=cq, o=o, L=Lj)
        w2 = ffn_w(1, 1, l)
        x, h2, G2, U2 = ffn_fwd(xb, P['ffn_norm2'][l], *w2)
        st.update(xb=xb, h2=h2, G2=G2, U2=U2)
        saved.append(st)

    loss_part, dx, dg_final = loss_head(x, P['final_norm'], target)

    gw = {}
    gv = {'final_norm': dg_final}
    dkv_acc = None
    for n in ('ffn_norm1', 'mix_norm', 'ffn_norm2'):
        gv[n] = [None] * NL
    gv['b_cq_norm'] = [None] * (NL - NA)
    ffn_g = {n: [None] * NL for n in FFN_COL + FFN_ROW}
    a_g = {'a_wqkv': [None] * NA, 'a_wo': [None] * NA}
    b_g = {n: [None] * (NL - NA) for n in ('b_wdq', 'b_wuq', 'b_wo')}

    for l in reversed(range(NL)):
        st = saved[l]
        w2 = ffn_w(1, 1, l)
        dxb, dout, dG, dU, gv['ffn_norm2'][l] = ffn_bwd_x(dx, st['xb'], P['ffn_norm2'][l], st['G2'], st['U2'], *w2)
        ffn_g['ffn2_wg'][l], ffn_g['ffn2_wu'][l], ffn_g['ffn2_wd'][l] = ffn_bwd_w(st['h2'], dout, st['G2'], st['U2'], dG, dU)
        xa = st['xa']
        if l < NA:
            do = mm(dxb, a_wo[l], tb=True, out_dtype=BF, name="mix_out_dx")
            a_g['a_wo'][l] = mm(st['o'], dxb, ta=True, out_dtype=BF, tm=1024, tn=1024, name="mix_out_dw")
            delta = head_delta(do, st['o'])
            acc = None
            for b, d in zip(biases, BRANCH_DILATIONS):
                acc = dil_bwd(st['qkv'], do, st['L'], delta, b, d, H, prev=acc)
            dh = None
            dws = []
            for part, t in enumerate(acc):
                wpart = a_wqkv[l][:, part * HW:(part + 1) * HW]
                dh = mm(t, wpart, tb=True, add=dh, name="a_qkv_dx")
                dws.append(mm(st['hm'], t, ta=True, out_dtype=BF, tm=1024, tn=1024, name="a_qkv_dw"))
            a_g['a_wqkv'][l] = jnp.concatenate(dws, axis=1)
            dxa, gv['mix_norm'][l] = norm_bwd(xa, P['mix_norm'][l], dh, dxb, name="mix_norm_bwd")
        else:
            jb = l - NA
            do = mm(dxb, b_wo[jb], tb=True, out_dtype=BF, name="mix_out_dx")
            b_g['b_wo'][jb] = mm(st['o'], dxb, ta=True, out_dtype=BF, tm=1024, tn=1024, name="mix_out_dw")
            delta = head_delta(do, st['o'])
            dqn, dqr = mla_bwd_q(st['qn'], st['qr'], kvs['kv'], kvs['kr'], do, st['L'], delta, rope, H)
            dkn, dv, dr = mla_bwd_kv(st['qn'], st['qr'], kvs['kv'], kvs['kr'], do, st['L'], delta, rope, H, prev=dkv_acc)
            dkv_acc = (jnp.concatenate([dkn, dv], axis=1), dr)
            dcq = mm(dqn, wuq_n[jb], tb=True, name="q_up_dx")
            dcq = mm(dqr, wuq_r[jb], tb=True, add=dcq, name="q_up_dx_add")
            dwn = mm(st['cq'], dqn, ta=True, out_dtype=F32, name="q_up_dw")
            dwr = mm(st['cq'], dqr, ta=True, out_dtype=F32, name="q_up_dw")
            b_g['b_wuq'][jb] = jnp.concatenate(
                [dwn.reshape(QL, H, HEAD), _rope_unpad(dwr.reshape(QL, H, HEAD))], axis=-1)
            dcq_pre, gv['b_cq_norm'][jb] = norm_bwd(st['cq_pre'], P['b_cq_norm'][jb], dcq, name="cq_norm_bwd")
            dh = mm(dcq_pre, wdq[jb], tb=True, name="q_down_dx")
            b_g['b_wdq'][jb] = mm(st['hm'], dcq_pre, ta=True, out_dtype=F32, tm=1024, name="q_down_dw")
            dxa, gv['mix_norm'][l] = norm_bwd(xa, P['mix_norm'][l], dh, dxb, name="mix_norm_bwd")
        w1 = ffn_w(0, 0, l)
        dx, dout, dG, dU, gv['ffn_norm1'][l] = ffn_bwd_x(dxa, st['x_in'], P['ffn_norm1'][l], st['G1'], st['U1'], *w1)
        ffn_g['ffn1_wg'][l], ffn_g['ffn1_wu'][l], ffn_g['ffn1_wd'][l] = ffn_bwd_w(st['h1'], dout, st['G1'], st['U1'], dG, dU)
        if l == NA:
            dkvm, dr = dkv_acc
            dckv = mm(dkvm, wkv, tb=True, name="kv_up_dx")
            dwkv = mm(kvs['ckv'], dkvm, ta=True, out_dtype=F32, name="kv_up_dw")
            gw['b_wuk'] = dwkv[:, :HW].reshape(KV, H, HEAD)
            gw['b_wuv'] = dwkv[:, HW:].reshape(KV, H, HEAD)
            dckv_pre, gv['b_ckv_norm'] = norm_bwd(kvs['ckv_pre'], P['b_ckv_norm'], dckv, name="ckv_norm_bwd")
            dh = mm(dckv_pre, wdkv, tb=True, name="kv_down_dx")
            dh = mm(dr, wkr, tb=True, add=dh, name="kv_rope_dx")
            gw['b_wdkv'] = mm(kvs['hkv'], dckv_pre, ta=True, out_dtype=F32, tm=1024, name="kv_down_dw")
            gw['b_wkr'] = _rope_unpad(mm(kvs['hkv'], dr, ta=True, out_dtype=F32, tm=1024, name="kv_rope_dw"))
            dx, gv['kv_norm'] = norm_bwd(kvs['x'], P['kv_norm'], dh, dx, name="kv_norm_bwd")

    for n in a_g:
        gw[n] = jnp.stack(a_g[n])
    for n in b_g:
        gw[n] = jnp.stack(b_g[n])
    for n in ('ffn_norm1', 'mix_norm', 'ffn_norm2', 'b_cq_norm'):
        gv[n] = jnp.stack(gv[n])

    ga = jnp.stack([t for n in FFN_COL for t in ffn_g[n]], axis=1)
    gb = jnp.stack([t for n in FFN_ROW for t in ffn_g[n]], axis=1)
    split_cols = lambda t: jnp.moveaxis(t.reshape(t.shape[:-1] + (N_DEV, t.shape[-1] // N_DEV)), -2, 0)
    split_rows = lambda t, lead: jnp.moveaxis(
        t.reshape(t.shape[:lead] + (N_DEV, t.shape[lead] // N_DEV) + t.shape[lead + 1:]), lead, 0)
    shards = {
        'a_wqkv': split_cols(gw['a_wqkv']), 'a_wo': split_rows(gw['a_wo'], 1), 'b_wdkv': split_rows(gw['b_wdkv'], 0),
        'b_wkr': split_rows(gw['b_wkr'], 0), 'b_wuk': split_rows(gw['b_wuk'], 0), 'b_wuv': split_rows(gw['b_wuv'], 0),
        'b_wdq': split_rows(gw['b_wdq'], 1), 'b_wuq': split_rows(gw['b_wuq'], 1), 'b_wo': split_rows(gw['b_wo'], 1),
    }
    gc = jnp.stack([_pack([shards[n][p] for n in MIX_W], D, BF) for p in range(N_DEV)])
    vec_parts = [gv[n] for n in VEC_W] + [jnp.full((LANES,), loss_part, F32)]
    gvec = _pack(vec_parts, LANES, F32, row_mult=8)
    ra, rb, rc, rv = exchange([ga, gb, gc, gvec], [False, False, False, True])

    res = {}
    ra2 = ra.reshape(N_DEV, len(FFN_COL) * NL * D, Fs)
    for k, n in enumerate(FFN_COL):
        flat = lambda t: t.reshape(NL * D, Fs)
        out = adamw(ra2, flat(P[n]), flat(P['m_' + n]), flat(P['v_' + n]), k * NL * D, name="adamw_col")
        res[n] = [t.reshape(NL, D, Fs) for t in out]
    rb2 = rb.reshape(N_DEV, len(FFN_ROW) * NL * Fs, D)
    for k, n in enumerate(FFN_ROW):
        flat = lambda t: t.reshape(NL * Fs, D)
        out = adamw(rb2, flat(P[n]), flat(P['m_' + n]), flat(P['v_' + n]), k * NL * Fs, name="adamw_row")
        res[n] = [t.reshape(NL, Fs, D) for t in out]
    pk = lambda pre: _pack([P[pre + n] for n in MIX_W], D, F32)
    out = adamw(rc, pk(''), pk('m_'), pk('v_'), name="adamw_mix")
    for n, parts in zip(MIX_W, zip(*[_unpack(t, mix_shapes) for t in out])):
        res[n] = list(parts)
    vec_shapes = [P[n].shape for n in VEC_W] + [(LANES,)]
    ones = jnp.ones((LANES,), F32)
    pv = lambda pre: _pack([P[pre + n] for n in VEC_W] + [ones], LANES, F32, row_mult=8)
    out = adamw(rv, pv(''), pv('m_'), pv('v_'), name="adamw_vec")
    unp = [_unpack(t, vec_shapes) for t in out]
    for idx, n in enumerate(VEC_W):
        res[n] = [u[idx] for u in unp]
    loss = unp[0][-1][0]

    outs = [loss, dx[None]]
    for field in range(4):
        outs += [res[n][field] for n in W_NAMES]
    return tuple(outs)


def kernel(x, ffn_norm1, ffn1_wg, ffn1_wu, ffn1_wd, mix_norm, ffn_norm2, ffn2_wg, ffn2_wu, ffn2_wd, a_wqkv, a_wo, kv_norm, b_wdkv, b_ckv_norm, b_wkr, b_wuk, b_wuv, b_wdq, b_cq_norm, b_wuq, b_wo, final_norm, loss_target, m_ffn_norm1, m_ffn1_wg, m_ffn1_wu, m_ffn1_wd, m_mix_norm, m_ffn_norm2, m_ffn2_wg, m_ffn2_wu, m_ffn2_wd, m_a_wqkv, m_a_wo, m_kv_norm, m_b_wdkv, m_b_ckv_norm, m_b_wkr, m_b_wuk, m_b_wuv, m_b_wdq, m_b_cq_norm, m_b_wuq, m_b_wo, m_final_norm, v_ffn_norm1, v_ffn1_wg, v_ffn1_wu, v_ffn1_wd, v_mix_norm, v_ffn_norm2, v_ffn2_wg, v_ffn2_wu, v_ffn2_wd, v_a_wqkv, v_a_wo, v_kv_norm, v_b_wdkv, v_b_ckv_norm, v_b_wkr, v_b_wuk, v_b_wuv, v_b_wdq, v_b_cq_norm, v_b_wuq, v_b_wo, v_final_norm):
    return _step(dict(locals()))
```

```python
import functools
import math

import numpy as np
import jax
import jax.numpy as jnp
from jax import lax
from jax.experimental import pallas as pl
from jax.experimental.pallas import tpu as pltpu

BF = jnp.bfloat16
F32 = jnp.float32
MESH = pl.DeviceIdType.MESH
ANY = pl.BlockSpec(memory_space=pl.ANY)

N_DEV = 8
LANES = 128
HEAD = 128
ROPE_HALF = 32
DIL_N = 128
BRANCH_DILATIONS = (1, 4, 16)
ROPE_THETA = 10000.0
EPS = 1e-6
NEG = -1e30
VMEM_LIMIT = 58 * 1024 * 1024

ADAM_LR, ADAM_B1, ADAM_B2, ADAM_EPS, ADAM_WD, ADAM_STEP = 0.001, 0.9, 0.999, 1e-08, 0.01, 10

W_NAMES = ['ffn_norm1', 'ffn1_wg', 'ffn1_wu', 'ffn1_wd', 'mix_norm', 'ffn_norm2', 'ffn2_wg', 'ffn2_wu', 'ffn2_wd',
           'a_wqkv', 'a_wo', 'kv_norm', 'b_wdkv', 'b_ckv_norm', 'b_wkr', 'b_wuk', 'b_wuv', 'b_wdq', 'b_cq_norm',
           'b_wuq', 'b_wo', 'final_norm']
FFN_COL = ['ffn1_wg', 'ffn1_wu', 'ffn2_wg', 'ffn2_wu']
FFN_ROW = ['ffn1_wd', 'ffn2_wd']
MIX_W = ['a_wqkv', 'a_wo', 'b_wdkv', 'b_wkr', 'b_wuk', 'b_wuv', 'b_wdq', 'b_wuq', 'b_wo']
VEC_W = ['ffn_norm1', 'mix_norm', 'ffn_norm2', 'kv_norm', 'b_ckv_norm', 'b_cq_norm', 'final_norm']


def _tile(n, pref):
    t = min(n, pref)
    assert n % t == 0, (n, pref)
    return t


def _params(sem):
    return pltpu.CompilerParams(dimension_semantics=sem, vmem_limit_bytes=VMEM_LIMIT)


def _dot(a, b):
    return jnp.dot(a, b, preferred_element_type=F32)


def _dot_nt(a, b):
    return lax.dot_general(a, b, (((1,), (1,)), ((), ())), preferred_element_type=F32)


def _dot_tn(a, b):
    return lax.dot_general(a, b, (((0,), (0,)), ((), ())), preferred_element_type=F32)


def _rms(x, g):
    r = lax.rsqrt(jnp.mean(x * x, axis=-1, keepdims=True) + EPS)
    return x * r * g, r


def _rms_bwd(x, g, dh):
    r = lax.rsqrt(jnp.mean(x * x, axis=-1, keepdims=True) + EPS)
    xhat = x * r
    gd = dh * g
    dx = r * (gd - xhat * jnp.mean(gd * xhat, axis=-1, keepdims=True))
    return dx, jnp.sum(dh * xhat, axis=0, keepdims=True)


def _rope(t, cos, sin):
    return t * cos + pltpu.roll(t, 2 * ROPE_HALF, 1) * sin


def _rope_bwd(g, cos, sin):
    return g * cos + pltpu.roll(g * sin, 2 * ROPE_HALF, 1)


def _coords():
    return lax.axis_index("x"), lax.axis_index("y"), lax.axis_index("c")


def _lin(px, py, pc):
    return 4 * px + 2 * py + pc


def _gather_phases(ins, outs, ssem, rsem, lsem):
    n = len(ins)
    x, y, c = _coords()
    me, sibling = (x, y, c), (x, y, 1 - c)
    chips = [(1 - x, y), (x, 1 - y), (1 - x, 1 - y)]

    def copy(t, k, block, to, src=None):
        slot = outs[t].at[_lin(*block)]
        return pltpu.make_async_remote_copy(
            src_ref=slot if src is None else src, dst_ref=slot,
            send_sem=ssem.at[7 * t + k], recv_sem=rsem.at[7 * t + k],
            device_id=to, device_id_type=MESH)

    def mine(t):
        return pltpu.make_async_copy(ins[t], outs[t].at[_lin(*me)], lsem.at[t])

    def first(t):
        return [copy(t, 0, me, sibling, src=ins[t])] + [copy(t, 1 + j, me, (*chip, c), src=ins[t])
                                                       for j, chip in enumerate(chips)]

    def passed(t):
        return [copy(t, 4 + j, (*chip, c), sibling) for j, chip in enumerate(chips)]

    def start():
        for t in range(n):
            mine(t).start()
            for cp in first(t):
                cp.start()

    def middle():
        for t in range(n):
            fw = passed(t)
            for j, chip in enumerate(chips):
                copy(t, 1 + j, (*chip, c), me).wait_recv()
                fw[j].start()

    def finish():
        for t in range(n):
            copy(t, 0, sibling, me).wait_recv()
            for j, chip in enumerate(chips):
                copy(t, 4 + j, (*chip, 1 - c), me).wait_recv()
        for t in range(n):
            for cp in first(t) + passed(t):
                cp.wait_send()
            mine(t).wait()

    return start, middle, finish


def _exchange_phases(ins, outs, bcast, ssem, rsem, lsem):
    n = len(ins)
    x, y, c = _coords()
    me = _lin(x, y, c)
    peers = []
    for k in range(1, N_DEV):
        kx, ky, kc = (k >> 2) & 1, (k >> 1) & 1, k & 1
        peers.append((k, (1 - x if kx else x, 1 - y if ky else y, 1 - c if kc else c)))

    def own(t):
        return pltpu.make_async_copy(ins[t] if bcast[t] else ins[t].at[me], outs[t].at[me], lsem.at[t])

    def send(t, k, peer):
        return pltpu.make_async_remote_copy(
            src_ref=ins[t] if bcast[t] else ins[t].at[_lin(*peer)], dst_ref=outs[t].at[me],
            send_sem=ssem.at[7 * t + k - 1], recv_sem=rsem.at[7 * t + k - 1],
            device_id=peer, device_id_type=MESH)

    def arrival(t, k, peer):
        slot = outs[t].at[_lin(*peer)]
        return pltpu.make_async_remote_copy(
            src_ref=slot, dst_ref=slot, send_sem=ssem.at[7 * t + k - 1], recv_sem=rsem.at[7 * t + k - 1],
            device_id=peer, device_id_type=MESH)

    def start():
        for t in range(n):
            own(t).start()
            for k, peer in peers:
                send(t, k, peer).start()

    def finish():
        for t in range(n):
            for k, peer in peers:
                arrival(t, k, peer).wait_recv()
        for t in range(n):
            for k, peer in peers:
                send(t, k, peer).wait_send()
            own(t).wait()

    return start, finish


def _comm_sems(n):
    return [pltpu.SemaphoreType.DMA((7 * n,)), pltpu.SemaphoreType.DMA((7 * n,)), pltpu.SemaphoreType.DMA((n,))]


def _gathered(xs):
    return [jax.ShapeDtypeStruct((N_DEV,) + a.shape, a.dtype) for a in xs]


def all_gather(xs):
    n = len(xs)

    def body(*refs):
        start, middle, finish = _gather_phases(refs[:n], refs[n:2 * n], *refs[2 * n:])
        start()
        middle()
        finish()

    return pl.pallas_call(
        body, name="all_gather", out_shape=_gathered(xs),
        in_specs=[ANY] * n, out_specs=[ANY] * n, scratch_shapes=_comm_sems(n),
    )(*xs)


def exchange(xs, bcast):
    n = len(xs)

    def body(*refs):
        start, finish = _exchange_phases(refs[:n], refs[n:2 * n], bcast, *refs[2 * n:])
        start()
        finish()

    out_shape = [jax.ShapeDtypeStruct(((N_DEV,) + a.shape) if b else a.shape, a.dtype) for a, b in zip(xs, bcast)]
    return pl.pallas_call(
        body, name="grad_exchange", out_shape=out_shape,
        in_specs=[ANY] * n, out_specs=[ANY] * n, scratch_shapes=_comm_sems(n),
    )(*xs)


def mm(a, b, *, ta=False, tb=False, add=None, out_dtype=F32, tm=512, tn=512, tk=512, name="mm"):
    K, M = a.shape if ta else a.shape[::-1]
    N = b.shape[0] if tb else b.shape[1]
    assert (b.shape[1] if tb else b.shape[0]) == K and not (ta and tb)
    tm, tn, tk = _tile(M, tm), _tile(N, tn), _tile(K, tk)
    nk = K // tk
    has_add = add is not None

    def body(*refs):
        if has_add:
            a_ref, b_ref, add_ref, o_ref, acc = refs
        else:
            a_ref, b_ref, o_ref, acc = refs
        k = pl.program_id(2)

        @pl.when(k == 0)
        def _():
            acc[...] = jnp.zeros_like(acc)

        av = a_ref[...].astype(BF)
        bv = b_ref[...].astype(BF)
        if ta:
            acc[...] += _dot_tn(av, bv)
        elif tb:
            acc[...] += _dot_nt(av, bv)
        else:
            acc[...] += _dot(av, bv)

        @pl.when(k == nk - 1)
        def _():
            r = acc[...]
            if has_add:
                r = r + add_ref[...]
            o_ref[...] = r.astype(out_dtype)

    a_spec = pl.BlockSpec((tk, tm), lambda i, j, k: (k, i)) if ta else pl.BlockSpec((tm, tk), lambda i, j, k: (i, k))
    b_spec = pl.BlockSpec((tn, tk), lambda i, j, k: (j, k)) if tb else pl.BlockSpec((tk, tn), lambda i, j, k: (k, j))
    o_spec = pl.BlockSpec((tm, tn), lambda i, j, k: (i, j))
    in_specs = [a_spec, b_spec] + ([o_spec] if has_add else [])
    args = [a, b] + ([add] if has_add else [])
    return pl.pallas_call(
        body, name=name, grid=(M // tm, N // tn, nk),
        in_specs=in_specs, out_specs=o_spec,
        out_shape=jax.ShapeDtypeStruct((M, N), out_dtype),
        scratch_shapes=[pltpu.VMEM((tm, tn), F32)],
        compiler_params=_params(("parallel", "parallel", "arbitrary")),
    )(*args)


def norm_mm(x, gain, w, *, out_dtype, tn=512, rope=None, write_h=True, tm=512, name="norm_mm"):
    S, K = x.shape
    N = w.shape[1]
    tm, tn = _tile(S, tm), _tile(N, tn)
    if rope is not None:
        assert tn == LANES
    gain = gain.reshape(1, K)

    def body(*refs):
        refs = list(refs)
        x_ref, g_ref, w_ref = refs[:3]
        refs = refs[3:]
        if rope is not None:
            cos_ref, sin_ref = refs[:2]
            refs = refs[2:]
        y_ref = refs[0]
        h_ref = refs[1] if write_h else None
        hs = refs[-1]
        j = pl.program_id(1)

        @pl.when(j == 0)
        def _():
            h, _ = _rms(x_ref[...], g_ref[...])
            hb = h.astype(BF)
            hs[...] = hb
            if write_h:
                h_ref[...] = hb

        y = _dot(hs[...], w_ref[...])
        if rope is not None:
            y = _rope(y, cos_ref[...], sin_ref[...])
        y_ref[...] = y.astype(out_dtype)

    in_specs = [pl.BlockSpec((tm, K), lambda i, j: (i, 0)), pl.BlockSpec((1, K), lambda i, j: (0, 0)),
                pl.BlockSpec((K, tn), lambda i, j: (0, j))]
    args = [x, gain, w]
    if rope is not None:
        in_specs += [pl.BlockSpec((tm, LANES), lambda i, j: (i, 0))] * 2
        args += list(rope)
    out_specs = [pl.BlockSpec((tm, tn), lambda i, j: (i, j))]
    out_shape = [jax.ShapeDtypeStruct((S, N), out_dtype)]
    if write_h:
        out_specs.append(pl.BlockSpec((tm, K), lambda i, j: (i, 0)))
        out_shape.append(jax.ShapeDtypeStruct((S, K), BF))
    res = pl.pallas_call(
        body, name=name, grid=(S // tm, N // tn), in_specs=in_specs, out_specs=out_specs, out_shape=out_shape,
        scratch_shapes=[pltpu.VMEM((tm, K), BF)],
        compiler_params=_params(("parallel", "arbitrary")),
    )(*args)
    return res if write_h else res[0]


def norm_bwd(x, gain, dh, dres=None, *, tm=512, name="norm_bwd"):
    S, K = x.shape
    tm = _tile(S, tm)
    gain = gain.reshape(1, K)
    has_res = dres is not None

    def body(*refs):
        if has_res:
            x_ref, g_ref, dh_ref, dr_ref, dx_ref, dg_ref = refs
        else:
            x_ref, g_ref, dh_ref, dx_ref, dg_ref = refs

        @pl.when(pl.program_id(0) == 0)
        def _():
            dg_ref[...] = jnp.zeros_like(dg_ref)

        dx, dg = _rms_bwd(x_ref[...], g_ref[...], dh_ref[...].astype(F32))
        if has_res:
            dx = dx + dr_ref[...]
        dx_ref[...] = dx
        dg_ref[...] += jnp.broadcast_to(dg, dg_ref.shape)

    row = pl.BlockSpec((tm, K), lambda i: (i, 0))
    in_specs = [row, pl.BlockSpec((1, K), lambda i: (0, 0)), row] + ([row] if has_res else [])
    args = [x, gain, dh] + ([dres] if has_res else [])
    dx, dg = pl.pallas_call(
        body, name=name, grid=(S // tm,), in_specs=in_specs,
        out_specs=[row, pl.BlockSpec((8, K), lambda i: (0, 0))],
        out_shape=[jax.ShapeDtypeStruct((S, K), F32), jax.ShapeDtypeStruct((8, K), F32)],
        compiler_params=_params(("arbitrary",)),
    )(*args)
    return dx, dg[0]


def loss_head(x, gain, target, *, tm=512):
    S, K = x.shape
    tm = _tile(S, tm)
    gain = gain.reshape(1, K)

    def body(x_ref, g_ref, t_ref, dx_ref, dg_ref, ls_ref):
        @pl.when(pl.program_id(0) == 0)
        def _():
            dg_ref[...] = jnp.zeros_like(dg_ref)
            ls_ref[...] = jnp.zeros_like(ls_ref)

        xv, g = x_ref[...], g_ref[...]
        y, _ = _rms(xv, g)
        e = y - t_ref[...]
        part = jnp.sum(jnp.mean(e * e, axis=-1, keepdims=True), axis=0, keepdims=True)
        ls_ref[...] += jnp.broadcast_to(0.5 * part, ls_ref.shape)
        dx, dg = _rms_bwd(xv, g, e / K)
        dx_ref[...] = dx
        dg_ref[...] += jnp.broadcast_to(dg, dg_ref.shape)

    row = pl.BlockSpec((tm, K), lambda i: (i, 0))
    dx, dg, ls = pl.pallas_call(
        body, name="loss_head", grid=(S // tm,),
        in_specs=[row, pl.BlockSpec((1, K), lambda i: (0, 0)), row],
        out_specs=[row, pl.BlockSpec((8, K), lambda i: (0, 0)), pl.BlockSpec((8, LANES), lambda i: (0, 0))],
        out_shape=[jax.ShapeDtypeStruct((S, K), F32), jax.ShapeDtypeStruct((8, K), F32),
                   jax.ShapeDtypeStruct((8, LANES), F32)],
        compiler_params=_params(("arbitrary",)),
    )(x, gain, target)
    return ls[0, 0], dx, dg[0]


def _once(shape, index_map):
    return pl.BlockSpec(shape, index_map, pipeline_mode=pl.Buffered(1))


def ffn_fwd(x, gain, wg, wu, wd, carry=(), *, tm=512):
    S, D = x.shape
    NB, _, Fs = wg.shape
    tm = _tile(S, tm)
    ni = S // tm
    nc = len(carry)
    gain = gain.reshape(1, D)

    def body(*refs):
        x_ref, g_ref, wg_ref, wu_ref, wd_ref = refs[:5]
        c_in = refs[5:5 + nc]
        xo_ref, h_ref, G_ref, U_ref = refs[5 + nc:9 + nc]
        c_out = refs[9 + nc:9 + 2 * nc]
        hs, acc = refs[9 + 2 * nc:11 + 2 * nc]
        i, j = pl.program_id(0), pl.program_id(1)
        if nc:
            start, middle, finish = _gather_phases(c_in, c_out, *refs[11 + 2 * nc:])
            pl.when((i == 0) & (j == 0))(start)
            pl.when((i == ni // 2) & (j == 0))(middle)

        @pl.when(j == 0)
        def _():
            h, _ = _rms(x_ref[...], g_ref[...])
            hb = h.astype(BF)
            hs[...] = hb
            h_ref[...] = hb
            acc[...] = jnp.zeros_like(acc)

        h = hs[...]
        g = _dot(h, wg_ref[...])
        u = _dot(h, wu_ref[...])
        G_ref[...] = g.astype(BF)
        U_ref[...] = u.astype(BF)
        a = (g * jax.nn.sigmoid(g) * u).astype(BF)
        acc[...] += _dot(a, wd_ref[...])

        @pl.when(j == NB - 1)
        def _():
            xo_ref[...] = x_ref[...] + 0.5 * acc[...]

        if nc:
            pl.when((i == ni - 1) & (j == NB - 1))(finish)

    row = lambda i, j: (i, 0)
    blk = lambda i, j: (j, 0, 0)
    hid = pl.BlockSpec((None, tm, Fs), lambda i, j: (j, i, 0))
    return pl.pallas_call(
        body, name="ffn_fwd_gather" if nc else "ffn_fwd", grid=(ni, NB),
        in_specs=[_once((tm, D), row), pl.BlockSpec((1, D), lambda i, j: (0, 0)),
                  pl.BlockSpec((None, D, Fs), blk), pl.BlockSpec((None, D, Fs), blk), pl.BlockSpec((None, Fs, D), blk)]
                 + [ANY] * nc,
        out_specs=[_once((tm, D), row), _once((tm, D), row), hid, hid] + [ANY] * nc,
        out_shape=[jax.ShapeDtypeStruct((S, D), F32), jax.ShapeDtypeStruct((S, D), BF),
                   jax.ShapeDtypeStruct((NB, S, Fs), BF), jax.ShapeDtypeStruct((NB, S, Fs), BF)] + _gathered(carry),
        scratch_shapes=[pltpu.VMEM((tm, D), BF), pltpu.VMEM((tm, D), F32)] + (_comm_sems(nc) if nc else []),
        compiler_params=_params(("arbitrary", "arbitrary")),
    )(x, gain, wg, wu, wd, *carry)


def ffn_bwd_x(dy, x, gain, G, U, wg, wu, wd, carry=(), *, tm=256):
    S, D = x.shape
    NB, _, Fs = wg.shape
    tm = _tile(S, tm)
    ni = S // tm
    nc = len(carry)
    gain = gain.reshape(1, D)

    def body(*refs):
        dy_ref, x_ref, g_ref, G_ref, U_ref, wg_ref, wu_ref, wd_ref = refs[:8]
        c_in = refs[8:8 + nc]
        dx_ref, do_ref, dG_ref, dU_ref, dg_ref = refs[8 + nc:13 + nc]
        c_out = refs[13 + nc:13 + 2 * nc]
        dob, acc = refs[13 + 2 * nc:15 + 2 * nc]
        i, j = pl.program_id(0), pl.program_id(1)
        if nc:
            start, finish = _exchange_phases(c_in, c_out, [False] * nc, *refs[15 + 2 * nc:])
            pl.when((i == 0) & (j == 0))(start)

        @pl.when(j == 0)
        def _():
            d = (0.5 * dy_ref[...]).astype(BF)
            dob[...] = d
            do_ref[...] = d
            acc[...] = jnp.zeros_like(acc)

        @pl.when((i == 0) & (j == 0))
        def _():
            dg_ref[...] = jnp.zeros_like(dg_ref)

        dA = _dot_nt(dob[...], wd_ref[...])
        g = G_ref[...].astype(F32)
        u = U_ref[...].astype(F32)
        sig = jax.nn.sigmoid(g)
        dG = (dA * u * (sig * (1.0 + g * (1.0 - sig)))).astype(BF)
        dU = (dA * (g * sig)).astype(BF)
        dG_ref[...] = dG
        dU_ref[...] = dU
        acc[...] += _dot_nt(dG, wg_ref[...]) + _dot_nt(dU, wu_ref[...])

        @pl.when(j == NB - 1)
        def _():
            dxn, dg = _rms_bwd(x_ref[...], g_ref[...], acc[...])
            dx_ref[...] = dy_ref[...] + dxn
            dg_ref[...] += jnp.broadcast_to(dg, dg_ref.shape)

        if nc:
            pl.when((i == ni - 1) & (j == NB - 1))(finish)

    row = lambda i, j: (i, 0)
    blk = lambda i, j: (j, 0, 0)
    hid = pl.BlockSpec((None, tm, Fs), lambda i, j: (j, i, 0))
    dx, dout, dG, dU, dg, *got = pl.pallas_call(
        body, name="ffn_bwd_x_exchange" if nc else "ffn_bwd_x", grid=(ni, NB),
        in_specs=[_once((tm, D), row), _once((tm, D), row), pl.BlockSpec((1, D), lambda i, j: (0, 0)), hid, hid,
                  pl.BlockSpec((None, D, Fs), blk), pl.BlockSpec((None, D, Fs), blk), pl.BlockSpec((None, Fs, D), blk)]
                 + [ANY] * nc,
        out_specs=[_once((tm, D), row), _once((tm, D), row), hid, hid, pl.BlockSpec((8, D), lambda i, j: (0, 0))]
                  + [ANY] * nc,
        out_shape=[jax.ShapeDtypeStruct((S, D), F32), jax.ShapeDtypeStruct((S, D), BF),
                   jax.ShapeDtypeStruct((NB, S, Fs), BF), jax.ShapeDtypeStruct((NB, S, Fs), BF),
                   jax.ShapeDtypeStruct((8, D), F32)] + [jax.ShapeDtypeStruct(a.shape, a.dtype) for a in carry],
        scratch_shapes=[pltpu.VMEM((tm, D), BF), pltpu.VMEM((tm, D), F32)] + (_comm_sems(nc) if nc else []),
        compiler_params=_params(("arbitrary", "arbitrary")),
    )(dy, x, gain, G, U, wg, wu, wd, *carry)
    return dx, dout, dG, dU, dg[0], got


def ffn_bwd_w(h, dout, G, U, dG, dU, *, tm=256):
    S, D = h.shape
    NB, _, Fs = G.shape
    tm = _tile(S, tm)
    ni = S // tm

    def body(h_ref, do_ref, G_ref, U_ref, dG_ref, dU_ref, wg_ref, wu_ref, wd_ref, ag, au, ad):
        i = pl.program_id(1)

        @pl.when(i == 0)
        def _():
            ag[...] = jnp.zeros_like(ag)
            au[...] = jnp.zeros_like(au)
            ad[...] = jnp.zeros_like(ad)

        h = h_ref[...]
        ag[...] += _dot_tn(h, dG_ref[...])
        au[...] += _dot_tn(h, dU_ref[...])
        g = G_ref[...].astype(F32)
        a = (g * jax.nn.sigmoid(g) * U_ref[...].astype(F32)).astype(BF)
        ad[...] += _dot_tn(a, do_ref[...])

        @pl.when(i == ni - 1)
        def _():
            wg_ref[...] = ag[...].astype(BF)
            wu_ref[...] = au[...].astype(BF)
            wd_ref[...] = ad[...].astype(BF)

    row = pl.BlockSpec((tm, D), lambda j, i: (i, 0))
    hid = pl.BlockSpec((None, tm, Fs), lambda j, i: (j, i, 0))
    blk = lambda j, i: (j, 0, 0)
    return pl.pallas_call(
        body, name="ffn_bwd_w", grid=(NB, ni),
        in_specs=[row, row, hid, hid, hid, hid],
        out_specs=[pl.BlockSpec((None, D, Fs), blk), pl.BlockSpec((None, D, Fs), blk), pl.BlockSpec((None, Fs, D), blk)],
        out_shape=[jax.ShapeDtypeStruct((NB, D, Fs), BF), jax.ShapeDtypeStruct((NB, D, Fs), BF),
                   jax.ShapeDtypeStruct((NB, Fs, D), BF)],
        scratch_shapes=[pltpu.VMEM((D, Fs), F32), pltpu.VMEM((D, Fs), F32), pltpu.VMEM((Fs, D), F32)],
        compiler_params=_params(("parallel", "arbitrary")),
    )(h, dout, G, U, dG, dU)


def dilated_bias(H, d):
    n = DIL_N
    slopes = 2.0 ** (-8.0 * (np.arange(H) + 1) / H)
    i = np.arange(n)[:, None]
    j = np.arange(2 * n)[None, :]
    steps = n + i - j
    band = (steps >= 0) & (steps <= n)
    first = band & (j >= n)
    bias = -slopes[:, None, None] * (d * steps).astype(np.float64)[None]
    out = np.stack([np.where(band[None], bias, NEG), np.where(first[None], bias, NEG)], axis=1)
    return jnp.asarray(out, dtype=F32)


def dil_fwd(qkv, bias, d, H):
    S = qkv.shape[0]
    Sd = S // d
    TQ = _tile(Sd, 512)
    nsub, nc = TQ // DIL_N, Sd // TQ
    scale = HEAD ** -0.5
    qv = qkv.reshape(Sd, d * 3 * H * HEAD)

    def body(q_ref, kc_ref, kp_ref, vc_ref, vp_ref, b_ref, o_ref, l_ref):
        c = pl.program_id(2)
        for i in range(nsub):
            sl = slice(i * DIL_N, (i + 1) * DIL_N)
            if i == 0:
                k2 = jnp.concatenate([kp_ref[...], kc_ref[0:DIL_N, :]], axis=0)
                v2 = jnp.concatenate([vp_ref[...], vc_ref[0:DIL_N, :]], axis=0)
                b = jnp.where(c == 0, b_ref[1], b_ref[0])
            else:
                k2 = kc_ref[(i - 1) * DIL_N:(i + 1) * DIL_N, :]
                v2 = vc_ref[(i - 1) * DIL_N:(i + 1) * DIL_N, :]
                b = b_ref[0]
            s = _dot_nt(q_ref[sl, :], k2) * scale + b
            m = jnp.max(s, axis=1, keepdims=True)
            p = jnp.exp(s - m)
            l = jnp.sum(p, axis=1, keepdims=True)
            o_ref[sl, :] = _dot(p.astype(BF), v2) / l
            l_ref[sl, :] = jnp.broadcast_to(m + jnp.log(l), (DIL_N, LANES))

    W3 = 3 * H
    chunk = lambda off: pl.BlockSpec((TQ, HEAD), lambda h, r, c: (c, r * W3 + off * H + h))
    halo = lambda off: pl.BlockSpec((DIL_N, HEAD), lambda h, r, c: (jnp.maximum(c * nsub - 1, 0), r * W3 + off * H + h))
    out = pl.BlockSpec((TQ, HEAD), lambda h, r, c: (c, r * H + h))
    o, l = pl.pallas_call(
        body, name=f"dil_fwd_d{d}", grid=(H, d, nc),
        in_specs=[chunk(0), chunk(1), halo(1), chunk(2), halo(2),
                  pl.BlockSpec((None, 2, DIL_N, 2 * DIL_N), lambda h, r, c: (h, 0, 0, 0))],
        out_specs=[out, out],
        out_shape=[jax.ShapeDtypeStruct((Sd, d * H * HEAD), F32)] * 2,
        compiler_params=_params(("parallel", "parallel", "arbitrary")),
    )(qv, qv, qv, qv, qv, bias)
    return o.reshape(S, H * HEAD), l.reshape(S, H * HEAD)


def dil_combine(os, ls, *, tm=256):
    S, W = os[0].shape
    tm = _tile(S, tm)

    def body(o1, o2, o3, l1, l2, l3, o_ref, L_ref):
        a, b, c = l1[...], l2[...], l3[...]
        m = jnp.maximum(jnp.maximum(a, b), c)
        ea, eb, ec = jnp.exp(a - m), jnp.exp(b - m), jnp.exp(c - m)
        z = ea + eb + ec
        o_ref[...] = ((ea * o1[...] + eb * o2[...] + ec * o3[...]) / z).astype(BF)
        L_ref[...] = m + jnp.log(z)

    row = pl.BlockSpec((tm, W), lambda i: (i, 0))
    return pl.pallas_call(
        body, name="dil_combine", grid=(S // tm,), in_specs=[row] * 6, out_specs=[row, row],
        out_shape=[jax.ShapeDtypeStruct((S, W), BF), jax.ShapeDtypeStruct((S, W), F32)],
        compiler_params=_params(("parallel",)),
    )(*os, *ls)


def head_delta(do, o, *, tm=512):
    S, W = o.shape
    tm = _tile(S, tm)

    def body(do_ref, o_ref, d_ref):
        s = jnp.sum(do_ref[...].astype(F32) * o_ref[...].astype(F32), axis=1, keepdims=True)
        d_ref[...] = jnp.broadcast_to(s, d_ref.shape)

    blk = pl.BlockSpec((tm, HEAD), lambda i, h: (i, h))
    return pl.pallas_call(
        body, name="head_delta", grid=(S // tm, W // HEAD), in_specs=[blk, blk], out_specs=blk,
        out_shape=jax.ShapeDtypeStruct((S, W), F32),
        compiler_params=_params(("parallel", "parallel")),
    )(do, o)


def dil_bwd(qkv, do, L, delta, bias, d, H, prev=None):
    S = qkv.shape[0]
    Sd = S // d
    TQ = _tile(Sd, 512)
    nsub, nc = TQ // DIL_N, Sd // TQ
    nblk = Sd // DIL_N
    scale = HEAD ** -0.5
    W3 = 3 * H
    qv = qkv.reshape(Sd, d * W3 * HEAD)
    dov, Lv, dlv = (t.reshape(Sd, d * H * HEAD) for t in (do, L, delta))
    has_prev = prev is not None

    def body(*refs):
        (qc, qn, kc, kp, vc, vp, doc, don, Lc, Ln, dlc, dln, b_ref) = refs[:13]
        refs = refs[13:]
        if has_prev:
            dqi, dki, dvi = refs[:3]
            refs = refs[3:]
        dqo, dko, dvo, dk_acc, dv_acc = refs
        c = pl.program_id(2)
        dk_acc[...] = jnp.zeros_like(dk_acc)
        dv_acc[...] = jnp.zeros_like(dv_acc)
        for i in range(nsub):
            sl = slice(i * DIL_N, (i + 1) * DIL_N)
            if i == 0:
                k2 = jnp.concatenate([kp[...], kc[0:DIL_N, :]], axis=0)
                v2 = jnp.concatenate([vp[...], vc[0:DIL_N, :]], axis=0)
                b = jnp.where(c == 0, b_ref[1], b_ref[0])
            else:
                k2 = kc[(i - 1) * DIL_N:(i + 1) * DIL_N, :]
                v2 = vc[(i - 1) * DIL_N:(i + 1) * DIL_N, :]
                b = b_ref[0]
            q, dov_ = qc[sl, :], doc[sl, :]
            Lq, dl = Lc[sl, :], dlc[sl, :]
            s = _dot_nt(q, k2) * scale + b
            p = jnp.exp(s - jnp.concatenate([Lq, Lq], axis=1))
            dp = _dot_nt(dov_, v2)
            ds = p * (dp - jnp.concatenate([dl, dl], axis=1))
            dsb, pb = ds.astype(BF), p.astype(BF)
            dq = _dot(dsb, k2) * scale
            dqo[sl, :] = dq + dqi[sl, :] if has_prev else dq
            dk2 = _dot_tn(dsb, q) * scale
            dv2 = _dot_tn(pb, dov_)
            if i >= 1:
                pv = slice((i - 1) * DIL_N, i * DIL_N)
                dk_acc[pv, :] += dk2[:DIL_N]
                dv_acc[pv, :] += dv2[:DIL_N]
            dk_acc[sl, :] += dk2[DIL_N:]
            dv_acc[sl, :] += dv2[DIL_N:]

        @pl.when(c < nc - 1)
        def _():
            last = slice((nsub - 1) * DIL_N, nsub * DIL_N)
            q, dov_ = qn[...], don[...]
            k1, v1 = kc[last, :], vc[last, :]
            s = _dot_nt(q, k1) * scale + b_ref[0][:, :DIL_N]
            p = jnp.exp(s - Ln[...])
            ds = p * (_dot_nt(dov_, v1) - dln[...])
            dk_acc[last, :] += _dot_tn(ds.astype(BF), q) * scale
            dv_acc[last, :] += _dot_tn(p.astype(BF), dov_)

        if has_prev:
            dko[...] = dk_acc[...] + dki[...]
            dvo[...] = dv_acc[...] + dvi[...]
        else:
            dko[...] = dk_acc[...]
            dvo[...] = dv_acc[...]

    nxt = lambda c: jnp.minimum((c + 1) * nsub, nblk - 1)
    prv = lambda c: jnp.maximum(c * nsub - 1, 0)
    chunk3 = lambda off: pl.BlockSpec((TQ, HEAD), lambda h, r, c: (c, r * W3 + off * H + h))
    halo3 = lambda off, f: pl.BlockSpec((DIL_N, HEAD), lambda h, r, c: (f(c), r * W3 + off * H + h))
    chunk1 = pl.BlockSpec((TQ, HEAD), lambda h, r, c: (c, r * H + h))
    next1 = pl.BlockSpec((DIL_N, HEAD), lambda h, r, c: (nxt(c), r * H + h))
    in_specs = [chunk3(0), halo3(0, nxt), chunk3(1), halo3(1, prv), chunk3(2), halo3(2, prv),
                chunk1, next1, chunk1, next1, chunk1, next1,
                pl.BlockSpec((None, 2, DIL_N, 2 * DIL_N), lambda h, r, c: (h, 0, 0, 0))]
    args = [qv] * 6 + [dov, dov, Lv, Lv, dlv, dlv, bias]
    if has_prev:
        in_specs += [chunk1] * 3
        args += [t.reshape(Sd, d * H * HEAD) for t in prev]
    outs = pl.pallas_call(
        body, name=f"dil_bwd_d{d}", grid=(H, d, nc), in_specs=in_specs, out_specs=[chunk1] * 3,
        out_shape=[jax.ShapeDtypeStruct((Sd, d * H * HEAD), F32)] * 3,
        scratch_shapes=[pltpu.VMEM((TQ, HEAD), F32), pltpu.VMEM((TQ, HEAD), F32)],
        compiler_params=_params(("parallel", "parallel", "arbitrary")),
    )(*args)
    return tuple(t.reshape(S, H * HEAD) for t in outs)


def _causal_mask(s, qi, ki, tq, tk):
    qpos = qi * tq + lax.broadcasted_iota(jnp.int32, s.shape, 0)
    kpos = ki * tk + lax.broadcasted_iota(jnp.int32, s.shape, 1)
    return jnp.where(kpos <= qpos, s, NEG)


def mla_fwd(qn, qr, kv, kr, H, *, tq=1024, tk=512):
    S = qn.shape[0]
    tq, tk = _tile(S, tq), _tile(S, tk)
    nk = S // tk
    scale = (HEAD + 2 * ROPE_HALF) ** -0.5

    def body(qn_ref, qr_ref, kn_ref, kr_ref, v_ref, o_ref, L_ref, m_s, l_s, acc):
        qi, ki = pl.program_id(1), pl.program_id(2)

        @pl.when(ki == 0)
        def _():
            m_s[...] = jnp.full_like(m_s, NEG)
            l_s[...] = jnp.zeros_like(l_s)
            acc[...] = jnp.zeros_like(acc)

        def step(masked):
            q = jnp.concatenate([qn_ref[...], qr_ref[...]], axis=1)
            k = jnp.concatenate([kn_ref[...], kr_ref[...]], axis=1)
            s = _dot_nt(q, k) * scale
            if masked:
                s = _causal_mask(s, qi, ki, tq, tk)
            m_prev = m_s[...]
            m_new = jnp.maximum(m_prev, jnp.max(s, axis=1, keepdims=True))
            alpha = jnp.exp(m_prev - m_new)
            p = jnp.exp(s - m_new)
            l_s[...] = alpha * l_s[...] + jnp.sum(p, axis=1, keepdims=True)
            acc[...] = alpha * acc[...] + _dot(p.astype(BF), v_ref[...])
            m_s[...] = m_new

        full = ki * tk + tk - 1 <= qi * tq
        live = ki * tk <= qi * tq + tq - 1
        pl.when(full)(lambda: step(False))
        pl.when(live & jnp.logical_not(full))(lambda: step(True))

        @pl.when(ki == nk - 1)
        def _():
            o_ref[...] = (acc[...] / l_s[...]).astype(BF)
            L_ref[...] = jnp.broadcast_to(m_s[...] + jnp.log(l_s[...]), L_ref.shape)

    kcl = lambda qi, ki: jnp.minimum(ki, (qi * tq + tq - 1) // tk)
    qs = pl.BlockSpec((tq, HEAD), lambda h, qi, ki: (qi, h))
    return pl.pallas_call(
        body, name="mla_fwd", grid=(H, S // tq, nk),
        in_specs=[qs, qs, pl.BlockSpec((tk, HEAD), lambda h, qi, ki: (kcl(qi, ki), h)),
                  pl.BlockSpec((tk, LANES), lambda h, qi, ki: (kcl(qi, ki), 0)),
                  pl.BlockSpec((tk, HEAD), lambda h, qi, ki: (kcl(qi, ki), H + h))],
        out_specs=[qs, qs],
        out_shape=[jax.ShapeDtypeStruct((S, H * HEAD), BF), jax.ShapeDtypeStruct((S, H * HEAD), F32)],
        scratch_shapes=[pltpu.VMEM((tq, 1), F32), pltpu.VMEM((tq, 1), F32), pltpu.VMEM((tq, HEAD), F32)],
        compiler_params=_params(("parallel", "parallel", "arbitrary")),
    )(qn, qr, kv, kr, kv)


def mla_bwd_q(qn, qr, kv, kr, do, L, delta, rope, H, *, tq=1024, tk=512):
    S = qn.shape[0]
    tq, tk = _tile(S, tq), _tile(S, tk)
    nk = S // tk
    scale = (HEAD + 2 * ROPE_HALF) ** -0.5

    def body(qn_ref, qr_ref, kn_ref, kr_ref, v_ref, do_ref, L_ref, dl_ref, cos_ref, sin_ref, dqn_ref, dqr_ref, acc):
        qi, ki = pl.program_id(1), pl.program_id(2)

        @pl.when(ki == 0)
        def _():
            acc[...] = jnp.zeros_like(acc)

        def step(masked):
            q = jnp.concatenate([qn_ref[...], qr_ref[...]], axis=1)
            k = jnp.concatenate([kn_ref[...], kr_ref[...]], axis=1)
            s = _dot_nt(q, k) * scale
            if masked:
                s = _causal_mask(s, qi, ki, tq, tk)
            p = jnp.exp(s - L_ref[:, 0:1])
            dp = _dot_nt(do_ref[...], v_ref[...])
            ds = (p * (dp - dl_ref[:, 0:1])).astype(BF)
            acc[...] += _dot(ds, k)

        full = ki * tk + tk - 1 <= qi * tq
        live = ki * tk <= qi * tq + tq - 1
        pl.when(full)(lambda: step(False))
        pl.when(live & jnp.logical_not(full))(lambda: step(True))

        @pl.when(ki == nk - 1)
        def _():
            dq = acc[...] * scale
            dqn_ref[...] = dq[:, :HEAD].astype(BF)
            dqr_ref[...] = _rope_bwd(dq[:, HEAD:], cos_ref[...], sin_ref[...]).astype(BF)

    kcl = lambda qi, ki: jnp.minimum(ki, (qi * tq + tq - 1) // tk)
    qs = pl.BlockSpec((tq, HEAD), lambda h, qi, ki: (qi, h))
    tab = pl.BlockSpec((tq, LANES), lambda h, qi, ki: (qi, 0))
    return pl.pallas_call(
        body, name="mla_bwd_q", grid=(H, S // tq, nk),
        in_specs=[qs, qs, pl.BlockSpec((tk, HEAD), lambda h, qi, ki: (kcl(qi, ki), h)),
                  pl.BlockSpec((tk, LANES), lambda h, qi, ki: (kcl(qi, ki), 0)),
                  pl.BlockSpec((tk, HEAD), lambda h, qi, ki: (kcl(qi, ki), H + h)),
                  qs, qs, qs, tab, tab],
        out_specs=[qs, qs],
        out_shape=[jax.ShapeDtypeStruct((S, H * HEAD), BF)] * 2,
        scratch_shapes=[pltpu.VMEM((tq, 2 * HEAD), F32)],
        compiler_params=_params(("parallel", "parallel", "arbitrary")),
    )(qn, qr, kv, kr, kv, do, L, delta, *rope)


def mla_bwd_kv(qn, qr, kv, kr, do, L, delta, rope, H, prev=None, *, tq=512, tk=1024):
    S = qn.shape[0]
    tq, tk = _tile(S, tq), _tile(S, tk)
    nq = S // tq
    scale = (HEAD + 2 * ROPE_HALF) ** -0.5
    has_prev = prev is not None

    def body(*refs):
        (qn_ref, qr_ref, kn_ref, kr_ref, v_ref, do_ref, L_ref, dl_ref, cos_ref, sin_ref) = refs[:10]
        refs = refs[10:]
        if has_prev:
            pk_ref, pv_ref, pr_ref = refs[:3]
            refs = refs[3:]
        dk_ref, dv_ref, dr_ref, dk_acc, dv_acc, dr_acc = refs
        ki, h, qi = pl.program_id(0), pl.program_id(1), pl.program_id(2)

        @pl.when(qi == 0)
        def _():
            dk_acc[...] = jnp.zeros_like(dk_acc)
            dv_acc[...] = jnp.zeros_like(dv_acc)

        @pl.when((qi == 0) & (h == 0))
        def _():
            dr_acc[...] = jnp.zeros_like(dr_acc)

        def step(masked):
            q = jnp.concatenate([qn_ref[...], qr_ref[...]], axis=1)
            k = jnp.concatenate([kn_ref[...], kr_ref[...]], axis=1)
            s = _dot_nt(q, k) * scale
            if masked:
                s = _causal_mask(s, qi, ki, tq, tk)
            p = jnp.exp(s - L_ref[:, 0:1])
            dov = do_ref[...]
            dv_acc[...] += _dot_tn(p.astype(BF), dov)
            dp = _dot_nt(dov, v_ref[...])
            ds = (p * (dp - dl_ref[:, 0:1])).astype(BF)
            dk_acc[...] += _dot_tn(ds, q)

        full = ki * tk + tk - 1 <= qi * tq
        live = ki * tk <= qi * tq + tq - 1
        pl.when(full)(lambda: step(False))
        pl.when(live & jnp.logical_not(full))(lambda: step(True))

        @pl.when(qi == nq - 1)
        def _():
            dk = dk_acc[...] * scale
            dkn, dv = dk[:, :HEAD], dv_acc[...]
            if has_prev:
                dkn, dv = dkn + pk_ref[...], dv + pv_ref[...]
            dk_ref[...] = dkn
            dv_ref[...] = dv
            dr_acc[...] += dk[:, HEAD:]

        @pl.when((qi == nq - 1) & (h == H - 1))
        def _():
            dr = _rope_bwd(dr_acc[...], cos_ref[...], sin_ref[...])
            dr_ref[...] = dr + pr_ref[...] if has_prev else dr

    qcl = lambda ki, qi: jnp.maximum(qi, (ki * tk) // tq)
    qs = pl.BlockSpec((tq, HEAD), lambda ki, h, qi: (qcl(ki, qi), h))
    kn = pl.BlockSpec((tk, HEAD), lambda ki, h, qi: (ki, h))
    vs = pl.BlockSpec((tk, HEAD), lambda ki, h, qi: (ki, H + h))
    k1 = pl.BlockSpec((tk, LANES), lambda ki, h, qi: (ki, 0))
    in_specs = [qs, qs, kn, k1, vs, qs, qs, qs, k1, k1]
    args = [qn, qr, kv, kr, kv, do, L, delta, *rope]
    if has_prev:
        in_specs += [kn, vs, k1]
        args += [prev[0], prev[0], prev[1]]
    dkn, dv, dr = pl.pallas_call(
        body, name="mla_bwd_kv", grid=(S // tk, H, nq), in_specs=in_specs,
        out_specs=[kn, kn, k1],
        out_shape=[jax.ShapeDtypeStruct((S, H * HEAD), F32), jax.ShapeDtypeStruct((S, H * HEAD), F32),
                   jax.ShapeDtypeStruct((S, LANES), F32)],
        scratch_shapes=[pltpu.VMEM((tk, 2 * HEAD), F32), pltpu.VMEM((tk, HEAD), F32), pltpu.VMEM((tk, LANES), F32)],
        compiler_params=_params(("parallel", "arbitrary", "arbitrary")),
    )(*args)
    return dkn, dv, dr


def adamw(recv, w, m, v, row_off=0, *, name="adamw"):
    R, C = w.shape
    tr = 16
    while tr * 2 * C * 44 <= 6 * 1024 * 1024 and R % (tr * 2) == 0 and row_off % (tr * 2) == 0:
        tr *= 2
    tr = min(tr, R)
    assert R % tr == 0 and row_off % tr == 0
    off = row_off // tr

    def body(r_ref, w_ref, m_ref, v_ref, g_ref, d_ref, mo_ref, vo_ref):
        g = r_ref[0].astype(F32)
        for s in range(1, N_DEV):
            g = g + r_ref[s].astype(F32)
        m2 = ADAM_B1 * m_ref[...] + (1.0 - ADAM_B1) * g
        v2 = ADAM_B2 * v_ref[...] + (1.0 - ADAM_B2) * (g * g)
        m_hat = m2 / (1.0 - ADAM_B1 ** ADAM_STEP)
        v_hat = v2 / (1.0 - ADAM_B2 ** ADAM_STEP)
        g_ref[...] = g
        d_ref[...] = -ADAM_LR * (m_hat / (jnp.sqrt(v_hat) + ADAM_EPS) + ADAM_WD * w_ref[...])
        mo_ref[...] = m2
        vo_ref[...] = v2

    row = pl.BlockSpec((tr, C), lambda i: (i, 0))
    return pl.pallas_call(
        body, name=name, grid=(R // tr,),
        in_specs=[pl.BlockSpec((N_DEV, tr, C), lambda i: (0, i + off, 0)), row, row, row],
        out_specs=[row] * 4, out_shape=[jax.ShapeDtypeStruct((R, C), F32)] * 4,
        compiler_params=_params(("parallel",)),
    )(recv, w, m, v)


def _pack(ts, width, dtype, row_mult=16):
    flat = jnp.concatenate([t.astype(dtype).reshape(-1) for t in ts])
    n = flat.shape[0]
    rows = -(-n // (width * row_mult)) * row_mult
    return jnp.pad(flat, (0, rows * width - n)).reshape(rows, width)


def _unpack(buf, shapes):
    lead = buf.shape[:-2]
    flat = buf.reshape(lead + (-1,))
    out, off = [], 0
    for s in shapes:
        n = int(np.prod(s))
        out.append(flat[..., off:off + n].reshape(lead + tuple(s)))
        off += n
    return out


def _rope_pad(r):
    z = jnp.zeros(r.shape[:-1] + (ROPE_HALF,), r.dtype)
    return jnp.concatenate([r[..., :ROPE_HALF], z, r[..., ROPE_HALF:], z], axis=-1)


def _rope_unpad(r):
    return jnp.concatenate([r[..., :ROPE_HALF], r[..., 2 * ROPE_HALF:3 * ROPE_HALF]], axis=-1)


def _step(P):
    x0 = P['x'][0]
    target = P['loss_target'][0]
    S, D = x0.shape
    NL = P['ffn_norm1'].shape[0]
    NA = P['a_wqkv'].shape[0]
    Fs = P['ffn1_wg'].shape[2]
    H = D // HEAD
    KV = P['b_wdkv'].shape[1]
    QL = P['b_wdq'].shape[2]
    HW = H * HEAD

    ffn_seq = [(l, f) for l in range(NL) for f in (0, 1)]
    ffn_loc = {(l, f): [P[n][l].astype(BF) for n in (f'ffn{f + 1}_wg', f'ffn{f + 1}_wu', f'ffn{f + 1}_wd')]
               for l, f in ffn_seq}
    mix_shapes = [P[n].shape for n in MIX_W]
    wc_loc = _pack([P[n] for n in MIX_W], D, BF)
    *w_first, WC = all_gather(ffn_loc[ffn_seq[0]] + [wc_loc])
    ffn_w = {ffn_seq[0]: w_first}
    g = dict(zip(MIX_W, _unpack(WC, mix_shapes)))
    cols = lambda t: jnp.moveaxis(t, 0, -2).reshape(t.shape[1:-1] + (N_DEV * t.shape[-1],))
    rows = lambda t, lead: jnp.moveaxis(t, 0, lead).reshape(t.shape[1:1 + lead] + (N_DEV * t.shape[1 + lead],) + t.shape[2 + lead:])
    a_wqkv = cols(g['a_wqkv'])
    a_wo = rows(g['a_wo'], 1)
    b_wo = rows(g['b_wo'], 1)
    wdkv = rows(g['b_wdkv'], 0)
    wkr = _rope_pad(rows(g['b_wkr'], 0))
    wuk = rows(g['b_wuk'], 0).reshape(KV, HW)
    wuv = rows(g['b_wuv'], 0).reshape(KV, HW)
    wkv = jnp.concatenate([wuk, wuv], axis=1)
    wdq = rows(g['b_wdq'], 1)
    wuq = rows(g['b_wuq'], 1)
    wuq_n = wuq[..., :HEAD].reshape(-1, QL, HW)
    wuq_r = _rope_pad(wuq[..., HEAD:]).reshape(-1, QL, HW)

    def ffn_forward(x, gain, key):
        nxt = ffn_seq.index(key) + 1
        carry = ffn_loc[ffn_seq[nxt]] if nxt < len(ffn_seq) else []
        xo, h, G, U, *got = ffn_fwd(x, gain, *ffn_w[key], carry=carry)
        if carry:
            ffn_w[ffn_seq[nxt]] = got
        return xo, h, G, U

    inv = 1.0 / (ROPE_THETA ** (jnp.arange(0, 2 * ROPE_HALF, 2, dtype=F32) / (2 * ROPE_HALF)))
    ang = jnp.arange(S, dtype=F32)[:, None] * inv[None, :]
    z = jnp.zeros((S, ROPE_HALF), F32)
    rope = (jnp.concatenate([jnp.cos(ang), z, jnp.cos(ang), z], axis=1),
            jnp.concatenate([-jnp.sin(ang), z, jnp.sin(ang), z], axis=1))
    biases = [dilated_bias(H, d) for d in BRANCH_DILATIONS]

    saved = []
    kvs = None
    x = x0
    for l in range(NL):
        st = {'x_in': x}
        if l == NA:
            ckv_pre, hkv = norm_mm(x, P['kv_norm'], wdkv, out_dtype=F32, name="kv_down")
            kr = norm_mm(x, P['kv_norm'], wkr, out_dtype=BF, tn=LANES, rope=rope, write_h=False, name="kv_rope")
            kvm, ckv = norm_mm(ckv_pre, P['b_ckv_norm'], wkv, out_dtype=BF, name="kv_up")
            kvs = dict(x=x, ckv_pre=ckv_pre, hkv=hkv, kr=kr, kv=kvm, ckv=ckv)
        xa, h1, G1, U1 = ffn_forward(x, P['ffn_norm1'][l], (l, 0))
        st.update(h1=h1, G1=G1, U1=U1, xa=xa)
        if l < NA:
            qkv, hm = norm_mm(xa, P['mix_norm'][l], a_wqkv[l], out_dtype=BF, name="a_qkv")
            os, ls = zip(*[dil_fwd(qkv, b, d, H) for b, d in zip(biases, BRANCH_DILATIONS)])
            o, Lj = dil_combine(os, ls)
            xb = mm(o, a_wo[l], add=xa, name="mix_out")
            st.update(qkv=qkv, hm=hm, o=o, L=Lj)
        else:
            jb = l - NA
            cq_pre, hm = norm_mm(xa, P['mix_norm'][l], wdq[jb], out_dtype=F32, name="q_down")
            qn, cq = norm_mm(cq_pre, P['b_cq_norm'][jb], wuq_n[jb], out_dtype=BF, name="q_up")
            qr = norm_mm(cq_pre, P['b_cq_norm'][jb], wuq_r[jb], out_dtype=BF, tn=LANES, rope=rope, write_h=False,
                         name="q_rope")
            o, Lj = mla_fwd(qn, qr, kvs['kv'], kvs['kr'], H)
            xb = mm(o, b_wo[jb], add=xa, name="mix_out")
            st.update(cq_pre=cq_pre, hm=hm, qn=qn, qr=qr, cq=cq, o=o, L=Lj)
        x, h2, G2, U2 = ffn_forward(xb, P['ffn_norm2'][l], (l, 1))
        st.update(xb=xb, h2=h2, G2=G2, U2=U2)
        saved.append(st)

    loss_part, dx, dg_final = loss_head(x, P['final_norm'], target)

    gw = {}
    gv = {'final_norm': dg_final}
    dkv_acc = None
    for n in ('ffn_norm1', 'mix_norm', 'ffn_norm2'):
        gv[n] = [None] * NL
    gv['b_cq_norm'] = [None] * (NL - NA)
    ffn_r = {}
    pending = []
    a_g = {'a_wqkv': [None] * NA, 'a_wo': [None] * NA}
    b_g = {n: [None] * (NL - NA) for n in ('b_wdq', 'b_wuq', 'b_wo')}

    def ffn_backward(dy, x_in, gain, key, h, G, U):
        sent, carry = pending.pop() if pending else (None, [])
        dxi, dout, dG, dU, dgain, got = ffn_bwd_x(dy, x_in, gain, G, U, *ffn_w[key], carry=carry)
        if carry:
            ffn_r[sent] = got
        pending.append((key, list(ffn_bwd_w(h, dout, G, U, dG, dU))))
        return dxi, dgain

    for l in reversed(range(NL)):
        st = saved[l]
        dxb, gv['ffn_norm2'][l] = ffn_backward(dx, st['xb'], P['ffn_norm2'][l], (l, 1), st['h2'], st['G2'], st['U2'])
        xa = st['xa']
        if l < NA:
            do = mm(dxb, a_wo[l], tb=True, out_dtype=BF, name="mix_out_dx")
            a_g['a_wo'][l] = mm(st['o'], dxb, ta=True, out_dtype=BF, tm=1024, tn=1024, name="mix_out_dw")
            delta = head_delta(do, st['o'])
            acc = None
            for b, d in zip(biases, BRANCH_DILATIONS):
                acc = dil_bwd(st['qkv'], do, st['L'], delta, b, d, H, prev=acc)
            dh = None
            dws = []
            for part, t in enumerate(acc):
                wpart = a_wqkv[l][:, part * HW:(part + 1) * HW]
                dh = mm(t, wpart, tb=True, add=dh, name="a_qkv_dx")
                dws.append(mm(st['hm'], t, ta=True, out_dtype=BF, tm=1024, tn=1024, name="a_qkv_dw"))
            a_g['a_wqkv'][l] = jnp.concatenate(dws, axis=1)
            dxa, gv['mix_norm'][l] = norm_bwd(xa, P['mix_norm'][l], dh, dxb, name="mix_norm_bwd")
        else:
            jb = l - NA
            do = mm(dxb, b_wo[jb], tb=True, out_dtype=BF, name="mix_out_dx")
            b_g['b_wo'][jb] = mm(st['o'], dxb, ta=True, out_dtype=BF, tm=1024, tn=1024, name="mix_out_dw")
            delta = head_delta(do, st['o'])
            dqn, dqr = mla_bwd_q(st['qn'], st['qr'], kvs['kv'], kvs['kr'], do, st['L'], delta, rope, H)
            dkn, dv, dr = mla_bwd_kv(st['qn'], st['qr'], kvs['kv'], kvs['kr'], do, st['L'], delta, rope, H, prev=dkv_acc)
            dkv_acc = (jnp.concatenate([dkn, dv], axis=1), dr)
            dcq = mm(dqn, wuq_n[jb], tb=True, name="q_up_dx")
            dcq = mm(dqr, wuq_r[jb], tb=True, add=dcq, name="q_up_dx_add")
            dwn = mm(st['cq'], dqn, ta=True, out_dtype=F32, name="q_up_dw")
            dwr = mm(st['cq'], dqr, ta=True, out_dtype=F32, name="q_up_dw")
            b_g['b_wuq'][jb] = jnp.concatenate(
                [dwn.reshape(QL, H, HEAD), _rope_unpad(dwr.reshape(QL, H, HEAD))], axis=-1)
            dcq_pre, gv['b_cq_norm'][jb] = norm_bwd(st['cq_pre'], P['b_cq_norm'][jb], dcq, name="cq_norm_bwd")
            dh = mm(dcq_pre, wdq[jb], tb=True, name="q_down_dx")
            b_g['b_wdq'][jb] = mm(st['hm'], dcq_pre, ta=True, out_dtype=F32, tm=1024, name="q_down_dw")
            dxa, gv['mix_norm'][l] = norm_bwd(xa, P['mix_norm'][l], dh, dxb, name="mix_norm_bwd")
        dx, gv['ffn_norm1'][l] = ffn_backward(dxa, st['x_in'], P['ffn_norm1'][l], (l, 0), st['h1'], st['G1'], st['U1'])
        if l == NA:
            dkvm, dr = dkv_acc
            dckv = mm(dkvm, wkv, tb=True, name="kv_up_dx")
            dwkv = mm(kvs['ckv'], dkvm, ta=True, out_dtype=F32, name="kv_up_dw")
            gw['b_wuk'] = dwkv[:, :HW].reshape(KV, H, HEAD)
            gw['b_wuv'] = dwkv[:, HW:].reshape(KV, H, HEAD)
            dckv_pre, gv['b_ckv_norm'] = norm_bwd(kvs['ckv_pre'], P['b_ckv_norm'], dckv, name="ckv_norm_bwd")
            dh = mm(dckv_pre, wdkv, tb=True, name="kv_down_dx")
            dh = mm(dr, wkr, tb=True, add=dh, name="kv_rope_dx")
            gw['b_wdkv'] = mm(kvs['hkv'], dckv_pre, ta=True, out_dtype=F32, tm=1024, name="kv_down_dw")
            gw['b_wkr'] = _rope_unpad(mm(kvs['hkv'], dr, ta=True, out_dtype=F32, tm=1024, name="kv_rope_dw"))
            dx, gv['kv_norm'] = norm_bwd(kvs['x'], P['kv_norm'], dh, dx, name="kv_norm_bwd")

    for n in a_g:
        gw[n] = jnp.stack(a_g[n])
    for n in b_g:
        gw[n] = jnp.stack(b_g[n])
    for n in ('ffn_norm1', 'mix_norm', 'ffn_norm2', 'b_cq_norm'):
        gv[n] = jnp.stack(gv[n])

    split_cols = lambda t: jnp.moveaxis(t.reshape(t.shape[:-1] + (N_DEV, t.shape[-1] // N_DEV)), -2, 0)
    split_rows = lambda t, lead: jnp.moveaxis(
        t.reshape(t.shape[:lead] + (N_DEV, t.shape[lead] // N_DEV) + t.shape[lead + 1:]), lead, 0)
    shards = {
        'a_wqkv': split_cols(gw['a_wqkv']), 'a_wo': split_rows(gw['a_wo'], 1), 'b_wdkv': split_rows(gw['b_wdkv'], 0),
        'b_wkr': split_rows(gw['b_wkr'], 0), 'b_wuk': split_rows(gw['b_wuk'], 0), 'b_wuv': split_rows(gw['b_wuv'], 0),
        'b_wdq': split_rows(gw['b_wdq'], 1), 'b_wuq': split_rows(gw['b_wuq'], 1), 'b_wo': split_rows(gw['b_wo'], 1),
    }
    gc = jnp.stack([_pack([shards[n][p] for n in MIX_W], D, BF) for p in range(N_DEV)])
    vec_parts = [gv[n] for n in VEC_W] + [jnp.full((LANES,), loss_part, F32)]
    gvec = _pack(vec_parts, LANES, F32, row_mult=8)
    last_key, last = pending.pop()
    *got, rc, rv = exchange(last + [gc, gvec], [False] * len(last) + [False, True])
    ffn_r[last_key] = got

    res = {}
    for f in (0, 1):
        for which, kind in enumerate(('wg', 'wu', 'wd')):
            n = f'ffn{f + 1}_{kind}'
            recv = jnp.concatenate([ffn_r[(l, f)][which] for l in range(NL)], axis=1)
            flat = lambda t: t.reshape(recv.shape[1:])
            out = adamw(recv, flat(P[n]), flat(P['m_' + n]), flat(P['v_' + n]),
                        name="adamw_row" if kind == 'wd' else "adamw_col")
            res[n] = [t.reshape(P[n].shape) for t in out]
    pk = lambda pre: _pack([P[pre + n] for n in MIX_W], D, F32)
    out = adamw(rc, pk(''), pk('m_'), pk('v_'), name="adamw_mix")
    for n, parts in zip(MIX_W, zip(*[_unpack(t, mix_shapes) for t in out])):
        res[n] = list(parts)
    vec_shapes = [P[n].shape for n in VEC_W] + [(LANES,)]
    ones = jnp.ones((LANES,), F32)
    pv = lambda pre: _pack([P[pre + n] for n in VEC_W] + [ones], LANES, F32, row_mult=8)
    out = adamw(rv, pv(''), pv('m_'), pv('v_'), name="adamw_vec")
    unp = [_unpack(t, vec_shapes) for t in out]
    for idx, n in enumerate(VEC_W):
        res[n] = [u[idx] for u in unp]
    loss = unp[0][-1][0]

    outs = [loss, dx[None]]
    for field in range(4):
        outs += [res[n][field] for n in W_NAMES]
    return tuple(outs)


def kernel(x, ffn_norm1, ffn1_wg, ffn1_wu, ffn1_wd, mix_norm, ffn_norm2, ffn2_wg, ffn2_wu, ffn2_wd, a_wqkv, a_wo, kv_norm, b_wdkv, b_ckv_norm, b_wkr, b_wuk, b_wuv, b_wdq, b_cq_norm, b_wuq, b_wo, final_norm, loss_target, m_ffn_norm1, m_ffn1_wg, m_ffn1_wu, m_ffn1_wd, m_mix_norm, m_ffn_norm2, m_ffn2_wg, m_ffn2_wu, m_ffn2_wd, m_a_wqkv, m_a_wo, m_kv_norm, m_b_wdkv, m_b_ckv_norm, m_b_wkr, m_b_wuk, m_b_wuv, m_b_wdq, m_b_cq_norm, m_b_wuq, m_b_wo, m_final_norm, v_ffn_norm1, v_ffn1_wg, v_ffn1_wu, v_ffn1_wd, v_mix_norm, v_ffn_norm2, v_ffn2_wg, v_ffn2_wu, v_ffn2_wd, v_a_wqkv, v_a_wo, v_kv_norm, v_b_wdkv, v_b_ckv_norm, v_b_wkr, v_b_wuk, v_b_wuv, v_b_wdq, v_b_cq_norm, v_b_wuq, v_b_wo, v_final_norm):
    return _step(dict(locals()))
```

```python
import functools
import math

import numpy as np
import jax
import jax.numpy as jnp
from jax import lax
from jax.experimental import pallas as pl
from jax.experimental.pallas import tpu as pltpu

BF = jnp.bfloat16
F32 = jnp.float32
MESH = pl.DeviceIdType.MESH
ANY = pl.BlockSpec(memory_space=pl.ANY)

N_DEV = 8
LANES = 128
HEAD = 128
ROPE_HALF = 32
DIL_N = 128
BRANCH_DILATIONS = (1, 4, 16)
ROPE_THETA = 10000.0
MLA_SCALE = (HEAD + 2 * ROPE_HALF) ** -0.5
LOG2E = math.log2(math.e)
EPS = 1e-6
NEG = -1e30
VMEM_LIMIT = 58 * 1024 * 1024

ADAM_LR, ADAM_B1, ADAM_B2, ADAM_EPS, ADAM_WD, ADAM_STEP = 0.001, 0.9, 0.999, 1e-08, 0.01, 10

W_NAMES = ['ffn_norm1', 'ffn1_wg', 'ffn1_wu', 'ffn1_wd', 'mix_norm', 'ffn_norm2', 'ffn2_wg', 'ffn2_wu', 'ffn2_wd',
           'a_wqkv', 'a_wo', 'kv_norm', 'b_wdkv', 'b_ckv_norm', 'b_wkr', 'b_wuk', 'b_wuv', 'b_wdq', 'b_cq_norm',
           'b_wuq', 'b_wo', 'final_norm']
FFN_COL = ['ffn1_wg', 'ffn1_wu', 'ffn2_wg', 'ffn2_wu']
FFN_ROW = ['ffn1_wd', 'ffn2_wd']
MIX_W = ['a_wqkv', 'a_wo', 'b_wdkv', 'b_wkr', 'b_wuk', 'b_wuv', 'b_wdq', 'b_wuq', 'b_wo']
VEC_W = ['ffn_norm1', 'mix_norm', 'ffn_norm2', 'kv_norm', 'b_ckv_norm', 'b_cq_norm', 'final_norm']


def _tile(n, pref):
    t = min(n, pref)
    assert n % t == 0, (n, pref)
    return t


def _params(sem):
    return pltpu.CompilerParams(dimension_semantics=sem, vmem_limit_bytes=VMEM_LIMIT)


def _dot(a, b):
    return jnp.dot(a, b, preferred_element_type=F32)


def _dot_nt(a, b):
    return lax.dot_general(a, b, (((1,), (1,)), ((), ())), preferred_element_type=F32)


def _dot_tn(a, b):
    return lax.dot_general(a, b, (((0,), (0,)), ((), ())), preferred_element_type=F32)


def _rms(x, g):
    r = lax.rsqrt(jnp.mean(x * x, axis=-1, keepdims=True) + EPS)
    return x * r * g, r


def _rms_bwd(x, g, dh):
    r = lax.rsqrt(jnp.mean(x * x, axis=-1, keepdims=True) + EPS)
    xhat = x * r
    gd = dh * g
    dx = r * (gd - xhat * jnp.mean(gd * xhat, axis=-1, keepdims=True))
    return dx, jnp.sum(dh * xhat, axis=0, keepdims=True)


def _rope(t, cos, sin):
    return t * cos + pltpu.roll(t, 2 * ROPE_HALF, 1) * sin


def _rope_bwd(g, cos, sin):
    return g * cos + pltpu.roll(g * sin, 2 * ROPE_HALF, 1)


def _coords():
    return lax.axis_index("x"), lax.axis_index("y"), lax.axis_index("c")


def _lin(px, py, pc):
    return 4 * px + 2 * py + pc


def _gather_phases(ins, outs, ssem, rsem, lsem):
    n = len(ins)
    x, y, c = _coords()
    me, sibling = (x, y, c), (x, y, 1 - c)
    chips = [(1 - x, y), (x, 1 - y), (1 - x, 1 - y)]

    def copy(t, k, block, to, src=None):
        slot = outs[t].at[_lin(*block)]
        return pltpu.make_async_remote_copy(
            src_ref=slot if src is None else src, dst_ref=slot,
            send_sem=ssem.at[7 * t + k], recv_sem=rsem.at[7 * t + k],
            device_id=to, device_id_type=MESH)

    def mine(t):
        return pltpu.make_async_copy(ins[t], outs[t].at[_lin(*me)], lsem.at[t])

    def first(t):
        return [copy(t, 0, me, sibling, src=ins[t])] + [copy(t, 1 + j, me, (*chip, c), src=ins[t])
                                                       for j, chip in enumerate(chips)]

    def passed(t):
        return [copy(t, 4 + j, (*chip, c), sibling) for j, chip in enumerate(chips)]

    def start():
        for t in range(n):
            mine(t).start()
            for cp in first(t):
                cp.start()

    def middle():
        for t in range(n):
            fw = passed(t)
            for j, chip in enumerate(chips):
                copy(t, 1 + j, (*chip, c), me).wait_recv()
                fw[j].start()

    def finish():
        for t in range(n):
            copy(t, 0, sibling, me).wait_recv()
            for j, chip in enumerate(chips):
                copy(t, 4 + j, (*chip, 1 - c), me).wait_recv()
        for t in range(n):
            for cp in first(t) + passed(t):
                cp.wait_send()
            mine(t).wait()

    return start, middle, finish


def _exchange_phases(ins, outs, bcast, ssem, rsem, lsem):
    n = len(ins)
    x, y, c = _coords()
    me = _lin(x, y, c)
    peers = []
    for k in range(1, N_DEV):
        kx, ky, kc = (k >> 2) & 1, (k >> 1) & 1, k & 1
        peers.append((k, (1 - x if kx else x, 1 - y if ky else y, 1 - c if kc else c)))

    def own(t):
        return pltpu.make_async_copy(ins[t] if bcast[t] else ins[t].at[me], outs[t].at[me], lsem.at[t])

    def send(t, k, peer):
        return pltpu.make_async_remote_copy(
            src_ref=ins[t] if bcast[t] else ins[t].at[_lin(*peer)], dst_ref=outs[t].at[me],
            send_sem=ssem.at[7 * t + k - 1], recv_sem=rsem.at[7 * t + k - 1],
            device_id=peer, device_id_type=MESH)

    def arrival(t, k, peer):
        slot = outs[t].at[_lin(*peer)]
        return pltpu.make_async_remote_copy(
            src_ref=slot, dst_ref=slot, send_sem=ssem.at[7 * t + k - 1], recv_sem=rsem.at[7 * t + k - 1],
            device_id=peer, device_id_type=MESH)

    def start():
        for t in range(n):
            own(t).start()
            for k, peer in peers:
                send(t, k, peer).start()

    def finish():
        for t in range(n):
            for k, peer in peers:
                arrival(t, k, peer).wait_recv()
        for t in range(n):
            for k, peer in peers:
                send(t, k, peer).wait_send()
            own(t).wait()

    return start, finish


def _comm_sems(n):
    return [pltpu.SemaphoreType.DMA((7 * n,)), pltpu.SemaphoreType.DMA((7 * n,)), pltpu.SemaphoreType.DMA((n,))]


def _gathered(xs):
    return [jax.ShapeDtypeStruct((N_DEV,) + a.shape, a.dtype) for a in xs]


def all_gather(xs):
    n = len(xs)

    def body(*refs):
        start, middle, finish = _gather_phases(refs[:n], refs[n:2 * n], *refs[2 * n:])
        start()
        middle()
        finish()

    return pl.pallas_call(
        body, name="all_gather", out_shape=_gathered(xs),
        in_specs=[ANY] * n, out_specs=[ANY] * n, scratch_shapes=_comm_sems(n),
    )(*xs)


def exchange(xs, bcast):
    n = len(xs)

    def body(*refs):
        start, finish = _exchange_phases(refs[:n], refs[n:2 * n], bcast, *refs[2 * n:])
        start()
        finish()

    out_shape = [jax.ShapeDtypeStruct(((N_DEV,) + a.shape) if b else a.shape, a.dtype) for a, b in zip(xs, bcast)]
    return pl.pallas_call(
        body, name="grad_exchange", out_shape=out_shape,
        in_specs=[ANY] * n, out_specs=[ANY] * n, scratch_shapes=_comm_sems(n),
    )(*xs)


def mm(a, b, *, ta=False, tb=False, add=None, out_dtype=F32, tm=512, tn=512, tk=512, name="mm"):
    K, M = a.shape if ta else a.shape[::-1]
    N = b.shape[0] if tb else b.shape[1]
    assert (b.shape[1] if tb else b.shape[0]) == K and not (ta and tb)
    tm, tn, tk = _tile(M, tm), _tile(N, tn), _tile(K, tk)
    nk = K // tk
    has_add = add is not None

    def body(*refs):
        if has_add:
            a_ref, b_ref, add_ref, o_ref, acc = refs
        else:
            a_ref, b_ref, o_ref, acc = refs
        k = pl.program_id(2)

        @pl.when(k == 0)
        def _():
            acc[...] = jnp.zeros_like(acc)

        av = a_ref[...].astype(BF)
        bv = b_ref[...].astype(BF)
        if ta:
            acc[...] += _dot_tn(av, bv)
        elif tb:
            acc[...] += _dot_nt(av, bv)
        else:
            acc[...] += _dot(av, bv)

        @pl.when(k == nk - 1)
        def _():
            r = acc[...]
            if has_add:
                r = r + add_ref[...]
            o_ref[...] = r.astype(out_dtype)

    a_spec = pl.BlockSpec((tk, tm), lambda i, j, k: (k, i)) if ta else pl.BlockSpec((tm, tk), lambda i, j, k: (i, k))
    b_spec = pl.BlockSpec((tn, tk), lambda i, j, k: (j, k)) if tb else pl.BlockSpec((tk, tn), lambda i, j, k: (k, j))
    o_spec = pl.BlockSpec((tm, tn), lambda i, j, k: (i, j))
    in_specs = [a_spec, b_spec] + ([o_spec] if has_add else [])
    args = [a, b] + ([add] if has_add else [])
    return pl.pallas_call(
        body, name=name, grid=(M // tm, N // tn, nk),
        in_specs=in_specs, out_specs=o_spec,
        out_shape=jax.ShapeDtypeStruct((M, N), out_dtype),
        scratch_shapes=[pltpu.VMEM((tm, tn), F32)],
        compiler_params=_params(("parallel", "parallel", "arbitrary")),
    )(*args)


def norm_mm(x, gain, w, *, out_dtype, tn=512, rope=None, write_h=True, tm=512, name="norm_mm"):
    S, K = x.shape
    N = w.shape[1]
    tm, tn = _tile(S, tm), _tile(N, tn)
    if rope is not None:
        assert tn == LANES
    gain = gain.reshape(1, K)

    def body(*refs):
        refs = list(refs)
        x_ref, g_ref, w_ref = refs[:3]
        refs = refs[3:]
        if rope is not None:
            cos_ref, sin_ref = refs[:2]
            refs = refs[2:]
        y_ref = refs[0]
        h_ref = refs[1] if write_h else None
        hs = refs[-1]
        j = pl.program_id(1)

        @pl.when(j == 0)
        def _():
            h, _ = _rms(x_ref[...], g_ref[...])
            hb = h.astype(BF)
            hs[...] = hb
            if write_h:
                h_ref[...] = hb

        y = _dot(hs[...], w_ref[...])
        if rope is not None:
            y = _rope(y, cos_ref[...], sin_ref[...])
        y_ref[...] = y.astype(out_dtype)

    in_specs = [pl.BlockSpec((tm, K), lambda i, j: (i, 0)), pl.BlockSpec((1, K), lambda i, j: (0, 0)),
                pl.BlockSpec((K, tn), lambda i, j: (0, j))]
    args = [x, gain, w]
    if rope is not None:
        in_specs += [pl.BlockSpec((tm, LANES), lambda i, j: (i, 0))] * 2
        args += list(rope)
    out_specs = [pl.BlockSpec((tm, tn), lambda i, j: (i, j))]
    out_shape = [jax.ShapeDtypeStruct((S, N), out_dtype)]
    if write_h:
        out_specs.append(pl.BlockSpec((tm, K), lambda i, j: (i, 0)))
        out_shape.append(jax.ShapeDtypeStruct((S, K), BF))
    res = pl.pallas_call(
        body, name=name, grid=(S // tm, N // tn), in_specs=in_specs, out_specs=out_specs, out_shape=out_shape,
        scratch_shapes=[pltpu.VMEM((tm, K), BF)],
        compiler_params=_params(("parallel", "arbitrary")),
    )(*args)
    return res if write_h else res[0]


def norm_bwd(x, gain, dh, dres=None, *, tm=512, name="norm_bwd"):
    S, K = x.shape
    tm = _tile(S, tm)
    gain = gain.reshape(1, K)
    has_res = dres is not None

    def body(*refs):
        if has_res:
            x_ref, g_ref, dh_ref, dr_ref, dx_ref, dg_ref = refs
        else:
            x_ref, g_ref, dh_ref, dx_ref, dg_ref = refs

        @pl.when(pl.program_id(0) == 0)
        def _():
            dg_ref[...] = jnp.zeros_like(dg_ref)

        dx, dg = _rms_bwd(x_ref[...], g_ref[...], dh_ref[...].astype(F32))
        if has_res:
            dx = dx + dr_ref[...]
        dx_ref[...] = dx
        dg_ref[...] += jnp.broadcast_to(dg, dg_ref.shape)

    row = pl.BlockSpec((tm, K), lambda i: (i, 0))
    in_specs = [row, pl.BlockSpec((1, K), lambda i: (0, 0)), row] + ([row] if has_res else [])
    args = [x, gain, dh] + ([dres] if has_res else [])
    dx, dg = pl.pallas_call(
        body, name=name, grid=(S // tm,), in_specs=in_specs,
        out_specs=[row, pl.BlockSpec((8, K), lambda i: (0, 0))],
        out_shape=[jax.ShapeDtypeStruct((S, K), F32), jax.ShapeDtypeStruct((8, K), F32)],
        compiler_params=_params(("arbitrary",)),
    )(*args)
    return dx, dg[0]


def loss_head(x, gain, target, *, tm=512):
    S, K = x.shape
    tm = _tile(S, tm)
    gain = gain.reshape(1, K)

    def body(x_ref, g_ref, t_ref, dx_ref, dg_ref, ls_ref):
        @pl.when(pl.program_id(0) == 0)
        def _():
            dg_ref[...] = jnp.zeros_like(dg_ref)
            ls_ref[...] = jnp.zeros_like(ls_ref)

        xv, g = x_ref[...], g_ref[...]
        y, _ = _rms(xv, g)
        e = y - t_ref[...]
        part = jnp.sum(jnp.mean(e * e, axis=-1, keepdims=True), axis=0, keepdims=True)
        ls_ref[...] += jnp.broadcast_to(0.5 * part, ls_ref.shape)
        dx, dg = _rms_bwd(xv, g, e / K)
        dx_ref[...] = dx
        dg_ref[...] += jnp.broadcast_to(dg, dg_ref.shape)

    row = pl.BlockSpec((tm, K), lambda i: (i, 0))
    dx, dg, ls = pl.pallas_call(
        body, name="loss_head", grid=(S // tm,),
        in_specs=[row, pl.BlockSpec((1, K), lambda i: (0, 0)), row],
        out_specs=[row, pl.BlockSpec((8, K), lambda i: (0, 0)), pl.BlockSpec((8, LANES), lambda i: (0, 0))],
        out_shape=[jax.ShapeDtypeStruct((S, K), F32), jax.ShapeDtypeStruct((8, K), F32),
                   jax.ShapeDtypeStruct((8, LANES), F32)],
        compiler_params=_params(("arbitrary",)),
    )(x, gain, target)
    return ls[0, 0], dx, dg[0]


def _once(shape, index_map):
    return pl.BlockSpec(shape, index_map, pipeline_mode=pl.Buffered(1))


def ffn_fwd(x, gain, wg, wu, wd, carry=(), *, tm=512):
    S, D = x.shape
    NB, _, Fs = wg.shape
    tm = _tile(S, tm)
    ni = S // tm
    nc = len(carry)
    gain = gain.reshape(1, D)

    def body(*refs):
        x_ref, g_ref, wg_ref, wu_ref, wd_ref = refs[:5]
        c_in = refs[5:5 + nc]
        xo_ref, h_ref, G_ref, U_ref = refs[5 + nc:9 + nc]
        c_out = refs[9 + nc:9 + 2 * nc]
        hs, acc = refs[9 + 2 * nc:11 + 2 * nc]
        i, j = pl.program_id(0), pl.program_id(1)
        if nc:
            start, middle, finish = _gather_phases(c_in, c_out, *refs[11 + 2 * nc:])
            pl.when((i == 0) & (j == 0))(start)
            pl.when((i == ni // 2) & (j == 0))(middle)

        @pl.when(j == 0)
        def _():
            h, _ = _rms(x_ref[...], g_ref[...])
            hb = h.astype(BF)
            hs[...] = hb
            h_ref[...] = hb
            acc[...] = jnp.zeros_like(acc)

        h = hs[...]
        g = _dot(h, wg_ref[...])
        u = _dot(h, wu_ref[...])
        G_ref[...] = g.astype(BF)
        U_ref[...] = u.astype(BF)
        a = (g * jax.nn.sigmoid(g) * u).astype(BF)
        acc[...] += _dot(a, wd_ref[...])

        @pl.when(j == NB - 1)
        def _():
            xo_ref[...] = x_ref[...] + 0.5 * acc[...]

        if nc:
            pl.when((i == ni - 1) & (j == NB - 1))(finish)

    row = lambda i, j: (i, 0)
    blk = lambda i, j: (j, 0, 0)
    hid = pl.BlockSpec((None, tm, Fs), lambda i, j: (j, i, 0))
    return pl.pallas_call(
        body, name="ffn_fwd_gather" if nc else "ffn_fwd", grid=(ni, NB),
        in_specs=[_once((tm, D), row), pl.BlockSpec((1, D), lambda i, j: (0, 0)),
                  pl.BlockSpec((None, D, Fs), blk), pl.BlockSpec((None, D, Fs), blk), pl.BlockSpec((None, Fs, D), blk)]
                 + [ANY] * nc,
        out_specs=[_once((tm, D), row), _once((tm, D), row), hid, hid] + [ANY] * nc,
        out_shape=[jax.ShapeDtypeStruct((S, D), F32), jax.ShapeDtypeStruct((S, D), BF),
                   jax.ShapeDtypeStruct((NB, S, Fs), BF), jax.ShapeDtypeStruct((NB, S, Fs), BF)] + _gathered(carry),
        scratch_shapes=[pltpu.VMEM((tm, D), BF), pltpu.VMEM((tm, D), F32)] + (_comm_sems(nc) if nc else []),
        compiler_params=_params(("arbitrary", "arbitrary")),
    )(x, gain, wg, wu, wd, *carry)


def ffn_bwd_x(dy, x, gain, G, U, wg, wu, wd, carry=(), *, tm=256):
    S, D = x.shape
    NB, _, Fs = wg.shape
    tm = _tile(S, tm)
    ni = S // tm
    nc = len(carry)
    gain = gain.reshape(1, D)

    def body(*refs):
        dy_ref, x_ref, g_ref, G_ref, U_ref, wg_ref, wu_ref, wd_ref = refs[:8]
        c_in = refs[8:8 + nc]
        dx_ref, do_ref, dG_ref, dU_ref, dg_ref = refs[8 + nc:13 + nc]
        c_out = refs[13 + nc:13 + 2 * nc]
        dob, acc = refs[13 + 2 * nc:15 + 2 * nc]
        i, j = pl.program_id(0), pl.program_id(1)
        if nc:
            start, finish = _exchange_phases(c_in, c_out, [False] * nc, *refs[15 + 2 * nc:])
            pl.when((i == 0) & (j == 0))(start)

        @pl.when(j == 0)
        def _():
            d = (0.5 * dy_ref[...]).astype(BF)
            dob[...] = d
            do_ref[...] = d
            acc[...] = jnp.zeros_like(acc)

        @pl.when((i == 0) & (j == 0))
        def _():
            dg_ref[...] = jnp.zeros_like(dg_ref)

        dA = _dot_nt(dob[...], wd_ref[...])
        g = G_ref[...].astype(F32)
        u = U_ref[...].astype(F32)
        sig = jax.nn.sigmoid(g)
        dG = (dA * u * (sig * (1.0 + g * (1.0 - sig)))).astype(BF)
        dU = (dA * (g * sig)).astype(BF)
        dG_ref[...] = dG
        dU_ref[...] = dU
        acc[...] += _dot_nt(dG, wg_ref[...]) + _dot_nt(dU, wu_ref[...])

        @pl.when(j == NB - 1)
        def _():
            dxn, dg = _rms_bwd(x_ref[...], g_ref[...], acc[...])
            dx_ref[...] = dy_ref[...] + dxn
            dg_ref[...] += jnp.broadcast_to(dg, dg_ref.shape)

        if nc:
            pl.when((i == ni - 1) & (j == NB - 1))(finish)

    row = lambda i, j: (i, 0)
    blk = lambda i, j: (j, 0, 0)
    hid = pl.BlockSpec((None, tm, Fs), lambda i, j: (j, i, 0))
    dx, dout, dG, dU, dg, *got = pl.pallas_call(
        body, name="ffn_bwd_x_exchange" if nc else "ffn_bwd_x", grid=(ni, NB),
        in_specs=[_once((tm, D), row), _once((tm, D), row), pl.BlockSpec((1, D), lambda i, j: (0, 0)), hid, hid,
                  pl.BlockSpec((None, D, Fs), blk), pl.BlockSpec((None, D, Fs), blk), pl.BlockSpec((None, Fs, D), blk)]
                 + [ANY] * nc,
        out_specs=[_once((tm, D), row), _once((tm, D), row), hid, hid, pl.BlockSpec((8, D), lambda i, j: (0, 0))]
                  + [ANY] * nc,
        out_shape=[jax.ShapeDtypeStruct((S, D), F32), jax.ShapeDtypeStruct((S, D), BF),
                   jax.ShapeDtypeStruct((NB, S, Fs), BF), jax.ShapeDtypeStruct((NB, S, Fs), BF),
                   jax.ShapeDtypeStruct((8, D), F32)] + [jax.ShapeDtypeStruct(a.shape, a.dtype) for a in carry],
        scratch_shapes=[pltpu.VMEM((tm, D), BF), pltpu.VMEM((tm, D), F32)] + (_comm_sems(nc) if nc else []),
        compiler_params=_params(("arbitrary", "arbitrary")),
    )(dy, x, gain, G, U, wg, wu, wd, *carry)
    return dx, dout, dG, dU, dg[0], got


def ffn_bwd_w(h, dout, G, U, dG, dU, *, tm=256):
    S, D = h.shape
    NB, _, Fs = G.shape
    tm = _tile(S, tm)
    ni = S // tm

    def body(h_ref, do_ref, G_ref, U_ref, dG_ref, dU_ref, wg_ref, wu_ref, wd_ref, ag, au, ad):
        i = pl.program_id(1)

        @pl.when(i == 0)
        def _():
            ag[...] = jnp.zeros_like(ag)
            au[...] = jnp.zeros_like(au)
            ad[...] = jnp.zeros_like(ad)

        h = h_ref[...]
        ag[...] += _dot_tn(h, dG_ref[...])
        au[...] += _dot_tn(h, dU_ref[...])
        g = G_ref[...].astype(F32)
        a = (g * jax.nn.sigmoid(g) * U_ref[...].astype(F32)).astype(BF)
        ad[...] += _dot_tn(a, do_ref[...])

        @pl.when(i == ni - 1)
        def _():
            wg_ref[...] = ag[...].astype(BF)
            wu_ref[...] = au[...].astype(BF)
            wd_ref[...] = ad[...].astype(BF)

    row = pl.BlockSpec((tm, D), lambda j, i: (i, 0))
    hid = pl.BlockSpec((None, tm, Fs), lambda j, i: (j, i, 0))
    blk = lambda j, i: (j, 0, 0)
    return pl.pallas_call(
        body, name="ffn_bwd_w", grid=(NB, ni),
        in_specs=[row, row, hid, hid, hid, hid],
        out_specs=[pl.BlockSpec((None, D, Fs), blk), pl.BlockSpec((None, D, Fs), blk), pl.BlockSpec((None, Fs, D), blk)],
        out_shape=[jax.ShapeDtypeStruct((NB, D, Fs), BF), jax.ShapeDtypeStruct((NB, D, Fs), BF),
                   jax.ShapeDtypeStruct((NB, Fs, D), BF)],
        scratch_shapes=[pltpu.VMEM((D, Fs), F32), pltpu.VMEM((D, Fs), F32), pltpu.VMEM((Fs, D), F32)],
        compiler_params=_params(("parallel", "arbitrary")),
    )(h, dout, G, U, dG, dU)


def dilated_bias(H, d):
    n = DIL_N
    slopes = 2.0 ** (-8.0 * (np.arange(H) + 1) / H)
    i = np.arange(n)[:, None]
    j = np.arange(2 * n)[None, :]
    steps = n + i - j
    band = (steps >= 0) & (steps <= n)
    first = band & (j >= n)
    bias = -slopes[:, None, None] * (d * steps).astype(np.float64)[None]
    out = np.stack([np.where(band[None], bias, NEG), np.where(first[None], bias, NEG)], axis=1)
    return jnp.asarray(out, dtype=F32)


def dil_fwd(qkv, bias, d, H):
    S = qkv.shape[0]
    Sd = S // d
    TQ = _tile(Sd, 512)
    nsub, nc = TQ // DIL_N, Sd // TQ
    scale = HEAD ** -0.5
    qv = qkv.reshape(Sd, d * 3 * H * HEAD)

    def body(q_ref, kc_ref, kp_ref, vc_ref, vp_ref, b_ref, o_ref, l_ref):
        c = pl.program_id(2)
        for i in range(nsub):
            sl = slice(i * DIL_N, (i + 1) * DIL_N)
            if i == 0:
                k2 = jnp.concatenate([kp_ref[...], kc_ref[0:DIL_N, :]], axis=0)
                v2 = jnp.concatenate([vp_ref[...], vc_ref[0:DIL_N, :]], axis=0)
                b = jnp.where(c == 0, b_ref[1], b_ref[0])
            else:
                k2 = kc_ref[(i - 1) * DIL_N:(i + 1) * DIL_N, :]
                v2 = vc_ref[(i - 1) * DIL_N:(i + 1) * DIL_N, :]
                b = b_ref[0]
            s = _dot_nt(q_ref[sl, :], k2) * scale + b
            m = jnp.max(s, axis=1, keepdims=True)
            p = jnp.exp(s - m)
            l = jnp.sum(p, axis=1, keepdims=True)
            o_ref[sl, :] = _dot(p.astype(BF), v2) / l
            l_ref[sl, :] = jnp.broadcast_to(m + jnp.log(l), (DIL_N, LANES))

    W3 = 3 * H
    chunk = lambda off: pl.BlockSpec((TQ, HEAD), lambda h, r, c: (c, r * W3 + off * H + h))
    halo = lambda off: pl.BlockSpec((DIL_N, HEAD), lambda h, r, c: (jnp.maximum(c * nsub - 1, 0), r * W3 + off * H + h))
    out = pl.BlockSpec((TQ, HEAD), lambda h, r, c: (c, r * H + h))
    o, l = pl.pallas_call(
        body, name=f"dil_fwd_d{d}", grid=(H, d, nc),
        in_specs=[chunk(0), chunk(1), halo(1), chunk(2), halo(2),
                  pl.BlockSpec((None, 2, DIL_N, 2 * DIL_N), lambda h, r, c: (h, 0, 0, 0))],
        out_specs=[out, out],
        out_shape=[jax.ShapeDtypeStruct((Sd, d * H * HEAD), F32)] * 2,
        compiler_params=_params(("parallel", "parallel", "arbitrary")),
    )(qv, qv, qv, qv, qv, bias)
    return o.reshape(S, H * HEAD), l.reshape(S, H * HEAD)


def dil_combine(os, ls, *, tm=256):
    S, W = os[0].shape
    tm = _tile(S, tm)

    def body(o1, o2, o3, l1, l2, l3, o_ref, L_ref):
        a, b, c = l1[...], l2[...], l3[...]
        m = jnp.maximum(jnp.maximum(a, b), c)
        ea, eb, ec = jnp.exp(a - m), jnp.exp(b - m), jnp.exp(c - m)
        z = ea + eb + ec
        o_ref[...] = ((ea * o1[...] + eb * o2[...] + ec * o3[...]) / z).astype(BF)
        L_ref[...] = m + jnp.log(z)

    row = pl.BlockSpec((tm, W), lambda i: (i, 0))
    return pl.pallas_call(
        body, name="dil_combine", grid=(S // tm,), in_specs=[row] * 6, out_specs=[row, row],
        out_shape=[jax.ShapeDtypeStruct((S, W), BF), jax.ShapeDtypeStruct((S, W), F32)],
        compiler_params=_params(("parallel",)),
    )(*os, *ls)


def head_delta(do, o, *, tm=512):
    S, W = o.shape
    tm = _tile(S, tm)

    def body(do_ref, o_ref, d_ref):
        s = jnp.sum(do_ref[...].astype(F32) * o_ref[...].astype(F32), axis=1, keepdims=True)
        d_ref[...] = jnp.broadcast_to(s, d_ref.shape)

    blk = pl.BlockSpec((tm, HEAD), lambda i, h: (i, h))
    return pl.pallas_call(
        body, name="head_delta", grid=(S // tm, W // HEAD), in_specs=[blk, blk], out_specs=blk,
        out_shape=jax.ShapeDtypeStruct((S, W), F32),
        compiler_params=_params(("parallel", "parallel")),
    )(do, o)


def dil_bwd(qkv, do, L, delta, bias, d, H, prev=None):
    S = qkv.shape[0]
    Sd = S // d
    TQ = _tile(Sd, 512)
    nsub, nc = TQ // DIL_N, Sd // TQ
    nblk = Sd // DIL_N
    scale = HEAD ** -0.5
    W3 = 3 * H
    qv = qkv.reshape(Sd, d * W3 * HEAD)
    dov, Lv, dlv = (t.reshape(Sd, d * H * HEAD) for t in (do, L, delta))
    has_prev = prev is not None

    def body(*refs):
        (qc, qn, kc, kp, vc, vp, doc, don, Lc, Ln, dlc, dln, b_ref) = refs[:13]
        refs = refs[13:]
        if has_prev:
            dqi, dki, dvi = refs[:3]
            refs = refs[3:]
        dqo, dko, dvo, dk_acc, dv_acc = refs
        c = pl.program_id(2)
        dk_acc[...] = jnp.zeros_like(dk_acc)
        dv_acc[...] = jnp.zeros_like(dv_acc)
        for i in range(nsub):
            sl = slice(i * DIL_N, (i + 1) * DIL_N)
            if i == 0:
                k2 = jnp.concatenate([kp[...], kc[0:DIL_N, :]], axis=0)
                v2 = jnp.concatenate([vp[...], vc[0:DIL_N, :]], axis=0)
                b = jnp.where(c == 0, b_ref[1], b_ref[0])
            else:
                k2 = kc[(i - 1) * DIL_N:(i + 1) * DIL_N, :]
                v2 = vc[(i - 1) * DIL_N:(i + 1) * DIL_N, :]
                b = b_ref[0]
            q, dov_ = qc[sl, :], doc[sl, :]
            Lq, dl = Lc[sl, :], dlc[sl, :]
            s = _dot_nt(q, k2) * scale + b
            p = jnp.exp(s - jnp.concatenate([Lq, Lq], axis=1))
            dp = _dot_nt(dov_, v2)
            ds = p * (dp - jnp.concatenate([dl, dl], axis=1))
            dsb, pb = ds.astype(BF), p.astype(BF)
            dq = _dot(dsb, k2) * scale
            dqo[sl, :] = dq + dqi[sl, :] if has_prev else dq
            dk2 = _dot_tn(dsb, q) * scale
            dv2 = _dot_tn(pb, dov_)
            if i >= 1:
                pv = slice((i - 1) * DIL_N, i * DIL_N)
                dk_acc[pv, :] += dk2[:DIL_N]
                dv_acc[pv, :] += dv2[:DIL_N]
            dk_acc[sl, :] += dk2[DIL_N:]
            dv_acc[sl, :] += dv2[DIL_N:]

        @pl.when(c < nc - 1)
        def _():
            last = slice((nsub - 1) * DIL_N, nsub * DIL_N)
            q, dov_ = qn[...], don[...]
            k1, v1 = kc[last, :], vc[last, :]
            s = _dot_nt(q, k1) * scale + b_ref[0][:, :DIL_N]
            p = jnp.exp(s - Ln[...])
            ds = p * (_dot_nt(dov_, v1) - dln[...])
            dk_acc[last, :] += _dot_tn(ds.astype(BF), q) * scale
            dv_acc[last, :] += _dot_tn(p.astype(BF), dov_)

        if has_prev:
            dko[...] = dk_acc[...] + dki[...]
            dvo[...] = dv_acc[...] + dvi[...]
        else:
            dko[...] = dk_acc[...]
            dvo[...] = dv_acc[...]

    nxt = lambda c: jnp.minimum((c + 1) * nsub, nblk - 1)
    prv = lambda c: jnp.maximum(c * nsub - 1, 0)
    chunk3 = lambda off: pl.BlockSpec((TQ, HEAD), lambda h, r, c: (c, r * W3 + off * H + h))
    halo3 = lambda off, f: pl.BlockSpec((DIL_N, HEAD), lambda h, r, c: (f(c), r * W3 + off * H + h))
    chunk1 = pl.BlockSpec((TQ, HEAD), lambda h, r, c: (c, r * H + h))
    next1 = pl.BlockSpec((DIL_N, HEAD), lambda h, r, c: (nxt(c), r * H + h))
    in_specs = [chunk3(0), halo3(0, nxt), chunk3(1), halo3(1, prv), chunk3(2), halo3(2, prv),
                chunk1, next1, chunk1, next1, chunk1, next1,
                pl.BlockSpec((None, 2, DIL_N, 2 * DIL_N), lambda h, r, c: (h, 0, 0, 0))]
    args = [qv] * 6 + [dov, dov, Lv, Lv, dlv, dlv, bias]
    if has_prev:
        in_specs += [chunk1] * 3
        args += [t.reshape(Sd, d * H * HEAD) for t in prev]
    outs = pl.pallas_call(
        body, name=f"dil_bwd_d{d}", grid=(H, d, nc), in_specs=in_specs, out_specs=[chunk1] * 3,
        out_shape=[jax.ShapeDtypeStruct((Sd, d * H * HEAD), F32)] * 3,
        scratch_shapes=[pltpu.VMEM((TQ, HEAD), F32), pltpu.VMEM((TQ, HEAD), F32)],
        compiler_params=_params(("parallel", "parallel", "arbitrary")),
    )(*args)
    return tuple(t.reshape(S, H * HEAD) for t in outs)


def _causal_mask(s, qi, ki, tq, tk, row0=0):
    qpos = qi * tq + row0 + lax.broadcasted_iota(jnp.int32, s.shape, 0)
    kpos = ki * tk + lax.broadcasted_iota(jnp.int32, s.shape, 1)
    return jnp.where(kpos <= qpos, s, NEG)


def mla_fwd(qn, qr, kv, kr, H, *, tq=1024, tk=1024, sub=512):
    S = qn.shape[0]
    tq, tk = _tile(S, tq), _tile(S, tk)
    sub = _tile(tq, sub)
    nk = S // tk
    scale = MLA_SCALE
    c2 = scale * LOG2E

    def body(qn_ref, qr_ref, kn_ref, kr_ref, v_ref, o_ref, L_ref, m_s, l_s, acc):
        qi, ki = pl.program_id(1), pl.program_id(2)

        @pl.when(ki == 0)
        def _():
            m_s[...] = jnp.full_like(m_s, NEG)
            l_s[...] = jnp.zeros_like(l_s)
            acc[...] = jnp.zeros_like(acc)

        def step(masked):
            k = jnp.concatenate([kn_ref[...], kr_ref[...]], axis=1)
            v = v_ref[...]
            for r in range(tq // sub):
                rows = slice(r * sub, (r + 1) * sub)
                q = jnp.concatenate([qn_ref[rows, :], qr_ref[rows, :]], axis=1)
                s = _dot_nt(q, k)
                if masked:
                    s = _causal_mask(s, qi, ki, tq, tk, r * sub)
                m_prev = m_s[rows, :]
                m_new = jnp.maximum(m_prev, jnp.max(s, axis=1, keepdims=True))
                alpha = jnp.exp2((m_prev - m_new) * c2)
                p = jnp.exp2((s - jnp.tile(m_new, (1, tk // LANES))) * c2)
                l_s[rows, :] = alpha * l_s[rows, :] + jnp.sum(p, axis=1, keepdims=True)
                acc[rows, :] = alpha * acc[rows, :] + _dot(p.astype(BF), v)
                m_s[rows, :] = m_new

        full = ki * tk + tk - 1 <= qi * tq
        live = ki * tk <= qi * tq + tq - 1
        pl.when(full)(lambda: step(False))
        pl.when(live & jnp.logical_not(full))(lambda: step(True))

        @pl.when(ki == nk - 1)
        def _():
            o_ref[...] = (acc[...] / l_s[...]).astype(BF)
            L_ref[...] = m_s[...] * scale + jnp.log(l_s[...])

    kcl = lambda qi, ki: jnp.minimum(ki, (qi * tq + tq - 1) // tk)
    qs = pl.BlockSpec((tq, HEAD), lambda h, qi, ki: (qi, h))
    return pl.pallas_call(
        body, name="mla_fwd", grid=(H, S // tq, nk),
        in_specs=[qs, qs, pl.BlockSpec((tk, HEAD), lambda h, qi, ki: (kcl(qi, ki), h)),
                  pl.BlockSpec((tk, LANES), lambda h, qi, ki: (kcl(qi, ki), 0)),
                  pl.BlockSpec((tk, HEAD), lambda h, qi, ki: (kcl(qi, ki), H + h))],
        out_specs=[qs, qs],
        out_shape=[jax.ShapeDtypeStruct((S, H * HEAD), BF), jax.ShapeDtypeStruct((S, H * HEAD), F32)],
        scratch_shapes=[pltpu.VMEM((tq, LANES), F32), pltpu.VMEM((tq, LANES), F32), pltpu.VMEM((tq, HEAD), F32)],
        compiler_params=_params(("parallel", "parallel", "arbitrary")),
    )(qn, qr, kv, kr, kv)


def _mla_probs(s, L_rows, c2, width):
    return jnp.exp2(s * c2 - jnp.tile(L_rows * LOG2E, (1, width // LANES)))


def mla_bwd_q(qn, qr, kv, kr, do, L, delta, rope, H, *, tq=1024, tk=1024, sub=256):
    S = qn.shape[0]
    tq, tk = _tile(S, tq), _tile(S, tk)
    sub = _tile(tq, sub)
    nk = S // tk
    scale = MLA_SCALE
    c2 = scale * LOG2E

    def body(qn_ref, qr_ref, kn_ref, kr_ref, v_ref, do_ref, L_ref, dl_ref, cos_ref, sin_ref, dqn_ref, dqr_ref, acc):
        qi, ki = pl.program_id(1), pl.program_id(2)

        @pl.when(ki == 0)
        def _():
            acc[...] = jnp.zeros_like(acc)

        def step(masked):
            k = jnp.concatenate([kn_ref[...], kr_ref[...]], axis=1)
            v = v_ref[...]
            for r in range(tq // sub):
                rows = slice(r * sub, (r + 1) * sub)
                q = jnp.concatenate([qn_ref[rows, :], qr_ref[rows, :]], axis=1)
                s = _dot_nt(q, k)
                if masked:
                    s = _causal_mask(s, qi, ki, tq, tk, r * sub)
                p = _mla_probs(s, L_ref[rows, :], c2, tk)
                dp = _dot_nt(do_ref[rows, :], v)
                ds = (p * (dp - jnp.tile(dl_ref[rows, :], (1, tk // LANES)))).astype(BF)
                acc[rows, :] += _dot(ds, k)

        full = ki * tk + tk - 1 <= qi * tq
        live = ki * tk <= qi * tq + tq - 1
        pl.when(full)(lambda: step(False))
        pl.when(live & jnp.logical_not(full))(lambda: step(True))

        @pl.when(ki == nk - 1)
        def _():
            dq = acc[...] * scale
            dqn_ref[...] = dq[:, :HEAD].astype(BF)
            dqr_ref[...] = _rope_bwd(dq[:, HEAD:], cos_ref[...], sin_ref[...]).astype(BF)

    kcl = lambda qi, ki: jnp.minimum(ki, (qi * tq + tq - 1) // tk)
    qs = pl.BlockSpec((tq, HEAD), lambda h, qi, ki: (qi, h))
    tab = pl.BlockSpec((tq, LANES), lambda h, qi, ki: (qi, 0))
    return pl.pallas_call(
        body, name="mla_bwd_q", grid=(H, S // tq, nk),
        in_specs=[qs, qs, pl.BlockSpec((tk, HEAD), lambda h, qi, ki: (kcl(qi, ki), h)),
                  pl.BlockSpec((tk, LANES), lambda h, qi, ki: (kcl(qi, ki), 0)),
                  pl.BlockSpec((tk, HEAD), lambda h, qi, ki: (kcl(qi, ki), H + h)),
                  qs, qs, qs, tab, tab],
        out_specs=[qs, qs],
        out_shape=[jax.ShapeDtypeStruct((S, H * HEAD), BF)] * 2,
        scratch_shapes=[pltpu.VMEM((tq, 2 * HEAD), F32)],
        compiler_params=_params(("parallel", "parallel", "arbitrary")),
    )(qn, qr, kv, kr, kv, do, L, delta, *rope)


def mla_bwd_kv(qn, qr, kv, kr, do, L, delta, rope, H, prev=None, *, tq=1024, tk=1024, sub=256):
    S = qn.shape[0]
    tq, tk = _tile(S, tq), _tile(S, tk)
    sub = _tile(tq, sub)
    nq = S // tq
    scale = MLA_SCALE
    c2 = scale * LOG2E
    has_prev = prev is not None

    def body(*refs):
        (qn_ref, qr_ref, kn_ref, kr_ref, v_ref, do_ref, L_ref, dl_ref, cos_ref, sin_ref) = refs[:10]
        refs = refs[10:]
        if has_prev:
            pk_ref, pv_ref, pr_ref = refs[:3]
            refs = refs[3:]
        dk_ref, dv_ref, dr_ref, dk_acc, dv_acc, dr_acc = refs
        ki, h, qi = pl.program_id(0), pl.program_id(1), pl.program_id(2)

        @pl.when(qi == 0)
        def _():
            dk_acc[...] = jnp.zeros_like(dk_acc)
            dv_acc[...] = jnp.zeros_like(dv_acc)

        @pl.when((qi == 0) & (h == 0))
        def _():
            dr_acc[...] = jnp.zeros_like(dr_acc)

        def step(masked):
            k = jnp.concatenate([kn_ref[...], kr_ref[...]], axis=1)
            v = v_ref[...]
            for r in range(tq // sub):
                rows = slice(r * sub, (r + 1) * sub)
                q = jnp.concatenate([qn_ref[rows, :], qr_ref[rows, :]], axis=1)
                s = _dot_nt(q, k)
                if masked:
                    s = _causal_mask(s, qi, ki, tq, tk, r * sub)
                p = _mla_probs(s, L_ref[rows, :], c2, tk)
                dov = do_ref[rows, :]
                dv_acc[...] += _dot_tn(p.astype(BF), dov)
                dp = _dot_nt(dov, v)
                ds = (p * (dp - jnp.tile(dl_ref[rows, :], (1, tk // LANES)))).astype(BF)
                dk_acc[...] += _dot_tn(ds, q)

        full = ki * tk + tk - 1 <= qi * tq
        live = ki * tk <= qi * tq + tq - 1
        pl.when(full)(lambda: step(False))
        pl.when(live & jnp.logical_not(full))(lambda: step(True))

        @pl.when(qi == nq - 1)
        def _():
            dk = dk_acc[...] * scale
            dkn, dv = dk[:, :HEAD], dv_acc[...]
            if has_prev:
                dkn, dv = dkn + pk_ref[...], dv + pv_ref[...]
            dk_ref[...] = dkn
            dv_ref[...] = dv
            dr_acc[...] += dk[:, HEAD:]

        @pl.when((qi == nq - 1) & (h == H - 1))
        def _():
            dr = _rope_bwd(dr_acc[...], cos_ref[...], sin_ref[...])
            dr_ref[...] = dr + pr_ref[...] if has_prev else dr

    qcl = lambda ki, qi: jnp.maximum(qi, (ki * tk) // tq)
    qs = pl.BlockSpec((tq, HEAD), lambda ki, h, qi: (qcl(ki, qi), h))
    kn = pl.BlockSpec((tk, HEAD), lambda ki, h, qi: (ki, h))
    vs = pl.BlockSpec((tk, HEAD), lambda ki, h, qi: (ki, H + h))
    k1 = pl.BlockSpec((tk, LANES), lambda ki, h, qi: (ki, 0))
    in_specs = [qs, qs, kn, k1, vs, qs, qs, qs, k1, k1]
    args = [qn, qr, kv, kr, kv, do, L, delta, *rope]
    if has_prev:
        in_specs += [kn, vs, k1]
        args += [prev[0], prev[0], prev[1]]
    dkn, dv, dr = pl.pallas_call(
        body, name="mla_bwd_kv", grid=(S // tk, H, nq), in_specs=in_specs,
        out_specs=[kn, kn, k1],
        out_shape=[jax.ShapeDtypeStruct((S, H * HEAD), F32), jax.ShapeDtypeStruct((S, H * HEAD), F32),
                   jax.ShapeDtypeStruct((S, LANES), F32)],
        scratch_shapes=[pltpu.VMEM((tk, 2 * HEAD), F32), pltpu.VMEM((tk, HEAD), F32), pltpu.VMEM((tk, LANES), F32)],
        compiler_params=_params(("parallel", "arbitrary", "arbitrary")),
    )(*args)
    return dkn, dv, dr


def mla_bwd(qn, qr, kv, kr, do, L, delta, rope, H, prev=None, *, tq=1024, tk=1024, sub=512):
    S = qn.shape[0]
    tq, tk = _tile(S, tq), _tile(S, tk)
    sub = _tile(tq, sub)
    nq, nk = S // tq, S // tk
    scale = MLA_SCALE
    c2 = scale * LOG2E
    has_prev = prev is not None

    def body(*refs):
        (qn_ref, qr_ref, kn_ref, kr_ref, v_ref, do_ref, L_ref, dl_ref, cq_ref, sq_ref, ck_ref, sk_ref) = refs[:12]
        refs = refs[12:]
        if has_prev:
            pk_ref, pv_ref, pr_ref = refs[:3]
            refs = refs[3:]
        dqn_ref, dqr_ref, dk_ref, dv_ref, dr_ref, dq_full, dk_acc, dv_acc = refs
        h, ki, qi = pl.program_id(0), pl.program_id(1), pl.program_id(2)
        qrows = pl.ds(pl.multiple_of(qi * tq, tq), tq)
        krows = pl.ds(pl.multiple_of(ki * tk, tk), tk)

        @pl.when(qi == 0)
        def _():
            dk_acc[...] = jnp.zeros_like(dk_acc)
            dv_acc[...] = jnp.zeros_like(dv_acc)

        @pl.when((qi == 0) & (h == 0))
        def _():
            dr_ref[krows, :] = jnp.zeros((tk, LANES), F32)

        @pl.when(ki == 0)
        def _():
            dq_full[qrows, :] = jnp.zeros((tq, 2 * HEAD), F32)

        def step(masked):
            k = jnp.concatenate([kn_ref[...], kr_ref[...]], axis=1)
            v = v_ref[...]
            for r in range(tq // sub):
                rows = slice(r * sub, (r + 1) * sub)
                q = jnp.concatenate([qn_ref[rows, :], qr_ref[rows, :]], axis=1)
                s = _dot_nt(q, k)
                if masked:
                    s = _causal_mask(s, qi, ki, tq, tk, r * sub)
                p = _mla_probs(s, L_ref[rows, :], c2, tk)
                dov = do_ref[rows, :]
                dv_acc[...] += _dot_tn(p.astype(BF), dov)
                dp = _dot_nt(dov, v)
                ds = (p * (dp - jnp.tile(dl_ref[rows, :], (1, tk // LANES)))).astype(BF)
                dk_acc[...] += _dot_tn(ds, q)
                dq_full[pl.ds(pl.multiple_of(qi * tq + r * sub, sub), sub), :] += _dot(ds, k)

        full = ki * tk + tk - 1 <= qi * tq
        live = ki * tk <= qi * tq + tq - 1
        pl.when(full)(lambda: step(False))
        pl.when(live & jnp.logical_not(full))(lambda: step(True))

        @pl.when(ki == (qi * tq + tq - 1) // tk)
        def _():
            dq = dq_full[qrows, :] * scale
            dqn_ref[qrows, :] = dq[:, :HEAD].astype(BF)
            dqr_ref[qrows, :] = _rope_bwd(dq[:, HEAD:], cq_ref[...], sq_ref[...]).astype(BF)

        @pl.when(qi == nq - 1)
        def _():
            dk = dk_acc[...] * scale
            dkn, dv = dk[:, :HEAD], dv_acc[...]
            if has_prev:
                dkn, dv = dkn + pk_ref[...], dv + pv_ref[...]
            dk_ref[...] = dkn
            dv_ref[...] = dv
            dr_ref[krows, :] += dk[:, HEAD:]

        @pl.when((qi == nq - 1) & (h == H - 1))
        def _():
            dr = _rope_bwd(dr_ref[krows, :], ck_ref[...], sk_ref[...])
            dr_ref[krows, :] = dr + pr_ref[...] if has_prev else dr

    qcl = lambda ki, qi: jnp.maximum(qi, (ki * tk) // tq)
    qs = pl.BlockSpec((tq, HEAD), lambda h, ki, qi: (qcl(ki, qi), h))
    qt = pl.BlockSpec((tq, LANES), lambda h, ki, qi: (qcl(ki, qi), 0))
    kn = pl.BlockSpec((tk, HEAD), lambda h, ki, qi: (ki, h))
    vs = pl.BlockSpec((tk, HEAD), lambda h, ki, qi: (ki, H + h))
    k1 = pl.BlockSpec((tk, LANES), lambda h, ki, qi: (ki, 0))
    head = pl.BlockSpec((S, HEAD), lambda h, ki, qi: (0, h))
    in_specs = [qs, qs, kn, k1, vs, qs, qs, qs, qt, qt, k1, k1]
    args = [qn, qr, kv, kr, kv, do, L, delta, *rope, *rope]
    if has_prev:
        in_specs += [kn, vs, k1]
        args += [prev[0], prev[0], prev[1]]
    return pl.pallas_call(
        body, name="mla_bwd", grid=(H, nk, nq), in_specs=in_specs,
        out_specs=[head, head, kn, kn, pl.BlockSpec((S, LANES), lambda h, ki, qi: (0, 0))],
        out_shape=[jax.ShapeDtypeStruct((S, H * HEAD), BF), jax.ShapeDtypeStruct((S, H * HEAD), BF),
                   jax.ShapeDtypeStruct((S, H * HEAD), F32), jax.ShapeDtypeStruct((S, H * HEAD), F32),
                   jax.ShapeDtypeStruct((S, LANES), F32)],
        scratch_shapes=[pltpu.VMEM((S, 2 * HEAD), F32), pltpu.VMEM((tk, 2 * HEAD), F32), pltpu.VMEM((tk, HEAD), F32)],
        compiler_params=_params(("arbitrary", "arbitrary", "arbitrary")),
    )(*args)


def adamw(recvs, w, m, v, *, name="adamw"):
    R, C = w.shape
    n = len(recvs)
    R1 = R // n
    assert all(r.shape == (N_DEV, R1, C) for r in recvs)
    tr = 16
    while tr * 2 * C * 44 <= 6 * 1024 * 1024 and R1 % (tr * 2) == 0:
        tr *= 2
    tr = min(tr, R1)
    assert R1 % tr == 0
    nb = R1 // tr

    def body(*refs):
        r_refs = refs[:n]
        w_ref, m_ref, v_ref, g_ref, d_ref, mo_ref, vo_ref = refs[n:]
        for slab in range(n):
            @pl.when(pl.program_id(0) == slab)
            def _(r_ref=r_refs[slab]):
                g = r_ref[0].astype(F32)
                for s in range(1, N_DEV):
                    g = g + r_ref[s].astype(F32)
                m2 = ADAM_B1 * m_ref[...] + (1.0 - ADAM_B1) * g
                v2 = ADAM_B2 * v_ref[...] + (1.0 - ADAM_B2) * (g * g)
                m_hat = m2 / (1.0 - ADAM_B1 ** ADAM_STEP)
                v_hat = v2 / (1.0 - ADAM_B2 ** ADAM_STEP)
                g_ref[...] = g
                d_ref[...] = -ADAM_LR * (m_hat / (jnp.sqrt(v_hat) + ADAM_EPS) + ADAM_WD * w_ref[...])
                mo_ref[...] = m2
                vo_ref[...] = v2

    def recv_spec(slab):
        return pl.BlockSpec((N_DEV, tr, C), lambda l, i: (0, jnp.where(l == slab, i, jnp.where(l < slab, 0, nb - 1)), 0))

    row = pl.BlockSpec((tr, C), lambda l, i: (l * nb + i, 0))
    return pl.pallas_call(
        body, name=name, grid=(n, nb),
        in_specs=[recv_spec(slab) for slab in range(n)] + [row, row, row],
        out_specs=[row] * 4, out_shape=[jax.ShapeDtypeStruct((R, C), F32)] * 4,
        compiler_params=_params(("arbitrary", "arbitrary")),
    )(*recvs, w, m, v)


def _pack(ts, width, dtype, row_mult=16):
    flat = jnp.concatenate([t.astype(dtype).reshape(-1) for t in ts])
    n = flat.shape[0]
    rows = -(-n // (width * row_mult)) * row_mult
    return jnp.pad(flat, (0, rows * width - n)).reshape(rows, width)


def _unpack(buf, shapes):
    lead = buf.shape[:-2]
    flat = buf.reshape(lead + (-1,))
    out, off = [], 0
    for s in shapes:
        n = int(np.prod(s))
        out.append(flat[..., off:off + n].reshape(lead + tuple(s)))
        off += n
    return out


def _rope_pad(r):
    z = jnp.zeros(r.shape[:-1] + (ROPE_HALF,), r.dtype)
    return jnp.concatenate([r[..., :ROPE_HALF], z, r[..., ROPE_HALF:], z], axis=-1)


def _rope_unpad(r):
    return jnp.concatenate([r[..., :ROPE_HALF], r[..., 2 * ROPE_HALF:3 * ROPE_HALF]], axis=-1)


def _step(P):
    x0 = P['x'][0]
    target = P['loss_target'][0]
    S, D = x0.shape
    NL = P['ffn_norm1'].shape[0]
    NA = P['a_wqkv'].shape[0]
    Fs = P['ffn1_wg'].shape[2]
    H = D // HEAD
    KV = P['b_wdkv'].shape[1]
    QL = P['b_wdq'].shape[2]
    HW = H * HEAD

    ffn_seq = [(l, f) for l in range(NL) for f in (0, 1)]
    ffn_loc = {(l, f): [P[n][l].astype(BF) for n in (f'ffn{f + 1}_wg', f'ffn{f + 1}_wu', f'ffn{f + 1}_wd')]
               for l, f in ffn_seq}
    mix_shapes = [P[n].shape for n in MIX_W]
    wc_loc = _pack([P[n] for n in MIX_W], D, BF)
    *w_first, WC = all_gather(ffn_loc[ffn_seq[0]] + [wc_loc])
    ffn_w = {ffn_seq[0]: w_first}
    g = dict(zip(MIX_W, _unpack(WC, mix_shapes)))
    cols = lambda t: jnp.moveaxis(t, 0, -2).reshape(t.shape[1:-1] + (N_DEV * t.shape[-1],))
    rows = lambda t, lead: jnp.moveaxis(t, 0, lead).reshape(t.shape[1:1 + lead] + (N_DEV * t.shape[1 + lead],) + t.shape[2 + lead:])
    a_wqkv = cols(g['a_wqkv'])
    a_wo = rows(g['a_wo'], 1)
    b_wo = rows(g['b_wo'], 1)
    wdkv = rows(g['b_wdkv'], 0)
    wkr = _rope_pad(rows(g['b_wkr'], 0))
    wuk = rows(g['b_wuk'], 0).reshape(KV, HW)
    wuv = rows(g['b_wuv'], 0).reshape(KV, HW)
    wkv = jnp.concatenate([wuk, wuv], axis=1)
    wdq = rows(g['b_wdq'], 1)
    wuq = rows(g['b_wuq'], 1)
    wuq_n = wuq[..., :HEAD].reshape(-1, QL, HW)
    wuq_r = _rope_pad(wuq[..., HEAD:]).reshape(-1, QL, HW)

    def ffn_forward(x, gain, key):
        nxt = ffn_seq.index(key) + 1
        carry = ffn_loc[ffn_seq[nxt]] if nxt < len(ffn_seq) else []
        xo, h, G, U, *got = ffn_fwd(x, gain, *ffn_w[key], carry=carry)
        if carry:
            ffn_w[ffn_seq[nxt]] = got
        return xo, h, G, U

    inv = 1.0 / (ROPE_THETA ** (jnp.arange(0, 2 * ROPE_HALF, 2, dtype=F32) / (2 * ROPE_HALF)))
    ang = jnp.arange(S, dtype=F32)[:, None] * inv[None, :]
    z = jnp.zeros((S, ROPE_HALF), F32)
    rope = (jnp.concatenate([jnp.cos(ang), z, jnp.cos(ang), z], axis=1),
            jnp.concatenate([-jnp.sin(ang), z, jnp.sin(ang), z], axis=1))
    biases = [dilated_bias(H, d) for d in BRANCH_DILATIONS]

    saved = []
    kvs = None
    x = x0
    for l in range(NL):
        st = {'x_in': x}
        if l == NA:
            ckv_pre, hkv = norm_mm(x, P['kv_norm'], wdkv, out_dtype=F32, name="kv_down")
            kr = norm_mm(x, P['kv_norm'], wkr, out_dtype=BF, tn=LANES, rope=rope, write_h=False, name="kv_rope")
            kvm, ckv = norm_mm(ckv_pre, P['b_ckv_norm'], wkv, out_dtype=BF, name="kv_up")
            kvs = dict(x=x, ckv_pre=ckv_pre, hkv=hkv, kr=kr, kv=kvm, ckv=ckv)
        xa, h1, G1, U1 = ffn_forward(x, P['ffn_norm1'][l], (l, 0))
        st.update(h1=h1, G1=G1, U1=U1, xa=xa)
        if l < NA:
            qkv, hm = norm_mm(xa, P['mix_norm'][l], a_wqkv[l], out_dtype=BF, name="a_qkv")
            os, ls = zip(*[dil_fwd(qkv, b, d, H) for b, d in zip(biases, BRANCH_DILATIONS)])
            o, Lj = dil_combine(os, ls)
            xb = mm(o, a_wo[l], add=xa, name="mix_out")
            st.update(qkv=qkv, hm=hm, o=o, L=Lj)
        else:
            jb = l - NA
            cq_pre, hm = norm_mm(xa, P['mix_norm'][l], wdq[jb], out_dtype=F32, name="q_down")
            qn, cq = norm_mm(cq_pre, P['b_cq_norm'][jb], wuq_n[jb], out_dtype=BF, name="q_up")
            qr = norm_mm(cq_pre, P['b_cq_norm'][jb], wuq_r[jb], out_dtype=BF, tn=LANES, rope=rope, write_h=False,
                         name="q_rope")
            o, Lj = mla_fwd(qn, qr, kvs['kv'], kvs['kr'], H)
            xb = mm(o, b_wo[jb], add=xa, name="mix_out")
            st.update(cq_pre=cq_pre, hm=hm, qn=qn, qr=qr, cq=cq, o=o, L=Lj)
        x, h2, G2, U2 = ffn_forward(xb, P['ffn_norm2'][l], (l, 1))
        st.update(xb=xb, h2=h2, G2=G2, U2=U2)
        saved.append(st)

    loss_part, dx, dg_final = loss_head(x, P['final_norm'], target)

    gw = {}
    gv = {'final_norm': dg_final}
    dkv_acc = None
    for n in ('ffn_norm1', 'mix_norm', 'ffn_norm2'):
        gv[n] = [None] * NL
    gv['b_cq_norm'] = [None] * (NL - NA)
    ffn_r = {}
    pending = []
    a_g = {'a_wqkv': [None] * NA, 'a_wo': [None] * NA}
    b_g = {n: [None] * (NL - NA) for n in ('b_wdq', 'b_wuq', 'b_wo')}

    def ffn_backward(dy, x_in, gain, key, h, G, U):
        sent, carry = pending.pop() if pending else (None, [])
        dxi, dout, dG, dU, dgain, got = ffn_bwd_x(dy, x_in, gain, G, U, *ffn_w[key], carry=carry)
        if carry:
            ffn_r[sent] = got
        pending.append((key, list(ffn_bwd_w(h, dout, G, U, dG, dU))))
        return dxi, dgain

    for l in reversed(range(NL)):
        st = saved[l]
        dxb, gv['ffn_norm2'][l] = ffn_backward(dx, st['xb'], P['ffn_norm2'][l], (l, 1), st['h2'], st['G2'], st['U2'])
        xa = st['xa']
        if l < NA:
            do = mm(dxb, a_wo[l], tb=True, out_dtype=BF, name="mix_out_dx")
            a_g['a_wo'][l] = mm(st['o'], dxb, ta=True, out_dtype=BF, tm=1024, tn=1024, name="mix_out_dw")
            delta = head_delta(do, st['o'])
            acc = None
            for b, d in zip(biases, BRANCH_DILATIONS):
                acc = dil_bwd(st['qkv'], do, st['L'], delta, b, d, H, prev=acc)
            dh = None
            dws = []
            for part, t in enumerate(acc):
                wpart = a_wqkv[l][:, part * HW:(part + 1) * HW]
                dh = mm(t, wpart, tb=True, add=dh, name="a_qkv_dx")
                dws.append(mm(st['hm'], t, ta=True, out_dtype=BF, tm=1024, tn=1024, name="a_qkv_dw"))
            a_g['a_wqkv'][l] = jnp.concatenate(dws, axis=1)
            dxa, gv['mix_norm'][l] = norm_bwd(xa, P['mix_norm'][l], dh, dxb, name="mix_norm_bwd")
        else:
            jb = l - NA
            do = mm(dxb, b_wo[jb], tb=True, out_dtype=BF, name="mix_out_dx")
            b_g['b_wo'][jb] = mm(st['o'], dxb, ta=True, out_dtype=BF, tm=1024, tn=1024, name="mix_out_dw")
            delta = head_delta(do, st['o'])
            dqn, dqr, dkn, dv, dr = mla_bwd(st['qn'], st['qr'], kvs['kv'], kvs['kr'], do, st['L'], delta, rope, H,
                                            prev=dkv_acc)
            dkv_acc = (jnp.concatenate([dkn, dv], axis=1), dr)
            dcq = mm(dqn, wuq_n[jb], tb=True, name="q_up_dx")
            dcq = mm(dqr, wuq_r[jb], tb=True, add=dcq, name="q_up_dx_add")
            dwn = mm(st['cq'], dqn, ta=True, out_dtype=F32, name="q_up_dw")
            dwr = mm(st['cq'], dqr, ta=True, out_dtype=F32, name="q_up_dw")
            b_g['b_wuq'][jb] = jnp.concatenate(
                [dwn.reshape(QL, H, HEAD), _rope_unpad(dwr.reshape(QL, H, HEAD))], axis=-1)
            dcq_pre, gv['b_cq_norm'][jb] = norm_bwd(st['cq_pre'], P['b_cq_norm'][jb], dcq, name="cq_norm_bwd")
            dh = mm(dcq_pre, wdq[jb], tb=True, name="q_down_dx")
            b_g['b_wdq'][jb] = mm(st['hm'], dcq_pre, ta=True, out_dtype=F32, tm=1024, name="q_down_dw")
            dxa, gv['mix_norm'][l] = norm_bwd(xa, P['mix_norm'][l], dh, dxb, name="mix_norm_bwd")
        dx, gv['ffn_norm1'][l] = ffn_backward(dxa, st['x_in'], P['ffn_norm1'][l], (l, 0), st['h1'], st['G1'], st['U1'])
        if l == NA:
            dkvm, dr = dkv_acc
            dckv = mm(dkvm, wkv, tb=True, name="kv_up_dx")
            dwkv = mm(kvs['ckv'], dkvm, ta=True, out_dtype=F32, name="kv_up_dw")
            gw['b_wuk'] = dwkv[:, :HW].reshape(KV, H, HEAD)
            gw['b_wuv'] = dwkv[:, HW:].reshape(KV, H, HEAD)
            dckv_pre, gv['b_ckv_norm'] = norm_bwd(kvs['ckv_pre'], P['b_ckv_norm'], dckv, name="ckv_norm_bwd")
            dh = mm(dckv_pre, wdkv, tb=True, name="kv_down_dx")
            dh = mm(dr, wkr, tb=True, add=dh, name="kv_rope_dx")
            gw['b_wdkv'] = mm(kvs['hkv'], dckv_pre, ta=True, out_dtype=F32, tm=1024, name="kv_down_dw")
            gw['b_wkr'] = _rope_unpad(mm(kvs['hkv'], dr, ta=True, out_dtype=F32, tm=1024, name="kv_rope_dw"))
            dx, gv['kv_norm'] = norm_bwd(kvs['x'], P['kv_norm'], dh, dx, name="kv_norm_bwd")

    for n in a_g:
        gw[n] = jnp.stack(a_g[n])
    for n in b_g:
        gw[n] = jnp.stack(b_g[n])
    for n in ('ffn_norm1', 'mix_norm', 'ffn_norm2', 'b_cq_norm'):
        gv[n] = jnp.stack(gv[n])

    split_cols = lambda t: jnp.moveaxis(t.reshape(t.shape[:-1] + (N_DEV, t.shape[-1] // N_DEV)), -2, 0)
    split_rows = lambda t, lead: jnp.moveaxis(
        t.reshape(t.shape[:lead] + (N_DEV, t.shape[lead] // N_DEV) + t.shape[lead + 1:]), lead, 0)
    shards = {
        'a_wqkv': split_cols(gw['a_wqkv']), 'a_wo': split_rows(gw['a_wo'], 1), 'b_wdkv': split_rows(gw['b_wdkv'], 0),
        'b_wkr': split_rows(gw['b_wkr'], 0), 'b_wuk': split_rows(gw['b_wuk'], 0), 'b_wuv': split_rows(gw['b_wuv'], 0),
        'b_wdq': split_rows(gw['b_wdq'], 1), 'b_wuq': split_rows(gw['b_wuq'], 1), 'b_wo': split_rows(gw['b_wo'], 1),
    }
    gc = jnp.stack([_pack([shards[n][p] for n in MIX_W], D, BF) for p in range(N_DEV)])
    vec_parts = [gv[n] for n in VEC_W] + [jnp.full((LANES,), loss_part, F32)]
    gvec = _pack(vec_parts, LANES, F32, row_mult=8)
    last_key, last = pending.pop()
    *got, rc, rv = exchange(last + [gc, gvec], [False] * len(last) + [False, True])
    ffn_r[last_key] = got

    res = {}
    for f in (0, 1):
        for which, kind in enumerate(('wg', 'wu', 'wd')):
            n = f'ffn{f + 1}_{kind}'
            recvs = [ffn_r[(l, f)][which] for l in range(NL)]
            flat = lambda t: t.reshape((NL * t.shape[1], t.shape[2]))
            out = adamw(recvs, flat(P[n]), flat(P['m_' + n]), flat(P['v_' + n]),
                        name="adamw_row" if kind == 'wd' else "adamw_col")
            res[n] = [t.reshape(P[n].shape) for t in out]
    pk = lambda pre: _pack([P[pre + n] for n in MIX_W], D, F32)
    out = adamw([rc], pk(''), pk('m_'), pk('v_'), name="adamw_mix")
    for n, parts in zip(MIX_W, zip(*[_unpack(t, mix_shapes) for t in out])):
        res[n] = list(parts)
    vec_shapes = [P[n].shape for n in VEC_W] + [(LANES,)]
    ones = jnp.ones((LANES,), F32)
    pv = lambda pre: _pack([P[pre + n] for n in VEC_W] + [ones], LANES, F32, row_mult=8)
    out = adamw([rv], pv(''), pv('m_'), pv('v_'), name="adamw_vec")
    unp = [_unpack(t, vec_shapes) for t in out]
    for idx, n in enumerate(VEC_W):
        res[n] = [u[idx] for u in unp]
    loss = unp[0][-1][0]

    outs = [loss, dx[None]]
    for field in range(4):
        outs += [res[n][field] for n in W_NAMES]
    return tuple(outs)


def kernel(x, ffn_norm1, ffn1_wg, ffn1_wu, ffn1_wd, mix_norm, ffn_norm2, ffn2_wg, ffn2_wu, ffn2_wd, a_wqkv, a_wo, kv_norm, b_wdkv, b_ckv_norm, b_wkr, b_wuk, b_wuv, b_wdq, b_cq_norm, b_wuq, b_wo, final_norm, loss_target, m_ffn_norm1, m_ffn1_wg, m_ffn1_wu, m_ffn1_wd, m_mix_norm, m_ffn_norm2, m_ffn2_wg, m_ffn2_wu, m_ffn2_wd, m_a_wqkv, m_a_wo, m_kv_norm, m_b_wdkv, m_b_ckv_norm, m_b_wkr, m_b_wuk, m_b_wuv, m_b_wdq, m_b_cq_norm, m_b_wuq, m_b_wo, m_final_norm, v_ffn_norm1, v_ffn1_wg, v_ffn1_wu, v_ffn1_wd, v_mix_norm, v_ffn_norm2, v_ffn2_wg, v_ffn2_wu, v_ffn2_wd, v_a_wqkv, v_a_wo, v_kv_norm, v_b_wdkv, v_b_ckv_norm, v_b_wkr, v_b_wuk, v_b_wuv, v_b_wdq, v_b_cq_norm, v_b_wuq, v_b_wo, v_final_norm):
    return _step(dict(locals()))
```

```python
import functools
import math

import numpy as np
import jax
import jax.numpy as jnp
from jax import lax
from jax.experimental import pallas as pl
from jax.experimental.pallas import tpu as pltpu

BF = jnp.bfloat16
F32 = jnp.float32
MESH = pl.DeviceIdType.MESH
ANY = pl.BlockSpec(memory_space=pl.ANY)

N_DEV = 8
LANES = 128
HEAD = 128
ROPE_HALF = 32
DIL_N = 128
BRANCH_DILATIONS = (1, 4, 16)
ROPE_THETA = 10000.0
MLA_SCALE = (HEAD + 2 * ROPE_HALF) ** -0.5
LOG2E = math.log2(math.e)
EPS = 1e-6
NEG = -1e30
VMEM_LIMIT = 58 * 1024 * 1024

ADAM_LR, ADAM_B1, ADAM_B2, ADAM_EPS, ADAM_WD, ADAM_STEP = 0.001, 0.9, 0.999, 1e-08, 0.01, 10

W_NAMES = ['ffn_norm1', 'ffn1_wg', 'ffn1_wu', 'ffn1_wd', 'mix_norm', 'ffn_norm2', 'ffn2_wg', 'ffn2_wu', 'ffn2_wd',
           'a_wqkv', 'a_wo', 'kv_norm', 'b_wdkv', 'b_ckv_norm', 'b_wkr', 'b_wuk', 'b_wuv', 'b_wdq', 'b_cq_norm',
           'b_wuq', 'b_wo', 'final_norm']
FFN_COL = ['ffn1_wg', 'ffn1_wu', 'ffn2_wg', 'ffn2_wu']
FFN_ROW = ['ffn1_wd', 'ffn2_wd']
MIX_W = ['a_wqkv', 'a_wo', 'b_wdkv', 'b_wkr', 'b_wuk', 'b_wuv', 'b_wdq', 'b_wuq', 'b_wo']
VEC_W = ['ffn_norm1', 'mix_norm', 'ffn_norm2', 'kv_norm', 'b_ckv_norm', 'b_cq_norm', 'final_norm']


def _tile(n, pref):
    t = min(n, pref)
    assert n % t == 0, (n, pref)
    return t


def _params(sem):
    return pltpu.CompilerParams(dimension_semantics=sem, vmem_limit_bytes=VMEM_LIMIT)


def _dot(a, b):
    return jnp.dot(a, b, preferred_element_type=F32)


def _dot_nt(a, b):
    return lax.dot_general(a, b, (((1,), (1,)), ((), ())), preferred_element_type=F32)


def _dot_tn(a, b):
    return lax.dot_general(a, b, (((0,), (0,)), ((), ())), preferred_element_type=F32)


def _rms(x, g):
    r = lax.rsqrt(jnp.mean(x * x, axis=-1, keepdims=True) + EPS)
    return x * r * g, r


def _rms_bwd(x, g, dh):
    r = lax.rsqrt(jnp.mean(x * x, axis=-1, keepdims=True) + EPS)
    xhat = x * r
    gd = dh * g
    dx = r * (gd - xhat * jnp.mean(gd * xhat, axis=-1, keepdims=True))
    return dx, jnp.sum(dh * xhat, axis=0, keepdims=True)


def _rope(t, cos, sin):
    return t * cos + pltpu.roll(t, 2 * ROPE_HALF, 1) * sin


def _rope_bwd(g, cos, sin):
    return g * cos + pltpu.roll(g * sin, 2 * ROPE_HALF, 1)


def _coords():
    return lax.axis_index("x"), lax.axis_index("y"), lax.axis_index("c")


def _lin(px, py, pc):
    return 4 * px + 2 * py + pc


def _gather_phases(ins, outs, ssem, rsem, lsem):
    n = len(ins)
    x, y, c = _coords()
    me, sibling = (x, y, c), (x, y, 1 - c)
    chips = [(1 - x, y), (x, 1 - y), (1 - x, 1 - y)]

    def copy(t, k, block, to, src=None):
        slot = outs[t].at[_lin(*block)]
        return pltpu.make_async_remote_copy(
            src_ref=slot if src is None else src, dst_ref=slot,
            send_sem=ssem.at[7 * t + k], recv_sem=rsem.at[7 * t + k],
            device_id=to, device_id_type=MESH)

    def mine(t):
        return pltpu.make_async_copy(ins[t], outs[t].at[_lin(*me)], lsem.at[t])

    def first(t):
        return [copy(t, 0, me, sibling, src=ins[t])] + [copy(t, 1 + j, me, (*chip, c), src=ins[t])
                                                       for j, chip in enumerate(chips)]

    def passed(t):
        return [copy(t, 4 + j, (*chip, c), sibling) for j, chip in enumerate(chips)]

    def start():
        for t in range(n):
            mine(t).start()
            for cp in first(t):
                cp.start()

    def middle():
        for t in range(n):
            fw = passed(t)
            for j, chip in enumerate(chips):
                copy(t, 1 + j, (*chip, c), me).wait_recv()
                fw[j].start()

    def finish():
        for t in range(n):
            copy(t, 0, sibling, me).wait_recv()
            for j, chip in enumerate(chips):
                copy(t, 4 + j, (*chip, 1 - c), me).wait_recv()
        for t in range(n):
            for cp in first(t) + passed(t):
                cp.wait_send()
            mine(t).wait()

    return start, middle, finish


def _exchange_phases(ins, outs, bcast, ssem, rsem, lsem):
    n = len(ins)
    x, y, c = _coords()
    me = _lin(x, y, c)
    peers = []
    for k in range(1, N_DEV):
        kx, ky, kc = (k >> 2) & 1, (k >> 1) & 1, k & 1
        peers.append((k, (1 - x if kx else x, 1 - y if ky else y, 1 - c if kc else c)))

    def own(t):
        return pltpu.make_async_copy(ins[t] if bcast[t] else ins[t].at[me], outs[t].at[me], lsem.at[t])

    def send(t, k, peer):
        return pltpu.make_async_remote_copy(
            src_ref=ins[t] if bcast[t] else ins[t].at[_lin(*peer)], dst_ref=outs[t].at[me],
            send_sem=ssem.at[7 * t + k - 1], recv_sem=rsem.at[7 * t + k - 1],
            device_id=peer, device_id_type=MESH)

    def arrival(t, k, peer):
        slot = outs[t].at[_lin(*peer)]
        return pltpu.make_async_remote_copy(
            src_ref=slot, dst_ref=slot, send_sem=ssem.at[7 * t + k - 1], recv_sem=rsem.at[7 * t + k - 1],
            device_id=peer, device_id_type=MESH)

    def start():
        for t in range(n):
            own(t).start()
            for k, peer in peers:
                send(t, k, peer).start()

    def finish():
        for t in range(n):
            for k, peer in peers:
                arrival(t, k, peer).wait_recv()
        for t in range(n):
            for k, peer in peers:
                send(t, k, peer).wait_send()
            own(t).wait()

    return start, finish


def _comm_sems(n):
    return [pltpu.SemaphoreType.DMA((7 * n,)), pltpu.SemaphoreType.DMA((7 * n,)), pltpu.SemaphoreType.DMA((n,))]


def _gathered(xs):
    return [jax.ShapeDtypeStruct((N_DEV,) + a.shape, a.dtype) for a in xs]


def all_gather(xs):
    n = len(xs)

    def body(*refs):
        start, middle, finish = _gather_phases(refs[:n], refs[n:2 * n], *refs[2 * n:])
        start()
        middle()
        finish()

    return pl.pallas_call(
        body, name="all_gather", out_shape=_gathered(xs),
        in_specs=[ANY] * n, out_specs=[ANY] * n, scratch_shapes=_comm_sems(n),
    )(*xs)


def exchange(xs, bcast):
    n = len(xs)

    def body(*refs):
        start, finish = _exchange_phases(refs[:n], refs[n:2 * n], bcast, *refs[2 * n:])
        start()
        finish()

    out_shape = [jax.ShapeDtypeStruct(((N_DEV,) + a.shape) if b else a.shape, a.dtype) for a, b in zip(xs, bcast)]
    return pl.pallas_call(
        body, name="grad_exchange", out_shape=out_shape,
        in_specs=[ANY] * n, out_specs=[ANY] * n, scratch_shapes=_comm_sems(n),
    )(*xs)


def mm(a, b, *, ta=False, tb=False, add=None, out_dtype=F32, tm=512, tn=1024, tk=None, name="mm"):
    K, M = a.shape if ta else a.shape[::-1]
    N = b.shape[0] if tb else b.shape[1]
    assert (b.shape[1] if tb else b.shape[0]) == K and not (ta and tb)
    if tk is None:
        tk = 512 if ta else 2048
    tm, tn, tk = _tile(M, tm), _tile(N, tn), _tile(K, tk)
    nk = K // tk
    has_add = add is not None

    def body(*refs):
        if has_add:
            a_ref, b_ref, add_ref, o_ref, acc = refs
        else:
            a_ref, b_ref, o_ref, acc = refs
        k = pl.program_id(2)

        @pl.when(k == 0)
        def _():
            acc[...] = jnp.zeros_like(acc)

        av = a_ref[...].astype(BF)
        bv = b_ref[...].astype(BF)
        if ta:
            acc[...] += _dot_tn(av, bv)
        elif tb:
            acc[...] += _dot_nt(av, bv)
        else:
            acc[...] += _dot(av, bv)

        @pl.when(k == nk - 1)
        def _():
            r = acc[...]
            if has_add:
                r = r + add_ref[...]
            o_ref[...] = r.astype(out_dtype)

    a_spec = pl.BlockSpec((tk, tm), lambda i, j, k: (k, i)) if ta else pl.BlockSpec((tm, tk), lambda i, j, k: (i, k))
    b_spec = pl.BlockSpec((tn, tk), lambda i, j, k: (j, k)) if tb else pl.BlockSpec((tk, tn), lambda i, j, k: (k, j))
    o_spec = pl.BlockSpec((tm, tn), lambda i, j, k: (i, j))
    in_specs = [a_spec, b_spec] + ([o_spec] if has_add else [])
    args = [a, b] + ([add] if has_add else [])
    return pl.pallas_call(
        body, name=name, grid=(M // tm, N // tn, nk),
        in_specs=in_specs, out_specs=o_spec,
        out_shape=jax.ShapeDtypeStruct((M, N), out_dtype),
        scratch_shapes=[pltpu.VMEM((tm, tn), F32)],
        compiler_params=_params(("parallel", "parallel", "arbitrary")),
    )(*args)


def norm_mm(x, gain, w, *, out_dtype, tn=512, rope=None, write_h=True, tm=512, name="norm_mm"):
    S, K = x.shape
    N = w.shape[1]
    tm, tn = _tile(S, tm), _tile(N, tn)
    if rope is not None:
        assert tn == LANES
    gain = gain.reshape(1, K)

    def body(*refs):
        refs = list(refs)
        x_ref, g_ref, w_ref = refs[:3]
        refs = refs[3:]
        if rope is not None:
            cos_ref, sin_ref = refs[:2]
            refs = refs[2:]
        y_ref = refs[0]
        h_ref = refs[1] if write_h else None
        hs = refs[-1]
        j = pl.program_id(1)

        @pl.when(j == 0)
        def _():
            h, _ = _rms(x_ref[...], g_ref[...])
            hb = h.astype(BF)
            hs[...] = hb
            if write_h:
                h_ref[...] = hb

        y = _dot(hs[...], w_ref[...])
        if rope is not None:
            y = _rope(y, cos_ref[...], sin_ref[...])
        y_ref[...] = y.astype(out_dtype)

    in_specs = [pl.BlockSpec((tm, K), lambda i, j: (i, 0)), pl.BlockSpec((1, K), lambda i, j: (0, 0)),
                pl.BlockSpec((K, tn), lambda i, j: (0, j))]
    args = [x, gain, w]
    if rope is not None:
        in_specs += [pl.BlockSpec((tm, LANES), lambda i, j: (i, 0))] * 2
        args += list(rope)
    out_specs = [pl.BlockSpec((tm, tn), lambda i, j: (i, j))]
    out_shape = [jax.ShapeDtypeStruct((S, N), out_dtype)]
    if write_h:
        out_specs.append(pl.BlockSpec((tm, K), lambda i, j: (i, 0)))
        out_shape.append(jax.ShapeDtypeStruct((S, K), BF))
    res = pl.pallas_call(
        body, name=name, grid=(S // tm, N // tn), in_specs=in_specs, out_specs=out_specs, out_shape=out_shape,
        scratch_shapes=[pltpu.VMEM((tm, K), BF)],
        compiler_params=_params(("parallel", "arbitrary")),
    )(*args)
    return res if write_h else res[0]


def norm_bwd(x, gain, dh, dres=None, *, tm=512, name="norm_bwd"):
    S, K = x.shape
    tm = _tile(S, tm)
    gain = gain.reshape(1, K)
    has_res = dres is not None

    def body(*refs):
        if has_res:
            x_ref, g_ref, dh_ref, dr_ref, dx_ref, dg_ref = refs
        else:
            x_ref, g_ref, dh_ref, dx_ref, dg_ref = refs

        @pl.when(pl.program_id(0) == 0)
        def _():
            dg_ref[...] = jnp.zeros_like(dg_ref)

        dx, dg = _rms_bwd(x_ref[...], g_ref[...], dh_ref[...].astype(F32))
        if has_res:
            dx = dx + dr_ref[...]
        dx_ref[...] = dx
        dg_ref[...] += jnp.broadcast_to(dg, dg_ref.shape)

    row = pl.BlockSpec((tm, K), lambda i: (i, 0))
    in_specs = [row, pl.BlockSpec((1, K), lambda i: (0, 0)), row] + ([row] if has_res else [])
    args = [x, gain, dh] + ([dres] if has_res else [])
    dx, dg = pl.pallas_call(
        body, name=name, grid=(S // tm,), in_specs=in_specs,
        out_specs=[row, pl.BlockSpec((8, K), lambda i: (0, 0))],
        out_shape=[jax.ShapeDtypeStruct((S, K), F32), jax.ShapeDtypeStruct((8, K), F32)],
        compiler_params=_params(("arbitrary",)),
    )(*args)
    return dx, dg[0]


def loss_head(x, gain, target, *, tm=512):
    S, K = x.shape
    tm = _tile(S, tm)
    gain = gain.reshape(1, K)

    def body(x_ref, g_ref, t_ref, dx_ref, dg_ref, ls_ref):
        @pl.when(pl.program_id(0) == 0)
        def _():
            dg_ref[...] = jnp.zeros_like(dg_ref)
            ls_ref[...] = jnp.zeros_like(ls_ref)

        xv, g = x_ref[...], g_ref[...]
        y, _ = _rms(xv, g)
        e = y - t_ref[...]
        part = jnp.sum(jnp.mean(e * e, axis=-1, keepdims=True), axis=0, keepdims=True)
        ls_ref[...] += jnp.broadcast_to(0.5 * part, ls_ref.shape)
        dx, dg = _rms_bwd(xv, g, e / K)
        dx_ref[...] = dx
        dg_ref[...] += jnp.broadcast_to(dg, dg_ref.shape)

    row = pl.BlockSpec((tm, K), lambda i: (i, 0))
    dx, dg, ls = pl.pallas_call(
        body, name="loss_head", grid=(S // tm,),
        in_specs=[row, pl.BlockSpec((1, K), lambda i: (0, 0)), row],
        out_specs=[row, pl.BlockSpec((8, K), lambda i: (0, 0)), pl.BlockSpec((8, LANES), lambda i: (0, 0))],
        out_shape=[jax.ShapeDtypeStruct((S, K), F32), jax.ShapeDtypeStruct((8, K), F32),
                   jax.ShapeDtypeStruct((8, LANES), F32)],
        compiler_params=_params(("arbitrary",)),
    )(x, gain, target)
    return ls[0, 0], dx, dg[0]


def _once(shape, index_map):
    return pl.BlockSpec(shape, index_map, pipeline_mode=pl.Buffered(1))


def ffn_fwd(x, gain, wg, wu, wd, carry=(), *, tm=512):
    S, D = x.shape
    NB, _, Fs = wg.shape
    tm = _tile(S, tm)
    ni = S // tm
    nc = len(carry)
    gain = gain.reshape(1, D)

    def body(*refs):
        x_ref, g_ref, wg_ref, wu_ref, wd_ref = refs[:5]
        c_in = refs[5:5 + nc]
        xo_ref, h_ref, G_ref, U_ref = refs[5 + nc:9 + nc]
        c_out = refs[9 + nc:9 + 2 * nc]
        hs, acc = refs[9 + 2 * nc:11 + 2 * nc]
        i, j = pl.program_id(0), pl.program_id(1)
        if nc:
            start, middle, finish = _gather_phases(c_in, c_out, *refs[11 + 2 * nc:])
            pl.when((i == 0) & (j == 0))(start)
            pl.when((i == ni // 2) & (j == 0))(middle)

        @pl.when(j == 0)
        def _():
            h, _ = _rms(x_ref[...], g_ref[...])
            hb = h.astype(BF)
            hs[...] = hb
            h_ref[...] = hb
            acc[...] = jnp.zeros_like(acc)

        h = hs[...]
        g = _dot(h, wg_ref[...])
        u = _dot(h, wu_ref[...])
        G_ref[...] = g.astype(BF)
        U_ref[...] = u.astype(BF)
        a = (g * jax.nn.sigmoid(g) * u).astype(BF)
        acc[...] += _dot(a, wd_ref[...])

        @pl.when(j == NB - 1)
        def _():
            xo_ref[...] = x_ref[...] + 0.5 * acc[...]

        if nc:
            pl.when((i == ni - 1) & (j == NB - 1))(finish)

    row = lambda i, j: (i, 0)
    blk = lambda i, j: (j, 0, 0)
    hid = pl.BlockSpec((None, tm, Fs), lambda i, j: (j, i, 0))
    return pl.pallas_call(
        body, name="ffn_fwd_gather" if nc else "ffn_fwd", grid=(ni, NB),
        in_specs=[_once((tm, D), row), pl.BlockSpec((1, D), lambda i, j: (0, 0)),
                  pl.BlockSpec((None, D, Fs), blk), pl.BlockSpec((None, D, Fs), blk), pl.BlockSpec((None, Fs, D), blk)]
                 + [ANY] * nc,
        out_specs=[_once((tm, D), row), _once((tm, D), row), hid, hid] + [ANY] * nc,
        out_shape=[jax.ShapeDtypeStruct((S, D), F32), jax.ShapeDtypeStruct((S, D), BF),
                   jax.ShapeDtypeStruct((NB, S, Fs), BF), jax.ShapeDtypeStruct((NB, S, Fs), BF)] + _gathered(carry),
        scratch_shapes=[pltpu.VMEM((tm, D), BF), pltpu.VMEM((tm, D), F32)] + (_comm_sems(nc) if nc else []),
        compiler_params=_params(("arbitrary", "arbitrary")),
    )(x, gain, wg, wu, wd, *carry)


def ffn_bwd_x(dy, x, gain, G, U, wg, wu, wd, carry=(), *, tm=512, sub=256):
    S, D = x.shape
    NB, _, Fs = wg.shape
    tm = _tile(S, tm)
    sub = _tile(tm, sub)
    ni = S // tm
    nc = len(carry)
    gain = gain.reshape(1, D)

    def body(*refs):
        dy_ref, x_ref, g_ref, G_ref, U_ref, wg_ref, wu_ref, wd_ref = refs[:8]
        c_in = refs[8:8 + nc]
        dx_ref, do_ref, dG_ref, dU_ref, dg_ref = refs[8 + nc:13 + nc]
        c_out = refs[13 + nc:13 + 2 * nc]
        dob, acc = refs[13 + 2 * nc:15 + 2 * nc]
        i, j = pl.program_id(0), pl.program_id(1)
        if nc:
            start, finish = _exchange_phases(c_in, c_out, [False] * nc, *refs[15 + 2 * nc:])
            pl.when((i == 0) & (j == 0))(start)

        @pl.when(j == 0)
        def _():
            d = (0.5 * dy_ref[...]).astype(BF)
            dob[...] = d
            do_ref[...] = d
            acc[...] = jnp.zeros_like(acc)

        @pl.when((i == 0) & (j == 0))
        def _():
            dg_ref[...] = jnp.zeros_like(dg_ref)

        for r in range(tm // sub):
            rows = slice(r * sub, (r + 1) * sub)
            dA = _dot_nt(dob[rows, :], wd_ref[...])
            g = G_ref[rows, :].astype(F32)
            u = U_ref[rows, :].astype(F32)
            sig = jax.nn.sigmoid(g)
            dG = (dA * u * (sig * (1.0 + g * (1.0 - sig)))).astype(BF)
            dU = (dA * (g * sig)).astype(BF)
            dG_ref[rows, :] = dG
            dU_ref[rows, :] = dU
            acc[rows, :] += _dot_nt(dG, wg_ref[...]) + _dot_nt(dU, wu_ref[...])

        @pl.when(j == NB - 1)
        def _():
            dxn, dg = _rms_bwd(x_ref[...], g_ref[...], acc[...])
            dx_ref[...] = dy_ref[...] + dxn
            dg_ref[...] += jnp.broadcast_to(dg, dg_ref.shape)

        if nc:
            pl.when((i == ni - 1) & (j == NB - 1))(finish)

    row = lambda i, j: (i, 0)
    blk = lambda i, j: (j, 0, 0)
    hid = pl.BlockSpec((None, tm, Fs), lambda i, j: (j, i, 0))
    dx, dout, dG, dU, dg, *got = pl.pallas_call(
        body, name="ffn_bwd_x_exchange" if nc else "ffn_bwd_x", grid=(ni, NB),
        in_specs=[_once((tm, D), row), _once((tm, D), row), pl.BlockSpec((1, D), lambda i, j: (0, 0)), hid, hid,
                  pl.BlockSpec((None, D, Fs), blk), pl.BlockSpec((None, D, Fs), blk), pl.BlockSpec((None, Fs, D), blk)]
                 + [ANY] * nc,
        out_specs=[_once((tm, D), row), _once((tm, D), row), hid, hid, pl.BlockSpec((8, D), lambda i, j: (0, 0))]
                  + [ANY] * nc,
        out_shape=[jax.ShapeDtypeStruct((S, D), F32), jax.ShapeDtypeStruct((S, D), BF),
                   jax.ShapeDtypeStruct((NB, S, Fs), BF), jax.ShapeDtypeStruct((NB, S, Fs), BF),
                   jax.ShapeDtypeStruct((8, D), F32)] + [jax.ShapeDtypeStruct(a.shape, a.dtype) for a in carry],
        scratch_shapes=[pltpu.VMEM((tm, D), BF), pltpu.VMEM((tm, D), F32)] + (_comm_sems(nc) if nc else []),
        compiler_params=_params(("arbitrary", "arbitrary")),
    )(dy, x, gain, G, U, wg, wu, wd, *carry)
    return dx, dout, dG, dU, dg[0], got


def ffn_bwd_w(h, dout, G, U, dG, dU, *, tm=256):
    S, D = h.shape
    NB, _, Fs = G.shape
    tm = _tile(S, tm)
    ni = S // tm

    def body(h_ref, do_ref, G_ref, U_ref, dG_ref, dU_ref, wg_ref, wu_ref, wd_ref, ag, au, ad):
        i = pl.program_id(1)

        @pl.when(i == 0)
        def _():
            ag[...] = jnp.zeros_like(ag)
            au[...] = jnp.zeros_like(au)
            ad[...] = jnp.zeros_like(ad)

        h = h_ref[...]
        ag[...] += _dot_tn(h, dG_ref[...])
        au[...] += _dot_tn(h, dU_ref[...])
        g = G_ref[...].astype(F32)
        a = (g * jax.nn.sigmoid(g) * U_ref[...].astype(F32)).astype(BF)
        ad[...] += _dot_tn(a, do_ref[...])

        @pl.when(i == ni - 1)
        def _():
            wg_ref[...] = ag[...].astype(BF)
            wu_ref[...] = au[...].astype(BF)
            wd_ref[...] = ad[...].astype(BF)

    row = pl.BlockSpec((tm, D), lambda j, i: (i, 0))
    hid = pl.BlockSpec((None, tm, Fs), lambda j, i: (j, i, 0))
    blk = lambda j, i: (j, 0, 0)
    return pl.pallas_call(
        body, name="ffn_bwd_w", grid=(NB, ni),
        in_specs=[row, row, hid, hid, hid, hid],
        out_specs=[pl.BlockSpec((None, D, Fs), blk), pl.BlockSpec((None, D, Fs), blk), pl.BlockSpec((None, Fs, D), blk)],
        out_shape=[jax.ShapeDtypeStruct((NB, D, Fs), BF), jax.ShapeDtypeStruct((NB, D, Fs), BF),
                   jax.ShapeDtypeStruct((NB, Fs, D), BF)],
        scratch_shapes=[pltpu.VMEM((D, Fs), F32), pltpu.VMEM((D, Fs), F32), pltpu.VMEM((Fs, D), F32)],
        compiler_params=_params(("parallel", "arbitrary")),
    )(h, dout, G, U, dG, dU)


def dilated_bias(H, d):
    n = DIL_N
    slopes = 2.0 ** (-8.0 * (np.arange(H) + 1) / H)
    i = np.arange(n)[:, None]
    j = np.arange(2 * n)[None, :]
    steps = n + i - j
    band = (steps >= 0) & (steps <= n)
    first = band & (j >= n)
    bias = -slopes[:, None, None] * (d * steps).astype(np.float64)[None]
    out = np.stack([np.where(band[None], bias, NEG), np.where(first[None], bias, NEG)], axis=1)
    return jnp.asarray(out, dtype=F32)


def dil_fwd(qkv, bias, d, H):
    S = qkv.shape[0]
    Sd = S // d
    TQ = _tile(Sd, 512)
    nsub, nc = TQ // DIL_N, Sd // TQ
    scale = HEAD ** -0.5
    qv = qkv.reshape(Sd, d * 3 * H * HEAD)

    def body(q_ref, kc_ref, kp_ref, vc_ref, vp_ref, b_ref, o_ref, l_ref):
        c = pl.program_id(2)
        for i in range(nsub):
            sl = slice(i * DIL_N, (i + 1) * DIL_N)
            if i == 0:
                k2 = jnp.concatenate([kp_ref[...], kc_ref[0:DIL_N, :]], axis=0)
                v2 = jnp.concatenate([vp_ref[...], vc_ref[0:DIL_N, :]], axis=0)
                b = jnp.where(c == 0, b_ref[1], b_ref[0])
            else:
                k2 = kc_ref[(i - 1) * DIL_N:(i + 1) * DIL_N, :]
                v2 = vc_ref[(i - 1) * DIL_N:(i + 1) * DIL_N, :]
                b = b_ref[0]
            s = _dot_nt(q_ref[sl, :], k2) * scale + b
            m = jnp.max(s, axis=1, keepdims=True)
            p = jnp.exp(s - m)
            l = jnp.sum(p, axis=1, keepdims=True)
            o_ref[sl, :] = _dot(p.astype(BF), v2) / l
            l_ref[sl, :] = jnp.broadcast_to(m + jnp.log(l), (DIL_N, LANES))

    W3 = 3 * H
    chunk = lambda off: pl.BlockSpec((TQ, HEAD), lambda h, r, c: (c, r * W3 + off * H + h))
    halo = lambda off: pl.BlockSpec((DIL_N, HEAD), lambda h, r, c: (jnp.maximum(c * nsub - 1, 0), r * W3 + off * H + h))
    out = pl.BlockSpec((TQ, HEAD), lambda h, r, c: (c, r * H + h))
    o, l = pl.pallas_call(
        body, name=f"dil_fwd_d{d}", grid=(H, d, nc),
        in_specs=[chunk(0), chunk(1), halo(1), chunk(2), halo(2),
                  pl.BlockSpec((None, 2, DIL_N, 2 * DIL_N), lambda h, r, c: (h, 0, 0, 0))],
        out_specs=[out, out],
        out_shape=[jax.ShapeDtypeStruct((Sd, d * H * HEAD), F32)] * 2,
        compiler_params=_params(("parallel", "parallel", "arbitrary")),
    )(qv, qv, qv, qv, qv, bias)
    return o.reshape(S, H * HEAD), l.reshape(S, H * HEAD)


def dil_combine(os, ls, *, tm=256):
    S, W = os[0].shape
    tm = _tile(S, tm)

    def body(o1, o2, o3, l1, l2, l3, o_ref, L_ref):
        a, b, c = l1[...], l2[...], l3[...]
        m = jnp.maximum(jnp.maximum(a, b), c)
        ea, eb, ec = jnp.exp(a - m), jnp.exp(b - m), jnp.exp(c - m)
        z = ea + eb + ec
        o_ref[...] = ((ea * o1[...] + eb * o2[...] + ec * o3[...]) / z).astype(BF)
        L_ref[...] = m + jnp.log(z)

    row = pl.BlockSpec((tm, W), lambda i: (i, 0))
    return pl.pallas_call(
        body, name="dil_combine", grid=(S // tm,), in_specs=[row] * 6, out_specs=[row, row],
        out_shape=[jax.ShapeDtypeStruct((S, W), BF), jax.ShapeDtypeStruct((S, W), F32)],
        compiler_params=_params(("parallel",)),
    )(*os, *ls)


def head_delta(do, o, *, tm=512):
    S, W = o.shape
    tm = _tile(S, tm)

    def body(do_ref, o_ref, d_ref):
        s = jnp.sum(do_ref[...].astype(F32) * o_ref[...].astype(F32), axis=1, keepdims=True)
        d_ref[...] = jnp.broadcast_to(s, d_ref.shape)

    blk = pl.BlockSpec((tm, HEAD), lambda i, h: (i, h))
    return pl.pallas_call(
        body, name="head_delta", grid=(S // tm, W // HEAD), in_specs=[blk, blk], out_specs=blk,
        out_shape=jax.ShapeDtypeStruct((S, W), F32),
        compiler_params=_params(("parallel", "parallel")),
    )(do, o)


def dil_bwd(qkv, do, L, delta, bias, d, H, prev=None):
    S = qkv.shape[0]
    Sd = S // d
    TQ = _tile(Sd, 512)
    nsub, nc = TQ // DIL_N, Sd // TQ
    nblk = Sd // DIL_N
    scale = HEAD ** -0.5
    W3 = 3 * H
    qv = qkv.reshape(Sd, d * W3 * HEAD)
    dov, Lv, dlv = (t.reshape(Sd, d * H * HEAD) for t in (do, L, delta))
    has_prev = prev is not None

    def body(*refs):
        (qc, qn, kc, kp, vc, vp, doc, don, Lc, Ln, dlc, dln, b_ref) = refs[:13]
        refs = refs[13:]
        if has_prev:
            dqi, dki, dvi = refs[:3]
            refs = refs[3:]
        dqo, dko, dvo, dk_acc, dv_acc = refs
        c = pl.program_id(2)
        dk_acc[...] = jnp.zeros_like(dk_acc)
        dv_acc[...] = jnp.zeros_like(dv_acc)
        for i in range(nsub):
            sl = slice(i * DIL_N, (i + 1) * DIL_N)
            if i == 0:
                k2 = jnp.concatenate([kp[...], kc[0:DIL_N, :]], axis=0)
                v2 = jnp.concatenate([vp[...], vc[0:DIL_N, :]], axis=0)
                b = jnp.where(c == 0, b_ref[1], b_ref[0])
            else:
                k2 = kc[(i - 1) * DIL_N:(i + 1) * DIL_N, :]
                v2 = vc[(i - 1) * DIL_N:(i + 1) * DIL_N, :]
                b = b_ref[0]
            q, dov_ = qc[sl, :], doc[sl, :]
            Lq, dl = Lc[sl, :], dlc[sl, :]
            s = _dot_nt(q, k2) * scale + b
            p = jnp.exp(s - jnp.concatenate([Lq, Lq], axis=1))
            dp = _dot_nt(dov_, v2)
            ds = p * (dp - jnp.concatenate([dl, dl], axis=1))
            dsb, pb = ds.astype(BF), p.astype(BF)
            dq = _dot(dsb, k2) * scale
            dqo[sl, :] = dq + dqi[sl, :] if has_prev else dq
            dk2 = _dot_tn(dsb, q) * scale
            dv2 = _dot_tn(pb, dov_)
            if i >= 1:
                pv = slice((i - 1) * DIL_N, i * DIL_N)
                dk_acc[pv, :] += dk2[:DIL_N]
                dv_acc[pv, :] += dv2[:DIL_N]
            dk_acc[sl, :] += dk2[DIL_N:]
            dv_acc[sl, :] += dv2[DIL_N:]

        @pl.when(c < nc - 1)
        def _():
            last = slice((nsub - 1) * DIL_N, nsub * DIL_N)
            q, dov_ = qn[...], don[...]
            k1, v1 = kc[last, :], vc[last, :]
            s = _dot_nt(q, k1) * scale + b_ref[0][:, :DIL_N]
            p = jnp.exp(s - Ln[...])
            ds = p * (_dot_nt(dov_, v1) - dln[...])
            dk_acc[last, :] += _dot_tn(ds.astype(BF), q) * scale
            dv_acc[last, :] += _dot_tn(p.astype(BF), dov_)

        if has_prev:
            dko[...] = dk_acc[...] + dki[...]
            dvo[...] = dv_acc[...] + dvi[...]
        else:
            dko[...] = dk_acc[...]
            dvo[...] = dv_acc[...]

    nxt = lambda c: jnp.minimum((c + 1) * nsub, nblk - 1)
    prv = lambda c: jnp.maximum(c * nsub - 1, 0)
    chunk3 = lambda off: pl.BlockSpec((TQ, HEAD), lambda h, r, c: (c, r * W3 + off * H + h))
    halo3 = lambda off, f: pl.BlockSpec((DIL_N, HEAD), lambda h, r, c: (f(c), r * W3 + off * H + h))
    chunk1 = pl.BlockSpec((TQ, HEAD), lambda h, r, c: (c, r * H + h))
    next1 = pl.BlockSpec((DIL_N, HEAD), lambda h, r, c: (nxt(c), r * H + h))
    in_specs = [chunk3(0), halo3(0, nxt), chunk3(1), halo3(1, prv), chunk3(2), halo3(2, prv),
                chunk1, next1, chunk1, next1, chunk1, next1,
                pl.BlockSpec((None, 2, DIL_N, 2 * DIL_N), lambda h, r, c: (h, 0, 0, 0))]
    args = [qv] * 6 + [dov, dov, Lv, Lv, dlv, dlv, bias]
    if has_prev:
        in_specs += [chunk1] * 3
        args += [t.reshape(Sd, d * H * HEAD) for t in prev]
    outs = pl.pallas_call(
        body, name=f"dil_bwd_d{d}", grid=(H, d, nc), in_specs=in_specs, out_specs=[chunk1] * 3,
        out_shape=[jax.ShapeDtypeStruct((Sd, d * H * HEAD), F32)] * 3,
        scratch_shapes=[pltpu.VMEM((TQ, HEAD), F32), pltpu.VMEM((TQ, HEAD), F32)],
        compiler_params=_params(("parallel", "parallel", "arbitrary")),
    )(*args)
    return tuple(t.reshape(S, H * HEAD) for t in outs)


DIL_CHUNK = DIL_N * max(BRANCH_DILATIONS)
DIL_UNROLL = 4


def _rows(start, size, d):
    return pl.ds(pl.multiple_of(start, DIL_N), size) if d == 1 else pl.ds(start, size, stride=d)


def dil_attn_fwd(qkv, biases, H):
    S = qkv.shape[0]
    C = DIL_CHUNK
    assert S % C == 0
    scale = HEAD ** -0.5
    nbr = len(BRANCH_DILATIONS)

    def body(*refs):
        q_ref, kc_ref, kp_ref, vc_ref, vp_ref = refs[:5]
        b_refs = refs[5:5 + nbr]
        o_ref, L_ref, qf, kf, vf = refs[5 + nbr:10 + nbr]
        ogs = refs[10 + nbr:10 + 2 * nbr]
        lgs = refs[10 + 2 * nbr:10 + 3 * nbr]
        c = pl.program_id(1)
        qf[...] = q_ref[...].astype(F32)
        kf[0:C, :] = kp_ref[...].astype(F32)
        kf[C:2 * C, :] = kc_ref[...].astype(F32)
        vf[0:C, :] = vp_ref[...].astype(F32)
        vf[C:2 * C, :] = vc_ref[...].astype(F32)
        for d, b_ref, og, lg in zip(BRANCH_DILATIONS, b_refs, ogs, lgs):
            span = DIL_N * d

            def block(t, carry, d=d, b_ref=b_ref, og=og, lg=lg, span=span):
                r, b = t % d, t // d
                q0 = b * span + r
                q = qf[_rows(q0, DIL_N, d), :].astype(BF)
                k2 = kf[_rows(C + q0 - span, 2 * DIL_N, d), :].astype(BF)
                v2 = vf[_rows(C + q0 - span, 2 * DIL_N, d), :].astype(BF)
                bias = jnp.where((c == 0) & (b == 0), b_ref[1], b_ref[0])
                s = _dot_nt(q, k2) * scale + bias
                m = jnp.max(s, axis=1, keepdims=True)
                p = jnp.exp(s - m)
                l = jnp.sum(p, axis=1, keepdims=True)
                og[_rows(q0, DIL_N, d), :] = _dot(p.astype(BF), v2) / l
                lg[_rows(q0, DIL_N, d), :] = jnp.broadcast_to(m + jnp.log(l), (DIL_N, LANES))
                return carry

            lax.fori_loop(0, C // DIL_N, block, 0, unroll=DIL_UNROLL)
        ls = [lg[...] for lg in lgs]
        m = functools.reduce(jnp.maximum, ls)
        es = [jnp.exp(l - m) for l in ls]
        z = functools.reduce(jnp.add, es)
        mix = functools.reduce(jnp.add, [e * og[...] for e, og in zip(es, ogs)])
        o_ref[...] = (mix / z).astype(BF)
        L_ref[...] = m + jnp.log(z)

    cur = lambda off: pl.BlockSpec((C, HEAD), lambda h, c: (c, off * H + h))
    prv = lambda off: pl.BlockSpec((C, HEAD), lambda h, c: (jnp.maximum(c - 1, 0), off * H + h))
    tab = pl.BlockSpec((None, 2, DIL_N, 2 * DIL_N), lambda h, c: (h, 0, 0, 0))
    out = pl.BlockSpec((C, HEAD), lambda h, c: (c, h))
    return pl.pallas_call(
        body, name="dil_attn_fwd", grid=(H, S // C),
        in_specs=[cur(0), cur(1), prv(1), cur(2), prv(2)] + [tab] * nbr,
        out_specs=[out, out],
        out_shape=[jax.ShapeDtypeStruct((S, H * HEAD), BF), jax.ShapeDtypeStruct((S, H * HEAD), F32)],
        scratch_shapes=[pltpu.VMEM((C, HEAD), F32), pltpu.VMEM((2 * C, HEAD), F32), pltpu.VMEM((2 * C, HEAD), F32)]
                       + [pltpu.VMEM((C, HEAD), F32)] * (2 * nbr),
        compiler_params=_params(("parallel", "arbitrary")),
    )(qkv, qkv, qkv, qkv, qkv, *biases)


def dil_attn_bwd(qkv, do, L, delta, biases, H):
    S = qkv.shape[0]
    C = DIL_CHUNK
    assert S % C == 0
    nc = S // C
    scale = HEAD ** -0.5
    nbr = len(BRANCH_DILATIONS)

    def body(*refs):
        (qc_ref, qn_ref, kc_ref, kp_ref, vc_ref, vp_ref, doc_ref, don_ref, Lc_ref, Ln_ref, dlc_ref, dln_ref) = refs[:12]
        b_refs = refs[12:12 + nbr]
        dq_ref, dk_ref, dv_ref, qf, dof, Lf, dlf, kf, vf, dq_acc, dk_acc, dv_acc = refs[12 + nbr:]
        c = pl.program_id(1)
        for buf, a_ref, b_ref in ((qf, qc_ref, qn_ref), (dof, doc_ref, don_ref), (Lf, Lc_ref, Ln_ref),
                                  (dlf, dlc_ref, dln_ref), (kf, kp_ref, kc_ref), (vf, vp_ref, vc_ref)):
            buf[0:C, :] = a_ref[...].astype(F32)
            buf[C:2 * C, :] = b_ref[...].astype(F32)
        dq_acc[...] = jnp.zeros_like(dq_acc)
        dk_acc[...] = jnp.zeros_like(dk_acc)
        dv_acc[...] = jnp.zeros_like(dv_acc)
        for d, b_ref in zip(BRANCH_DILATIONS, b_refs):
            span = DIL_N * d

            def block(t, carry, d=d, b_ref=b_ref, span=span):
                r, b = t % d, t // d
                q0 = b * span + r
                qrows = _rows(q0, DIL_N, d)
                krows = _rows(C + q0 - span, 2 * DIL_N, d)
                q, dov = qf[qrows, :].astype(BF), dof[qrows, :].astype(BF)
                k2, v2 = kf[krows, :].astype(BF), vf[krows, :].astype(BF)
                Lq, dl = Lf[qrows, :], dlf[qrows, :]
                bias = jnp.where((c == 0) & (b == 0), b_ref[1], b_ref[0])
                s = _dot_nt(q, k2) * scale + bias
                p = jnp.exp(s - jnp.concatenate([Lq, Lq], axis=1))
                ds = p * (_dot_nt(dov, v2) - jnp.concatenate([dl, dl], axis=1))
                dsb, pb = ds.astype(BF), p.astype(BF)
                dq_acc[qrows, :] += _dot(dsb, k2) * scale
                dk2 = _dot_tn(dsb, q) * scale
                dv2 = _dot_tn(pb, dov)
                dk_acc[qrows, :] += dk2[DIL_N:]
                dv_acc[qrows, :] += dv2[DIL_N:]

                @pl.when(b >= 1)
                def _():
                    prows = _rows(q0 - span, DIL_N, d)
                    dk_acc[prows, :] += dk2[:DIL_N]
                    dv_acc[prows, :] += dv2[:DIL_N]

                return carry

            lax.fori_loop(0, C // DIL_N, block, 0, unroll=DIL_UNROLL)

            @pl.when(c < nc - 1)
            def _(d=d, b_ref=b_ref, span=span):
                def nxt(r, carry):
                    qrows = _rows(C + r, DIL_N, d)
                    krows = _rows(C - span + r, DIL_N, d)
                    q, dov = qf[qrows, :].astype(BF), dof[qrows, :].astype(BF)
                    k1, v1 = kf[_rows(2 * C - span + r, DIL_N, d), :].astype(BF), vf[_rows(2 * C - span + r, DIL_N, d), :].astype(BF)
                    s = _dot_nt(q, k1) * scale + b_ref[0][:, :DIL_N]
                    p = jnp.exp(s - Lf[qrows, :])
                    ds = p * (_dot_nt(dov, v1) - dlf[qrows, :])
                    dk_acc[krows, :] += _dot_tn(ds.astype(BF), q) * scale
                    dv_acc[krows, :] += _dot_tn(p.astype(BF), dov)
                    return carry

                lax.fori_loop(0, d, nxt, 0, unroll=min(d, DIL_UNROLL))

        dq_ref[...] = dq_acc[...].astype(BF)
        dk_ref[...] = dk_acc[...].astype(BF)
        dv_ref[...] = dv_acc[...].astype(BF)

    W3 = lambda off: (lambda h, c: (c, off * H + h))
    cur3 = lambda off: pl.BlockSpec((C, HEAD), lambda h, c: (c, off * H + h))
    prv3 = lambda off: pl.BlockSpec((C, HEAD), lambda h, c: (jnp.maximum(c - 1, 0), off * H + h))
    nxt3 = lambda off: pl.BlockSpec((C, HEAD), lambda h, c: (jnp.minimum(c + 1, nc - 1), off * H + h))
    cur1 = pl.BlockSpec((C, HEAD), lambda h, c: (c, h))
    nxt1 = pl.BlockSpec((C, HEAD), lambda h, c: (jnp.minimum(c + 1, nc - 1), h))
    tab = pl.BlockSpec((None, 2, DIL_N, 2 * DIL_N), lambda h, c: (h, 0, 0, 0))
    return pl.pallas_call(
        body, name="dil_attn_bwd", grid=(H, nc),
        in_specs=[cur3(0), nxt3(0), cur3(1), prv3(1), cur3(2), prv3(2), cur1, nxt1, cur1, nxt1, cur1, nxt1] + [tab] * nbr,
        out_specs=[cur1] * 3,
        out_shape=[jax.ShapeDtypeStruct((S, H * HEAD), BF)] * 3,
        scratch_shapes=[pltpu.VMEM((2 * C, HEAD), F32)] * 6 + [pltpu.VMEM((C, HEAD), F32)] * 3,
        compiler_params=_params(("parallel", "arbitrary")),
    )(qkv, qkv, qkv, qkv, qkv, qkv, do, do, L, L, delta, delta, *biases)


def _causal_mask(s, qi, ki, tq, tk, row0=0):
    qpos = qi * tq + row0 + lax.broadcasted_iota(jnp.int32, s.shape, 0)
    kpos = ki * tk + lax.broadcasted_iota(jnp.int32, s.shape, 1)
    return jnp.where(kpos <= qpos, s, NEG)


def mla_fwd(qn, qr, kv, kr, H, *, tq=1024, tk=1024, sub=512):
    S = qn.shape[0]
    tq, tk = _tile(S, tq), _tile(S, tk)
    sub = _tile(tq, sub)
    nk = S // tk
    scale = MLA_SCALE
    c2 = scale * LOG2E

    def body(qn_ref, qr_ref, kn_ref, kr_ref, v_ref, o_ref, L_ref, m_s, l_s, acc):
        qi, ki = pl.program_id(1), pl.program_id(2)

        @pl.when(ki == 0)
        def _():
            m_s[...] = jnp.full_like(m_s, NEG)
            l_s[...] = jnp.zeros_like(l_s)
            acc[...] = jnp.zeros_like(acc)

        def step(masked):
            k = jnp.concatenate([kn_ref[...], kr_ref[...]], axis=1)
            v = v_ref[...]
            for r in range(tq // sub):
                rows = slice(r * sub, (r + 1) * sub)
                q = jnp.concatenate([qn_ref[rows, :], qr_ref[rows, :]], axis=1)
                s = _dot_nt(q, k)
                if masked:
                    s = _causal_mask(s, qi, ki, tq, tk, r * sub)
                m_prev = m_s[rows, :]
                m_new = jnp.maximum(m_prev, jnp.max(s, axis=1, keepdims=True))
                alpha = jnp.exp2((m_prev - m_new) * c2)
                p = jnp.exp2((s - jnp.tile(m_new, (1, tk // LANES))) * c2)
                l_s[rows, :] = alpha * l_s[rows, :] + jnp.sum(p, axis=1, keepdims=True)
                acc[rows, :] = alpha * acc[rows, :] + _dot(p.astype(BF), v)
                m_s[rows, :] = m_new

        full = ki * tk + tk - 1 <= qi * tq
        live = ki * tk <= qi * tq + tq - 1
        pl.when(full)(lambda: step(False))
        pl.when(live & jnp.logical_not(full))(lambda: step(True))

        @pl.when(ki == nk - 1)
        def _():
            o_ref[...] = (acc[...] / l_s[...]).astype(BF)
            L_ref[...] = m_s[...] * scale + jnp.log(l_s[...])

    kcl = lambda qi, ki: jnp.minimum(ki, (qi * tq + tq - 1) // tk)
    qs = pl.BlockSpec((tq, HEAD), lambda h, qi, ki: (qi, h))
    return pl.pallas_call(
        body, name="mla_fwd", grid=(H, S // tq, nk),
        in_specs=[qs, qs, pl.BlockSpec((tk, HEAD), lambda h, qi, ki: (kcl(qi, ki), h)),
                  pl.BlockSpec((tk, LANES), lambda h, qi, ki: (kcl(qi, ki), 0)),
                  pl.BlockSpec((tk, HEAD), lambda h, qi, ki: (kcl(qi, ki), H + h))],
        out_specs=[qs, qs],
        out_shape=[jax.ShapeDtypeStruct((S, H * HEAD), BF), jax.ShapeDtypeStruct((S, H * HEAD), F32)],
        scratch_shapes=[pltpu.VMEM((tq, LANES), F32), pltpu.VMEM((tq, LANES), F32), pltpu.VMEM((tq, HEAD), F32)],
        compiler_params=_params(("parallel", "parallel", "arbitrary")),
    )(qn, qr, kv, kr, kv)


def _mla_probs(s, L_rows, c2, width):
    return jnp.exp2(s * c2 - jnp.tile(L_rows * LOG2E, (1, width // LANES)))


def mla_bwd_q(qn, qr, kv, kr, do, L, delta, rope, H, *, tq=1024, tk=1024, sub=256):
    S = qn.shape[0]
    tq, tk = _tile(S, tq), _tile(S, tk)
    sub = _tile(tq, sub)
    nk = S // tk
    scale = MLA_SCALE
    c2 = scale * LOG2E

    def body(qn_ref, qr_ref, kn_ref, kr_ref, v_ref, do_ref, L_ref, dl_ref, cos_ref, sin_ref, dqn_ref, dqr_ref, acc):
        qi, ki = pl.program_id(1), pl.program_id(2)

        @pl.when(ki == 0)
        def _():
            acc[...] = jnp.zeros_like(acc)

        def step(masked):
            k = jnp.concatenate([kn_ref[...], kr_ref[...]], axis=1)
            v = v_ref[...]
            for r in range(tq // sub):
                rows = slice(r * sub, (r + 1) * sub)
                q = jnp.concatenate([qn_ref[rows, :], qr_ref[rows, :]], axis=1)
                s = _dot_nt(q, k)
                if masked:
                    s = _causal_mask(s, qi, ki, tq, tk, r * sub)
                p = _mla_probs(s, L_ref[rows, :], c2, tk)
                dp = _dot_nt(do_ref[rows, :], v)
                ds = (p * (dp - jnp.tile(dl_ref[rows, :], (1, tk // LANES)))).astype(BF)
                acc[rows, :] += _dot(ds, k)

        full = ki * tk + tk - 1 <= qi * tq
        live = ki * tk <= qi * tq + tq - 1
        pl.when(full)(lambda: step(False))
        pl.when(live & jnp.logical_not(full))(lambda: step(True))

        @pl.when(ki == nk - 1)
        def _():
            dq = acc[...] * scale
            dqn_ref[...] = dq[:, :HEAD].astype(BF)
            dqr_ref[...] = _rope_bwd(dq[:, HEAD:], cos_ref[...], sin_ref[...]).astype(BF)

    kcl = lambda qi, ki: jnp.minimum(ki, (qi * tq + tq - 1) // tk)
    qs = pl.BlockSpec((tq, HEAD), lambda h, qi, ki: (qi, h))
    tab = pl.BlockSpec((tq, LANES), lambda h, qi, ki: (qi, 0))
    return pl.pallas_call(
        body, name="mla_bwd_q", grid=(H, S // tq, nk),
        in_specs=[qs, qs, pl.BlockSpec((tk, HEAD), lambda h, qi, ki: (kcl(qi, ki), h)),
                  pl.BlockSpec((tk, LANES), lambda h, qi, ki: (kcl(qi, ki), 0)),
                  pl.BlockSpec((tk, HEAD), lambda h, qi, ki: (kcl(qi, ki), H + h)),
                  qs, qs, qs, tab, tab],
        out_specs=[qs, qs],
        out_shape=[jax.ShapeDtypeStruct((S, H * HEAD), BF)] * 2,
        scratch_shapes=[pltpu.VMEM((tq, 2 * HEAD), F32)],
        compiler_params=_params(("parallel", "parallel", "arbitrary")),
    )(qn, qr, kv, kr, kv, do, L, delta, *rope)


def mla_bwd_kv(qn, qr, kv, kr, do, L, delta, rope, H, prev=None, *, tq=1024, tk=1024, sub=256):
    S = qn.shape[0]
    tq, tk = _tile(S, tq), _tile(S, tk)
    sub = _tile(tq, sub)
    nq = S // tq
    scale = MLA_SCALE
    c2 = scale * LOG2E
    has_prev = prev is not None

    def body(*refs):
        (qn_ref, qr_ref, kn_ref, kr_ref, v_ref, do_ref, L_ref, dl_ref, cos_ref, sin_ref) = refs[:10]
        refs = refs[10:]
        if has_prev:
            pk_ref, pv_ref, pr_ref = refs[:3]
            refs = refs[3:]
        dk_ref, dv_ref, dr_ref, dk_acc, dv_acc, dr_acc = refs
        ki, h, qi = pl.program_id(0), pl.program_id(1), pl.program_id(2)

        @pl.when(qi == 0)
        def _():
            dk_acc[...] = jnp.zeros_like(dk_acc)
            dv_acc[...] = jnp.zeros_like(dv_acc)

        @pl.when((qi == 0) & (h == 0))
        def _():
            dr_acc[...] = jnp.zeros_like(dr_acc)

        def step(masked):
            k = jnp.concatenate([kn_ref[...], kr_ref[...]], axis=1)
            v = v_ref[...]
            for r in range(tq // sub):
                rows = slice(r * sub, (r + 1) * sub)
                q = jnp.concatenate([qn_ref[rows, :], qr_ref[rows, :]], axis=1)
                s = _dot_nt(q, k)
                if masked:
                    s = _causal_mask(s, qi, ki, tq, tk, r * sub)
                p = _mla_probs(s, L_ref[rows, :], c2, tk)
                dov = do_ref[rows, :]
                dv_acc[...] += _dot_tn(p.astype(BF), dov)
                dp = _dot_nt(dov, v)
                ds = (p * (dp - jnp.tile(dl_ref[rows, :], (1, tk // LANES)))).astype(BF)
                dk_acc[...] += _dot_tn(ds, q)

        full = ki * tk + tk - 1 <= qi * tq
        live = ki * tk <= qi * tq + tq - 1
        pl.when(full)(lambda: step(False))
        pl.when(live & jnp.logical_not(full))(lambda: step(True))

        @pl.when(qi == nq - 1)
        def _():
            dk = dk_acc[...] * scale
            dkn, dv = dk[:, :HEAD], dv_acc[...]
            if has_prev:
                dkn, dv = dkn + pk_ref[...], dv + pv_ref[...]
            dk_ref[...] = dkn
            dv_ref[...] = dv
            dr_acc[...] += dk[:, HEAD:]

        @pl.when((qi == nq - 1) & (h == H - 1))
        def _():
            dr = _rope_bwd(dr_acc[...], cos_ref[...], sin_ref[...])
            dr_ref[...] = dr + pr_ref[...] if has_prev else dr

    qcl = lambda ki, qi: jnp.maximum(qi, (ki * tk) // tq)
    qs = pl.BlockSpec((tq, HEAD), lambda ki, h, qi: (qcl(ki, qi), h))
    kn = pl.BlockSpec((tk, HEAD), lambda ki, h, qi: (ki, h))
    vs = pl.BlockSpec((tk, HEAD), lambda ki, h, qi: (ki, H + h))
    k1 = pl.BlockSpec((tk, LANES), lambda ki, h, qi: (ki, 0))
    in_specs = [qs, qs, kn, k1, vs, qs, qs, qs, k1, k1]
    args = [qn, qr, kv, kr, kv, do, L, delta, *rope]
    if has_prev:
        in_specs += [kn, vs, k1]
        args += [prev[0], prev[0], prev[1]]
    dkn, dv, dr = pl.pallas_call(
        body, name="mla_bwd_kv", grid=(S // tk, H, nq), in_specs=in_specs,
        out_specs=[kn, kn, k1],
        out_shape=[jax.ShapeDtypeStruct((S, H * HEAD), F32), jax.ShapeDtypeStruct((S, H * HEAD), F32),
                   jax.ShapeDtypeStruct((S, LANES), F32)],
        scratch_shapes=[pltpu.VMEM((tk, 2 * HEAD), F32), pltpu.VMEM((tk, HEAD), F32), pltpu.VMEM((tk, LANES), F32)],
        compiler_params=_params(("parallel", "arbitrary", "arbitrary")),
    )(*args)
    return dkn, dv, dr


def mla_bwd(qn, qr, kv, kr, do, L, delta, rope, H, prev=None, *, tq=1024, tk=1024, sub=512):
    S = qn.shape[0]
    tq, tk = _tile(S, tq), _tile(S, tk)
    sub = _tile(tq, sub)
    nq, nk = S // tq, S // tk
    scale = MLA_SCALE
    c2 = scale * LOG2E
    has_prev = prev is not None

    def body(*refs):
        (qn_ref, qr_ref, kn_ref, kr_ref, v_ref, do_ref, L_ref, dl_ref, cq_ref, sq_ref, ck_ref, sk_ref) = refs[:12]
        refs = refs[12:]
        if has_prev:
            pk_ref, pv_ref, pr_ref = refs[:3]
            refs = refs[3:]
        dqn_ref, dqr_ref, dk_ref, dv_ref, dr_ref, dq_full, dk_acc, dv_acc = refs
        h, ki, qi = pl.program_id(0), pl.program_id(1), pl.program_id(2)
        qrows = pl.ds(pl.multiple_of(qi * tq, tq), tq)
        krows = pl.ds(pl.multiple_of(ki * tk, tk), tk)

        @pl.when(qi == 0)
        def _():
            dk_acc[...] = jnp.zeros_like(dk_acc)
            dv_acc[...] = jnp.zeros_like(dv_acc)

        @pl.when((qi == 0) & (h == 0))
        def _():
            dr_ref[krows, :] = jnp.zeros((tk, LANES), F32)

        @pl.when(ki == 0)
        def _():
            dq_full[qrows, :] = jnp.zeros((tq, 2 * HEAD), F32)

        def step(masked):
            k = jnp.concatenate([kn_ref[...], kr_ref[...]], axis=1)
            v = v_ref[...]
            for r in range(tq // sub):
                rows = slice(r * sub, (r + 1) * sub)
                q = jnp.concatenate([qn_ref[rows, :], qr_ref[rows, :]], axis=1)
                s = _dot_nt(q, k)
                if masked:
                    s = _causal_mask(s, qi, ki, tq, tk, r * sub)
                p = _mla_probs(s, L_ref[rows, :], c2, tk)
                dov = do_ref[rows, :]
                dv_acc[...] += _dot_tn(p.astype(BF), dov)
                dp = _dot_nt(dov, v)
                ds = (p * (dp - jnp.tile(dl_ref[rows, :], (1, tk // LANES)))).astype(BF)
                dk_acc[...] += _dot_tn(ds, q)
                dq_full[pl.ds(pl.multiple_of(qi * tq + r * sub, sub), sub), :] += _dot(ds, k)

        full = ki * tk + tk - 1 <= qi * tq
        live = ki * tk <= qi * tq + tq - 1
        pl.when(full)(lambda: step(False))
        pl.when(live & jnp.logical_not(full))(lambda: step(True))

        @pl.when(ki == (qi * tq + tq - 1) // tk)
        def _():
            dq = dq_full[qrows, :] * scale
            dqn_ref[qrows, :] = dq[:, :HEAD].astype(BF)
            dqr_ref[qrows, :] = _rope_bwd(dq[:, HEAD:], cq_ref[...], sq_ref[...]).astype(BF)

        @pl.when(qi == nq - 1)
        def _():
            dk = dk_acc[...] * scale
            dkn, dv = dk[:, :HEAD], dv_acc[...]
            if has_prev:
                dkn, dv = dkn + pk_ref[...], dv + pv_ref[...]
            dk_ref[...] = dkn
            dv_ref[...] = dv
            dr_ref[krows, :] += dk[:, HEAD:]

        @pl.when((qi == nq - 1) & (h == H - 1))
        def _():
            dr = _rope_bwd(dr_ref[krows, :], ck_ref[...], sk_ref[...])
            dr_ref[krows, :] = dr + pr_ref[...] if has_prev else dr

    qcl = lambda ki, qi: jnp.maximum(qi, (ki * tk) // tq)
    qs = pl.BlockSpec((tq, HEAD), lambda h, ki, qi: (qcl(ki, qi), h))
    qt = pl.BlockSpec((tq, LANES), lambda h, ki, qi: (qcl(ki, qi), 0))
    kn = pl.BlockSpec((tk, HEAD), lambda h, ki, qi: (ki, h))
    vs = pl.BlockSpec((tk, HEAD), lambda h, ki, qi: (ki, H + h))
    k1 = pl.BlockSpec((tk, LANES), lambda h, ki, qi: (ki, 0))
    head = pl.BlockSpec((S, HEAD), lambda h, ki, qi: (0, h))
    in_specs = [qs, qs, kn, k1, vs, qs, qs, qs, qt, qt, k1, k1]
    args = [qn, qr, kv, kr, kv, do, L, delta, *rope, *rope]
    if has_prev:
        in_specs += [kn, vs, k1]
        args += [prev[0], prev[0], prev[1]]
    return pl.pallas_call(
        body, name="mla_bwd", grid=(H, nk, nq), in_specs=in_specs,
        out_specs=[head, head, kn, kn, pl.BlockSpec((S, LANES), lambda h, ki, qi: (0, 0))],
        out_shape=[jax.ShapeDtypeStruct((S, H * HEAD), BF), jax.ShapeDtypeStruct((S, H * HEAD), BF),
                   jax.ShapeDtypeStruct((S, H * HEAD), F32), jax.ShapeDtypeStruct((S, H * HEAD), F32),
                   jax.ShapeDtypeStruct((S, LANES), F32)],
        scratch_shapes=[pltpu.VMEM((S, 2 * HEAD), F32), pltpu.VMEM((tk, 2 * HEAD), F32), pltpu.VMEM((tk, HEAD), F32)],
        compiler_params=_params(("arbitrary", "arbitrary", "arbitrary")),
    )(*args)


def adamw(recvs, w, m, v, *, name="adamw"):
    R, C = w.shape
    n = len(recvs)
    R1 = R // n
    assert all(r.shape == (N_DEV, R1, C) for r in recvs)
    tr = 16
    while tr * 2 * C * 44 <= 6 * 1024 * 1024 and R1 % (tr * 2) == 0:
        tr *= 2
    tr = min(tr, R1)
    assert R1 % tr == 0
    nb = R1 // tr

    def body(*refs):
        r_refs = refs[:n]
        w_ref, m_ref, v_ref, g_ref, d_ref, mo_ref, vo_ref = refs[n:]
        for slab in range(n):
            @pl.when(pl.program_id(0) == slab)
            def _(r_ref=r_refs[slab]):
                g = r_ref[0].astype(F32)
                for s in range(1, N_DEV):
                    g = g + r_ref[s].astype(F32)
                m2 = ADAM_B1 * m_ref[...] + (1.0 - ADAM_B1) * g
                v2 = ADAM_B2 * v_ref[...] + (1.0 - ADAM_B2) * (g * g)
                m_hat = m2 / (1.0 - ADAM_B1 ** ADAM_STEP)
                v_hat = v2 / (1.0 - ADAM_B2 ** ADAM_STEP)
                g_ref[...] = g
                d_ref[...] = -ADAM_LR * (m_hat / (jnp.sqrt(v_hat) + ADAM_EPS) + ADAM_WD * w_ref[...])
                mo_ref[...] = m2
                vo_ref[...] = v2

    def recv_spec(slab):
        return pl.BlockSpec((N_DEV, tr, C), lambda l, i: (0, jnp.where(l == slab, i, jnp.where(l < slab, 0, nb - 1)), 0))

    row = pl.BlockSpec((tr, C), lambda l, i: (l * nb + i, 0))
    return pl.pallas_call(
        body, name=name, grid=(n, nb),
        in_specs=[recv_spec(slab) for slab in range(n)] + [row, row, row],
        out_specs=[row] * 4, out_shape=[jax.ShapeDtypeStruct((R, C), F32)] * 4,
        compiler_params=_params(("arbitrary", "arbitrary")),
    )(*recvs, w, m, v)


def _pack(ts, width, dtype, row_mult=16):
    flat = jnp.concatenate([t.astype(dtype).reshape(-1) for t in ts])
    n = flat.shape[0]
    rows = -(-n // (width * row_mult)) * row_mult
    return jnp.pad(flat, (0, rows * width - n)).reshape(rows, width)


def _unpack(buf, shapes):
    lead = buf.shape[:-2]
    flat = buf.reshape(lead + (-1,))
    out, off = [], 0
    for s in shapes:
        n = int(np.prod(s))
        out.append(flat[..., off:off + n].reshape(lead + tuple(s)))
        off += n
    return out


def _rope_pad(r):
    z = jnp.zeros(r.shape[:-1] + (ROPE_HALF,), r.dtype)
    return jnp.concatenate([r[..., :ROPE_HALF], z, r[..., ROPE_HALF:], z], axis=-1)


def _rope_unpad(r):
    return jnp.concatenate([r[..., :ROPE_HALF], r[..., 2 * ROPE_HALF:3 * ROPE_HALF]], axis=-1)


def _step(P):
    x0 = P['x'][0]
    target = P['loss_target'][0]
    S, D = x0.shape
    NL = P['ffn_norm1'].shape[0]
    NA = P['a_wqkv'].shape[0]
    Fs = P['ffn1_wg'].shape[2]
    H = D // HEAD
    KV = P['b_wdkv'].shape[1]
    QL = P['b_wdq'].shape[2]
    HW = H * HEAD

    ffn_seq = [(l, f) for l in range(NL) for f in (0, 1)]
    ffn_loc = {(l, f): [P[n][l].astype(BF) for n in (f'ffn{f + 1}_wg', f'ffn{f + 1}_wu', f'ffn{f + 1}_wd')]
               for l, f in ffn_seq}
    mix_shapes = [P[n].shape for n in MIX_W]
    wc_loc = _pack([P[n] for n in MIX_W], D, BF)
    *w_first, WC = all_gather(ffn_loc[ffn_seq[0]] + [wc_loc])
    ffn_w = {ffn_seq[0]: w_first}
    g = dict(zip(MIX_W, _unpack(WC, mix_shapes)))
    cols = lambda t: jnp.moveaxis(t, 0, -2).reshape(t.shape[1:-1] + (N_DEV * t.shape[-1],))
    rows = lambda t, lead: jnp.moveaxis(t, 0, lead).reshape(t.shape[1:1 + lead] + (N_DEV * t.shape[1 + lead],) + t.shape[2 + lead:])
    a_wqkv = cols(g['a_wqkv'])
    a_wo = rows(g['a_wo'], 1)
    b_wo = rows(g['b_wo'], 1)
    wdkv = rows(g['b_wdkv'], 0)
    wkr = _rope_pad(rows(g['b_wkr'], 0))
    wuk = rows(g['b_wuk'], 0).reshape(KV, HW)
    wuv = rows(g['b_wuv'], 0).reshape(KV, HW)
    wkv = jnp.concatenate([wuk, wuv], axis=1)
    wdq = rows(g['b_wdq'], 1)
    wuq = rows(g['b_wuq'], 1)
    wuq_n = wuq[..., :HEAD].reshape(-1, QL, HW)
    wuq_r = _rope_pad(wuq[..., HEAD:]).reshape(-1, QL, HW)

    def ffn_forward(x, gain, key):
        nxt = ffn_seq.index(key) + 1
        carry = ffn_loc[ffn_seq[nxt]] if nxt < len(ffn_seq) else []
        xo, h, G, U, *got = ffn_fwd(x, gain, *ffn_w[key], carry=carry)
        if carry:
            ffn_w[ffn_seq[nxt]] = got
        return xo, h, G, U

    inv = 1.0 / (ROPE_THETA ** (jnp.arange(0, 2 * ROPE_HALF, 2, dtype=F32) / (2 * ROPE_HALF)))
    ang = jnp.arange(S, dtype=F32)[:, None] * inv[None, :]
    z = jnp.zeros((S, ROPE_HALF), F32)
    rope = (jnp.concatenate([jnp.cos(ang), z, jnp.cos(ang), z], axis=1),
            jnp.concatenate([-jnp.sin(ang), z, jnp.sin(ang), z], axis=1))
    biases = [dilated_bias(H, d) for d in BRANCH_DILATIONS]

    saved = []
    kvs = None
    x = x0
    for l in range(NL):
        st = {'x_in': x}
        if l == NA:
            ckv_pre, hkv = norm_mm(x, P['kv_norm'], wdkv, out_dtype=F32, name="kv_down")
            kr = norm_mm(x, P['kv_norm'], wkr, out_dtype=BF, tn=LANES, rope=rope, write_h=False, name="kv_rope")
            kvm, ckv = norm_mm(ckv_pre, P['b_ckv_norm'], wkv, out_dtype=BF, name="kv_up")
            kvs = dict(x=x, ckv_pre=ckv_pre, hkv=hkv, kr=kr, kv=kvm, ckv=ckv)
        xa, h1, G1, U1 = ffn_forward(x, P['ffn_norm1'][l], (l, 0))
        st.update(h1=h1, G1=G1, U1=U1, xa=xa)
        if l < NA:
            qkv, hm = norm_mm(xa, P['mix_norm'][l], a_wqkv[l], out_dtype=BF, tn=3 * HW // 4, name="a_qkv")
            o, Lj = dil_attn_fwd(qkv, biases, H)
            xb = mm(o, a_wo[l], add=xa, name="mix_out")
            st.update(qkv=qkv, hm=hm, o=o, L=Lj)
        else:
            jb = l - NA
            cq_pre, hm = norm_mm(xa, P['mix_norm'][l], wdq[jb], out_dtype=F32, name="q_down")
            qn, cq = norm_mm(cq_pre, P['b_cq_norm'][jb], wuq_n[jb], out_dtype=BF, name="q_up")
            qr = norm_mm(cq_pre, P['b_cq_norm'][jb], wuq_r[jb], out_dtype=BF, tn=LANES, rope=rope, write_h=False,
                         name="q_rope")
            o, Lj = mla_fwd(qn, qr, kvs['kv'], kvs['kr'], H)
            xb = mm(o, b_wo[jb], add=xa, name="mix_out")
            st.update(cq_pre=cq_pre, hm=hm, qn=qn, qr=qr, cq=cq, o=o, L=Lj)
        x, h2, G2, U2 = ffn_forward(xb, P['ffn_norm2'][l], (l, 1))
        st.update(xb=xb, h2=h2, G2=G2, U2=U2)
        saved.append(st)

    loss_part, dx, dg_final = loss_head(x, P['final_norm'], target)

    gw = {}
    gv = {'final_norm': dg_final}
    dkv_acc = None
    for n in ('ffn_norm1', 'mix_norm', 'ffn_norm2'):
        gv[n] = [None] * NL
    gv['b_cq_norm'] = [None] * (NL - NA)
    ffn_r = {}
    pending = []
    a_g = {'a_wqkv': [None] * NA, 'a_wo': [None] * NA}
    b_g = {n: [None] * (NL - NA) for n in ('b_wdq', 'b_wuq', 'b_wo')}

    def ffn_backward(dy, x_in, gain, key, h, G, U):
        sent, carry = pending.pop() if pending else (None, [])
        dxi, dout, dG, dU, dgain, got = ffn_bwd_x(dy, x_in, gain, G, U, *ffn_w[key], carry=carry)
        if carry:
            ffn_r[sent] = got
        pending.append((key, list(ffn_bwd_w(h, dout, G, U, dG, dU))))
        return dxi, dgain

    for l in reversed(range(NL)):
        st = saved[l]
        dxb, gv['ffn_norm2'][l] = ffn_backward(dx, st['xb'], P['ffn_norm2'][l], (l, 1), st['h2'], st['G2'], st['U2'])
        xa = st['xa']
        if l < NA:
            do = mm(dxb, a_wo[l], tb=True, out_dtype=BF, name="mix_out_dx")
            a_g['a_wo'][l] = mm(st['o'], dxb, ta=True, out_dtype=BF, tm=1024, tn=1024, name="mix_out_dw")
            delta = head_delta(do, st['o'])
            dqkv = jnp.concatenate(dil_attn_bwd(st['qkv'], do, st['L'], delta, biases, H), axis=1)
            dh = mm(dqkv, a_wqkv[l], tb=True, name="a_qkv_dx")
            a_g['a_wqkv'][l] = mm(st['hm'], dqkv, ta=True, out_dtype=BF, tm=1024, tn=1024, name="a_qkv_dw")
            dxa, gv['mix_norm'][l] = norm_bwd(xa, P['mix_norm'][l], dh, dxb, name="mix_norm_bwd")
        else:
            jb = l - NA
            do = mm(dxb, b_wo[jb], tb=True, out_dtype=BF, name="mix_out_dx")
            b_g['b_wo'][jb] = mm(st['o'], dxb, ta=True, out_dtype=BF, tm=1024, tn=1024, name="mix_out_dw")
            delta = head_delta(do, st['o'])
            dqn, dqr, dkn, dv, dr = mla_bwd(st['qn'], st['qr'], kvs['kv'], kvs['kr'], do, st['L'], delta, rope, H,
                                            prev=dkv_acc)
            dkv_acc = (jnp.concatenate([dkn, dv], axis=1), dr)
            dcq = mm(dqn, wuq_n[jb], tb=True, name="q_up_dx")
            dcq = mm(dqr, wuq_r[jb], tb=True, add=dcq, name="q_up_dx_add")
            dwn = mm(st['cq'], dqn, ta=True, out_dtype=F32, name="q_up_dw")
            dwr = mm(st['cq'], dqr, ta=True, out_dtype=F32, name="q_up_dw")
            b_g['b_wuq'][jb] = jnp.concatenate(
                [dwn.reshape(QL, H, HEAD), _rope_unpad(dwr.reshape(QL, H, HEAD))], axis=-1)
            dcq_pre, gv['b_cq_norm'][jb] = norm_bwd(st['cq_pre'], P['b_cq_norm'][jb], dcq, name="cq_norm_bwd")
            dh = mm(dcq_pre, wdq[jb], tb=True, name="q_down_dx")
            b_g['b_wdq'][jb] = mm(st['hm'], dcq_pre, ta=True, out_dtype=F32, tm=1024, name="q_down_dw")
            dxa, gv['mix_norm'][l] = norm_bwd(xa, P['mix_norm'][l], dh, dxb, name="mix_norm_bwd")
        dx, gv['ffn_norm1'][l] = ffn_backward(dxa, st['x_in'], P['ffn_norm1'][l], (l, 0), st['h1'], st['G1'], st['U1'])
        if l == NA:
            dkvm, dr = dkv_acc
            dckv = mm(dkvm, wkv, tb=True, name="kv_up_dx")
            dwkv = mm(kvs['ckv'], dkvm, ta=True, out_dtype=F32, name="kv_up_dw")
            gw['b_wuk'] = dwkv[:, :HW].reshape(KV, H, HEAD)
            gw['b_wuv'] = dwkv[:, HW:].reshape(KV, H, HEAD)
            dckv_pre, gv['b_ckv_norm'] = norm_bwd(kvs['ckv_pre'], P['b_ckv_norm'], dckv, name="ckv_norm_bwd")
            dh = mm(dckv_pre, wdkv, tb=True, name="kv_down_dx")
            dh = mm(dr, wkr, tb=True, add=dh, name="kv_rope_dx")
            gw['b_wdkv'] = mm(kvs['hkv'], dckv_pre, ta=True, out_dtype=F32, tm=1024, name="kv_down_dw")
            gw['b_wkr'] = _rope_unpad(mm(kvs['hkv'], dr, ta=True, out_dtype=F32, tm=1024, name="kv_rope_dw"))
            dx, gv['kv_norm'] = norm_bwd(kvs['x'], P['kv_norm'], dh, dx, name="kv_norm_bwd")

    for n in a_g:
        gw[n] = jnp.stack(a_g[n])
    for n in b_g:
        gw[n] = jnp.stack(b_g[n])
    for n in ('ffn_norm1', 'mix_norm', 'ffn_norm2', 'b_cq_norm'):
        gv[n] = jnp.stack(gv[n])

    split_cols = lambda t: jnp.moveaxis(t.reshape(t.shape[:-1] + (N_DEV, t.shape[-1] // N_DEV)), -2, 0)
    split_rows = lambda t, lead: jnp.moveaxis(
        t.reshape(t.shape[:lead] + (N_DEV, t.shape[lead] // N_DEV) + t.shape[lead + 1:]), lead, 0)
    shards = {
        'a_wqkv': split_cols(gw['a_wqkv']), 'a_wo': split_rows(gw['a_wo'], 1), 'b_wdkv': split_rows(gw['b_wdkv'], 0),
        'b_wkr': split_rows(gw['b_wkr'], 0), 'b_wuk': split_rows(gw['b_wuk'], 0), 'b_wuv': split_rows(gw['b_wuv'], 0),
        'b_wdq': split_rows(gw['b_wdq'], 1), 'b_wuq': split_rows(gw['b_wuq'], 1), 'b_wo': split_rows(gw['b_wo'], 1),
    }
    gc = jnp.stack([_pack([shards[n][p] for n in MIX_W], D, BF) for p in range(N_DEV)])
    vec_parts = [gv[n] for n in VEC_W] + [jnp.full((LANES,), loss_part, F32)]
    gvec = _pack(vec_parts, LANES, F32, row_mult=8)
    last_key, last = pending.pop()
    *got, rc, rv = exchange(last + [gc, gvec], [False] * len(last) + [False, True])
    ffn_r[last_key] = got

    res = {}
    for f in (0, 1):
        for which, kind in enumerate(('wg', 'wu', 'wd')):
            n = f'ffn{f + 1}_{kind}'
            recvs = [ffn_r[(l, f)][which] for l in range(NL)]
            flat = lambda t: t.reshape((NL * t.shape[1], t.shape[2]))
            out = adamw(recvs, flat(P[n]), flat(P['m_' + n]), flat(P['v_' + n]),
                        name="adamw_row" if kind == 'wd' else "adamw_col")
            res[n] = [t.reshape(P[n].shape) for t in out]
    pk = lambda pre: _pack([P[pre + n] for n in MIX_W], D, F32)
    out = adamw([rc], pk(''), pk('m_'), pk('v_'), name="adamw_mix")
    for n, parts in zip(MIX_W, zip(*[_unpack(t, mix_shapes) for t in out])):
        res[n] = list(parts)
    vec_shapes = [P[n].shape for n in VEC_W] + [(LANES,)]
    ones = jnp.ones((LANES,), F32)
    pv = lambda pre: _pack([P[pre + n] for n in VEC_W] + [ones], LANES, F32, row_mult=8)
    out = adamw([rv], pv(''), pv('m_'), pv('v_'), name="adamw_vec")
    unp = [_unpack(t, vec_shapes) for t in out]
    for idx, n in enumerate(VEC_W):
        res[n] = [u[idx] for u in unp]
    loss = unp[0][-1][0]

    outs = [loss, dx[None]]
    for field in range(4):
        outs += [res[n][field] for n in W_NAMES]
    return tuple(outs)


def kernel(x, ffn_norm1, ffn1_wg, ffn1_wu, ffn1_wd, mix_norm, ffn_norm2, ffn2_wg, ffn2_wu, ffn2_wd, a_wqkv, a_wo, kv_norm, b_wdkv, b_ckv_norm, b_wkr, b_wuk, b_wuv, b_wdq, b_cq_norm, b_wuq, b_wo, final_norm, loss_target, m_ffn_norm1, m_ffn1_wg, m_ffn1_wu, m_ffn1_wd, m_mix_norm, m_ffn_norm2, m_ffn2_wg, m_ffn2_wu, m_ffn2_wd, m_a_wqkv, m_a_wo, m_kv_norm, m_b_wdkv, m_b_ckv_norm, m_b_wkr, m_b_wuk, m_b_wuv, m_b_wdq, m_b_cq_norm, m_b_wuq, m_b_wo, m_final_norm, v_ffn_norm1, v_ffn1_wg, v_ffn1_wu, v_ffn1_wd, v_mix_norm, v_ffn_norm2, v_ffn2_wg, v_ffn2_wu, v_ffn2_wd, v_a_wqkv, v_a_wo, v_kv_norm, v_b_wdkv, v_b_ckv_norm, v_b_wkr, v_b_wuk, v_b_wuv, v_b_wdq, v_b_cq_norm, v_b_wuq, v_b_wo, v_final_norm):
    return _step(dict(locals()))
```

```python
import functools
import math

import numpy as np
import jax
import jax.numpy as jnp
from jax import lax
from jax.experimental import pallas as pl
from jax.experimental.pallas import tpu as pltpu

BF = jnp.bfloat16
F32 = jnp.float32
MESH = pl.DeviceIdType.MESH
ANY = pl.BlockSpec(memory_space=pl.ANY)

N_DEV = 8
LANES = 128
HEAD = 128
ROPE_HALF = 32
DIL_N = 128
BRANCH_DILATIONS = (1, 4, 16)
ROPE_THETA = 10000.0
MLA_SCALE = (HEAD + 2 * ROPE_HALF) ** -0.5
LOG2E = math.log2(math.e)
EPS = 1e-6
NEG = -1e30
VMEM_LIMIT = 58 * 1024 * 1024

ADAM_LR, ADAM_B1, ADAM_B2, ADAM_EPS, ADAM_WD, ADAM_STEP = 0.001, 0.9, 0.999, 1e-08, 0.01, 10

W_NAMES = ['ffn_norm1', 'ffn1_wg', 'ffn1_wu', 'ffn1_wd', 'mix_norm', 'ffn_norm2', 'ffn2_wg', 'ffn2_wu', 'ffn2_wd',
           'a_wqkv', 'a_wo', 'kv_norm', 'b_wdkv', 'b_ckv_norm', 'b_wkr', 'b_wuk', 'b_wuv', 'b_wdq', 'b_cq_norm',
           'b_wuq', 'b_wo', 'final_norm']
FFN_COL = ['ffn1_wg', 'ffn1_wu', 'ffn2_wg', 'ffn2_wu']
FFN_ROW = ['ffn1_wd', 'ffn2_wd']
MIX_W = ['a_wqkv', 'a_wo', 'b_wdkv', 'b_wkr', 'b_wuk', 'b_wuv', 'b_wdq', 'b_wuq', 'b_wo']
VEC_W = ['ffn_norm1', 'mix_norm', 'ffn_norm2', 'kv_norm', 'b_ckv_norm', 'b_cq_norm', 'final_norm']


def _tile(n, pref):
    t = min(n, pref)
    assert n % t == 0, (n, pref)
    return t


def _params(sem):
    return pltpu.CompilerParams(dimension_semantics=sem, vmem_limit_bytes=VMEM_LIMIT)


def _dot(a, b):
    return jnp.dot(a, b, preferred_element_type=F32)


def _dot_nt(a, b):
    return lax.dot_general(a, b, (((1,), (1,)), ((), ())), preferred_element_type=F32)


def _dot_tn(a, b):
    return lax.dot_general(a, b, (((0,), (0,)), ((), ())), preferred_element_type=F32)


def _rms(x, g):
    r = lax.rsqrt(jnp.mean(x * x, axis=-1, keepdims=True) + EPS)
    return x * r * g, r


def _rms_bwd(x, g, dh):
    r = lax.rsqrt(jnp.mean(x * x, axis=-1, keepdims=True) + EPS)
    xhat = x * r
    gd = dh * g
    dx = r * (gd - xhat * jnp.mean(gd * xhat, axis=-1, keepdims=True))
    return dx, jnp.sum(dh * xhat, axis=0, keepdims=True)


def _rope(t, cos, sin):
    return t * cos + pltpu.roll(t, 2 * ROPE_HALF, 1) * sin


def _rope_bwd(g, cos, sin):
    return g * cos + pltpu.roll(g * sin, 2 * ROPE_HALF, 1)


def _coords():
    return lax.axis_index("x"), lax.axis_index("y"), lax.axis_index("c")


def _lin(px, py, pc):
    return 4 * px + 2 * py + pc


def _gather_phases(ins, outs, ssem, rsem, lsem):
    n = len(ins)
    x, y, c = _coords()
    me, sibling = (x, y, c), (x, y, 1 - c)
    chips = [(1 - x, y), (x, 1 - y), (1 - x, 1 - y)]

    def copy(t, k, block, to, src=None):
        slot = outs[t].at[_lin(*block)]
        return pltpu.make_async_remote_copy(
            src_ref=slot if src is None else src, dst_ref=slot,
            send_sem=ssem.at[7 * t + k], recv_sem=rsem.at[7 * t + k],
            device_id=to, device_id_type=MESH)

    def mine(t):
        return pltpu.make_async_copy(ins[t], outs[t].at[_lin(*me)], lsem.at[t])

    def first(t):
        return [copy(t, 0, me, sibling, src=ins[t])] + [copy(t, 1 + j, me, (*chip, c), src=ins[t])
                                                       for j, chip in enumerate(chips)]

    def passed(t):
        return [copy(t, 4 + j, (*chip, c), sibling) for j, chip in enumerate(chips)]

    def start():
        for t in range(n):
            mine(t).start()
            for cp in first(t):
                cp.start()

    def middle():
        for t in range(n):
            fw = passed(t)
            for j, chip in enumerate(chips):
                copy(t, 1 + j, (*chip, c), me).wait_recv()
                fw[j].start()

    def finish():
        for t in range(n):
            copy(t, 0, sibling, me).wait_recv()
            for j, chip in enumerate(chips):
                copy(t, 4 + j, (*chip, 1 - c), me).wait_recv()
        for t in range(n):
            for cp in first(t) + passed(t):
                cp.wait_send()
            mine(t).wait()

    return start, middle, finish


def _exchange_phases(ins, outs, bcast, ssem, rsem, lsem):
    n = len(ins)
    x, y, c = _coords()
    me = _lin(x, y, c)
    peers = []
    for k in range(1, N_DEV):
        kx, ky, kc = (k >> 2) & 1, (k >> 1) & 1, k & 1
        peers.append((k, (1 - x if kx else x, 1 - y if ky else y, 1 - c if kc else c)))

    def own(t):
        return pltpu.make_async_copy(ins[t] if bcast[t] else ins[t].at[me], outs[t].at[me], lsem.at[t])

    def send(t, k, peer):
        return pltpu.make_async_remote_copy(
            src_ref=ins[t] if bcast[t] else ins[t].at[_lin(*peer)], dst_ref=outs[t].at[me],
            send_sem=ssem.at[7 * t + k - 1], recv_sem=rsem.at[7 * t + k - 1],
            device_id=peer, device_id_type=MESH)

    def arrival(t, k, peer):
        slot = outs[t].at[_lin(*peer)]
        return pltpu.make_async_remote_copy(
            src_ref=slot, dst_ref=slot, send_sem=ssem.at[7 * t + k - 1], recv_sem=rsem.at[7 * t + k - 1],
            device_id=peer, device_id_type=MESH)

    def start():
        for t in range(n):
            own(t).start()
            for k, peer in peers:
                send(t, k, peer).start()

    def finish():
        for t in range(n):
            for k, peer in peers:
                arrival(t, k, peer).wait_recv()
        for t in range(n):
            for k, peer in peers:
                send(t, k, peer).wait_send()
            own(t).wait()

    return start, finish


def _comm_sems(n):
    return [pltpu.SemaphoreType.DMA((7 * n,)), pltpu.SemaphoreType.DMA((7 * n,)), pltpu.SemaphoreType.DMA((n,))]


def _gathered(xs):
    return [jax.ShapeDtypeStruct((N_DEV,) + a.shape, a.dtype) for a in xs]


def all_gather(xs):
    n = len(xs)

    def body(*refs):
        start, middle, finish = _gather_phases(refs[:n], refs[n:2 * n], *refs[2 * n:])
        start()
        middle()
        finish()

    return pl.pallas_call(
        body, name="all_gather", out_shape=_gathered(xs),
        in_specs=[ANY] * n, out_specs=[ANY] * n, scratch_shapes=_comm_sems(n),
    )(*xs)


def exchange(xs, bcast):
    n = len(xs)

    def body(*refs):
        start, finish = _exchange_phases(refs[:n], refs[n:2 * n], bcast, *refs[2 * n:])
        start()
        finish()

    out_shape = [jax.ShapeDtypeStruct(((N_DEV,) + a.shape) if b else a.shape, a.dtype) for a, b in zip(xs, bcast)]
    return pl.pallas_call(
        body, name="grad_exchange", out_shape=out_shape,
        in_specs=[ANY] * n, out_specs=[ANY] * n, scratch_shapes=_comm_sems(n),
    )(*xs)


def mm(a, b, *, ta=False, tb=False, add=None, out_dtype=F32, tm=512, tn=1024, tk=None, name="mm"):
    K, M = a.shape if ta else a.shape[::-1]
    N = b.shape[0] if tb else b.shape[1]
    assert (b.shape[1] if tb else b.shape[0]) == K and not (ta and tb)
    if tk is None:
        tk = 512 if ta else 2048
    tm, tn, tk = _tile(M, tm), _tile(N, tn), _tile(K, tk)
    nk = K // tk
    has_add = add is not None

    def body(*refs):
        if has_add:
            a_ref, b_ref, add_ref, o_ref, acc = refs
        else:
            a_ref, b_ref, o_ref, acc = refs
        k = pl.program_id(2)

        @pl.when(k == 0)
        def _():
            acc[...] = jnp.zeros_like(acc)

        av = a_ref[...].astype(BF)
        bv = b_ref[...].astype(BF)
        if ta:
            acc[...] += _dot_tn(av, bv)
        elif tb:
            acc[...] += _dot_nt(av, bv)
        else:
            acc[...] += _dot(av, bv)

        @pl.when(k == nk - 1)
        def _():
            r = acc[...]
            if has_add:
                r = r + add_ref[...]
            o_ref[...] = r.astype(out_dtype)

    a_spec = pl.BlockSpec((tk, tm), lambda i, j, k: (k, i)) if ta else pl.BlockSpec((tm, tk), lambda i, j, k: (i, k))
    b_spec = pl.BlockSpec((tn, tk), lambda i, j, k: (j, k)) if tb else pl.BlockSpec((tk, tn), lambda i, j, k: (k, j))
    o_spec = pl.BlockSpec((tm, tn), lambda i, j, k: (i, j))
    in_specs = [a_spec, b_spec] + ([o_spec] if has_add else [])
    args = [a, b] + ([add] if has_add else [])
    return pl.pallas_call(
        body, name=name, grid=(M // tm, N // tn, nk),
        in_specs=in_specs, out_specs=o_spec,
        out_shape=jax.ShapeDtypeStruct((M, N), out_dtype),
        scratch_shapes=[pltpu.VMEM((tm, tn), F32)],
        compiler_params=_params(("parallel", "parallel", "arbitrary")),
    )(*args)


def norm_mm(x, gain, w, *, out_dtype, tn=512, rope=None, write_h=True, tm=512, name="norm_mm"):
    S, K = x.shape
    N = w.shape[1]
    tm, tn = _tile(S, tm), _tile(N, tn)
    if rope is not None:
        assert tn == LANES
    gain = gain.reshape(1, K)

    def body(*refs):
        refs = list(refs)
        x_ref, g_ref, w_ref = refs[:3]
        refs = refs[3:]
        if rope is not None:
            cos_ref, sin_ref = refs[:2]
            refs = refs[2:]
        y_ref = refs[0]
        h_ref = refs[1] if write_h else None
        hs = refs[-1]
        j = pl.program_id(1)

        @pl.when(j == 0)
        def _():
            h, _ = _rms(x_ref[...], g_ref[...])
            hb = h.astype(BF)
            hs[...] = hb
            if write_h:
                h_ref[...] = hb

        y = _dot(hs[...], w_ref[...])
        if rope is not None:
            y = _rope(y, cos_ref[...], sin_ref[...])
        y_ref[...] = y.astype(out_dtype)

    in_specs = [pl.BlockSpec((tm, K), lambda i, j: (i, 0)), pl.BlockSpec((1, K), lambda i, j: (0, 0)),
                pl.BlockSpec((K, tn), lambda i, j: (0, j))]
    args = [x, gain, w]
    if rope is not None:
        in_specs += [pl.BlockSpec((tm, LANES), lambda i, j: (i, 0))] * 2
        args += list(rope)
    out_specs = [pl.BlockSpec((tm, tn), lambda i, j: (i, j))]
    out_shape = [jax.ShapeDtypeStruct((S, N), out_dtype)]
    if write_h:
        out_specs.append(pl.BlockSpec((tm, K), lambda i, j: (i, 0)))
        out_shape.append(jax.ShapeDtypeStruct((S, K), BF))
    res = pl.pallas_call(
        body, name=name, grid=(S // tm, N // tn), in_specs=in_specs, out_specs=out_specs, out_shape=out_shape,
        scratch_shapes=[pltpu.VMEM((tm, K), BF)],
        compiler_params=_params(("parallel", "arbitrary")),
    )(*args)
    return res if write_h else res[0]


def norm_bwd(x, gain, dh, dres=None, *, tm=512, name="norm_bwd"):
    S, K = x.shape
    tm = _tile(S, tm)
    gain = gain.reshape(1, K)
    has_res = dres is not None

    def body(*refs):
        if has_res:
            x_ref, g_ref, dh_ref, dr_ref, dx_ref, dg_ref = refs
        else:
            x_ref, g_ref, dh_ref, dx_ref, dg_ref = refs

        @pl.when(pl.program_id(0) == 0)
        def _():
            dg_ref[...] = jnp.zeros_like(dg_ref)

        dx, dg = _rms_bwd(x_ref[...], g_ref[...], dh_ref[...].astype(F32))
        if has_res:
            dx = dx + dr_ref[...]
        dx_ref[...] = dx
        dg_ref[...] += jnp.broadcast_to(dg, dg_ref.shape)

    row = pl.BlockSpec((tm, K), lambda i: (i, 0))
    in_specs = [row, pl.BlockSpec((1, K), lambda i: (0, 0)), row] + ([row] if has_res else [])
    args = [x, gain, dh] + ([dres] if has_res else [])
    dx, dg = pl.pallas_call(
        body, name=name, grid=(S // tm,), in_specs=in_specs,
        out_specs=[row, pl.BlockSpec((8, K), lambda i: (0, 0))],
        out_shape=[jax.ShapeDtypeStruct((S, K), F32), jax.ShapeDtypeStruct((8, K), F32)],
        compiler_params=_params(("arbitrary",)),
    )(*args)
    return dx, dg[0]


def loss_head(x, gain, target, *, tm=512):
    S, K = x.shape
    tm = _tile(S, tm)
    gain = gain.reshape(1, K)

    def body(x_ref, g_ref, t_ref, dx_ref, dg_ref, ls_ref):
        @pl.when(pl.program_id(0) == 0)
        def _():
            dg_ref[...] = jnp.zeros_like(dg_ref)
            ls_ref[...] = jnp.zeros_like(ls_ref)

        xv, g = x_ref[...], g_ref[...]
        y, _ = _rms(xv, g)
        e = y - t_ref[...]
        part = jnp.sum(jnp.mean(e * e, axis=-1, keepdims=True), axis=0, keepdims=True)
        ls_ref[...] += jnp.broadcast_to(0.5 * part, ls_ref.shape)
        dx, dg = _rms_bwd(xv, g, e / K)
        dx_ref[...] = dx
        dg_ref[...] += jnp.broadcast_to(dg, dg_ref.shape)

    row = pl.BlockSpec((tm, K), lambda i: (i, 0))
    dx, dg, ls = pl.pallas_call(
        body, name="loss_head", grid=(S // tm,),
        in_specs=[row, pl.BlockSpec((1, K), lambda i: (0, 0)), row],
        out_specs=[row, pl.BlockSpec((8, K), lambda i: (0, 0)), pl.BlockSpec((8, LANES), lambda i: (0, 0))],
        out_shape=[jax.ShapeDtypeStruct((S, K), F32), jax.ShapeDtypeStruct((8, K), F32),
                   jax.ShapeDtypeStruct((8, LANES), F32)],
        compiler_params=_params(("arbitrary",)),
    )(x, gain, target)
    return ls[0, 0], dx, dg[0]


def _once(shape, index_map):
    return pl.BlockSpec(shape, index_map, pipeline_mode=pl.Buffered(1))


def ffn_fwd(x, gain, wg, wu, wd, carry=(), *, tm=512):
    S, D = x.shape
    NB, _, Fs = wg.shape
    tm = _tile(S, tm)
    ni = S // tm
    nc = len(carry)
    gain = gain.reshape(1, D)

    def body(*refs):
        x_ref, g_ref, wg_ref, wu_ref, wd_ref = refs[:5]
        c_in = refs[5:5 + nc]
        xo_ref, h_ref, G_ref, U_ref = refs[5 + nc:9 + nc]
        c_out = refs[9 + nc:9 + 2 * nc]
        hs, acc = refs[9 + 2 * nc:11 + 2 * nc]
        i, j = pl.program_id(0), pl.program_id(1)
        if nc:
            start, middle, finish = _gather_phases(c_in, c_out, *refs[11 + 2 * nc:])
            pl.when((i == 0) & (j == 0))(start)
            pl.when((i == ni // 2) & (j == 0))(middle)

        @pl.when(j == 0)
        def _():
            h, _ = _rms(x_ref[...], g_ref[...])
            hb = h.astype(BF)
            hs[...] = hb
            h_ref[...] = hb
            acc[...] = jnp.zeros_like(acc)

        h = hs[...]
        g = _dot(h, wg_ref[...])
        u = _dot(h, wu_ref[...])
        G_ref[...] = g.astype(BF)
        U_ref[...] = u.astype(BF)
        a = (g * jax.nn.sigmoid(g) * u).astype(BF)
        acc[...] += _dot(a, wd_ref[...])

        @pl.when(j == NB - 1)
        def _():
            xo_ref[...] = x_ref[...] + 0.5 * acc[...]

        if nc:
            pl.when((i == ni - 1) & (j == NB - 1))(finish)

    row = lambda i, j: (i, 0)
    blk = lambda i, j: (j, 0, 0)
    hid = pl.BlockSpec((None, tm, Fs), lambda i, j: (j, i, 0))
    return pl.pallas_call(
        body, name="ffn_fwd_gather" if nc else "ffn_fwd", grid=(ni, NB),
        in_specs=[_once((tm, D), row), pl.BlockSpec((1, D), lambda i, j: (0, 0)),
                  pl.BlockSpec((None, D, Fs), blk), pl.BlockSpec((None, D, Fs), blk), pl.BlockSpec((None, Fs, D), blk)]
                 + [ANY] * nc,
        out_specs=[_once((tm, D), row), _once((tm, D), row), hid, hid] + [ANY] * nc,
        out_shape=[jax.ShapeDtypeStruct((S, D), F32), jax.ShapeDtypeStruct((S, D), BF),
                   jax.ShapeDtypeStruct((NB, S, Fs), BF), jax.ShapeDtypeStruct((NB, S, Fs), BF)] + _gathered(carry),
        scratch_shapes=[pltpu.VMEM((tm, D), BF), pltpu.VMEM((tm, D), F32)] + (_comm_sems(nc) if nc else []),
        compiler_params=_params(("arbitrary", "arbitrary")),
    )(x, gain, wg, wu, wd, *carry)


def ffn_bwd_x(dy, x, gain, G, U, wg, wu, wd, carry=(), *, tm=512, sub=256):
    S, D = x.shape
    NB, _, Fs = wg.shape
    tm = _tile(S, tm)
    sub = _tile(tm, sub)
    ni = S // tm
    nc = len(carry)
    gain = gain.reshape(1, D)

    def body(*refs):
        dy_ref, x_ref, g_ref, G_ref, U_ref, wg_ref, wu_ref, wd_ref = refs[:8]
        c_in = refs[8:8 + nc]
        dx_ref, do_ref, dG_ref, dU_ref, dg_ref = refs[8 + nc:13 + nc]
        c_out = refs[13 + nc:13 + 2 * nc]
        dob, acc = refs[13 + 2 * nc:15 + 2 * nc]
        i, j = pl.program_id(0), pl.program_id(1)
        if nc:
            start, finish = _exchange_phases(c_in, c_out, [False] * nc, *refs[15 + 2 * nc:])
            pl.when((i == 0) & (j == 0))(start)

        @pl.when(j == 0)
        def _():
            d = (0.5 * dy_ref[...]).astype(BF)
            dob[...] = d
            do_ref[...] = d
            acc[...] = jnp.zeros_like(acc)

        @pl.when((i == 0) & (j == 0))
        def _():
            dg_ref[...] = jnp.zeros_like(dg_ref)

        for r in range(tm // sub):
            rows = slice(r * sub, (r + 1) * sub)
            dA = _dot_nt(dob[rows, :], wd_ref[...])
            g = G_ref[rows, :].astype(F32)
            u = U_ref[rows, :].astype(F32)
            sig = jax.nn.sigmoid(g)
            dG = (dA * u * (sig * (1.0 + g * (1.0 - sig)))).astype(BF)
            dU = (dA * (g * sig)).astype(BF)
            dG_ref[rows, :] = dG
            dU_ref[rows, :] = dU
            acc[rows, :] += _dot_nt(dG, wg_ref[...]) + _dot_nt(dU, wu_ref[...])

        @pl.when(j == NB - 1)
        def _():
            dxn, dg = _rms_bwd(x_ref[...], g_ref[...], acc[...])
            dx_ref[...] = dy_ref[...] + dxn
            dg_ref[...] += jnp.broadcast_to(dg, dg_ref.shape)

        if nc:
            pl.when((i == ni - 1) & (j == NB - 1))(finish)

    row = lambda i, j: (i, 0)
    blk = lambda i, j: (j, 0, 0)
    hid = pl.BlockSpec((None, tm, Fs), lambda i, j: (j, i, 0))
    dx, dout, dG, dU, dg, *got = pl.pallas_call(
        body, name="ffn_bwd_x_exchange" if nc else "ffn_bwd_x", grid=(ni, NB),
        in_specs=[_once((tm, D), row), _once((tm, D), row), pl.BlockSpec((1, D), lambda i, j: (0, 0)), hid, hid,
                  pl.BlockSpec((None, D, Fs), blk), pl.BlockSpec((None, D, Fs), blk), pl.BlockSpec((None, Fs, D), blk)]
                 + [ANY] * nc,
        out_specs=[_once((tm, D), row), _once((tm, D), row), hid, hid, pl.BlockSpec((8, D), lambda i, j: (0, 0))]
                  + [ANY] * nc,
        out_shape=[jax.ShapeDtypeStruct((S, D), F32), jax.ShapeDtypeStruct((S, D), BF),
                   jax.ShapeDtypeStruct((NB, S, Fs), BF), jax.ShapeDtypeStruct((NB, S, Fs), BF),
                   jax.ShapeDtypeStruct((8, D), F32)] + [jax.ShapeDtypeStruct(a.shape, a.dtype) for a in carry],
        scratch_shapes=[pltpu.VMEM((tm, D), BF), pltpu.VMEM((tm, D), F32)] + (_comm_sems(nc) if nc else []),
        compiler_params=_params(("arbitrary", "arbitrary")),
    )(dy, x, gain, G, U, wg, wu, wd, *carry)
    return dx, dout, dG, dU, dg[0], got


def ffn_bwd_w(h, dout, G, U, dG, dU, carry=(), *, tm=512):
    S, D = h.shape
    NB, _, Fs = G.shape
    tm = _tile(S, tm)
    ni = S // tm
    nc = len(carry)

    def body(*refs):
        h_ref, do_ref, G_ref, U_ref, dG_ref, dU_ref = refs[:6]
        c_in = refs[6:6 + nc]
        wg_ref, wu_ref, wd_ref = refs[6 + nc:9 + nc]
        c_out = refs[9 + nc:9 + 2 * nc]
        ag, au, ad = refs[9 + 2 * nc:12 + 2 * nc]
        j, i = pl.program_id(0), pl.program_id(1)
        if nc:
            start, finish = _exchange_phases(c_in, c_out, [False] * nc, *refs[12 + 2 * nc:])
            pl.when((j == 0) & (i == 0))(start)

        @pl.when(i == 0)
        def _():
            ag[...] = jnp.zeros_like(ag)
            au[...] = jnp.zeros_like(au)
            ad[...] = jnp.zeros_like(ad)

        h = h_ref[...]
        ag[...] += _dot_tn(h, dG_ref[...])
        au[...] += _dot_tn(h, dU_ref[...])
        g = G_ref[...].astype(F32)
        a = (g * jax.nn.sigmoid(g) * U_ref[...].astype(F32)).astype(BF)
        ad[...] += _dot_tn(a, do_ref[...])

        @pl.when(i == ni - 1)
        def _():
            wg_ref[...] = ag[...].astype(BF)
            wu_ref[...] = au[...].astype(BF)
            wd_ref[...] = ad[...].astype(BF)

        if nc:
            pl.when((j == NB - 1) & (i == ni - 1))(finish)

    row = pl.BlockSpec((tm, D), lambda j, i: (i, 0))
    hid = pl.BlockSpec((None, tm, Fs), lambda j, i: (j, i, 0))
    blk = lambda j, i: (j, 0, 0)
    dwg, dwu, dwd, *got = pl.pallas_call(
        body, name="ffn_bwd_w_exchange" if nc else "ffn_bwd_w", grid=(NB, ni),
        in_specs=[row, row, hid, hid, hid, hid] + [ANY] * nc,
        out_specs=[_once((None, D, Fs), blk), _once((None, D, Fs), blk), _once((None, Fs, D), blk)] + [ANY] * nc,
        out_shape=[jax.ShapeDtypeStruct((NB, D, Fs), BF), jax.ShapeDtypeStruct((NB, D, Fs), BF),
                   jax.ShapeDtypeStruct((NB, Fs, D), BF)] + [jax.ShapeDtypeStruct(a.shape, a.dtype) for a in carry],
        scratch_shapes=[pltpu.VMEM((D, Fs), F32), pltpu.VMEM((D, Fs), F32), pltpu.VMEM((Fs, D), F32)]
                       + (_comm_sems(nc) if nc else []),
        compiler_params=_params(("arbitrary", "arbitrary")),
    )(h, dout, G, U, dG, dU, *carry)
    return [dwg, dwu, dwd], got


def dilated_bias(H, d):
    n = DIL_N
    slopes = 2.0 ** (-8.0 * (np.arange(H) + 1) / H)
    i = np.arange(n)[:, None]
    j = np.arange(2 * n)[None, :]
    steps = n + i - j
    band = (steps >= 0) & (steps <= n)
    first = band & (j >= n)
    bias = -slopes[:, None, None] * (d * steps).astype(np.float64)[None]
    out = np.stack([np.where(band[None], bias, NEG), np.where(first[None], bias, NEG)], axis=1)
    return jnp.asarray(out, dtype=F32)


def dil_fwd(qkv, bias, d, H):
    S = qkv.shape[0]
    Sd = S // d
    TQ = _tile(Sd, 512)
    nsub, nc = TQ // DIL_N, Sd // TQ
    scale = HEAD ** -0.5
    qv = qkv.reshape(Sd, d * 3 * H * HEAD)

    def body(q_ref, kc_ref, kp_ref, vc_ref, vp_ref, b_ref, o_ref, l_ref):
        c = pl.program_id(2)
        for i in range(nsub):
            sl = slice(i * DIL_N, (i + 1) * DIL_N)
            if i == 0:
                k2 = jnp.concatenate([kp_ref[...], kc_ref[0:DIL_N, :]], axis=0)
                v2 = jnp.concatenate([vp_ref[...], vc_ref[0:DIL_N, :]], axis=0)
                b = jnp.where(c == 0, b_ref[1], b_ref[0])
            else:
                k2 = kc_ref[(i - 1) * DIL_N:(i + 1) * DIL_N, :]
                v2 = vc_ref[(i - 1) * DIL_N:(i + 1) * DIL_N, :]
                b = b_ref[0]
            s = _dot_nt(q_ref[sl, :], k2) * scale + b
            m = jnp.max(s, axis=1, keepdims=True)
            p = jnp.exp(s - m)
            l = jnp.sum(p, axis=1, keepdims=True)
            o_ref[sl, :] = _dot(p.astype(BF), v2) / l
            l_ref[sl, :] = jnp.broadcast_to(m + jnp.log(l), (DIL_N, LANES))

    W3 = 3 * H
    chunk = lambda off: pl.BlockSpec((TQ, HEAD), lambda h, r, c: (c, r * W3 + off * H + h))
    halo = lambda off: pl.BlockSpec((DIL_N, HEAD), lambda h, r, c: (jnp.maximum(c * nsub - 1, 0), r * W3 + off * H + h))
    out = pl.BlockSpec((TQ, HEAD), lambda h, r, c: (c, r * H + h))
    o, l = pl.pallas_call(
        body, name=f"dil_fwd_d{d}", grid=(H, d, nc),
        in_specs=[chunk(0), chunk(1), halo(1), chunk(2), halo(2),
                  pl.BlockSpec((None, 2, DIL_N, 2 * DIL_N), lambda h, r, c: (h, 0, 0, 0))],
        out_specs=[out, out],
        out_shape=[jax.ShapeDtypeStruct((Sd, d * H * HEAD), F32)] * 2,
        compiler_params=_params(("parallel", "parallel", "arbitrary")),
    )(qv, qv, qv, qv, qv, bias)
    return o.reshape(S, H * HEAD), l.reshape(S, H * HEAD)


def dil_combine(os, ls, *, tm=256):
    S, W = os[0].shape
    tm = _tile(S, tm)

    def body(o1, o2, o3, l1, l2, l3, o_ref, L_ref):
        a, b, c = l1[...], l2[...], l3[...]
        m = jnp.maximum(jnp.maximum(a, b), c)
        ea, eb, ec = jnp.exp(a - m), jnp.exp(b - m), jnp.exp(c - m)
        z = ea + eb + ec
        o_ref[...] = ((ea * o1[...] + eb * o2[...] + ec * o3[...]) / z).astype(BF)
        L_ref[...] = m + jnp.log(z)

    row = pl.BlockSpec((tm, W), lambda i: (i, 0))
    return pl.pallas_call(
        body, name="dil_combine", grid=(S // tm,), in_specs=[row] * 6, out_specs=[row, row],
        out_shape=[jax.ShapeDtypeStruct((S, W), BF), jax.ShapeDtypeStruct((S, W), F32)],
        compiler_params=_params(("parallel",)),
    )(*os, *ls)


def head_delta(do, o, *, tm=512):
    S, W = o.shape
    tm = _tile(S, tm)

    def body(do_ref, o_ref, d_ref):
        for h in range(W // HEAD):
            cols = slice(h * HEAD, (h + 1) * HEAD)
            s = jnp.sum(do_ref[:, cols].astype(F32) * o_ref[:, cols].astype(F32), axis=1, keepdims=True)
            d_ref[:, cols] = jnp.broadcast_to(s, (tm, HEAD))

    blk = pl.BlockSpec((tm, W), lambda i: (i, 0))
    return pl.pallas_call(
        body, name="head_delta", grid=(S // tm,), in_specs=[blk, blk], out_specs=blk,
        out_shape=jax.ShapeDtypeStruct((S, W), F32),
        compiler_params=_params(("parallel",)),
    )(do, o)


def dil_bwd(qkv, do, L, delta, bias, d, H, prev=None):
    S = qkv.shape[0]
    Sd = S // d
    TQ = _tile(Sd, 512)
    nsub, nc = TQ // DIL_N, Sd // TQ
    nblk = Sd // DIL_N
    scale = HEAD ** -0.5
    W3 = 3 * H
    qv = qkv.reshape(Sd, d * W3 * HEAD)
    dov, Lv, dlv = (t.reshape(Sd, d * H * HEAD) for t in (do, L, delta))
    has_prev = prev is not None

    def body(*refs):
        (qc, qn, kc, kp, vc, vp, doc, don, Lc, Ln, dlc, dln, b_ref) = refs[:13]
        refs = refs[13:]
        if has_prev:
            dqi, dki, dvi = refs[:3]
            refs = refs[3:]
        dqo, dko, dvo, dk_acc, dv_acc = refs
        c = pl.program_id(2)
        dk_acc[...] = jnp.zeros_like(dk_acc)
        dv_acc[...] = jnp.zeros_like(dv_acc)
        for i in range(nsub):
            sl = slice(i * DIL_N, (i + 1) * DIL_N)
            if i == 0:
                k2 = jnp.concatenate([kp[...], kc[0:DIL_N, :]], axis=0)
                v2 = jnp.concatenate([vp[...], vc[0:DIL_N, :]], axis=0)
                b = jnp.where(c == 0, b_ref[1], b_ref[0])
            else:
                k2 = kc[(i - 1) * DIL_N:(i + 1) * DIL_N, :]
                v2 = vc[(i - 1) * DIL_N:(i + 1) * DIL_N, :]
                b = b_ref[0]
            q, dov_ = qc[sl, :], doc[sl, :]
            Lq, dl = Lc[sl, :], dlc[sl, :]
            s = _dot_nt(q, k2) * scale + b
            p = jnp.exp(s - jnp.concatenate([Lq, Lq], axis=1))
            dp = _dot_nt(dov_, v2)
            ds = p * (dp - jnp.concatenate([dl, dl], axis=1))
            dsb, pb = ds.astype(BF), p.astype(BF)
            dq = _dot(dsb, k2) * scale
            dqo[sl, :] = dq + dqi[sl, :] if has_prev else dq
            dk2 = _dot_tn(dsb, q) * scale
            dv2 = _dot_tn(pb, dov_)
            if i >= 1:
                pv = slice((i - 1) * DIL_N, i * DIL_N)
                dk_acc[pv, :] += dk2[:DIL_N]
                dv_acc[pv, :] += dv2[:DIL_N]
            dk_acc[sl, :] += dk2[DIL_N:]
            dv_acc[sl, :] += dv2[DIL_N:]

        @pl.when(c < nc - 1)
        def _():
            last = slice((nsub - 1) * DIL_N, nsub * DIL_N)
            q, dov_ = qn[...], don[...]
            k1, v1 = kc[last, :], vc[last, :]
            s = _dot_nt(q, k1) * scale + b_ref[0][:, :DIL_N]
            p = jnp.exp(s - Ln[...])
            ds = p * (_dot_nt(dov_, v1) - dln[...])
            dk_acc[last, :] += _dot_tn(ds.astype(BF), q) * scale
            dv_acc[last, :] += _dot_tn(p.astype(BF), dov_)

        if has_prev:
            dko[...] = dk_acc[...] + dki[...]
            dvo[...] = dv_acc[...] + dvi[...]
        else:
            dko[...] = dk_acc[...]
            dvo[...] = dv_acc[...]

    nxt = lambda c: jnp.minimum((c + 1) * nsub, nblk - 1)
    prv = lambda c: jnp.maximum(c * nsub - 1, 0)
    chunk3 = lambda off: pl.BlockSpec((TQ, HEAD), lambda h, r, c: (c, r * W3 + off * H + h))
    halo3 = lambda off, f: pl.BlockSpec((DIL_N, HEAD), lambda h, r, c: (f(c), r * W3 + off * H + h))
    chunk1 = pl.BlockSpec((TQ, HEAD), lambda h, r, c: (c, r * H + h))
    next1 = pl.BlockSpec((DIL_N, HEAD), lambda h, r, c: (nxt(c), r * H + h))
    in_specs = [chunk3(0), halo3(0, nxt), chunk3(1), halo3(1, prv), chunk3(2), halo3(2, prv),
                chunk1, next1, chunk1, next1, chunk1, next1,
                pl.BlockSpec((None, 2, DIL_N, 2 * DIL_N), lambda h, r, c: (h, 0, 0, 0))]
    args = [qv] * 6 + [dov, dov, Lv, Lv, dlv, dlv, bias]
    if has_prev:
        in_specs += [chunk1] * 3
        args += [t.reshape(Sd, d * H * HEAD) for t in prev]
    outs = pl.pallas_call(
        body, name=f"dil_bwd_d{d}", grid=(H, d, nc), in_specs=in_specs, out_specs=[chunk1] * 3,
        out_shape=[jax.ShapeDtypeStruct((Sd, d * H * HEAD), F32)] * 3,
        scratch_shapes=[pltpu.VMEM((TQ, HEAD), F32), pltpu.VMEM((TQ, HEAD), F32)],
        compiler_params=_params(("parallel", "parallel", "arbitrary")),
    )(*args)
    return tuple(t.reshape(S, H * HEAD) for t in outs)


DIL_CHUNK = DIL_N * max(BRANCH_DILATIONS)
DIL_UNROLL = 4


def _rows(start, size, d):
    return pl.ds(pl.multiple_of(start, DIL_N), size) if d == 1 else pl.ds(start, size, stride=d)


def dil_attn_fwd(qkv, biases, H):
    S = qkv.shape[0]
    C = DIL_CHUNK
    assert S % C == 0
    scale = HEAD ** -0.5
    nbr = len(BRANCH_DILATIONS)

    def body(*refs):
        q_ref, kc_ref, kp_ref, vc_ref, vp_ref = refs[:5]
        b_refs = refs[5:5 + nbr]
        o_ref, L_ref, qf, kf, vf = refs[5 + nbr:10 + nbr]
        ogs = refs[10 + nbr:10 + 2 * nbr]
        lgs = refs[10 + 2 * nbr:10 + 3 * nbr]
        c = pl.program_id(1)
        qf[...] = q_ref[...].astype(F32)
        kf[0:C, :] = kp_ref[...].astype(F32)
        kf[C:2 * C, :] = kc_ref[...].astype(F32)
        vf[0:C, :] = vp_ref[...].astype(F32)
        vf[C:2 * C, :] = vc_ref[...].astype(F32)
        for d, b_ref, og, lg in zip(BRANCH_DILATIONS, b_refs, ogs, lgs):
            span = DIL_N * d

            def block(t, carry, d=d, b_ref=b_ref, og=og, lg=lg, span=span):
                r, b = t % d, t // d
                q0 = b * span + r
                q = qf[_rows(q0, DIL_N, d), :].astype(BF)
                k2 = kf[_rows(C + q0 - span, 2 * DIL_N, d), :].astype(BF)
                v2 = vf[_rows(C + q0 - span, 2 * DIL_N, d), :].astype(BF)
                bias = jnp.where((c == 0) & (b == 0), b_ref[1], b_ref[0])
                s = _dot_nt(q, k2) * scale + bias
                m = jnp.max(s, axis=1, keepdims=True)
                p = jnp.exp(s - m)
                l = jnp.sum(p, axis=1, keepdims=True)
                og[_rows(q0, DIL_N, d), :] = _dot(p.astype(BF), v2) / l
                lg[_rows(q0, DIL_N, d), :] = jnp.broadcast_to(m + jnp.log(l), (DIL_N, LANES))
                return carry

            lax.fori_loop(0, C // DIL_N, block, 0, unroll=DIL_UNROLL)
        ls = [lg[...] for lg in lgs]
        m = functools.reduce(jnp.maximum, ls)
        es = [jnp.exp(l - m) for l in ls]
        z = functools.reduce(jnp.add, es)
        mix = functools.reduce(jnp.add, [e * og[...] for e, og in zip(es, ogs)])
        o_ref[...] = (mix / z).astype(BF)
        L_ref[...] = m + jnp.log(z)

    cur = lambda off: pl.BlockSpec((C, HEAD), lambda h, c: (c, off * H + h))
    prv = lambda off: pl.BlockSpec((C, HEAD), lambda h, c: (jnp.maximum(c - 1, 0), off * H + h))
    tab = pl.BlockSpec((None, 2, DIL_N, 2 * DIL_N), lambda h, c: (h, 0, 0, 0))
    out = pl.BlockSpec((C, HEAD), lambda h, c: (c, h))
    return pl.pallas_call(
        body, name="dil_attn_fwd", grid=(H, S // C),
        in_specs=[cur(0), cur(1), prv(1), cur(2), prv(2)] + [tab] * nbr,
        out_specs=[out, out],
        out_shape=[jax.ShapeDtypeStruct((S, H * HEAD), BF), jax.ShapeDtypeStruct((S, H * HEAD), F32)],
        scratch_shapes=[pltpu.VMEM((C, HEAD), F32), pltpu.VMEM((2 * C, HEAD), F32), pltpu.VMEM((2 * C, HEAD), F32)]
                       + [pltpu.VMEM((C, HEAD), F32)] * (2 * nbr),
        compiler_params=_params(("parallel", "arbitrary")),
    )(qkv, qkv, qkv, qkv, qkv, *biases)


def dil_attn_bwd(qkv, do, L, delta, biases, H):
    S = qkv.shape[0]
    C = DIL_CHUNK
    assert S % C == 0
    nc = S // C
    scale = HEAD ** -0.5
    nbr = len(BRANCH_DILATIONS)

    def body(*refs):
        (qc_ref, qn_ref, kc_ref, kp_ref, vc_ref, vp_ref, doc_ref, don_ref, Lc_ref, Ln_ref, dlc_ref, dln_ref) = refs[:12]
        b_refs = refs[12:12 + nbr]
        dq_ref, dk_ref, dv_ref, qf, dof, Lf, dlf, kf, vf, dq_acc, dk_acc, dv_acc = refs[12 + nbr:]
        c = pl.program_id(1)
        for buf, a_ref, b_ref in ((qf, qc_ref, qn_ref), (dof, doc_ref, don_ref), (Lf, Lc_ref, Ln_ref),
                                  (dlf, dlc_ref, dln_ref), (kf, kp_ref, kc_ref), (vf, vp_ref, vc_ref)):
            buf[0:C, :] = a_ref[...].astype(F32)
            buf[C:2 * C, :] = b_ref[...].astype(F32)
        dq_acc[...] = jnp.zeros_like(dq_acc)
        dk_acc[...] = jnp.zeros_like(dk_acc)
        dv_acc[...] = jnp.zeros_like(dv_acc)
        for d, b_ref in zip(BRANCH_DILATIONS, b_refs):
            span = DIL_N * d

            def block(t, carry, d=d, b_ref=b_ref, span=span):
                r, b = t % d, t // d
                q0 = b * span + r
                qrows = _rows(q0, DIL_N, d)
                krows = _rows(C + q0 - span, 2 * DIL_N, d)
                q, dov = qf[qrows, :].astype(BF), dof[qrows, :].astype(BF)
                k2, v2 = kf[krows, :].astype(BF), vf[krows, :].astype(BF)
                Lq, dl = Lf[qrows, :], dlf[qrows, :]
                bias = jnp.where((c == 0) & (b == 0), b_ref[1], b_ref[0])
                s = _dot_nt(q, k2) * scale + bias
                p = jnp.exp(s - jnp.concatenate([Lq, Lq], axis=1))
                ds = p * (_dot_nt(dov, v2) - jnp.concatenate([dl, dl], axis=1))
                dsb, pb = ds.astype(BF), p.astype(BF)
                dq_acc[qrows, :] += _dot(dsb, k2) * scale
                dk2 = _dot_tn(dsb, q) * scale
                dv2 = _dot_tn(pb, dov)
                dk_acc[qrows, :] += dk2[DIL_N:]
                dv_acc[qrows, :] += dv2[DIL_N:]

                @pl.when(b >= 1)
                def _():
                    prows = _rows(q0 - span, DIL_N, d)
                    dk_acc[prows, :] += dk2[:DIL_N]
                    dv_acc[prows, :] += dv2[:DIL_N]

                return carry

            lax.fori_loop(0, C // DIL_N, block, 0, unroll=DIL_UNROLL)

            @pl.when(c < nc - 1)
            def _(d=d, b_ref=b_ref, span=span):
                def nxt(r, carry):
                    qrows = _rows(C + r, DIL_N, d)
                    krows = _rows(C - span + r, DIL_N, d)
                    q, dov = qf[qrows, :].astype(BF), dof[qrows, :].astype(BF)
                    k1, v1 = kf[_rows(2 * C - span + r, DIL_N, d), :].astype(BF), vf[_rows(2 * C - span + r, DIL_N, d), :].astype(BF)
                    s = _dot_nt(q, k1) * scale + b_ref[0][:, :DIL_N]
                    p = jnp.exp(s - Lf[qrows, :])
                    ds = p * (_dot_nt(dov, v1) - dlf[qrows, :])
                    dk_acc[krows, :] += _dot_tn(ds.astype(BF), q) * scale
                    dv_acc[krows, :] += _dot_tn(p.astype(BF), dov)
                    return carry

                lax.fori_loop(0, d, nxt, 0, unroll=min(d, DIL_UNROLL))

        dq_ref[...] = dq_acc[...].astype(BF)
        dk_ref[...] = dk_acc[...].astype(BF)
        dv_ref[...] = dv_acc[...].astype(BF)

    W3 = lambda off: (lambda h, c: (c, off * H + h))
    cur3 = lambda off: pl.BlockSpec((C, HEAD), lambda h, c: (c, off * H + h))
    prv3 = lambda off: pl.BlockSpec((C, HEAD), lambda h, c: (jnp.maximum(c - 1, 0), off * H + h))
    nxt3 = lambda off: pl.BlockSpec((C, HEAD), lambda h, c: (jnp.minimum(c + 1, nc - 1), off * H + h))
    cur1 = pl.BlockSpec((C, HEAD), lambda h, c: (c, h))
    nxt1 = pl.BlockSpec((C, HEAD), lambda h, c: (jnp.minimum(c + 1, nc - 1), h))
    tab = pl.BlockSpec((None, 2, DIL_N, 2 * DIL_N), lambda h, c: (h, 0, 0, 0))
    return pl.pallas_call(
        body, name="dil_attn_bwd", grid=(H, nc),
        in_specs=[cur3(0), nxt3(0), cur3(1), prv3(1), cur3(2), prv3(2), cur1, nxt1, cur1, nxt1, cur1, nxt1] + [tab] * nbr,
        out_specs=[cur1] * 3,
        out_shape=[jax.ShapeDtypeStruct((S, H * HEAD), BF)] * 3,
        scratch_shapes=[pltpu.VMEM((2 * C, HEAD), F32)] * 6 + [pltpu.VMEM((C, HEAD), F32)] * 3,
        compiler_params=_params(("parallel", "arbitrary")),
    )(qkv, qkv, qkv, qkv, qkv, qkv, do, do, L, L, delta, delta, *biases)


def _causal_mask(s, qi, ki, tq, tk, row0=0):
    qpos = qi * tq + row0 + lax.broadcasted_iota(jnp.int32, s.shape, 0)
    kpos = ki * tk + lax.broadcasted_iota(jnp.int32, s.shape, 1)
    return jnp.where(kpos <= qpos, s, NEG)


def mla_fwd(qn, qr, kv, kr, H, *, tq=1024, tk=1024, sub=512):
    S = qn.shape[0]
    tq, tk = _tile(S, tq), _tile(S, tk)
    sub = _tile(tq, sub)
    nk = S // tk
    scale = MLA_SCALE
    c2 = scale * LOG2E

    def body(qn_ref, qr_ref, kn_ref, kr_ref, v_ref, o_ref, L_ref, m_s, l_s, acc):
        qi, ki = pl.program_id(1), pl.program_id(2)

        @pl.when(ki == 0)
        def _():
            m_s[...] = jnp.full_like(m_s, NEG)
            l_s[...] = jnp.zeros_like(l_s)
            acc[...] = jnp.zeros_like(acc)

        def step(masked):
            k = jnp.concatenate([kn_ref[...], kr_ref[...]], axis=1)
            v = v_ref[...]
            for r in range(tq // sub):
                rows = slice(r * sub, (r + 1) * sub)
                q = jnp.concatenate([qn_ref[rows, :], qr_ref[rows, :]], axis=1)
                s = _dot_nt(q, k)
                if masked:
                    s = _causal_mask(s, qi, ki, tq, tk, r * sub)
                m_prev = m_s[rows, :]
                m_new = jnp.maximum(m_prev, jnp.max(s, axis=1, keepdims=True))
                alpha = jnp.exp2((m_prev - m_new) * c2)
                p = jnp.exp2((s - jnp.tile(m_new, (1, tk // LANES))) * c2)
                l_s[rows, :] = alpha * l_s[rows, :] + jnp.sum(p, axis=1, keepdims=True)
                acc[rows, :] = alpha * acc[rows, :] + _dot(p.astype(BF), v)
                m_s[rows, :] = m_new

        full = ki * tk + tk - 1 <= qi * tq
        live = ki * tk <= qi * tq + tq - 1
        pl.when(full)(lambda: step(False))
        pl.when(live & jnp.logical_not(full))(lambda: step(True))

        @pl.when(ki == nk - 1)
        def _():
            o_ref[...] = (acc[...] / l_s[...]).astype(BF)
            L_ref[...] = m_s[...] * scale + jnp.log(l_s[...])

    kcl = lambda qi, ki: jnp.minimum(ki, (qi * tq + tq - 1) // tk)
    qs = pl.BlockSpec((tq, HEAD), lambda h, qi, ki: (qi, h))
    return pl.pallas_call(
        body, name="mla_fwd", grid=(H, S // tq, nk),
        in_specs=[qs, qs, pl.BlockSpec((tk, HEAD), lambda h, qi, ki: (kcl(qi, ki), h)),
                  pl.BlockSpec((tk, LANES), lambda h, qi, ki: (kcl(qi, ki), 0)),
                  pl.BlockSpec((tk, HEAD), lambda h, qi, ki: (kcl(qi, ki), H + h))],
        out_specs=[qs, qs],
        out_shape=[jax.ShapeDtypeStruct((S, H * HEAD), BF), jax.ShapeDtypeStruct((S, H * HEAD), F32)],
        scratch_shapes=[pltpu.VMEM((tq, LANES), F32), pltpu.VMEM((tq, LANES), F32), pltpu.VMEM((tq, HEAD), F32)],
        compiler_params=_params(("parallel", "parallel", "arbitrary")),
    )(qn, qr, kv, kr, kv)


def _mla_probs(s, L_rows, c2, width):
    return jnp.exp2(s * c2 - jnp.tile(L_rows * LOG2E, (1, width // LANES)))


def mla_bwd_q(qn, qr, kv, kr, do, L, delta, rope, H, *, tq=1024, tk=1024, sub=256):
    S = qn.shape[0]
    tq, tk = _tile(S, tq), _tile(S, tk)
    sub = _tile(tq, sub)
    nk = S // tk
    scale = MLA_SCALE
    c2 = scale * LOG2E

    def body(qn_ref, qr_ref, kn_ref, kr_ref, v_ref, do_ref, L_ref, dl_ref, cos_ref, sin_ref, dqn_ref, dqr_ref, acc):
        qi, ki = pl.program_id(1), pl.program_id(2)

        @pl.when(ki == 0)
        def _():
            acc[...] = jnp.zeros_like(acc)

        def step(masked):
            k = jnp.concatenate([kn_ref[...], kr_ref[...]], axis=1)
            v = v_ref[...]
            for r in range(tq // sub):
                rows = slice(r * sub, (r + 1) * sub)
                q = jnp.concatenate([qn_ref[rows, :], qr_ref[rows, :]], axis=1)
                s = _dot_nt(q, k)
                if masked:
                    s = _causal_mask(s, qi, ki, tq, tk, r * sub)
                p = _mla_probs(s, L_ref[rows, :], c2, tk)
                dp = _dot_nt(do_ref[rows, :], v)
                ds = (p * (dp - jnp.tile(dl_ref[rows, :], (1, tk // LANES)))).astype(BF)
                acc[rows, :] += _dot(ds, k)

        full = ki * tk + tk - 1 <= qi * tq
        live = ki * tk <= qi * tq + tq - 1
        pl.when(full)(lambda: step(False))
        pl.when(live & jnp.logical_not(full))(lambda: step(True))

        @pl.when(ki == nk - 1)
        def _():
            dq = acc[...] * scale
            dqn_ref[...] = dq[:, :HEAD].astype(BF)
            dqr_ref[...] = _rope_bwd(dq[:, HEAD:], cos_ref[...], sin_ref[...]).astype(BF)

    kcl = lambda qi, ki: jnp.minimum(ki, (qi * tq + tq - 1) // tk)
    qs = pl.BlockSpec((tq, HEAD), lambda h, qi, ki: (qi, h))
    tab = pl.BlockSpec((tq, LANES), lambda h, qi, ki: (qi, 0))
    return pl.pallas_call(
        body, name="mla_bwd_q", grid=(H, S // tq, nk),
        in_specs=[qs, qs, pl.BlockSpec((tk, HEAD), lambda h, qi, ki: (kcl(qi, ki), h)),
                  pl.BlockSpec((tk, LANES), lambda h, qi, ki: (kcl(qi, ki), 0)),
                  pl.BlockSpec((tk, HEAD), lambda h, qi, ki: (kcl(qi, ki), H + h)),
                  qs, qs, qs, tab, tab],
        out_specs=[qs, qs],
        out_shape=[jax.ShapeDtypeStruct((S, H * HEAD), BF)] * 2,
        scratch_shapes=[pltpu.VMEM((tq, 2 * HEAD), F32)],
        compiler_params=_params(("parallel", "parallel", "arbitrary")),
    )(qn, qr, kv, kr, kv, do, L, delta, *rope)


def mla_bwd_kv(qn, qr, kv, kr, do, L, delta, rope, H, prev=None, *, tq=1024, tk=1024, sub=256):
    S = qn.shape[0]
    tq, tk = _tile(S, tq), _tile(S, tk)
    sub = _tile(tq, sub)
    nq = S // tq
    scale = MLA_SCALE
    c2 = scale * LOG2E
    has_prev = prev is not None

    def body(*refs):
        (qn_ref, qr_ref, kn_ref, kr_ref, v_ref, do_ref, L_ref, dl_ref, cos_ref, sin_ref) = refs[:10]
        refs = refs[10:]
        if has_prev:
            pk_ref, pv_ref, pr_ref = refs[:3]
            refs = refs[3:]
        dk_ref, dv_ref, dr_ref, dk_acc, dv_acc, dr_acc = refs
        ki, h, qi = pl.program_id(0), pl.program_id(1), pl.program_id(2)

        @pl.when(qi == 0)
        def _():
            dk_acc[...] = jnp.zeros_like(dk_acc)
            dv_acc[...] = jnp.zeros_like(dv_acc)

        @pl.when((qi == 0) & (h == 0))
        def _():
            dr_acc[...] = jnp.zeros_like(dr_acc)

        def step(masked):
            k = jnp.concatenate([kn_ref[...], kr_ref[...]], axis=1)
            v = v_ref[...]
            for r in range(tq // sub):
                rows = slice(r * sub, (r + 1) * sub)
                q = jnp.concatenate([qn_ref[rows, :], qr_ref[rows, :]], axis=1)
                s = _dot_nt(q, k)
                if masked:
                    s = _causal_mask(s, qi, ki, tq, tk, r * sub)
                p = _mla_probs(s, L_ref[rows, :], c2, tk)
                dov = do_ref[rows, :]
                dv_acc[...] += _dot_tn(p.astype(BF), dov)
                dp = _dot_nt(dov, v)
                ds = (p * (dp - jnp.tile(dl_ref[rows, :], (1, tk // LANES)))).astype(BF)
                dk_acc[...] += _dot_tn(ds, q)

        full = ki * tk + tk - 1 <= qi * tq
        live = ki * tk <= qi * tq + tq - 1
        pl.when(full)(lambda: step(False))
        pl.when(live & jnp.logical_not(full))(lambda: step(True))

        @pl.when(qi == nq - 1)
        def _():
            dk = dk_acc[...] * scale
            dkn, dv = dk[:, :HEAD], dv_acc[...]
            if has_prev:
                dkn, dv = dkn + pk_ref[...], dv + pv_ref[...]
            dk_ref[...] = dkn
            dv_ref[...] = dv
            dr_acc[...] += dk[:, HEAD:]

        @pl.when((qi == nq - 1) & (h == H - 1))
        def _():
            dr = _rope_bwd(dr_acc[...], cos_ref[...], sin_ref[...])
            dr_ref[...] = dr + pr_ref[...] if has_prev else dr

    qcl = lambda ki, qi: jnp.maximum(qi, (ki * tk) // tq)
    qs = pl.BlockSpec((tq, HEAD), lambda ki, h, qi: (qcl(ki, qi), h))
    kn = pl.BlockSpec((tk, HEAD), lambda ki, h, qi: (ki, h))
    vs = pl.BlockSpec((tk, HEAD), lambda ki, h, qi: (ki, H + h))
    k1 = pl.BlockSpec((tk, LANES), lambda ki, h, qi: (ki, 0))
    in_specs = [qs, qs, kn, k1, vs, qs, qs, qs, k1, k1]
    args = [qn, qr, kv, kr, kv, do, L, delta, *rope]
    if has_prev:
        in_specs += [kn, vs, k1]
        args += [prev[0], prev[0], prev[1]]
    dkn, dv, dr = pl.pallas_call(
        body, name="mla_bwd_kv", grid=(S // tk, H, nq), in_specs=in_specs,
        out_specs=[kn, kn, k1],
        out_shape=[jax.ShapeDtypeStruct((S, H * HEAD), F32), jax.ShapeDtypeStruct((S, H * HEAD), F32),
                   jax.ShapeDtypeStruct((S, LANES), F32)],
        scratch_shapes=[pltpu.VMEM((tk, 2 * HEAD), F32), pltpu.VMEM((tk, HEAD), F32), pltpu.VMEM((tk, LANES), F32)],
        compiler_params=_params(("parallel", "arbitrary", "arbitrary")),
    )(*args)
    return dkn, dv, dr


def mla_bwd(qn, qr, kv, kr, do, L, delta, rope, H, prev=None, *, tq=1024, tk=1024, sub=512):
    S = qn.shape[0]
    tq, tk = _tile(S, tq), _tile(S, tk)
    sub = _tile(tq, sub)
    nq, nk = S // tq, S // tk
    scale = MLA_SCALE
    c2 = scale * LOG2E
    has_prev = prev is not None

    def body(*refs):
        (qn_ref, qr_ref, kn_ref, kr_ref, v_ref, do_ref, L_ref, dl_ref, cq_ref, sq_ref, ck_ref, sk_ref) = refs[:12]
        refs = refs[12:]
        if has_prev:
            pk_ref, pv_ref, pr_ref = refs[:3]
            refs = refs[3:]
        dqn_ref, dqr_ref, dk_ref, dv_ref, dr_ref, dq_full, dk_acc, dv_acc = refs
        h, ki, qi = pl.program_id(0), pl.program_id(1), pl.program_id(2)
        qrows = pl.ds(pl.multiple_of(qi * tq, tq), tq)
        krows = pl.ds(pl.multiple_of(ki * tk, tk), tk)

        @pl.when(qi == 0)
        def _():
            dk_acc[...] = jnp.zeros_like(dk_acc)
            dv_acc[...] = jnp.zeros_like(dv_acc)

        @pl.when((qi == 0) & (h == 0))
        def _():
            dr_ref[krows, :] = jnp.zeros((tk, LANES), F32)

        @pl.when(ki == 0)
        def _():
            dq_full[qrows, :] = jnp.zeros((tq, 2 * HEAD), F32)

        def step(masked):
            k = jnp.concatenate([kn_ref[...], kr_ref[...]], axis=1)
            v = v_ref[...]
            for r in range(tq // sub):
                rows = slice(r * sub, (r + 1) * sub)
                q = jnp.concatenate([qn_ref[rows, :], qr_ref[rows, :]], axis=1)
                s = _dot_nt(q, k)
                if masked:
                    s = _causal_mask(s, qi, ki, tq, tk, r * sub)
                p = _mla_probs(s, L_ref[rows, :], c2, tk)
                dov = do_ref[rows, :]
                dv_acc[...] += _dot_tn(p.astype(BF), dov)
                dp = _dot_nt(dov, v)
                ds = (p * (dp - jnp.tile(dl_ref[rows, :], (1, tk // LANES)))).astype(BF)
                dk_acc[...] += _dot_tn(ds, q)
                dq_full[pl.ds(pl.multiple_of(qi * tq + r * sub, sub), sub), :] += _dot(ds, k)

        full = ki * tk + tk - 1 <= qi * tq
        live = ki * tk <= qi * tq + tq - 1
        pl.when(full)(lambda: step(False))
        pl.when(live & jnp.logical_not(full))(lambda: step(True))

        @pl.when(ki == (qi * tq + tq - 1) // tk)
        def _():
            dq = dq_full[qrows, :] * scale
            dqn_ref[qrows, :] = dq[:, :HEAD].astype(BF)
            dqr_ref[qrows, :] = _rope_bwd(dq[:, HEAD:], cq_ref[...], sq_ref[...]).astype(BF)

        @pl.when(qi == nq - 1)
        def _():
            dk = dk_acc[...] * scale
            dkn, dv = dk[:, :HEAD], dv_acc[...]
            if has_prev:
                dkn, dv = dkn + pk_ref[...], dv + pv_ref[...]
            dk_ref[...] = dkn
            dv_ref[...] = dv
            dr_ref[krows, :] += dk[:, HEAD:]

        @pl.when((qi == nq - 1) & (h == H - 1))
        def _():
            dr = _rope_bwd(dr_ref[krows, :], ck_ref[...], sk_ref[...])
            dr_ref[krows, :] = dr + pr_ref[...] if has_prev else dr

    qcl = lambda ki, qi: jnp.maximum(qi, (ki * tk) // tq)
    qs = pl.BlockSpec((tq, HEAD), lambda h, ki, qi: (qcl(ki, qi), h))
    qt = pl.BlockSpec((tq, LANES), lambda h, ki, qi: (qcl(ki, qi), 0))
    kn = pl.BlockSpec((tk, HEAD), lambda h, ki, qi: (ki, h))
    vs = pl.BlockSpec((tk, HEAD), lambda h, ki, qi: (ki, H + h))
    k1 = pl.BlockSpec((tk, LANES), lambda h, ki, qi: (ki, 0))
    head = pl.BlockSpec((S, HEAD), lambda h, ki, qi: (0, h))
    in_specs = [qs, qs, kn, k1, vs, qs, qs, qs, qt, qt, k1, k1]
    args = [qn, qr, kv, kr, kv, do, L, delta, *rope, *rope]
    if has_prev:
        in_specs += [kn, vs, k1]
        args += [prev[0], prev[0], prev[1]]
    return pl.pallas_call(
        body, name="mla_bwd", grid=(H, nk, nq), in_specs=in_specs,
        out_specs=[head, head, kn, kn, pl.BlockSpec((S, LANES), lambda h, ki, qi: (0, 0))],
        out_shape=[jax.ShapeDtypeStruct((S, H * HEAD), BF), jax.ShapeDtypeStruct((S, H * HEAD), BF),
                   jax.ShapeDtypeStruct((S, H * HEAD), F32), jax.ShapeDtypeStruct((S, H * HEAD), F32),
                   jax.ShapeDtypeStruct((S, LANES), F32)],
        scratch_shapes=[pltpu.VMEM((S, 2 * HEAD), F32), pltpu.VMEM((tk, 2 * HEAD), F32), pltpu.VMEM((tk, HEAD), F32)],
        compiler_params=_params(("arbitrary", "arbitrary", "arbitrary")),
    )(*args)


def adamw(recvs, w, m, v, *, name="adamw"):
    R, C = w.shape
    n = len(recvs)
    R1 = R // n
    assert all(r.shape == (N_DEV, R1, C) for r in recvs)
    tr = 16
    while tr * 2 * C * 44 <= 6 * 1024 * 1024 and R1 % (tr * 2) == 0:
        tr *= 2
    tr = min(tr, R1)
    assert R1 % tr == 0
    nb = R1 // tr

    def body(*refs):
        r_refs = refs[:n]
        w_ref, m_ref, v_ref, g_ref, d_ref, mo_ref, vo_ref = refs[n:]
        for slab in range(n):
            @pl.when(pl.program_id(0) == slab)
            def _(r_ref=r_refs[slab]):
                g = r_ref[0].astype(F32)
                for s in range(1, N_DEV):
                    g = g + r_ref[s].astype(F32)
                m2 = ADAM_B1 * m_ref[...] + (1.0 - ADAM_B1) * g
                v2 = ADAM_B2 * v_ref[...] + (1.0 - ADAM_B2) * (g * g)
                m_hat = m2 / (1.0 - ADAM_B1 ** ADAM_STEP)
                v_hat = v2 / (1.0 - ADAM_B2 ** ADAM_STEP)
                g_ref[...] = g
                d_ref[...] = -ADAM_LR * (m_hat / (jnp.sqrt(v_hat) + ADAM_EPS) + ADAM_WD * w_ref[...])
                mo_ref[...] = m2
                vo_ref[...] = v2

    def recv_spec(slab):
        return pl.BlockSpec((N_DEV, tr, C), lambda l, i: (0, jnp.where(l == slab, i, jnp.where(l < slab, 0, nb - 1)), 0))

    row = pl.BlockSpec((tr, C), lambda l, i: (l * nb + i, 0))
    return pl.pallas_call(
        body, name=name, grid=(n, nb),
        in_specs=[recv_spec(slab) for slab in range(n)] + [row, row, row],
        out_specs=[row] * 4, out_shape=[jax.ShapeDtypeStruct((R, C), F32)] * 4,
        compiler_params=_params(("arbitrary", "arbitrary")),
    )(*recvs, w, m, v)


def _pack(ts, width, dtype, row_mult=16):
    flat = jnp.concatenate([t.astype(dtype).reshape(-1) for t in ts])
    n = flat.shape[0]
    rows = -(-n // (width * row_mult)) * row_mult
    return jnp.pad(flat, (0, rows * width - n)).reshape(rows, width)


def _unpack(buf, shapes):
    lead = buf.shape[:-2]
    flat = buf.reshape(lead + (-1,))
    out, off = [], 0
    for s in shapes:
        n = int(np.prod(s))
        out.append(flat[..., off:off + n].reshape(lead + tuple(s)))
        off += n
    return out


def _rope_pad(r):
    z = jnp.zeros(r.shape[:-1] + (ROPE_HALF,), r.dtype)
    return jnp.concatenate([r[..., :ROPE_HALF], z, r[..., ROPE_HALF:], z], axis=-1)


def _rope_unpad(r):
    return jnp.concatenate([r[..., :ROPE_HALF], r[..., 2 * ROPE_HALF:3 * ROPE_HALF]], axis=-1)


def _step(P):
    x0 = P['x'][0]
    target = P['loss_target'][0]
    S, D = x0.shape
    NL = P['ffn_norm1'].shape[0]
    NA = P['a_wqkv'].shape[0]
    Fs = P['ffn1_wg'].shape[2]
    H = D // HEAD
    KV = P['b_wdkv'].shape[1]
    QL = P['b_wdq'].shape[2]
    HW = H * HEAD

    ffn_seq = [(l, f) for l in range(NL) for f in (0, 1)]
    ffn_loc = {(l, f): [P[n][l].astype(BF) for n in (f'ffn{f + 1}_wg', f'ffn{f + 1}_wu', f'ffn{f + 1}_wd')]
               for l, f in ffn_seq}
    mix_shapes = [P[n].shape for n in MIX_W]
    wc_loc = _pack([P[n] for n in MIX_W], D, BF)
    ffn_w = {ffn_seq[0]: all_gather(ffn_loc[ffn_seq[0]])}

    def ffn_forward(x, gain, key, extra=()):
        nxt = ffn_seq.index(key) + 1
        carry = (ffn_loc[ffn_seq[nxt]] if nxt < len(ffn_seq) else []) + list(extra)
        xo, h, G, U, *got = ffn_fwd(x, gain, *ffn_w[key], carry=carry)
        if nxt < len(ffn_seq):
            ffn_w[ffn_seq[nxt]] = got[:3]
        return (xo, h, G, U), got[len(got) - len(extra):]

    first_half, (WC,) = ffn_forward(x0, P['ffn_norm1'][0], ffn_seq[0], extra=[wc_loc])
    g = dict(zip(MIX_W, _unpack(WC, mix_shapes)))
    cols = lambda t: jnp.moveaxis(t, 0, -2).reshape(t.shape[1:-1] + (N_DEV * t.shape[-1],))
    rows = lambda t, lead: jnp.moveaxis(t, 0, lead).reshape(t.shape[1:1 + lead] + (N_DEV * t.shape[1 + lead],) + t.shape[2 + lead:])
    a_wqkv = cols(g['a_wqkv'])
    a_wo = rows(g['a_wo'], 1)
    b_wo = rows(g['b_wo'], 1)
    wdkv = rows(g['b_wdkv'], 0)
    wkr = _rope_pad(rows(g['b_wkr'], 0))
    wuk = rows(g['b_wuk'], 0).reshape(KV, HW)
    wuv = rows(g['b_wuv'], 0).reshape(KV, HW)
    wkv = jnp.concatenate([wuk, wuv], axis=1)
    wdq = rows(g['b_wdq'], 1)
    wuq = rows(g['b_wuq'], 1)
    wuq_n = wuq[..., :HEAD].reshape(-1, QL, HW)
    wuq_r = _rope_pad(wuq[..., HEAD:]).reshape(-1, QL, HW)

    inv = 1.0 / (ROPE_THETA ** (jnp.arange(0, 2 * ROPE_HALF, 2, dtype=F32) / (2 * ROPE_HALF)))
    ang = jnp.arange(S, dtype=F32)[:, None] * inv[None, :]
    z = jnp.zeros((S, ROPE_HALF), F32)
    rope = (jnp.concatenate([jnp.cos(ang), z, jnp.cos(ang), z], axis=1),
            jnp.concatenate([-jnp.sin(ang), z, jnp.sin(ang), z], axis=1))
    biases = [dilated_bias(H, d) for d in BRANCH_DILATIONS]

    saved = []
    kvs = None
    x = x0
    for l in range(NL):
        st = {'x_in': x}
        if l == NA:
            ckv_pre, hkv = norm_mm(x, P['kv_norm'], wdkv, out_dtype=F32, name="kv_down")
            kr = norm_mm(x, P['kv_norm'], wkr, out_dtype=BF, tn=LANES, rope=rope, write_h=False, name="kv_rope")
            kvm, ckv = norm_mm(ckv_pre, P['b_ckv_norm'], wkv, out_dtype=BF, name="kv_up")
            kvs = dict(x=x, ckv_pre=ckv_pre, hkv=hkv, kr=kr, kv=kvm, ckv=ckv)
        xa, h1, G1, U1 = first_half if l == 0 else ffn_forward(x, P['ffn_norm1'][l], (l, 0))[0]
        st.update(h1=h1, G1=G1, U1=U1, xa=xa)
        if l < NA:
            qkv, hm = norm_mm(xa, P['mix_norm'][l], a_wqkv[l], out_dtype=BF, tn=3 * HW // 4, name="a_qkv")
            o, Lj = dil_attn_fwd(qkv, biases, H)
            xb = mm(o, a_wo[l], add=xa, name="mix_out")
            st.update(qkv=qkv, hm=hm, o=o, L=Lj)
        else:
            jb = l - NA
            cq_pre, hm = norm_mm(xa, P['mix_norm'][l], wdq[jb], out_dtype=F32, name="q_down")
            qn, cq = norm_mm(cq_pre, P['b_cq_norm'][jb], wuq_n[jb], out_dtype=BF, name="q_up")
            qr = norm_mm(cq_pre, P['b_cq_norm'][jb], wuq_r[jb], out_dtype=BF, tn=LANES, rope=rope, write_h=False,
                         name="q_rope")
            o, Lj = mla_fwd(qn, qr, kvs['kv'], kvs['kr'], H)
            xb = mm(o, b_wo[jb], add=xa, name="mix_out")
            st.update(cq_pre=cq_pre, hm=hm, qn=qn, qr=qr, cq=cq, o=o, L=Lj)
        x, h2, G2, U2 = ffn_forward(xb, P['ffn_norm2'][l], (l, 1))[0]
        st.update(xb=xb, h2=h2, G2=G2, U2=U2)
        saved.append(st)

    loss_part, dx, dg_final = loss_head(x, P['final_norm'], target)

    gw = {}
    gv = {'final_norm': dg_final}
    dkv_acc = None
    for n in ('ffn_norm1', 'mix_norm', 'ffn_norm2'):
        gv[n] = [None] * NL
    gv['b_cq_norm'] = [None] * (NL - NA)
    ffn_r = {}
    pending = []
    a_g = {'a_wqkv': [None] * NA, 'a_wo': [None] * NA}
    b_g = {n: [None] * (NL - NA) for n in ('b_wdq', 'b_wuq', 'b_wo')}

    def ffn_backward(dy, x_in, gain, key, h, G, U, w_carry=()):
        sent, carry = pending.pop() if pending else (None, [])
        dxi, dout, dG, dU, dgain, got = ffn_bwd_x(dy, x_in, gain, G, U, *ffn_w[key], carry=carry)
        if carry:
            ffn_r[sent] = got
        dws, got_w = ffn_bwd_w(h, dout, G, U, dG, dU, carry=w_carry)
        pending.append((key, dws))
        return dxi, dgain, got_w

    split_cols = lambda t: jnp.moveaxis(t.reshape(t.shape[:-1] + (N_DEV, t.shape[-1] // N_DEV)), -2, 0)
    split_rows = lambda t, lead: jnp.moveaxis(
        t.reshape(t.shape[:lead] + (N_DEV, t.shape[lead] // N_DEV) + t.shape[lead + 1:]), lead, 0)

    def pack_mix_grads():
        for n in a_g:
            gw[n] = jnp.stack(a_g[n])
        for n in b_g:
            gw[n] = jnp.stack(b_g[n])
        shards = {
            'a_wqkv': split_cols(gw['a_wqkv']), 'a_wo': split_rows(gw['a_wo'], 1), 'b_wdkv': split_rows(gw['b_wdkv'], 0),
            'b_wkr': split_rows(gw['b_wkr'], 0), 'b_wuk': split_rows(gw['b_wuk'], 0), 'b_wuv': split_rows(gw['b_wuv'], 0),
            'b_wdq': split_rows(gw['b_wdq'], 1), 'b_wuq': split_rows(gw['b_wuq'], 1), 'b_wo': split_rows(gw['b_wo'], 1),
        }
        return jnp.stack([_pack([shards[n][p] for n in MIX_W], D, BF) for p in range(N_DEV)])

    assert 1 <= NA < NL
    rc = None
    for l in reversed(range(NL)):
        st = saved[l]
        dxb, gv['ffn_norm2'][l], _ = ffn_backward(dx, st['xb'], P['ffn_norm2'][l], (l, 1), st['h2'], st['G2'], st['U2'])
        xa = st['xa']
        if l < NA:
            do = mm(dxb, a_wo[l], tb=True, out_dtype=BF, name="mix_out_dx")
            a_g['a_wo'][l] = mm(st['o'], dxb, ta=True, out_dtype=BF, tm=1024, tn=1024, name="mix_out_dw")
            delta = head_delta(do, st['o'])
            dqkv = jnp.concatenate(dil_attn_bwd(st['qkv'], do, st['L'], delta, biases, H), axis=1)
            dh = mm(dqkv, a_wqkv[l], tb=True, name="a_qkv_dx")
            a_g['a_wqkv'][l] = mm(st['hm'], dqkv, ta=True, out_dtype=BF, tm=1024, tn=1024, name="a_qkv_dw")
            dxa, gv['mix_norm'][l] = norm_bwd(xa, P['mix_norm'][l], dh, dxb, name="mix_norm_bwd")
        else:
            jb = l - NA
            do = mm(dxb, b_wo[jb], tb=True, out_dtype=BF, name="mix_out_dx")
            b_g['b_wo'][jb] = mm(st['o'], dxb, ta=True, out_dtype=BF, tm=1024, tn=1024, name="mix_out_dw")
            delta = head_delta(do, st['o'])
            dqn, dqr, dkn, dv, dr = mla_bwd(st['qn'], st['qr'], kvs['kv'], kvs['kr'], do, st['L'], delta, rope, H,
                                            prev=dkv_acc)
            dkv_acc = (jnp.concatenate([dkn, dv], axis=1), dr)
            dcq = mm(dqn, wuq_n[jb], tb=True, name="q_up_dx")
            dcq = mm(dqr, wuq_r[jb], tb=True, add=dcq, name="q_up_dx_add")
            dwn = mm(st['cq'], dqn, ta=True, out_dtype=F32, name="q_up_dw")
            dwr = mm(st['cq'], dqr, ta=True, out_dtype=F32, name="q_up_dw")
            b_g['b_wuq'][jb] = jnp.concatenate(
                [dwn.reshape(QL, H, HEAD), _rope_unpad(dwr.reshape(QL, H, HEAD))], axis=-1)
            dcq_pre, gv['b_cq_norm'][jb] = norm_bwd(st['cq_pre'], P['b_cq_norm'][jb], dcq, name="cq_norm_bwd")
            dh = mm(dcq_pre, wdq[jb], tb=True, name="q_down_dx")
            b_g['b_wdq'][jb] = mm(st['hm'], dcq_pre, ta=True, out_dtype=F32, tm=1024, name="q_down_dw")
            dxa, gv['mix_norm'][l] = norm_bwd(xa, P['mix_norm'][l], dh, dxb, name="mix_norm_bwd")
        w_carry = [pack_mix_grads()] if l == 0 else []
        dx, gv['ffn_norm1'][l], got_w = ffn_backward(dxa, st['x_in'], P['ffn_norm1'][l], (l, 0), st['h1'], st['G1'],
                                                     st['U1'], w_carry)
        if w_carry:
            rc = got_w[0]
        if l == NA:
            dkvm, dr = dkv_acc
            dckv = mm(dkvm, wkv, tb=True, name="kv_up_dx")
            dwkv = mm(kvs['ckv'], dkvm, ta=True, out_dtype=F32, name="kv_up_dw")
            gw['b_wuk'] = dwkv[:, :HW].reshape(KV, H, HEAD)
            gw['b_wuv'] = dwkv[:, HW:].reshape(KV, H, HEAD)
            dckv_pre, gv['b_ckv_norm'] = norm_bwd(kvs['ckv_pre'], P['b_ckv_norm'], dckv, name="ckv_norm_bwd")
            dh = mm(dckv_pre, wdkv, tb=True, name="kv_down_dx")
            dh = mm(dr, wkr, tb=True, add=dh, name="kv_rope_dx")
            gw['b_wdkv'] = mm(kvs['hkv'], dckv_pre, ta=True, out_dtype=F32, tm=1024, name="kv_down_dw")
            gw['b_wkr'] = _rope_unpad(mm(kvs['hkv'], dr, ta=True, out_dtype=F32, tm=1024, name="kv_rope_dw"))
            dx, gv['kv_norm'] = norm_bwd(kvs['x'], P['kv_norm'], dh, dx, name="kv_norm_bwd")

    for n in ('ffn_norm1', 'mix_norm', 'ffn_norm2', 'b_cq_norm'):
        gv[n] = jnp.stack(gv[n])

    vec_parts = [gv[n] for n in VEC_W] + [jnp.full((LANES,), loss_part, F32)]
    gvec = _pack(vec_parts, LANES, F32, row_mult=8)
    last_key, last = pending.pop()
    *got, rv = exchange(last + [gvec], [False] * len(last) + [True])
    ffn_r[last_key] = got

    res = {}
    for f in (0, 1):
        for which, kind in enumerate(('wg', 'wu', 'wd')):
            n = f'ffn{f + 1}_{kind}'
            recvs = [ffn_r[(l, f)][which] for l in range(NL)]
            flat = lambda t: t.reshape((NL * t.shape[1], t.shape[2]))
            out = adamw(recvs, flat(P[n]), flat(P['m_' + n]), flat(P['v_' + n]),
                        name="adamw_row" if kind == 'wd' else "adamw_col")
            res[n] = [t.reshape(P[n].shape) for t in out]
    pk = lambda pre: _pack([P[pre + n] for n in MIX_W], D, F32)
    out = adamw([rc], pk(''), pk('m_'), pk('v_'), name="adamw_mix")
    for n, parts in zip(MIX_W, zip(*[_unpack(t, mix_shapes) for t in out])):
        res[n] = list(parts)
    vec_shapes = [P[n].shape for n in VEC_W] + [(LANES,)]
    ones = jnp.ones((LANES,), F32)
    pv = lambda pre: _pack([P[pre + n] for n in VEC_W] + [ones], LANES, F32, row_mult=8)
    out = adamw([rv], pv(''), pv('m_'), pv('v_'), name="adamw_vec")
    unp = [_unpack(t, vec_shapes) for t in out]
    for idx, n in enumerate(VEC_W):
        res[n] = [u[idx] for u in unp]
    loss = unp[0][-1][0]

    outs = [loss, dx[None]]
    for field in range(4):
        outs += [res[n][field] for n in W_NAMES]
    return tuple(outs)


def kernel(x, ffn_norm1, ffn1_wg, ffn1_wu, ffn1_wd, mix_norm, ffn_norm2, ffn2_wg, ffn2_wu, ffn2_wd, a_wqkv, a_wo, kv_norm, b_wdkv, b_ckv_norm, b_wkr, b_wuk, b_wuv, b_wdq, b_cq_norm, b_wuq, b_wo, final_norm, loss_target, m_ffn_norm1, m_ffn1_wg, m_ffn1_wu, m_ffn1_wd, m_mix_norm, m_ffn_norm2, m_ffn2_wg, m_ffn2_wu, m_ffn2_wd, m_a_wqkv, m_a_wo, m_kv_norm, m_b_wdkv, m_b_ckv_norm, m_b_wkr, m_b_wuk, m_b_wuv, m_b_wdq, m_b_cq_norm, m_b_wuq, m_b_wo, m_final_norm, v_ffn_norm1, v_ffn1_wg, v_ffn1_wu, v_ffn1_wd, v_mix_norm, v_ffn_norm2, v_ffn2_wg, v_ffn2_wu, v_ffn2_wd, v_a_wqkv, v_a_wo, v_kv_norm, v_b_wdkv, v_b_ckv_norm, v_b_wkr, v_b_wuk, v_b_wuv, v_b_wdq, v_b_cq_norm, v_b_wuq, v_b_wo, v_final_norm):
    return _step(dict(locals()))
```

```python
import functools
import math

import numpy as np
import jax
import jax.numpy as jnp
from jax import lax
from jax.experimental import pallas as pl
from jax.experimental.pallas import tpu as pltpu

BF = jnp.bfloat16
F32 = jnp.float32
MESH = pl.DeviceIdType.MESH
ANY = pl.BlockSpec(memory_space=pl.ANY)

N_DEV = 8
LANES = 128
HEAD = 128
ROPE_HALF = 32
DIL_N = 128
BRANCH_DILATIONS = (1, 4, 16)
ROPE_THETA = 10000.0
MLA_SCALE = (HEAD + 2 * ROPE_HALF) ** -0.5
LOG2E = math.log2(math.e)
EPS = 1e-6
NEG = -1e30
VMEM_LIMIT = 58 * 1024 * 1024

ADAM_LR, ADAM_B1, ADAM_B2, ADAM_EPS, ADAM_WD, ADAM_STEP = 0.001, 0.9, 0.999, 1e-08, 0.01, 10

W_NAMES = ['ffn_norm1', 'ffn1_wg', 'ffn1_wu', 'ffn1_wd', 'mix_norm', 'ffn_norm2', 'ffn2_wg', 'ffn2_wu', 'ffn2_wd',
           'a_wqkv', 'a_wo', 'kv_norm', 'b_wdkv', 'b_ckv_norm', 'b_wkr', 'b_wuk', 'b_wuv', 'b_wdq', 'b_cq_norm',
           'b_wuq', 'b_wo', 'final_norm']
MIX_W = ['a_wqkv', 'a_wo', 'b_wdkv', 'b_wkr', 'b_wuk', 'b_wuv', 'b_wdq', 'b_wuq', 'b_wo']
VEC_W = ['ffn_norm1', 'mix_norm', 'ffn_norm2', 'kv_norm', 'b_ckv_norm', 'b_cq_norm', 'final_norm']


def _tile(n, pref):
    t = min(n, pref)
    assert n % t == 0, (n, pref)
    return t


def _params(sem):
    return pltpu.CompilerParams(dimension_semantics=sem, vmem_limit_bytes=VMEM_LIMIT)


def _dot(a, b):
    return jnp.dot(a, b, preferred_element_type=F32)


def _dot_nt(a, b):
    return lax.dot_general(a, b, (((1,), (1,)), ((), ())), preferred_element_type=F32)


def _dot_tn(a, b):
    return lax.dot_general(a, b, (((0,), (0,)), ((), ())), preferred_element_type=F32)


def _rms(x, g):
    r = lax.rsqrt(jnp.mean(x * x, axis=-1, keepdims=True) + EPS)
    return x * r * g, r


def _rms_bwd(x, g, dh):
    r = lax.rsqrt(jnp.mean(x * x, axis=-1, keepdims=True) + EPS)
    xhat = x * r
    gd = dh * g
    dx = r * (gd - xhat * jnp.mean(gd * xhat, axis=-1, keepdims=True))
    return dx, jnp.sum(dh * xhat, axis=0, keepdims=True)


def _rope(t, cos, sin):
    return t * cos + pltpu.roll(t, 2 * ROPE_HALF, 1) * sin


def _rope_bwd(g, cos, sin):
    return g * cos + pltpu.roll(g * sin, 2 * ROPE_HALF, 1)


def _coords():
    return lax.axis_index("x"), lax.axis_index("y"), lax.axis_index("c")


def _lin(px, py, pc):
    return 4 * px + 2 * py + pc


def _gather_phases(ins, outs, ssem, rsem, lsem):
    n = len(ins)
    x, y, c = _coords()
    me, sibling = (x, y, c), (x, y, 1 - c)
    chips = [(1 - x, y), (x, 1 - y), (1 - x, 1 - y)]

    def copy(t, k, block, to, src=None):
        slot = outs[t].at[_lin(*block)]
        return pltpu.make_async_remote_copy(
            src_ref=slot if src is None else src, dst_ref=slot,
            send_sem=ssem.at[7 * t + k], recv_sem=rsem.at[7 * t + k],
            device_id=to, device_id_type=MESH)

    def mine(t):
        return pltpu.make_async_copy(ins[t], outs[t].at[_lin(*me)], lsem.at[t])

    def first(t):
        return [copy(t, 0, me, sibling, src=ins[t])] + [copy(t, 1 + j, me, (*chip, c), src=ins[t])
                                                       for j, chip in enumerate(chips)]

    def passed(t):
        return [copy(t, 4 + j, (*chip, c), sibling) for j, chip in enumerate(chips)]

    def start():
        for t in range(n):
            mine(t).start()
            for cp in first(t):
                cp.start()

    def middle():
        for t in range(n):
            fw = passed(t)
            for j, chip in enumerate(chips):
                copy(t, 1 + j, (*chip, c), me).wait_recv()
                fw[j].start()

    def finish():
        for t in range(n):
            copy(t, 0, sibling, me).wait_recv()
            for j, chip in enumerate(chips):
                copy(t, 4 + j, (*chip, 1 - c), me).wait_recv()
        for t in range(n):
            for cp in first(t) + passed(t):
                cp.wait_send()
            mine(t).wait()

    return start, middle, finish


def _exchange_phases(ins, outs, bcast, ssem, rsem, lsem):
    n = len(ins)
    x, y, c = _coords()
    me = _lin(x, y, c)
    peers = []
    for k in range(1, N_DEV):
        kx, ky, kc = (k >> 2) & 1, (k >> 1) & 1, k & 1
        peers.append((k, (1 - x if kx else x, 1 - y if ky else y, 1 - c if kc else c)))

    def own(t):
        return pltpu.make_async_copy(ins[t] if bcast[t] else ins[t].at[me], outs[t].at[me], lsem.at[t])

    def send(t, k, peer):
        return pltpu.make_async_remote_copy(
            src_ref=ins[t] if bcast[t] else ins[t].at[_lin(*peer)], dst_ref=outs[t].at[me],
            send_sem=ssem.at[7 * t + k - 1], recv_sem=rsem.at[7 * t + k - 1],
            device_id=peer, device_id_type=MESH)

    def arrival(t, k, peer):
        slot = outs[t].at[_lin(*peer)]
        return pltpu.make_async_remote_copy(
            src_ref=slot, dst_ref=slot, send_sem=ssem.at[7 * t + k - 1], recv_sem=rsem.at[7 * t + k - 1],
            device_id=peer, device_id_type=MESH)

    def start():
        for t in range(n):
            own(t).start()
            for k, peer in peers:
                send(t, k, peer).start()

    def finish():
        for t in range(n):
            for k, peer in peers:
                arrival(t, k, peer).wait_recv()
        for t in range(n):
            for k, peer in peers:
                send(t, k, peer).wait_send()
            own(t).wait()

    return start, finish


def _comm_sems(n):
    return [pltpu.SemaphoreType.DMA((7 * n,)), pltpu.SemaphoreType.DMA((7 * n,)), pltpu.SemaphoreType.DMA((n,))]


def _gathered(xs):
    return [jax.ShapeDtypeStruct((N_DEV,) + a.shape, a.dtype) for a in xs]


def all_gather(xs):
    n = len(xs)

    def body(*refs):
        start, middle, finish = _gather_phases(refs[:n], refs[n:2 * n], *refs[2 * n:])
        start()
        middle()
        finish()

    return pl.pallas_call(
        body, name="all_gather", out_shape=_gathered(xs),
        in_specs=[ANY] * n, out_specs=[ANY] * n, scratch_shapes=_comm_sems(n),
    )(*xs)


def exchange(xs, bcast):
    n = len(xs)

    def body(*refs):
        start, finish = _exchange_phases(refs[:n], refs[n:2 * n], bcast, *refs[2 * n:])
        start()
        finish()

    out_shape = [jax.ShapeDtypeStruct(((N_DEV,) + a.shape) if b else a.shape, a.dtype) for a, b in zip(xs, bcast)]
    return pl.pallas_call(
        body, name="grad_exchange", out_shape=out_shape,
        in_specs=[ANY] * n, out_specs=[ANY] * n, scratch_shapes=_comm_sems(n),
    )(*xs)


def mm(a, b, *, ta=False, tb=False, add=None, out_dtype=F32, tm=512, tn=1024, tk=None, name="mm"):
    K, M = a.shape if ta else a.shape[::-1]
    N = b.shape[0] if tb else b.shape[1]
    assert (b.shape[1] if tb else b.shape[0]) == K and not (ta and tb)
    if tk is None:
        tk = 512 if ta else 2048
    tm, tn, tk = _tile(M, tm), _tile(N, tn), _tile(K, tk)
    nk = K // tk
    has_add = add is not None

    def body(*refs):
        if has_add:
            a_ref, b_ref, add_ref, o_ref, acc = refs
        else:
            a_ref, b_ref, o_ref, acc = refs
        k = pl.program_id(2)

        @pl.when(k == 0)
        def _():
            acc[...] = jnp.zeros_like(acc)

        av = a_ref[...].astype(BF)
        bv = b_ref[...].astype(BF)
        if ta:
            acc[...] += _dot_tn(av, bv)
        elif tb:
            acc[...] += _dot_nt(av, bv)
        else:
            acc[...] += _dot(av, bv)

        @pl.when(k == nk - 1)
        def _():
            r = acc[...]
            if has_add:
                r = r + add_ref[...]
            o_ref[...] = r.astype(out_dtype)

    a_spec = pl.BlockSpec((tk, tm), lambda i, j, k: (k, i)) if ta else pl.BlockSpec((tm, tk), lambda i, j, k: (i, k))
    b_spec = pl.BlockSpec((tn, tk), lambda i, j, k: (j, k)) if tb else pl.BlockSpec((tk, tn), lambda i, j, k: (k, j))
    o_spec = pl.BlockSpec((tm, tn), lambda i, j, k: (i, j))
    in_specs = [a_spec, b_spec] + ([o_spec] if has_add else [])
    args = [a, b] + ([add] if has_add else [])
    return pl.pallas_call(
        body, name=name, grid=(M // tm, N // tn, nk),
        in_specs=in_specs, out_specs=o_spec,
        out_shape=jax.ShapeDtypeStruct((M, N), out_dtype),
        scratch_shapes=[pltpu.VMEM((tm, tn), F32)],
        compiler_params=_params(("parallel", "parallel", "arbitrary")),
    )(*args)


def norm_mm(x, gain, w, *, out_dtype, tn=512, rope=None, write_h=True, tm=512, name="norm_mm"):
    S, K = x.shape
    N = w.shape[1]
    tm, tn = _tile(S, tm), _tile(N, tn)
    if rope is not None:
        assert tn == LANES
    gain = gain.reshape(1, K)

    def body(*refs):
        refs = list(refs)
        x_ref, g_ref, w_ref = refs[:3]
        refs = refs[3:]
        if rope is not None:
            cos_ref, sin_ref = refs[:2]
            refs = refs[2:]
        y_ref = refs[0]
        h_ref = refs[1] if write_h else None
        hs = refs[-1]
        j = pl.program_id(1)

        @pl.when(j == 0)
        def _():
            h, _ = _rms(x_ref[...], g_ref[...])
            hb = h.astype(BF)
            hs[...] = hb
            if write_h:
                h_ref[...] = hb

        y = _dot(hs[...], w_ref[...])
        if rope is not None:
            y = _rope(y, cos_ref[...], sin_ref[...])
        y_ref[...] = y.astype(out_dtype)

    in_specs = [pl.BlockSpec((tm, K), lambda i, j: (i, 0)), pl.BlockSpec((1, K), lambda i, j: (0, 0)),
                pl.BlockSpec((K, tn), lambda i, j: (0, j))]
    args = [x, gain, w]
    if rope is not None:
        in_specs += [pl.BlockSpec((tm, LANES), lambda i, j: (i, 0))] * 2
        args += list(rope)
    out_specs = [pl.BlockSpec((tm, tn), lambda i, j: (i, j))]
    out_shape = [jax.ShapeDtypeStruct((S, N), out_dtype)]
    if write_h:
        out_specs.append(pl.BlockSpec((tm, K), lambda i, j: (i, 0)))
        out_shape.append(jax.ShapeDtypeStruct((S, K), BF))
    res = pl.pallas_call(
        body, name=name, grid=(S // tm, N // tn), in_specs=in_specs, out_specs=out_specs, out_shape=out_shape,
        scratch_shapes=[pltpu.VMEM((tm, K), BF)],
        compiler_params=_params(("parallel", "arbitrary")),
    )(*args)
    return res if write_h else res[0]


def norm_bwd(x, gain, dh, dres=None, *, tm=512, name="norm_bwd"):
    S, K = x.shape
    tm = _tile(S, tm)
    gain = gain.reshape(1, K)
    has_res = dres is not None

    def body(*refs):
        if has_res:
            x_ref, g_ref, dh_ref, dr_ref, dx_ref, dg_ref = refs
        else:
            x_ref, g_ref, dh_ref, dx_ref, dg_ref = refs

        @pl.when(pl.program_id(0) == 0)
        def _():
            dg_ref[...] = jnp.zeros_like(dg_ref)

        dx, dg = _rms_bwd(x_ref[...], g_ref[...], dh_ref[...].astype(F32))
        if has_res:
            dx = dx + dr_ref[...]
        dx_ref[...] = dx
        dg_ref[...] += jnp.broadcast_to(dg, dg_ref.shape)

    row = pl.BlockSpec((tm, K), lambda i: (i, 0))
    in_specs = [row, pl.BlockSpec((1, K), lambda i: (0, 0)), row] + ([row] if has_res else [])
    args = [x, gain, dh] + ([dres] if has_res else [])
    dx, dg = pl.pallas_call(
        body, name=name, grid=(S // tm,), in_specs=in_specs,
        out_specs=[row, pl.BlockSpec((8, K), lambda i: (0, 0))],
        out_shape=[jax.ShapeDtypeStruct((S, K), F32), jax.ShapeDtypeStruct((8, K), F32)],
        compiler_params=_params(("arbitrary",)),
    )(*args)
    return dx, dg[0]


def loss_head(x, gain, target, *, tm=512):
    S, K = x.shape
    tm = _tile(S, tm)
    gain = gain.reshape(1, K)

    def body(x_ref, g_ref, t_ref, dx_ref, dg_ref, ls_ref):
        @pl.when(pl.program_id(0) == 0)
        def _():
            dg_ref[...] = jnp.zeros_like(dg_ref)
            ls_ref[...] = jnp.zeros_like(ls_ref)

        xv, g = x_ref[...], g_ref[...]
        y, _ = _rms(xv, g)
        e = y - t_ref[...]
        part = jnp.sum(jnp.mean(e * e, axis=-1, keepdims=True), axis=0, keepdims=True)
        ls_ref[...] += jnp.broadcast_to(0.5 * part, ls_ref.shape)
        dx, dg = _rms_bwd(xv, g, e / K)
        dx_ref[...] = dx
        dg_ref[...] += jnp.broadcast_to(dg, dg_ref.shape)

    row = pl.BlockSpec((tm, K), lambda i: (i, 0))
    dx, dg, ls = pl.pallas_call(
        body, name="loss_head", grid=(S // tm,),
        in_specs=[row, pl.BlockSpec((1, K), lambda i: (0, 0)), row],
        out_specs=[row, pl.BlockSpec((8, K), lambda i: (0, 0)), pl.BlockSpec((8, LANES), lambda i: (0, 0))],
        out_shape=[jax.ShapeDtypeStruct((S, K), F32), jax.ShapeDtypeStruct((8, K), F32),
                   jax.ShapeDtypeStruct((8, LANES), F32)],
        compiler_params=_params(("arbitrary",)),
    )(x, gain, target)
    return ls[0, 0], dx, dg[0]


def _once(shape, index_map):
    return pl.BlockSpec(shape, index_map, pipeline_mode=pl.Buffered(1))


def ffn_fwd(x, gain, wg, wu, wd, carry=(), *, tm=512):
    S, D = x.shape
    NB, _, Fs = wg.shape
    tm = _tile(S, tm)
    ni = S // tm
    nc = len(carry)
    gain = gain.reshape(1, D)

    def body(*refs):
        x_ref, g_ref, wg_ref, wu_ref, wd_ref = refs[:5]
        c_in = refs[5:5 + nc]
        xo_ref, h_ref, G_ref, U_ref = refs[5 + nc:9 + nc]
        c_out = refs[9 + nc:9 + 2 * nc]
        hs, acc = refs[9 + 2 * nc:11 + 2 * nc]
        i, j = pl.program_id(0), pl.program_id(1)
        if nc:
            start, middle, finish = _gather_phases(c_in, c_out, *refs[11 + 2 * nc:])
            pl.when((i == 0) & (j == 0))(start)
            pl.when((i == ni // 2) & (j == 0))(middle)

        @pl.when(j == 0)
        def _():
            h, _ = _rms(x_ref[...], g_ref[...])
            hb = h.astype(BF)
            hs[...] = hb
            h_ref[...] = hb
            acc[...] = jnp.zeros_like(acc)

        h = hs[...]
        g = _dot(h, wg_ref[...])
        u = _dot(h, wu_ref[...])
        G_ref[...] = g.astype(BF)
        U_ref[...] = u.astype(BF)
        a = (g * jax.nn.sigmoid(g) * u).astype(BF)
        acc[...] += _dot(a, wd_ref[...])

        @pl.when(j == NB - 1)
        def _():
            xo_ref[...] = x_ref[...] + 0.5 * acc[...]

        if nc:
            pl.when((i == ni - 1) & (j == NB - 1))(finish)

    row = lambda i, j: (i, 0)
    blk = lambda i, j: (j, 0, 0)
    hid = pl.BlockSpec((None, tm, Fs), lambda i, j: (j, i, 0))
    return pl.pallas_call(
        body, name="ffn_fwd_gather" if nc else "ffn_fwd", grid=(ni, NB),
        in_specs=[pl.BlockSpec((tm, D), row), pl.BlockSpec((1, D), lambda i, j: (0, 0)),
                  pl.BlockSpec((None, D, Fs), blk), pl.BlockSpec((None, D, Fs), blk), pl.BlockSpec((None, Fs, D), blk)]
                 + [ANY] * nc,
        out_specs=[pl.BlockSpec((tm, D), row), pl.BlockSpec((tm, D), row), hid, hid] + [ANY] * nc,
        out_shape=[jax.ShapeDtypeStruct((S, D), F32), jax.ShapeDtypeStruct((S, D), BF),
                   jax.ShapeDtypeStruct((NB, S, Fs), BF), jax.ShapeDtypeStruct((NB, S, Fs), BF)] + _gathered(carry),
        scratch_shapes=[pltpu.VMEM((tm, D), BF), pltpu.VMEM((tm, D), F32)] + (_comm_sems(nc) if nc else []),
        compiler_params=_params(("arbitrary", "arbitrary")),
    )(x, gain, wg, wu, wd, *carry)


def ffn_bwd_x(dy, x, gain, G, U, wg, wu, wd, carry=(), *, tm=512, sub=256):
    S, D = x.shape
    NB, _, Fs = wg.shape
    tm = _tile(S, tm)
    sub = _tile(tm, sub)
    ni = S // tm
    nc = len(carry)
    gain = gain.reshape(1, D)

    def body(*refs):
        dy_ref, x_ref, g_ref, G_ref, U_ref, wg_ref, wu_ref, wd_ref = refs[:8]
        c_in = refs[8:8 + nc]
        dx_ref, do_ref, dG_ref, dU_ref, dg_ref = refs[8 + nc:13 + nc]
        c_out = refs[13 + nc:13 + 2 * nc]
        dob, acc = refs[13 + 2 * nc:15 + 2 * nc]
        i, j = pl.program_id(0), pl.program_id(1)
        if nc:
            start, finish = _exchange_phases(c_in, c_out, [False] * nc, *refs[15 + 2 * nc:])
            pl.when((i == 0) & (j == 0))(start)

        @pl.when(j == 0)
        def _():
            d = (0.5 * dy_ref[...]).astype(BF)
            dob[...] = d
            do_ref[...] = d
            acc[...] = jnp.zeros_like(acc)

        @pl.when((i == 0) & (j == 0))
        def _():
            dg_ref[...] = jnp.zeros_like(dg_ref)

        for r in range(tm // sub):
            rows = slice(r * sub, (r + 1) * sub)
            dA = _dot_nt(dob[rows, :], wd_ref[...])
            g = G_ref[rows, :].astype(F32)
            u = U_ref[rows, :].astype(F32)
            sig = jax.nn.sigmoid(g)
            dG = (dA * u * (sig * (1.0 + g * (1.0 - sig)))).astype(BF)
            dU = (dA * (g * sig)).astype(BF)
            dG_ref[rows, :] = dG
            dU_ref[rows, :] = dU
            acc[rows, :] += _dot_nt(dG, wg_ref[...]) + _dot_nt(dU, wu_ref[...])

        @pl.when(j == NB - 1)
        def _():
            dxn, dg = _rms_bwd(x_ref[...], g_ref[...], acc[...])
            dx_ref[...] = dy_ref[...] + dxn
            dg_ref[...] += jnp.broadcast_to(dg, dg_ref.shape)

        if nc:
            pl.when((i == ni - 1) & (j == NB - 1))(finish)

    row = lambda i, j: (i, 0)
    blk = lambda i, j: (j, 0, 0)
    hid = pl.BlockSpec((None, tm, Fs), lambda i, j: (j, i, 0))
    dx, dout, dG, dU, dg, *got = pl.pallas_call(
        body, name="ffn_bwd_x_exchange" if nc else "ffn_bwd_x", grid=(ni, NB),
        in_specs=[_once((tm, D), row), _once((tm, D), row), pl.BlockSpec((1, D), lambda i, j: (0, 0)), hid, hid,
                  pl.BlockSpec((None, D, Fs), blk), pl.BlockSpec((None, D, Fs), blk), pl.BlockSpec((None, Fs, D), blk)]
                 + [ANY] * nc,
        out_specs=[_once((tm, D), row), _once((tm, D), row), hid, hid, pl.BlockSpec((8, D), lambda i, j: (0, 0))]
                  + [ANY] * nc,
        out_shape=[jax.ShapeDtypeStruct((S, D), F32), jax.ShapeDtypeStruct((S, D), BF),
                   jax.ShapeDtypeStruct((NB, S, Fs), BF), jax.ShapeDtypeStruct((NB, S, Fs), BF),
                   jax.ShapeDtypeStruct((8, D), F32)] + [jax.ShapeDtypeStruct(a.shape, a.dtype) for a in carry],
        scratch_shapes=[pltpu.VMEM((tm, D), BF), pltpu.VMEM((tm, D), F32)] + (_comm_sems(nc) if nc else []),
        compiler_params=_params(("arbitrary", "arbitrary")),
    )(dy, x, gain, G, U, wg, wu, wd, *carry)
    return dx, dout, dG, dU, dg[0], got


def ffn_bwd_w(h, dout, G, U, dG, dU, carry=(), *, tm=512):
    S, D = h.shape
    NB, _, Fs = G.shape
    tm = _tile(S, tm)
    ni = S // tm
    nc = len(carry)

    def body(*refs):
        h_ref, do_ref, G_ref, U_ref, dG_ref, dU_ref = refs[:6]
        c_in = refs[6:6 + nc]
        wg_ref, wu_ref, wd_ref = refs[6 + nc:9 + nc]
        c_out = refs[9 + nc:9 + 2 * nc]
        ag, au, ad = refs[9 + 2 * nc:12 + 2 * nc]
        j, i = pl.program_id(0), pl.program_id(1)
        if nc:
            start, finish = _exchange_phases(c_in, c_out, [False] * nc, *refs[12 + 2 * nc:])
            pl.when((j == 0) & (i == 0))(start)

        @pl.when(i == 0)
        def _():
            ag[...] = jnp.zeros_like(ag)
            au[...] = jnp.zeros_like(au)
            ad[...] = jnp.zeros_like(ad)

        h = h_ref[...]
        ag[...] += _dot_tn(h, dG_ref[...])
        au[...] += _dot_tn(h, dU_ref[...])
        g = G_ref[...].astype(F32)
        a = (g * jax.nn.sigmoid(g) * U_ref[...].astype(F32)).astype(BF)
        ad[...] += _dot_tn(a, do_ref[...])

        @pl.when(i == ni - 1)
        def _():
            wg_ref[...] = ag[...].astype(BF)
            wu_ref[...] = au[...].astype(BF)
            wd_ref[...] = ad[...].astype(BF)

        if nc:
            pl.when((j == NB - 1) & (i == ni - 1))(finish)

    row = pl.BlockSpec((tm, D), lambda j, i: (i, 0))
    hid = pl.BlockSpec((None, tm, Fs), lambda j, i: (j, i, 0))
    blk = lambda j, i: (j, 0, 0)
    dwg, dwu, dwd, *got = pl.pallas_call(
        body, name="ffn_bwd_w_exchange" if nc else "ffn_bwd_w", grid=(NB, ni),
        in_specs=[row, row, hid, hid, hid, hid] + [ANY] * nc,
        out_specs=[_once((None, D, Fs), blk), _once((None, D, Fs), blk), _once((None, Fs, D), blk)] + [ANY] * nc,
        out_shape=[jax.ShapeDtypeStruct((NB, D, Fs), BF), jax.ShapeDtypeStruct((NB, D, Fs), BF),
                   jax.ShapeDtypeStruct((NB, Fs, D), BF)] + [jax.ShapeDtypeStruct(a.shape, a.dtype) for a in carry],
        scratch_shapes=[pltpu.VMEM((D, Fs), F32), pltpu.VMEM((D, Fs), F32), pltpu.VMEM((Fs, D), F32)]
                       + (_comm_sems(nc) if nc else []),
        compiler_params=_params(("arbitrary", "arbitrary")),
    )(h, dout, G, U, dG, dU, *carry)
    return [dwg, dwu, dwd], got


def dilated_bias(H, d):
    n = DIL_N
    slopes = 2.0 ** (-8.0 * (np.arange(H) + 1) / H)
    i = np.arange(n)[:, None]
    j = np.arange(2 * n)[None, :]
    steps = n + i - j
    band = (steps >= 0) & (steps <= n)
    first = band & (j >= n)
    bias = -slopes[:, None, None] * (d * steps).astype(np.float64)[None]
    out = np.stack([np.where(band[None], bias, NEG), np.where(first[None], bias, NEG)], axis=1)
    return jnp.asarray(out, dtype=F32)


def head_delta(do, o, *, tm=512):
    S, W = o.shape
    tm = _tile(S, tm)

    def body(do_ref, o_ref, d_ref):
        for h in range(W // HEAD):
            cols = slice(h * HEAD, (h + 1) * HEAD)
            s = jnp.sum(do_ref[:, cols].astype(F32) * o_ref[:, cols].astype(F32), axis=1, keepdims=True)
            d_ref[:, cols] = jnp.broadcast_to(s, (tm, HEAD))

    blk = pl.BlockSpec((tm, W), lambda i: (i, 0))
    return pl.pallas_call(
        body, name="head_delta", grid=(S // tm,), in_specs=[blk, blk], out_specs=blk,
        out_shape=jax.ShapeDtypeStruct((S, W), F32),
        compiler_params=_params(("parallel",)),
    )(do, o)


DIL_CHUNK = DIL_N * max(BRANCH_DILATIONS)
DIL_UNROLL = 4


def _rows(start, size, d):
    return pl.ds(pl.multiple_of(start, DIL_N), size) if d == 1 else pl.ds(start, size, stride=d)


def dil_attn_fwd(qkv, biases, H):
    S = qkv.shape[0]
    C = DIL_CHUNK
    assert S % C == 0
    scale = HEAD ** -0.5
    nbr = len(BRANCH_DILATIONS)

    def body(*refs):
        q_ref, kc_ref, kp_ref, vc_ref, vp_ref = refs[:5]
        b_refs = refs[5:5 + nbr]
        o_ref, L_ref, qf, kf, vf = refs[5 + nbr:10 + nbr]
        ogs = refs[10 + nbr:10 + 2 * nbr]
        lgs = refs[10 + 2 * nbr:10 + 3 * nbr]
        c = pl.program_id(1)
        qf[...] = q_ref[...].astype(F32)
        kf[0:C, :] = kp_ref[...].astype(F32)
        kf[C:2 * C, :] = kc_ref[...].astype(F32)
        vf[0:C, :] = vp_ref[...].astype(F32)
        vf[C:2 * C, :] = vc_ref[...].astype(F32)
        for d, b_ref, og, lg in zip(BRANCH_DILATIONS, b_refs, ogs, lgs):
            span = DIL_N * d

            def block(t, carry, d=d, b_ref=b_ref, og=og, lg=lg, span=span):
                r, b = t % d, t // d
                q0 = b * span + r
                q = qf[_rows(q0, DIL_N, d), :].astype(BF)
                k2 = kf[_rows(C + q0 - span, 2 * DIL_N, d), :].astype(BF)
                v2 = vf[_rows(C + q0 - span, 2 * DIL_N, d), :].astype(BF)
                bias = jnp.where((c == 0) & (b == 0), b_ref[1], b_ref[0])
                s = _dot_nt(q, k2) * scale + bias
                m = jnp.max(s, axis=1, keepdims=True)
                p = jnp.exp(s - m)
                l = jnp.sum(p, axis=1, keepdims=True)
                og[_rows(q0, DIL_N, d), :] = _dot(p.astype(BF), v2) / l
                lg[_rows(q0, DIL_N, d), :] = jnp.broadcast_to(m + jnp.log(l), (DIL_N, LANES))
                return carry

            lax.fori_loop(0, C // DIL_N, block, 0, unroll=DIL_UNROLL)
        ls = [lg[...] for lg in lgs]
        m = functools.reduce(jnp.maximum, ls)
        es = [jnp.exp(l - m) for l in ls]
        z = functools.reduce(jnp.add, es)
        mix = functools.reduce(jnp.add, [e * og[...] for e, og in zip(es, ogs)])
        o_ref[...] = (mix / z).astype(BF)
        L_ref[...] = m + jnp.log(z)

    cur = lambda off: pl.BlockSpec((C, HEAD), lambda h, c: (c, off * H + h))
    prv = lambda off: pl.BlockSpec((C, HEAD), lambda h, c: (jnp.maximum(c - 1, 0), off * H + h))
    tab = pl.BlockSpec((None, 2, DIL_N, 2 * DIL_N), lambda h, c: (h, 0, 0, 0))
    out = pl.BlockSpec((C, HEAD), lambda h, c: (c, h))
    return pl.pallas_call(
        body, name="dil_attn_fwd", grid=(H, S // C),
        in_specs=[cur(0), cur(1), prv(1), cur(2), prv(2)] + [tab] * nbr,
        out_specs=[out, out],
        out_shape=[jax.ShapeDtypeStruct((S, H * HEAD), BF), jax.ShapeDtypeStruct((S, H * HEAD), F32)],
        scratch_shapes=[pltpu.VMEM((C, HEAD), F32), pltpu.VMEM((2 * C, HEAD), F32), pltpu.VMEM((2 * C, HEAD), F32)]
                       + [pltpu.VMEM((C, HEAD), F32)] * (2 * nbr),
        compiler_params=_params(("parallel", "arbitrary")),
    )(qkv, qkv, qkv, qkv, qkv, *biases)


def dil_attn_bwd(qkv, do, L, delta, biases, H):
    S = qkv.shape[0]
    C = DIL_CHUNK
    assert S % C == 0
    nc = S // C
    scale = HEAD ** -0.5
    nbr = len(BRANCH_DILATIONS)

    def body(*refs):
        (qc_ref, qn_ref, kc_ref, kp_ref, vc_ref, vp_ref, doc_ref, don_ref, Lc_ref, Ln_ref, dlc_ref, dln_ref) = refs[:12]
        b_refs = refs[12:12 + nbr]
        (dq_ref, dk_ref, dv_ref, qf, dof, Lf, dlf, kf, vf, dq_acc, dk_acc, dv_acc,
         dq_g, dka_g, dva_g, dkb_g, dvb_g) = refs[12 + nbr:]
        c = pl.program_id(1)
        for buf, a_ref, b_ref in ((qf, qc_ref, qn_ref), (dof, doc_ref, don_ref), (Lf, Lc_ref, Ln_ref),
                                  (dlf, dlc_ref, dln_ref), (kf, kp_ref, kc_ref), (vf, vp_ref, vc_ref)):
            buf[0:C, :] = a_ref[...].astype(F32)
            buf[C:2 * C, :] = b_ref[...].astype(F32)
        for g_idx, (d, b_ref) in enumerate(zip(BRANCH_DILATIONS, b_refs)):
            span = DIL_N * d
            dkb_g[C:2 * C, :] = jnp.zeros((C, HEAD), F32)
            dvb_g[C:2 * C, :] = jnp.zeros((C, HEAD), F32)

            def block(t, carry, d=d, b_ref=b_ref, span=span):
                r, b = t % d, t // d
                q0 = b * span + r
                qrows = _rows(q0, DIL_N, d)
                krows = _rows(C + q0 - span, 2 * DIL_N, d)
                q, dov = qf[qrows, :].astype(BF), dof[qrows, :].astype(BF)
                k2, v2 = kf[krows, :].astype(BF), vf[krows, :].astype(BF)
                Lq, dl = Lf[qrows, :], dlf[qrows, :]
                bias = jnp.where((c == 0) & (b == 0), b_ref[1], b_ref[0])
                s = _dot_nt(q, k2) * scale + bias
                p = jnp.exp(s - jnp.concatenate([Lq, Lq], axis=1))
                ds = p * (_dot_nt(dov, v2) - jnp.concatenate([dl, dl], axis=1))
                dsb, pb = ds.astype(BF), p.astype(BF)
                dq_g[qrows, :] = _dot(dsb, k2) * scale
                dk2 = _dot_tn(dsb, q) * scale
                dv2 = _dot_tn(pb, dov)
                dka_g[qrows, :] = dk2[DIL_N:]
                dva_g[qrows, :] = dv2[DIL_N:]
                prows = _rows(C + q0 - span, DIL_N, d)
                dkb_g[prows, :] = dk2[:DIL_N]
                dvb_g[prows, :] = dv2[:DIL_N]
                return carry

            lax.fori_loop(0, C // DIL_N, block, 0, unroll=DIL_UNROLL)

            @pl.when(c < nc - 1)
            def _(d=d, b_ref=b_ref, span=span):
                def nxt(r, carry):
                    qrows = _rows(C + r, DIL_N, d)
                    krows = _rows(2 * C - span + r, DIL_N, d)
                    q, dov = qf[qrows, :].astype(BF), dof[qrows, :].astype(BF)
                    k1, v1 = kf[krows, :].astype(BF), vf[krows, :].astype(BF)
                    s = _dot_nt(q, k1) * scale + b_ref[0][:, :DIL_N]
                    p = jnp.exp(s - Lf[qrows, :])
                    ds = p * (_dot_nt(dov, v1) - dlf[qrows, :])
                    dkb_g[krows, :] = _dot_tn(ds.astype(BF), q) * scale
                    dvb_g[krows, :] = _dot_tn(p.astype(BF), dov)
                    return carry

                lax.fori_loop(0, d, nxt, 0, unroll=min(d, DIL_UNROLL))

            if g_idx == 0:
                dq_acc[...] = dq_g[...]
                dk_acc[...] = dka_g[...] + dkb_g[C:2 * C, :]
                dv_acc[...] = dva_g[...] + dvb_g[C:2 * C, :]
            else:
                dq_acc[...] += dq_g[...]
                dk_acc[...] += dka_g[...] + dkb_g[C:2 * C, :]
                dv_acc[...] += dva_g[...] + dvb_g[C:2 * C, :]

        dq_ref[...] = dq_acc[...].astype(BF)
        dk_ref[...] = dk_acc[...].astype(BF)
        dv_ref[...] = dv_acc[...].astype(BF)

    cur3 = lambda off: pl.BlockSpec((C, HEAD), lambda h, c: (c, off * H + h))
    prv3 = lambda off: pl.BlockSpec((C, HEAD), lambda h, c: (jnp.maximum(c - 1, 0), off * H + h))
    nxt3 = lambda off: pl.BlockSpec((C, HEAD), lambda h, c: (jnp.minimum(c + 1, nc - 1), off * H + h))
    cur1 = pl.BlockSpec((C, HEAD), lambda h, c: (c, h))
    nxt1 = pl.BlockSpec((C, HEAD), lambda h, c: (jnp.minimum(c + 1, nc - 1), h))
    tab = pl.BlockSpec((None, 2, DIL_N, 2 * DIL_N), lambda h, c: (h, 0, 0, 0))
    return pl.pallas_call(
        body, name="dil_attn_bwd", grid=(H, nc),
        in_specs=[cur3(0), nxt3(0), cur3(1), prv3(1), cur3(2), prv3(2), cur1, nxt1, cur1, nxt1, cur1, nxt1] + [tab] * nbr,
        out_specs=[cur1] * 3,
        out_shape=[jax.ShapeDtypeStruct((S, H * HEAD), BF)] * 3,
        scratch_shapes=[pltpu.VMEM((2 * C, HEAD), F32)] * 6 + [pltpu.VMEM((C, HEAD), F32)] * 6
                       + [pltpu.VMEM((2 * C, HEAD), F32)] * 2,
        compiler_params=_params(("parallel", "arbitrary")),
    )(qkv, qkv, qkv, qkv, qkv, qkv, do, do, L, L, delta, delta, *biases)


def _causal_mask(s, qi, ki, tq, tk, row0=0):
    qpos = qi * tq + row0 + lax.broadcasted_iota(jnp.int32, s.shape, 0)
    kpos = ki * tk + lax.broadcasted_iota(jnp.int32, s.shape, 1)
    return jnp.where(kpos <= qpos, s, NEG)


def mla_fwd(qn, qr, kv, kr, H, *, tq=1024, tk=1024, sub=512):
    S = qn.shape[0]
    tq, tk = _tile(S, tq), _tile(S, tk)
    sub = _tile(tq, sub)
    nk = S // tk
    scale = MLA_SCALE
    c2 = scale * LOG2E

    def body(qn_ref, qr_ref, kn_ref, kr_ref, v_ref, o_ref, L_ref, m_s, l_s, acc):
        qi, ki = pl.program_id(1), pl.program_id(2)

        @pl.when(ki == 0)
        def _():
            m_s[...] = jnp.full_like(m_s, NEG)
            l_s[...] = jnp.zeros_like(l_s)
            acc[...] = jnp.zeros_like(acc)

        def step(masked):
            k = jnp.concatenate([kn_ref[...], kr_ref[...]], axis=1)
            v = v_ref[...]
            for r in range(tq // sub):
                rows = slice(r * sub, (r + 1) * sub)
                q = jnp.concatenate([qn_ref[rows, :], qr_ref[rows, :]], axis=1)
                s = _dot_nt(q, k)
                if masked:
                    s = _causal_mask(s, qi, ki, tq, tk, r * sub)
                m_prev = m_s[rows, :]
                m_new = jnp.maximum(m_prev, jnp.max(s, axis=1, keepdims=True))
                alpha = jnp.exp2((m_prev - m_new) * c2)
                p = jnp.exp2((s - jnp.tile(m_new, (1, tk // LANES))) * c2)
                l_s[rows, :] = alpha * l_s[rows, :] + jnp.sum(p, axis=1, keepdims=True)
                acc[rows, :] = alpha * acc[rows, :] + _dot(p.astype(BF), v)
                m_s[rows, :] = m_new

        full = ki * tk + tk - 1 <= qi * tq
        live = ki * tk <= qi * tq + tq - 1
        pl.when(full)(lambda: step(False))
        pl.when(live & jnp.logical_not(full))(lambda: step(True))

        @pl.when(ki == nk - 1)
        def _():
            o_ref[...] = (acc[...] / l_s[...]).astype(BF)
            L_ref[...] = m_s[...] * scale + jnp.log(l_s[...])

    kcl = lambda qi, ki: jnp.minimum(ki, (qi * tq + tq - 1) // tk)
    qs = pl.BlockSpec((tq, HEAD), lambda h, qi, ki: (qi, h))
    return pl.pallas_call(
        body, name="mla_fwd", grid=(H, S // tq, nk),
        in_specs=[qs, qs, pl.BlockSpec((tk, HEAD), lambda h, qi, ki: (kcl(qi, ki), h)),
                  pl.BlockSpec((tk, LANES), lambda h, qi, ki: (kcl(qi, ki), 0)),
                  pl.BlockSpec((tk, HEAD), lambda h, qi, ki: (kcl(qi, ki), H + h))],
        out_specs=[qs, qs],
        out_shape=[jax.ShapeDtypeStruct((S, H * HEAD), BF), jax.ShapeDtypeStruct((S, H * HEAD), F32)],
        scratch_shapes=[pltpu.VMEM((tq, LANES), F32), pltpu.VMEM((tq, LANES), F32), pltpu.VMEM((tq, HEAD), F32)],
        compiler_params=_params(("parallel", "parallel", "arbitrary")),
    )(qn, qr, kv, kr, kv)


def _mla_probs(s, L_rows, c2, width):
    return jnp.exp2(s * c2 - jnp.tile(L_rows * LOG2E, (1, width // LANES)))


def mla_bwd(qn, qr, kv, kr, do, L, delta, rope, H, prev=None, *, tq=1024, tk=1024, sub=512):
    S = qn.shape[0]
    tq, tk = _tile(S, tq), _tile(S, tk)
    sub = _tile(tq, sub)
    nq, nk = S // tq, S // tk
    scale = MLA_SCALE
    c2 = scale * LOG2E
    has_prev = prev is not None

    def body(*refs):
        (qn_ref, qr_ref, kn_ref, kr_ref, v_ref, do_ref, L_ref, dl_ref, cq_ref, sq_ref, ck_ref, sk_ref) = refs[:12]
        refs = refs[12:]
        if has_prev:
            pk_ref, pv_ref, pr_ref = refs[:3]
            refs = refs[3:]
        dqn_ref, dqr_ref, dk_ref, dv_ref, dr_ref, dq_full, dk_acc, dv_acc = refs
        h, ki, qi = pl.program_id(0), pl.program_id(1), pl.program_id(2)
        qrows = pl.ds(pl.multiple_of(qi * tq, tq), tq)
        krows = pl.ds(pl.multiple_of(ki * tk, tk), tk)

        @pl.when(qi == 0)
        def _():
            dk_acc[...] = jnp.zeros_like(dk_acc)
            dv_acc[...] = jnp.zeros_like(dv_acc)

        @pl.when((qi == 0) & (h == 0))
        def _():
            dr_ref[krows, :] = jnp.zeros((tk, LANES), F32)

        @pl.when(ki == 0)
        def _():
            dq_full[qrows, :] = jnp.zeros((tq, 2 * HEAD), F32)

        def step(masked):
            k = jnp.concatenate([kn_ref[...], kr_ref[...]], axis=1)
            v = v_ref[...]
            for r in range(tq // sub):
                rows = slice(r * sub, (r + 1) * sub)
                q = jnp.concatenate([qn_ref[rows, :], qr_ref[rows, :]], axis=1)
                s = _dot_nt(q, k)
                if masked:
                    s = _causal_mask(s, qi, ki, tq, tk, r * sub)
                p = _mla_probs(s, L_ref[rows, :], c2, tk)
                dov = do_ref[rows, :]
                dv_acc[...] += _dot_tn(p.astype(BF), dov)
                dp = _dot_nt(dov, v)
                ds = (p * (dp - jnp.tile(dl_ref[rows, :], (1, tk // LANES)))).astype(BF)
                dk_acc[...] += _dot_tn(ds, q)
                dq_full[pl.ds(pl.multiple_of(qi * tq + r * sub, sub), sub), :] += _dot(ds, k)

        full = ki * tk + tk - 1 <= qi * tq
        live = ki * tk <= qi * tq + tq - 1
        pl.when(full)(lambda: step(False))
        pl.when(live & jnp.logical_not(full))(lambda: step(True))

        @pl.when(ki == (qi * tq + tq - 1) // tk)
        def _():
            dq = dq_full[qrows, :] * scale
            dqn_ref[qrows, :] = dq[:, :HEAD].astype(BF)
            dqr_ref[qrows, :] = _rope_bwd(dq[:, HEAD:], cq_ref[...], sq_ref[...]).astype(BF)

        @pl.when(qi == nq - 1)
        def _():
            dk = dk_acc[...] * scale
            dkn, dv = dk[:, :HEAD], dv_acc[...]
            if has_prev:
                dkn, dv = dkn + pk_ref[...], dv + pv_ref[...]
            dk_ref[...] = dkn
            dv_ref[...] = dv
            dr_ref[krows, :] += dk[:, HEAD:]

        @pl.when((qi == nq - 1) & (h == H - 1))
        def _():
            dr = _rope_bwd(dr_ref[krows, :], ck_ref[...], sk_ref[...])
            dr_ref[krows, :] = dr + pr_ref[...] if has_prev else dr

    qcl = lambda ki, qi: jnp.maximum(qi, (ki * tk) // tq)
    qs = pl.BlockSpec((tq, HEAD), lambda h, ki, qi: (qcl(ki, qi), h))
    qt = pl.BlockSpec((tq, LANES), lambda h, ki, qi: (qcl(ki, qi), 0))
    kn = pl.BlockSpec((tk, HEAD), lambda h, ki, qi: (ki, h))
    vs = pl.BlockSpec((tk, HEAD), lambda h, ki, qi: (ki, H + h))
    k1 = pl.BlockSpec((tk, LANES), lambda h, ki, qi: (ki, 0))
    head = pl.BlockSpec((S, HEAD), lambda h, ki, qi: (0, h))
    in_specs = [qs, qs, kn, k1, vs, qs, qs, qs, qt, qt, k1, k1]
    args = [qn, qr, kv, kr, kv, do, L, delta, *rope, *rope]
    if has_prev:
        in_specs += [kn, vs, k1]
        args += [prev[0], prev[0], prev[1]]
    return pl.pallas_call(
        body, name="mla_bwd", grid=(H, nk, nq), in_specs=in_specs,
        out_specs=[head, head, kn, kn, pl.BlockSpec((S, LANES), lambda h, ki, qi: (0, 0))],
        out_shape=[jax.ShapeDtypeStruct((S, H * HEAD), BF), jax.ShapeDtypeStruct((S, H * HEAD), BF),
                   jax.ShapeDtypeStruct((S, H * HEAD), F32), jax.ShapeDtypeStruct((S, H * HEAD), F32),
                   jax.ShapeDtypeStruct((S, LANES), F32)],
        scratch_shapes=[pltpu.VMEM((S, 2 * HEAD), F32), pltpu.VMEM((tk, 2 * HEAD), F32), pltpu.VMEM((tk, HEAD), F32)],
        compiler_params=_params(("arbitrary", "arbitrary", "arbitrary")),
    )(*args)


def adamw(recvs, w, m, v, *, name="adamw"):
    R, C = w.shape
    n = len(recvs)
    R1 = R // n
    assert all(r.shape == (N_DEV, R1, C) for r in recvs)
    tr = 16
    while tr * 2 * C * 44 <= 6 * 1024 * 1024 and R1 % (tr * 2) == 0:
        tr *= 2
    tr = min(tr, R1)
    assert R1 % tr == 0
    nb = R1 // tr

    def body(*refs):
        r_refs = refs[:n]
        w_ref, m_ref, v_ref, g_ref, d_ref, mo_ref, vo_ref = refs[n:]
        for slab in range(n):
            @pl.when(pl.program_id(0) == slab)
            def _(r_ref=r_refs[slab]):
                g = r_ref[0].astype(F32)
                for s in range(1, N_DEV):
                    g = g + r_ref[s].astype(F32)
                m2 = ADAM_B1 * m_ref[...] + (1.0 - ADAM_B1) * g
                v2 = ADAM_B2 * v_ref[...] + (1.0 - ADAM_B2) * (g * g)
                m_hat = m2 / (1.0 - ADAM_B1 ** ADAM_STEP)
                v_hat = v2 / (1.0 - ADAM_B2 ** ADAM_STEP)
                g_ref[...] = g
                d_ref[...] = -ADAM_LR * (m_hat / (jnp.sqrt(v_hat) + ADAM_EPS) + ADAM_WD * w_ref[...])
                mo_ref[...] = m2
                vo_ref[...] = v2

    def recv_spec(slab):
        return pl.BlockSpec((N_DEV, tr, C), lambda l, i: (0, jnp.where(l == slab, i, jnp.where(l < slab, 0, nb - 1)), 0))

    row = pl.BlockSpec((tr, C), lambda l, i: (l * nb + i, 0))
    return pl.pallas_call(
        body, name=name, grid=(n, nb),
        in_specs=[recv_spec(slab) for slab in range(n)] + [row, row, row],
        out_specs=[row] * 4, out_shape=[jax.ShapeDtypeStruct((R, C), F32)] * 4,
        compiler_params=_params(("arbitrary", "arbitrary")),
    )(*recvs, w, m, v)


def _pack(ts, width, dtype, row_mult=16):
    flat = jnp.concatenate([t.astype(dtype).reshape(-1) for t in ts])
    n = flat.shape[0]
    rows = -(-n // (width * row_mult)) * row_mult
    return jnp.pad(flat, (0, rows * width - n)).reshape(rows, width)


def _unpack(buf, shapes):
    lead = buf.shape[:-2]
    flat = buf.reshape(lead + (-1,))
    out, off = [], 0
    for s in shapes:
        n = int(np.prod(s))
        out.append(flat[..., off:off + n].reshape(lead + tuple(s)))
        off += n
    return out


def _rope_pad(r):
    z = jnp.zeros(r.shape[:-1] + (ROPE_HALF,), r.dtype)
    return jnp.concatenate([r[..., :ROPE_HALF], z, r[..., ROPE_HALF:], z], axis=-1)


def _rope_unpad(r):
    return jnp.concatenate([r[..., :ROPE_HALF], r[..., 2 * ROPE_HALF:3 * ROPE_HALF]], axis=-1)


def _step(P):
    x0 = P['x'][0]
    target = P['loss_target'][0]
    S, D = x0.shape
    NL = P['ffn_norm1'].shape[0]
    NA = P['a_wqkv'].shape[0]
    Fs = P['ffn1_wg'].shape[2]
    H = D // HEAD
    KV = P['b_wdkv'].shape[1]
    QL = P['b_wdq'].shape[2]
    HW = H * HEAD

    ffn_seq = [(l, f) for l in range(NL) for f in (0, 1)]
    ffn_loc = {(l, f): [P[n][l].astype(BF) for n in (f'ffn{f + 1}_wg', f'ffn{f + 1}_wu', f'ffn{f + 1}_wd')]
               for l, f in ffn_seq}
    mix_shapes = [P[n].shape for n in MIX_W]
    wc_loc = _pack([P[n] for n in MIX_W], D, BF)
    ffn_w = {ffn_seq[0]: all_gather(ffn_loc[ffn_seq[0]])}

    def ffn_forward(x, gain, key, extra=()):
        nxt = ffn_seq.index(key) + 1
        carry = (ffn_loc[ffn_seq[nxt]] if nxt < len(ffn_seq) else []) + list(extra)
        xo, h, G, U, *got = ffn_fwd(x, gain, *ffn_w[key], carry=carry)
        if nxt < len(ffn_seq):
            ffn_w[ffn_seq[nxt]] = got[:3]
        return (xo, h, G, U), got[len(got) - len(extra):]

    first_half, (WC,) = ffn_forward(x0, P['ffn_norm1'][0], ffn_seq[0], extra=[wc_loc])
    g = dict(zip(MIX_W, _unpack(WC, mix_shapes)))
    cols = lambda t: jnp.moveaxis(t, 0, -2).reshape(t.shape[1:-1] + (N_DEV * t.shape[-1],))
    rows = lambda t, lead: jnp.moveaxis(t, 0, lead).reshape(t.shape[1:1 + lead] + (N_DEV * t.shape[1 + lead],) + t.shape[2 + lead:])
    a_wqkv = cols(g['a_wqkv'])
    a_wo = rows(g['a_wo'], 1)
    b_wo = rows(g['b_wo'], 1)
    wdkv = rows(g['b_wdkv'], 0)
    wkr = _rope_pad(rows(g['b_wkr'], 0))
    wuk = rows(g['b_wuk'], 0).reshape(KV, HW)
    wuv = rows(g['b_wuv'], 0).reshape(KV, HW)
    wkv = jnp.concatenate([wuk, wuv], axis=1)
    wdq = rows(g['b_wdq'], 1)
    wuq = rows(g['b_wuq'], 1)
    wuq_n = wuq[..., :HEAD].reshape(-1, QL, HW)
    wuq_r = _rope_pad(wuq[..., HEAD:]).reshape(-1, QL, HW)

    inv = 1.0 / (ROPE_THETA ** (jnp.arange(0, 2 * ROPE_HALF, 2, dtype=F32) / (2 * ROPE_HALF)))
    ang = jnp.arange(S, dtype=F32)[:, None] * inv[None, :]
    z = jnp.zeros((S, ROPE_HALF), F32)
    rope = (jnp.concatenate([jnp.cos(ang), z, jnp.cos(ang), z], axis=1),
            jnp.concatenate([-jnp.sin(ang), z, jnp.sin(ang), z], axis=1))
    biases = [dilated_bias(H, d) for d in BRANCH_DILATIONS]

    saved = []
    kvs = None
    x = x0
    for l in range(NL):
        st = {'x_in': x}
        if l == NA:
            ckv_pre, hkv = norm_mm(x, P['kv_norm'], wdkv, out_dtype=F32, name="kv_down")
            kr = norm_mm(x, P['kv_norm'], wkr, out_dtype=BF, tn=LANES, rope=rope, write_h=False, name="kv_rope")
            kvm, ckv = norm_mm(ckv_pre, P['b_ckv_norm'], wkv, out_dtype=BF, name="kv_up")
            kvs = dict(x=x, ckv_pre=ckv_pre, hkv=hkv, kr=kr, kv=kvm, ckv=ckv)
        xa, h1, G1, U1 = first_half if l == 0 else ffn_forward(x, P['ffn_norm1'][l], (l, 0))[0]
        st.update(h1=h1, G1=G1, U1=U1, xa=xa)
        if l < NA:
            qkv, hm = norm_mm(xa, P['mix_norm'][l], a_wqkv[l], out_dtype=BF, tn=3 * HW // 4, name="a_qkv")
            o, Lj = dil_attn_fwd(qkv, biases, H)
            xb = mm(o, a_wo[l], add=xa, name="mix_out")
            st.update(qkv=qkv, hm=hm, o=o, L=Lj)
        else:
            jb = l - NA
            cq_pre, hm = norm_mm(xa, P['mix_norm'][l], wdq[jb], out_dtype=F32, name="q_down")
            qn, cq = norm_mm(cq_pre, P['b_cq_norm'][jb], wuq_n[jb], out_dtype=BF, name="q_up")
            qr = norm_mm(cq_pre, P['b_cq_norm'][jb], wuq_r[jb], out_dtype=BF, tn=LANES, rope=rope, write_h=False,
                         name="q_rope")
            o, Lj = mla_fwd(qn, qr, kvs['kv'], kvs['kr'], H)
            xb = mm(o, b_wo[jb], add=xa, name="mix_out")
            st.update(cq_pre=cq_pre, hm=hm, qn=qn, qr=qr, cq=cq, o=o, L=Lj)
        x, h2, G2, U2 = ffn_forward(xb, P['ffn_norm2'][l], (l, 1))[0]
        st.update(xb=xb, h2=h2, G2=G2, U2=U2)
        saved.append(st)

    loss_part, dx, dg_final = loss_head(x, P['final_norm'], target)

    gw = {}
    gv = {'final_norm': dg_final}
    dkv_acc = None
    for n in ('ffn_norm1', 'mix_norm', 'ffn_norm2'):
        gv[n] = [None] * NL
    gv['b_cq_norm'] = [None] * (NL - NA)
    ffn_r = {}
    pending = []
    a_g = {'a_wqkv': [None] * NA, 'a_wo': [None] * NA}
    b_g = {n: [None] * (NL - NA) for n in ('b_wdq', 'b_wuq', 'b_wo')}

    def ffn_backward(dy, x_in, gain, key, h, G, U, w_carry=()):
        sent, carry = pending.pop() if pending else (None, [])
        dxi, dout, dG, dU, dgain, got = ffn_bwd_x(dy, x_in, gain, G, U, *ffn_w[key], carry=carry)
        if carry:
            ffn_r[sent] = got
        dws, got_w = ffn_bwd_w(h, dout, G, U, dG, dU, carry=w_carry)
        pending.append((key, dws))
        return dxi, dgain, got_w

    split_cols = lambda t: jnp.moveaxis(t.reshape(t.shape[:-1] + (N_DEV, t.shape[-1] // N_DEV)), -2, 0)
    split_rows = lambda t, lead: jnp.moveaxis(
        t.reshape(t.shape[:lead] + (N_DEV, t.shape[lead] // N_DEV) + t.shape[lead + 1:]), lead, 0)

    def pack_mix_grads():
        for n in a_g:
            gw[n] = jnp.stack(a_g[n])
        for n in b_g:
            gw[n] = jnp.stack(b_g[n])
        shards = {
            'a_wqkv': split_cols(gw['a_wqkv']), 'a_wo': split_rows(gw['a_wo'], 1), 'b_wdkv': split_rows(gw['b_wdkv'], 0),
            'b_wkr': split_rows(gw['b_wkr'], 0), 'b_wuk': split_rows(gw['b_wuk'], 0), 'b_wuv': split_rows(gw['b_wuv'], 0),
            'b_wdq': split_rows(gw['b_wdq'], 1), 'b_wuq': split_rows(gw['b_wuq'], 1), 'b_wo': split_rows(gw['b_wo'], 1),
        }
        flat = jnp.concatenate([shards[n].astype(BF).reshape(N_DEV, -1) for n in MIX_W], axis=1)
        rows = wc_loc.shape[0]
        return jnp.pad(flat, ((0, 0), (0, rows * D - flat.shape[1]))).reshape(N_DEV, rows, D)

    assert 1 <= NA < NL
    rc = None
    for l in reversed(range(NL)):
        st = saved[l]
        dxb, gv['ffn_norm2'][l], _ = ffn_backward(dx, st['xb'], P['ffn_norm2'][l], (l, 1), st['h2'], st['G2'], st['U2'])
        xa = st['xa']
        if l < NA:
            do = mm(dxb, a_wo[l], tb=True, out_dtype=BF, name="mix_out_dx")
            a_g['a_wo'][l] = mm(st['o'], dxb, ta=True, out_dtype=BF, tm=1024, tn=1024, name="mix_out_dw")
            delta = head_delta(do, st['o'])
            dqkv = jnp.concatenate(dil_attn_bwd(st['qkv'], do, st['L'], delta, biases, H), axis=1)
            dh = mm(dqkv, a_wqkv[l], tb=True, name="a_qkv_dx")
            a_g['a_wqkv'][l] = mm(st['hm'], dqkv, ta=True, out_dtype=BF, tm=1024, tn=1024, name="a_qkv_dw")
            dxa, gv['mix_norm'][l] = norm_bwd(xa, P['mix_norm'][l], dh, dxb, name="mix_norm_bwd")
        else:
            jb = l - NA
            do = mm(dxb, b_wo[jb], tb=True, out_dtype=BF, name="mix_out_dx")
            b_g['b_wo'][jb] = mm(st['o'], dxb, ta=True, out_dtype=BF, tm=1024, tn=1024, name="mix_out_dw")
            delta = head_delta(do, st['o'])
            dqn, dqr, dkn, dv, dr = mla_bwd(st['qn'], st['qr'], kvs['kv'], kvs['kr'], do, st['L'], delta, rope, H,
                                            prev=dkv_acc)
            dkv_acc = (jnp.concatenate([dkn, dv], axis=1), dr)
            dcq = mm(dqn, wuq_n[jb], tb=True, name="q_up_dx")
            dcq = mm(dqr, wuq_r[jb], tb=True, add=dcq, name="q_up_dx_add")
            dwn = mm(st['cq'], dqn, ta=True, out_dtype=F32, name="q_up_dw")
            dwr = mm(st['cq'], dqr, ta=True, out_dtype=F32, name="q_up_dw")
            b_g['b_wuq'][jb] = jnp.concatenate(
                [dwn.reshape(QL, H, HEAD), _rope_unpad(dwr.reshape(QL, H, HEAD))], axis=-1)
            dcq_pre, gv['b_cq_norm'][jb] = norm_bwd(st['cq_pre'], P['b_cq_norm'][jb], dcq, name="cq_norm_bwd")
            dh = mm(dcq_pre, wdq[jb], tb=True, name="q_down_dx")
            b_g['b_wdq'][jb] = mm(st['hm'], dcq_pre, ta=True, out_dtype=F32, tm=1024, name="q_down_dw")
            dxa, gv['mix_norm'][l] = norm_bwd(xa, P['mix_norm'][l], dh, dxb, name="mix_norm_bwd")
        w_carry = [pack_mix_grads()] if l == 0 else []
        dx, gv['ffn_norm1'][l], got_w = ffn_backward(dxa, st['x_in'], P['ffn_norm1'][l], (l, 0), st['h1'], st['G1'],
                                                     st['U1'], w_carry)
        if w_carry:
            rc = got_w[0]
        if l == NA:
            dkvm, dr = dkv_acc
            dckv = mm(dkvm, wkv, tb=True, name="kv_up_dx")
            dwkv = mm(kvs['ckv'], dkvm, ta=True, out_dtype=F32, name="kv_up_dw")
            gw['b_wuk'] = dwkv[:, :HW].reshape(KV, H, HEAD)
            gw['b_wuv'] = dwkv[:, HW:].reshape(KV, H, HEAD)
            dckv_pre, gv['b_ckv_norm'] = norm_bwd(kvs['ckv_pre'], P['b_ckv_norm'], dckv, name="ckv_norm_bwd")
            dh = mm(dckv_pre, wdkv, tb=True, name="kv_down_dx")
            dh = mm(dr, wkr, tb=True, add=dh, name="kv_rope_dx")
            gw['b_wdkv'] = mm(kvs['hkv'], dckv_pre, ta=True, out_dtype=F32, tm=1024, name="kv_down_dw")
            gw['b_wkr'] = _rope_unpad(mm(kvs['hkv'], dr, ta=True, out_dtype=F32, tm=1024, name="kv_rope_dw"))
            dx, gv['kv_norm'] = norm_bwd(kvs['x'], P['kv_norm'], dh, dx, name="kv_norm_bwd")

    for n in ('ffn_norm1', 'mix_norm', 'ffn_norm2', 'b_cq_norm'):
        gv[n] = jnp.stack(gv[n])

    vec_parts = [gv[n] for n in VEC_W] + [jnp.full((LANES,), loss_part, F32)]
    gvec = _pack(vec_parts, LANES, F32, row_mult=8)
    last_key, last = pending.pop()
    *got, rv = exchange(last + [gvec], [False] * len(last) + [True])
    ffn_r[last_key] = got

    res = {}
    for f in (0, 1):
        for which, kind in enumerate(('wg', 'wu', 'wd')):
            n = f'ffn{f + 1}_{kind}'
            recvs = [ffn_r[(l, f)][which] for l in range(NL)]
            flat = lambda t: t.reshape((NL * t.shape[1], t.shape[2]))
            out = adamw(recvs, flat(P[n]), flat(P['m_' + n]), flat(P['v_' + n]),
                        name="adamw_row" if kind == 'wd' else "adamw_col")
            res[n] = [t.reshape(P[n].shape) for t in out]
    pk = lambda pre: _pack([P[pre + n] for n in MIX_W], D, F32)
    out = adamw([rc], pk(''), pk('m_'), pk('v_'), name="adamw_mix")
    for n, parts in zip(MIX_W, zip(*[_unpack(t, mix_shapes) for t in out])):
        res[n] = list(parts)
    vec_shapes = [P[n].shape for n in VEC_W] + [(LANES,)]
    ones = jnp.ones((LANES,), F32)
    pv = lambda pre: _pack([P[pre + n] for n in VEC_W] + [ones], LANES, F32, row_mult=8)
    out = adamw([rv], pv(''), pv('m_'), pv('v_'), name="adamw_vec")
    unp = [_unpack(t, vec_shapes) for t in out]
    for idx, n in enumerate(VEC_W):
        res[n] = [u[idx] for u in unp]
    loss = unp[0][-1][0]

    outs = [loss, dx[None]]
    for field in range(4):
        outs += [res[n][field] for n in W_NAMES]
    return tuple(outs)


def kernel(x, ffn_norm1, ffn1_wg, ffn1_wu, ffn1_wd, mix_norm, ffn_norm2, ffn2_wg, ffn2_wu, ffn2_wd, a_wqkv, a_wo, kv_norm, b_wdkv, b_ckv_norm, b_wkr, b_wuk, b_wuv, b_wdq, b_cq_norm, b_wuq, b_wo, final_norm, loss_target, m_ffn_norm1, m_ffn1_wg, m_ffn1_wu, m_ffn1_wd, m_mix_norm, m_ffn_norm2, m_ffn2_wg, m_ffn2_wu, m_ffn2_wd, m_a_wqkv, m_a_wo, m_kv_norm, m_b_wdkv, m_b_ckv_norm, m_b_wkr, m_b_wuk, m_b_wuv, m_b_wdq, m_b_cq_norm, m_b_wuq, m_b_wo, m_final_norm, v_ffn_norm1, v_ffn1_wg, v_ffn1_wu, v_ffn1_wd, v_mix_norm, v_ffn_norm2, v_ffn2_wg, v_ffn2_wu, v_ffn2_wd, v_a_wqkv, v_a_wo, v_kv_norm, v_b_wdkv, v_b_ckv_norm, v_b_wkr, v_b_wuk, v_b_wuv, v_b_wdq, v_b_cq_norm, v_b_wuq, v_b_wo, v_final_norm):
    return _step(dict(locals()))
```

```python
import functools
import math

import numpy as np
import jax
import jax.numpy as jnp
from jax import lax
from jax.experimental import pallas as pl
from jax.experimental.pallas import tpu as pltpu

BF = jnp.bfloat16
F32 = jnp.float32
MESH = pl.DeviceIdType.MESH
ANY = pl.BlockSpec(memory_space=pl.ANY)

N_DEV = 8
LANES = 128
HEAD = 128
ROPE_HALF = 32
DIL_N = 128
BRANCH_DILATIONS = (1, 4, 16)
ROPE_THETA = 10000.0
MLA_SCALE = (HEAD + 2 * ROPE_HALF) ** -0.5
LOG2E = math.log2(math.e)
EPS = 1e-6
NEG = -1e30
VMEM_LIMIT = 58 * 1024 * 1024

ADAM_LR, ADAM_B1, ADAM_B2, ADAM_EPS, ADAM_WD, ADAM_STEP = 0.001, 0.9, 0.999, 1e-08, 0.01, 10

W_NAMES = ['ffn_norm1', 'ffn1_wg', 'ffn1_wu', 'ffn1_wd', 'mix_norm', 'ffn_norm2', 'ffn2_wg', 'ffn2_wu', 'ffn2_wd',
           'a_wqkv', 'a_wo', 'kv_norm', 'b_wdkv', 'b_ckv_norm', 'b_wkr', 'b_wuk', 'b_wuv', 'b_wdq', 'b_cq_norm',
           'b_wuq', 'b_wo', 'final_norm']
MIX_W = ['a_wqkv', 'a_wo', 'b_wdkv', 'b_wkr', 'b_wuk', 'b_wuv', 'b_wdq', 'b_wuq', 'b_wo']
VEC_W = ['ffn_norm1', 'mix_norm', 'ffn_norm2', 'kv_norm', 'b_ckv_norm', 'b_cq_norm', 'final_norm']


def _tile(n, pref):
    t = min(n, pref)
    assert n % t == 0, (n, pref)
    return t


def _params(sem):
    return pltpu.CompilerParams(dimension_semantics=sem, vmem_limit_bytes=VMEM_LIMIT)


def _dot(a, b):
    return jnp.dot(a, b, preferred_element_type=F32)


def _dot_nt(a, b):
    return lax.dot_general(a, b, (((1,), (1,)), ((), ())), preferred_element_type=F32)


def _dot_tn(a, b):
    return lax.dot_general(a, b, (((0,), (0,)), ((), ())), preferred_element_type=F32)


def _rms(x, g):
    r = lax.rsqrt(jnp.mean(x * x, axis=-1, keepdims=True) + EPS)
    return x * r * g, r


def _rms_bwd(x, g, dh):
    r = lax.rsqrt(jnp.mean(x * x, axis=-1, keepdims=True) + EPS)
    xhat = x * r
    gd = dh * g
    dx = r * (gd - xhat * jnp.mean(gd * xhat, axis=-1, keepdims=True))
    return dx, jnp.sum(dh * xhat, axis=0, keepdims=True)


def _rope(t, cos, sin):
    return t * cos + pltpu.roll(t, 2 * ROPE_HALF, 1) * sin


def _rope_bwd(g, cos, sin):
    return g * cos + pltpu.roll(g * sin, 2 * ROPE_HALF, 1)


def _coords():
    return lax.axis_index("x"), lax.axis_index("y"), lax.axis_index("c")


def _lin(px, py, pc):
    return 4 * px + 2 * py + pc


def _gather_phases(ins, outs, ssem, rsem, lsem):
    n = len(ins)
    x, y, c = _coords()
    me, sibling = (x, y, c), (x, y, 1 - c)
    chips = [(1 - x, y), (x, 1 - y), (1 - x, 1 - y)]

    def copy(t, k, block, to, src=None):
        slot = outs[t].at[_lin(*block)]
        return pltpu.make_async_remote_copy(
            src_ref=slot if src is None else src, dst_ref=slot,
            send_sem=ssem.at[7 * t + k], recv_sem=rsem.at[7 * t + k],
            device_id=to, device_id_type=MESH)

    def mine(t):
        return pltpu.make_async_copy(ins[t], outs[t].at[_lin(*me)], lsem.at[t])

    def first(t):
        return [copy(t, 0, me, sibling, src=ins[t])] + [copy(t, 1 + j, me, (*chip, c), src=ins[t])
                                                       for j, chip in enumerate(chips)]

    def passed(t):
        return [copy(t, 4 + j, (*chip, c), sibling) for j, chip in enumerate(chips)]

    def start():
        for t in range(n):
            mine(t).start()
            for cp in first(t):
                cp.start()

    def middle():
        for t in range(n):
            fw = passed(t)
            for j, chip in enumerate(chips):
                copy(t, 1 + j, (*chip, c), me).wait_recv()
                fw[j].start()

    def finish():
        for t in range(n):
            copy(t, 0, sibling, me).wait_recv()
            for j, chip in enumerate(chips):
                copy(t, 4 + j, (*chip, 1 - c), me).wait_recv()
        for t in range(n):
            for cp in first(t) + passed(t):
                cp.wait_send()
            mine(t).wait()

    return start, middle, finish


def _exchange_phases(ins, outs, bcast, ssem, rsem, lsem):
    n = len(ins)
    x, y, c = _coords()
    me = _lin(x, y, c)
    peers = []
    for k in range(1, N_DEV):
        kx, ky, kc = (k >> 2) & 1, (k >> 1) & 1, k & 1
        peers.append((k, (1 - x if kx else x, 1 - y if ky else y, 1 - c if kc else c)))

    def own(t):
        return pltpu.make_async_copy(ins[t] if bcast[t] else ins[t].at[me], outs[t].at[me], lsem.at[t])

    def send(t, k, peer):
        return pltpu.make_async_remote_copy(
            src_ref=ins[t] if bcast[t] else ins[t].at[_lin(*peer)], dst_ref=outs[t].at[me],
            send_sem=ssem.at[7 * t + k - 1], recv_sem=rsem.at[7 * t + k - 1],
            device_id=peer, device_id_type=MESH)

    def arrival(t, k, peer):
        slot = outs[t].at[_lin(*peer)]
        return pltpu.make_async_remote_copy(
            src_ref=slot, dst_ref=slot, send_sem=ssem.at[7 * t + k - 1], recv_sem=rsem.at[7 * t + k - 1],
            device_id=peer, device_id_type=MESH)

    def start():
        for t in range(n):
            own(t).start()
            for k, peer in peers:
                send(t, k, peer).start()

    def finish():
        for t in range(n):
            for k, peer in peers:
                arrival(t, k, peer).wait_recv()
        for t in range(n):
            for k, peer in peers:
                send(t, k, peer).wait_send()
            own(t).wait()

    return start, finish


def _comm_sems(n):
    return [pltpu.SemaphoreType.DMA((7 * n,)), pltpu.SemaphoreType.DMA((7 * n,)), pltpu.SemaphoreType.DMA((n,))]


def _gathered(xs):
    return [jax.ShapeDtypeStruct((N_DEV,) + a.shape, a.dtype) for a in xs]


def all_gather(xs):
    n = len(xs)

    def body(*refs):
        start, middle, finish = _gather_phases(refs[:n], refs[n:2 * n], *refs[2 * n:])
        start()
        middle()
        finish()

    return pl.pallas_call(
        body, name="all_gather", out_shape=_gathered(xs),
        in_specs=[ANY] * n, out_specs=[ANY] * n, scratch_shapes=_comm_sems(n),
    )(*xs)


def exchange(xs, bcast):
    n = len(xs)

    def body(*refs):
        start, finish = _exchange_phases(refs[:n], refs[n:2 * n], bcast, *refs[2 * n:])
        start()
        finish()

    out_shape = [jax.ShapeDtypeStruct(((N_DEV,) + a.shape) if b else a.shape, a.dtype) for a, b in zip(xs, bcast)]
    return pl.pallas_call(
        body, name="grad_exchange", out_shape=out_shape,
        in_specs=[ANY] * n, out_specs=[ANY] * n, scratch_shapes=_comm_sems(n),
    )(*xs)


def mm(a, b, *, ta=False, tb=False, add=None, out_dtype=F32, tm=512, tn=1024, tk=None, name="mm"):
    K, M = a.shape if ta else a.shape[::-1]
    N = b.shape[0] if tb else b.shape[1]
    assert (b.shape[1] if tb else b.shape[0]) == K and not (ta and tb)
    if tk is None:
        tk = 512 if ta else 2048
    tm, tn, tk = _tile(M, tm), _tile(N, tn), _tile(K, tk)
    nk = K // tk
    has_add = add is not None

    def body(*refs):
        if has_add:
            a_ref, b_ref, add_ref, o_ref, acc = refs
        else:
            a_ref, b_ref, o_ref, acc = refs
        k = pl.program_id(2)

        @pl.when(k == 0)
        def _():
            acc[...] = jnp.zeros_like(acc)

        av = a_ref[...].astype(BF)
        bv = b_ref[...].astype(BF)
        if ta:
            acc[...] += _dot_tn(av, bv)
        elif tb:
            acc[...] += _dot_nt(av, bv)
        else:
            acc[...] += _dot(av, bv)

        @pl.when(k == nk - 1)
        def _():
            r = acc[...]
            if has_add:
                r = r + add_ref[...]
            o_ref[...] = r.astype(out_dtype)

    a_spec = pl.BlockSpec((tk, tm), lambda i, j, k: (k, i)) if ta else pl.BlockSpec((tm, tk), lambda i, j, k: (i, k))
    b_spec = pl.BlockSpec((tn, tk), lambda i, j, k: (j, k)) if tb else pl.BlockSpec((tk, tn), lambda i, j, k: (k, j))
    o_spec = pl.BlockSpec((tm, tn), lambda i, j, k: (i, j))
    in_specs = [a_spec, b_spec] + ([o_spec] if has_add else [])
    args = [a, b] + ([add] if has_add else [])
    return pl.pallas_call(
        body, name=name, grid=(M // tm, N // tn, nk),
        in_specs=in_specs, out_specs=o_spec,
        out_shape=jax.ShapeDtypeStruct((M, N), out_dtype),
        scratch_shapes=[pltpu.VMEM((tm, tn), F32)],
        compiler_params=_params(("parallel", "parallel", "arbitrary")),
    )(*args)


def norm_mm(x, gain, w, *, out_dtype, tn=512, rope=None, write_h=True, tm=512, name="norm_mm"):
    S, K = x.shape
    N = w.shape[1]
    tm, tn = _tile(S, tm), _tile(N, tn)
    if rope is not None:
        assert tn == LANES
    gain = gain.reshape(1, K)

    def body(*refs):
        refs = list(refs)
        x_ref, g_ref, w_ref = refs[:3]
        refs = refs[3:]
        if rope is not None:
            cos_ref, sin_ref = refs[:2]
            refs = refs[2:]
        y_ref = refs[0]
        h_ref = refs[1] if write_h else None
        hs = refs[-1]
        j = pl.program_id(1)

        @pl.when(j == 0)
        def _():
            h, _ = _rms(x_ref[...], g_ref[...])
            hb = h.astype(BF)
            hs[...] = hb
            if write_h:
                h_ref[...] = hb

        y = _dot(hs[...], w_ref[...])
        if rope is not None:
            y = _rope(y, cos_ref[...], sin_ref[...])
        y_ref[...] = y.astype(out_dtype)

    in_specs = [pl.BlockSpec((tm, K), lambda i, j: (i, 0)), pl.BlockSpec((1, K), lambda i, j: (0, 0)),
                pl.BlockSpec((K, tn), lambda i, j: (0, j))]
    args = [x, gain, w]
    if rope is not None:
        in_specs += [pl.BlockSpec((tm, LANES), lambda i, j: (i, 0))] * 2
        args += list(rope)
    out_specs = [pl.BlockSpec((tm, tn), lambda i, j: (i, j))]
    out_shape = [jax.ShapeDtypeStruct((S, N), out_dtype)]
    if write_h:
        out_specs.append(pl.BlockSpec((tm, K), lambda i, j: (i, 0)))
        out_shape.append(jax.ShapeDtypeStruct((S, K), BF))
    res = pl.pallas_call(
        body, name=name, grid=(S // tm, N // tn), in_specs=in_specs, out_specs=out_specs, out_shape=out_shape,
        scratch_shapes=[pltpu.VMEM((tm, K), BF)],
        compiler_params=_params(("parallel", "arbitrary")),
    )(*args)
    return res if write_h else res[0]


def norm_bwd(x, gain, dh, dres=None, *, tm=512, name="norm_bwd"):
    S, K = x.shape
    tm = _tile(S, tm)
    gain = gain.reshape(1, K)
    has_res = dres is not None

    def body(*refs):
        if has_res:
            x_ref, g_ref, dh_ref, dr_ref, dx_ref, dg_ref = refs
        else:
            x_ref, g_ref, dh_ref, dx_ref, dg_ref = refs

        @pl.when(pl.program_id(0) == 0)
        def _():
            dg_ref[...] = jnp.zeros_like(dg_ref)

        dx, dg = _rms_bwd(x_ref[...], g_ref[...], dh_ref[...].astype(F32))
        if has_res:
            dx = dx + dr_ref[...]
        dx_ref[...] = dx
        dg_ref[...] += jnp.broadcast_to(dg, dg_ref.shape)

    row = pl.BlockSpec((tm, K), lambda i: (i, 0))
    in_specs = [row, pl.BlockSpec((1, K), lambda i: (0, 0)), row] + ([row] if has_res else [])
    args = [x, gain, dh] + ([dres] if has_res else [])
    dx, dg = pl.pallas_call(
        body, name=name, grid=(S // tm,), in_specs=in_specs,
        out_specs=[row, pl.BlockSpec((8, K), lambda i: (0, 0))],
        out_shape=[jax.ShapeDtypeStruct((S, K), F32), jax.ShapeDtypeStruct((8, K), F32)],
        compiler_params=_params(("arbitrary",)),
    )(*args)
    return dx, dg[0]


def loss_head(x, gain, target, *, tm=512):
    S, K = x.shape
    tm = _tile(S, tm)
    gain = gain.reshape(1, K)

    def body(x_ref, g_ref, t_ref, dx_ref, dg_ref, ls_ref):
        @pl.when(pl.program_id(0) == 0)
        def _():
            dg_ref[...] = jnp.zeros_like(dg_ref)
            ls_ref[...] = jnp.zeros_like(ls_ref)

        xv, g = x_ref[...], g_ref[...]
        y, _ = _rms(xv, g)
        e = y - t_ref[...]
        part = jnp.sum(jnp.mean(e * e, axis=-1, keepdims=True), axis=0, keepdims=True)
        ls_ref[...] += jnp.broadcast_to(0.5 * part, ls_ref.shape)
        dx, dg = _rms_bwd(xv, g, e / K)
        dx_ref[...] = dx
        dg_ref[...] += jnp.broadcast_to(dg, dg_ref.shape)

    row = pl.BlockSpec((tm, K), lambda i: (i, 0))
    dx, dg, ls = pl.pallas_call(
        body, name="loss_head", grid=(S // tm,),
        in_specs=[row, pl.BlockSpec((1, K), lambda i: (0, 0)), row],
        out_specs=[row, pl.BlockSpec((8, K), lambda i: (0, 0)), pl.BlockSpec((8, LANES), lambda i: (0, 0))],
        out_shape=[jax.ShapeDtypeStruct((S, K), F32), jax.ShapeDtypeStruct((8, K), F32),
                   jax.ShapeDtypeStruct((8, LANES), F32)],
        compiler_params=_params(("arbitrary",)),
    )(x, gain, target)
    return ls[0, 0], dx, dg[0]


def _once(shape, index_map):
    return pl.BlockSpec(shape, index_map, pipeline_mode=pl.Buffered(1))


def ffn_fwd(x, gain, wg, wu, wd, carry=(), *, tm=512):
    S, D = x.shape
    NB, _, Fs = wg.shape
    tm = _tile(S, tm)
    ni = S // tm
    nc = len(carry)
    gain = gain.reshape(1, D)

    def body(*refs):
        x_ref, g_ref, wg_ref, wu_ref, wd_ref = refs[:5]
        c_in = refs[5:5 + nc]
        xo_ref, h_ref, G_ref, U_ref = refs[5 + nc:9 + nc]
        c_out = refs[9 + nc:9 + 2 * nc]
        hs, acc = refs[9 + 2 * nc:11 + 2 * nc]
        i, j = pl.program_id(0), pl.program_id(1)
        if nc:
            start, middle, finish = _gather_phases(c_in, c_out, *refs[11 + 2 * nc:])
            pl.when((i == 0) & (j == 0))(start)
            pl.when((i == ni // 2) & (j == 0))(middle)

        @pl.when(j == 0)
        def _():
            h, _ = _rms(x_ref[...], g_ref[...])
            hb = h.astype(BF)
            hs[...] = hb
            h_ref[...] = hb
            acc[...] = jnp.zeros_like(acc)

        h = hs[...]
        g = _dot(h, wg_ref[...])
        u = _dot(h, wu_ref[...])
        G_ref[...] = g.astype(BF)
        U_ref[...] = u.astype(BF)
        a = (g * jax.nn.sigmoid(g) * u).astype(BF)
        acc[...] += _dot(a, wd_ref[...])

        @pl.when(j == NB - 1)
        def _():
            xo_ref[...] = x_ref[...] + 0.5 * acc[...]

        if nc:
            pl.when((i == ni - 1) & (j == NB - 1))(finish)

    row = lambda i, j: (i, 0)
    blk = lambda i, j: (j, 0, 0)
    hid = pl.BlockSpec((None, tm, Fs), lambda i, j: (j, i, 0))
    return pl.pallas_call(
        body, name="ffn_fwd_gather" if nc else "ffn_fwd", grid=(ni, NB),
        in_specs=[pl.BlockSpec((tm, D), row), pl.BlockSpec((1, D), lambda i, j: (0, 0)),
                  pl.BlockSpec((None, D, Fs), blk), pl.BlockSpec((None, D, Fs), blk), pl.BlockSpec((None, Fs, D), blk)]
                 + [ANY] * nc,
        out_specs=[pl.BlockSpec((tm, D), row), pl.BlockSpec((tm, D), row), hid, hid] + [ANY] * nc,
        out_shape=[jax.ShapeDtypeStruct((S, D), F32), jax.ShapeDtypeStruct((S, D), BF),
                   jax.ShapeDtypeStruct((NB, S, Fs), BF), jax.ShapeDtypeStruct((NB, S, Fs), BF)] + _gathered(carry),
        scratch_shapes=[pltpu.VMEM((tm, D), BF), pltpu.VMEM((tm, D), F32)] + (_comm_sems(nc) if nc else []),
        compiler_params=_params(("arbitrary", "arbitrary")),
    )(x, gain, wg, wu, wd, *carry)


def ffn_bwd_x(dy, x, gain, G, U, wg, wu, wd, carry=(), *, tm=512, sub=256):
    S, D = x.shape
    NB, _, Fs = wg.shape
    tm = _tile(S, tm)
    sub = _tile(tm, sub)
    ni = S // tm
    nc = len(carry)
    gain = gain.reshape(1, D)

    def body(*refs):
        dy_ref, x_ref, g_ref, G_ref, U_ref, wg_ref, wu_ref, wd_ref = refs[:8]
        c_in = refs[8:8 + nc]
        dx_ref, do_ref, dG_ref, dU_ref, dg_ref = refs[8 + nc:13 + nc]
        c_out = refs[13 + nc:13 + 2 * nc]
        dob, acc = refs[13 + 2 * nc:15 + 2 * nc]
        i, j = pl.program_id(0), pl.program_id(1)
        if nc:
            start, finish = _exchange_phases(c_in, c_out, [False] * nc, *refs[15 + 2 * nc:])
            pl.when((i == 0) & (j == 0))(start)

        @pl.when(j == 0)
        def _():
            d = (0.5 * dy_ref[...]).astype(BF)
            dob[...] = d
            do_ref[...] = d
            acc[...] = jnp.zeros_like(acc)

        @pl.when((i == 0) & (j == 0))
        def _():
            dg_ref[...] = jnp.zeros_like(dg_ref)

        for r in range(tm // sub):
            rows = slice(r * sub, (r + 1) * sub)
            dA = _dot_nt(dob[rows, :], wd_ref[...])
            g = G_ref[rows, :].astype(F32)
            u = U_ref[rows, :].astype(F32)
            sig = jax.nn.sigmoid(g)
            dG = (dA * u * (sig * (1.0 + g * (1.0 - sig)))).astype(BF)
            dU = (dA * (g * sig)).astype(BF)
            dG_ref[rows, :] = dG
            dU_ref[rows, :] = dU
            acc[rows, :] += _dot_nt(dG, wg_ref[...]) + _dot_nt(dU, wu_ref[...])

        @pl.when(j == NB - 1)
        def _():
            dxn, dg = _rms_bwd(x_ref[...], g_ref[...], acc[...])
            dx_ref[...] = dy_ref[...] + dxn
            dg_ref[...] += jnp.broadcast_to(dg, dg_ref.shape)

        if nc:
            pl.when((i == ni - 1) & (j == NB - 1))(finish)

    row = lambda i, j: (i, 0)
    blk = lambda i, j: (j, 0, 0)
    hid = pl.BlockSpec((None, tm, Fs), lambda i, j: (j, i, 0))
    dx, dout, dG, dU, dg, *got = pl.pallas_call(
        body, name="ffn_bwd_x_exchange" if nc else "ffn_bwd_x", grid=(ni, NB),
        in_specs=[_once((tm, D), row), _once((tm, D), row), pl.BlockSpec((1, D), lambda i, j: (0, 0)), hid, hid,
                  pl.BlockSpec((None, D, Fs), blk), pl.BlockSpec((None, D, Fs), blk), pl.BlockSpec((None, Fs, D), blk)]
                 + [ANY] * nc,
        out_specs=[_once((tm, D), row), _once((tm, D), row), hid, hid, pl.BlockSpec((8, D), lambda i, j: (0, 0))]
                  + [ANY] * nc,
        out_shape=[jax.ShapeDtypeStruct((S, D), F32), jax.ShapeDtypeStruct((S, D), BF),
                   jax.ShapeDtypeStruct((NB, S, Fs), BF), jax.ShapeDtypeStruct((NB, S, Fs), BF),
                   jax.ShapeDtypeStruct((8, D), F32)] + [jax.ShapeDtypeStruct(a.shape, a.dtype) for a in carry],
        scratch_shapes=[pltpu.VMEM((tm, D), BF), pltpu.VMEM((tm, D), F32)] + (_comm_sems(nc) if nc else []),
        compiler_params=_params(("arbitrary", "arbitrary")),
    )(dy, x, gain, G, U, wg, wu, wd, *carry)
    return dx, dout, dG, dU, dg[0], got


def ffn_bwd_w(h, dout, G, U, dG, dU, carry=(), *, tm=512):
    S, D = h.shape
    NB, _, Fs = G.shape
    tm = _tile(S, tm)
    ni = S // tm
    nc = len(carry)

    def body(*refs):
        h_ref, do_ref, G_ref, U_ref, dG_ref, dU_ref = refs[:6]
        c_in = refs[6:6 + nc]
        wg_ref, wu_ref, wd_ref = refs[6 + nc:9 + nc]
        c_out = refs[9 + nc:9 + 2 * nc]
        ag, au, ad = refs[9 + 2 * nc:12 + 2 * nc]
        j, i = pl.program_id(0), pl.program_id(1)
        if nc:
            start, finish = _exchange_phases(c_in, c_out, [False] * nc, *refs[12 + 2 * nc:])
            pl.when((j == 0) & (i == 0))(start)

        @pl.when(i == 0)
        def _():
            ag[...] = jnp.zeros_like(ag)
            au[...] = jnp.zeros_like(au)
            ad[...] = jnp.zeros_like(ad)

        h = h_ref[...]
        ag[...] += _dot_tn(h, dG_ref[...])
        au[...] += _dot_tn(h, dU_ref[...])
        g = G_ref[...].astype(F32)
        a = (g * jax.nn.sigmoid(g) * U_ref[...].astype(F32)).astype(BF)
        ad[...] += _dot_tn(a, do_ref[...])

        @pl.when(i == ni - 1)
        def _():
            wg_ref[...] = ag[...].astype(BF)
            wu_ref[...] = au[...].astype(BF)
            wd_ref[...] = ad[...].astype(BF)

        if nc:
            pl.when((j == NB - 1) & (i == ni - 1))(finish)

    row = pl.BlockSpec((tm, D), lambda j, i: (i, 0))
    hid = pl.BlockSpec((None, tm, Fs), lambda j, i: (j, i, 0))
    blk = lambda j, i: (j, 0, 0)
    dwg, dwu, dwd, *got = pl.pallas_call(
        body, name="ffn_bwd_w_exchange" if nc else "ffn_bwd_w", grid=(NB, ni),
        in_specs=[row, row, hid, hid, hid, hid] + [ANY] * nc,
        out_specs=[_once((None, D, Fs), blk), _once((None, D, Fs), blk), _once((None, Fs, D), blk)] + [ANY] * nc,
        out_shape=[jax.ShapeDtypeStruct((NB, D, Fs), BF), jax.ShapeDtypeStruct((NB, D, Fs), BF),
                   jax.ShapeDtypeStruct((NB, Fs, D), BF)] + [jax.ShapeDtypeStruct(a.shape, a.dtype) for a in carry],
        scratch_shapes=[pltpu.VMEM((D, Fs), F32), pltpu.VMEM((D, Fs), F32), pltpu.VMEM((Fs, D), F32)]
                       + (_comm_sems(nc) if nc else []),
        compiler_params=_params(("arbitrary", "arbitrary")),
    )(h, dout, G, U, dG, dU, *carry)
    return [dwg, dwu, dwd], got


def dilated_bias(H, d):
    n = DIL_N
    slopes = 2.0 ** (-8.0 * (np.arange(H) + 1) / H)
    i = np.arange(n)[:, None]
    j = np.arange(2 * n)[None, :]
    steps = n + i - j
    band = (steps >= 0) & (steps <= n)
    first = band & (j >= n)
    bias = -slopes[:, None, None] * (d * steps).astype(np.float64)[None]
    out = np.stack([np.where(band[None], bias, NEG), np.where(first[None], bias, NEG)], axis=1)
    return jnp.asarray(out, dtype=F32)


def head_delta(do, o, *, tm=512):
    S, W = o.shape
    tm = _tile(S, tm)

    def body(do_ref, o_ref, d_ref):
        for h in range(W // HEAD):
            cols = slice(h * HEAD, (h + 1) * HEAD)
            s = jnp.sum(do_ref[:, cols].astype(F32) * o_ref[:, cols].astype(F32), axis=1, keepdims=True)
            d_ref[:, cols] = jnp.broadcast_to(s, (tm, HEAD))

    blk = pl.BlockSpec((tm, W), lambda i: (i, 0))
    return pl.pallas_call(
        body, name="head_delta", grid=(S // tm,), in_specs=[blk, blk], out_specs=blk,
        out_shape=jax.ShapeDtypeStruct((S, W), F32),
        compiler_params=_params(("parallel",)),
    )(do, o)


DIL_CHUNK = DIL_N * max(BRANCH_DILATIONS)
DIL_UNROLL = 4


def _rows(start, size, d):
    return pl.ds(pl.multiple_of(start, DIL_N), size) if d == 1 else pl.ds(start, size, stride=d)


def dil_attn_fwd(qkv, biases, H):
    S = qkv.shape[0]
    C = DIL_CHUNK
    assert S % C == 0
    scale = HEAD ** -0.5
    nbr = len(BRANCH_DILATIONS)

    def body(*refs):
        q_ref, kc_ref, kp_ref, vc_ref, vp_ref = refs[:5]
        b_refs = refs[5:5 + nbr]
        o_ref, L_ref, qf, kf, vf = refs[5 + nbr:10 + nbr]
        ogs = refs[10 + nbr:10 + 2 * nbr]
        lgs = refs[10 + 2 * nbr:10 + 3 * nbr]
        c = pl.program_id(1)
        qf[...] = q_ref[...].astype(F32)
        kf[0:C, :] = kp_ref[...].astype(F32)
        kf[C:2 * C, :] = kc_ref[...].astype(F32)
        vf[0:C, :] = vp_ref[...].astype(F32)
        vf[C:2 * C, :] = vc_ref[...].astype(F32)
        for d, b_ref, og, lg in zip(BRANCH_DILATIONS, b_refs, ogs, lgs):
            span = DIL_N * d

            def block(t, carry, d=d, b_ref=b_ref, og=og, lg=lg, span=span):
                r, b = t % d, t // d
                q0 = b * span + r
                q = qf[_rows(q0, DIL_N, d), :].astype(BF)
                k2 = kf[_rows(C + q0 - span, 2 * DIL_N, d), :].astype(BF)
                v2 = vf[_rows(C + q0 - span, 2 * DIL_N, d), :].astype(BF)
                bias = jnp.where((c == 0) & (b == 0), b_ref[1], b_ref[0])
                s = _dot_nt(q, k2) * scale + bias
                m = jnp.max(s, axis=1, keepdims=True)
                p = jnp.exp(s - m)
                l = jnp.sum(p, axis=1, keepdims=True)
                og[_rows(q0, DIL_N, d), :] = _dot(p.astype(BF), v2) / l
                lg[_rows(q0, DIL_N, d), :] = jnp.broadcast_to(m + jnp.log(l), (DIL_N, LANES))
                return carry

            lax.fori_loop(0, C // DIL_N, block, 0, unroll=DIL_UNROLL)
        ls = [lg[...] for lg in lgs]
        m = functools.reduce(jnp.maximum, ls)
        es = [jnp.exp(l - m) for l in ls]
        z = functools.reduce(jnp.add, es)
        mix = functools.reduce(jnp.add, [e * og[...] for e, og in zip(es, ogs)])
        o_ref[...] = (mix / z).astype(BF)
        L_ref[...] = m + jnp.log(z)

    cur = lambda off: pl.BlockSpec((C, HEAD), lambda h, c: (c, off * H + h))
    prv = lambda off: pl.BlockSpec((C, HEAD), lambda h, c: (jnp.maximum(c - 1, 0), off * H + h))
    tab = pl.BlockSpec((None, 2, DIL_N, 2 * DIL_N), lambda h, c: (h, 0, 0, 0))
    out = pl.BlockSpec((C, HEAD), lambda h, c: (c, h))
    return pl.pallas_call(
        body, name="dil_attn_fwd", grid=(H, S // C),
        in_specs=[cur(0), cur(1), prv(1), cur(2), prv(2)] + [tab] * nbr,
        out_specs=[out, out],
        out_shape=[jax.ShapeDtypeStruct((S, H * HEAD), BF), jax.ShapeDtypeStruct((S, H * HEAD), F32)],
        scratch_shapes=[pltpu.VMEM((C, HEAD), F32), pltpu.VMEM((2 * C, HEAD), F32), pltpu.VMEM((2 * C, HEAD), F32)]
                       + [pltpu.VMEM((C, HEAD), F32)] * (2 * nbr),
        compiler_params=_params(("parallel", "arbitrary")),
    )(qkv, qkv, qkv, qkv, qkv, *biases)


def dil_attn_bwd(qkv, do, L, delta, biases, H):
    S = qkv.shape[0]
    C = DIL_CHUNK
    assert S % C == 0
    nc = S // C
    scale = HEAD ** -0.5
    nbr = len(BRANCH_DILATIONS)

    def body(*refs):
        (qc_ref, qn_ref, kc_ref, kp_ref, vc_ref, vp_ref, doc_ref, don_ref, Lc_ref, Ln_ref, dlc_ref, dln_ref) = refs[:12]
        b_refs = refs[12:12 + nbr]
        (dq_ref, dk_ref, dv_ref, qf, dof, Lf, dlf, kf, vf, dq_acc, dk_acc, dv_acc,
         dq_g, dka_g, dva_g, dkb_g, dvb_g) = refs[12 + nbr:]
        c = pl.program_id(1)
        for buf, a_ref, b_ref in ((qf, qc_ref, qn_ref), (dof, doc_ref, don_ref), (Lf, Lc_ref, Ln_ref),
                                  (dlf, dlc_ref, dln_ref), (kf, kp_ref, kc_ref), (vf, vp_ref, vc_ref)):
            buf[0:C, :] = a_ref[...].astype(F32)
            buf[C:2 * C, :] = b_ref[...].astype(F32)
        for g_idx, (d, b_ref) in enumerate(zip(BRANCH_DILATIONS, b_refs)):
            span = DIL_N * d
            dkb_g[C:2 * C, :] = jnp.zeros((C, HEAD), F32)
            dvb_g[C:2 * C, :] = jnp.zeros((C, HEAD), F32)

            def block(t, carry, d=d, b_ref=b_ref, span=span):
                r, b = t % d, t // d
                q0 = b * span + r
                qrows = _rows(q0, DIL_N, d)
                krows = _rows(C + q0 - span, 2 * DIL_N, d)
                q, dov = qf[qrows, :].astype(BF), dof[qrows, :].astype(BF)
                k2, v2 = kf[krows, :].astype(BF), vf[krows, :].astype(BF)
                Lq, dl = Lf[qrows, :], dlf[qrows, :]
                bias = jnp.where((c == 0) & (b == 0), b_ref[1], b_ref[0])
                s = _dot_nt(q, k2) * scale + bias
                p = jnp.exp(s - jnp.concatenate([Lq, Lq], axis=1))
                ds = p * (_dot_nt(dov, v2) - jnp.concatenate([dl, dl], axis=1))
                dsb, pb = ds.astype(BF), p.astype(BF)
                dq_g[qrows, :] = _dot(dsb, k2) * scale
                dk2 = _dot_tn(dsb, q) * scale
                dv2 = _dot_tn(pb, dov)
                dka_g[qrows, :] = dk2[DIL_N:]
                dva_g[qrows, :] = dv2[DIL_N:]
                prows = _rows(C + q0 - span, DIL_N, d)
                dkb_g[prows, :] = dk2[:DIL_N]
                dvb_g[prows, :] = dv2[:DIL_N]
                return carry

            lax.fori_loop(0, C // DIL_N, block, 0, unroll=DIL_UNROLL)

            @pl.when(c < nc - 1)
            def _(d=d, b_ref=b_ref, span=span):
                def nxt(r, carry):
                    qrows = _rows(C + r, DIL_N, d)
                    krows = _rows(2 * C - span + r, DIL_N, d)
                    q, dov = qf[qrows, :].astype(BF), dof[qrows, :].astype(BF)
                    k1, v1 = kf[krows, :].astype(BF), vf[krows, :].astype(BF)
                    s = _dot_nt(q, k1) * scale + b_ref[0][:, :DIL_N]
                    p = jnp.exp(s - Lf[qrows, :])
                    ds = p * (_dot_nt(dov, v1) - dlf[qrows, :])
                    dkb_g[krows, :] = _dot_tn(ds.astype(BF), q) * scale
                    dvb_g[krows, :] = _dot_tn(p.astype(BF), dov)
                    return carry

                lax.fori_loop(0, d, nxt, 0, unroll=min(d, DIL_UNROLL))

            if g_idx == 0:
                dq_acc[...] = dq_g[...]
                dk_acc[...] = dka_g[...] + dkb_g[C:2 * C, :]
                dv_acc[...] = dva_g[...] + dvb_g[C:2 * C, :]
            else:
                dq_acc[...] += dq_g[...]
                dk_acc[...] += dka_g[...] + dkb_g[C:2 * C, :]
                dv_acc[...] += dva_g[...] + dvb_g[C:2 * C, :]

        dq_ref[...] = dq_acc[...].astype(BF)
        dk_ref[...] = dk_acc[...].astype(BF)
        dv_ref[...] = dv_acc[...].astype(BF)

    cur3 = lambda off: pl.BlockSpec((C, HEAD), lambda h, c: (c, off * H + h))
    prv3 = lambda off: pl.BlockSpec((C, HEAD), lambda h, c: (jnp.maximum(c - 1, 0), off * H + h))
    nxt3 = lambda off: pl.BlockSpec((C, HEAD), lambda h, c: (jnp.minimum(c + 1, nc - 1), off * H + h))
    cur1 = pl.BlockSpec((C, HEAD), lambda h, c: (c, h))
    nxt1 = pl.BlockSpec((C, HEAD), lambda h, c: (jnp.minimum(c + 1, nc - 1), h))
    tab = pl.BlockSpec((None, 2, DIL_N, 2 * DIL_N), lambda h, c: (h, 0, 0, 0))
    return pl.pallas_call(
        body, name="dil_attn_bwd", grid=(H, nc),
        in_specs=[cur3(0), nxt3(0), cur3(1), prv3(1), cur3(2), prv3(2), cur1, nxt1, cur1, nxt1, cur1, nxt1] + [tab] * nbr,
        out_specs=[cur1] * 3,
        out_shape=[jax.ShapeDtypeStruct((S, H * HEAD), BF)] * 3,
        scratch_shapes=[pltpu.VMEM((2 * C, HEAD), F32)] * 6 + [pltpu.VMEM((C, HEAD), F32)] * 6
                       + [pltpu.VMEM((2 * C, HEAD), F32)] * 2,
        compiler_params=_params(("parallel", "arbitrary")),
    )(qkv, qkv, qkv, qkv, qkv, qkv, do, do, L, L, delta, delta, *biases)


def _causal_mask(s, qi, ki, tq, tk, row0=0):
    qpos = qi * tq + row0 + lax.broadcasted_iota(jnp.int32, s.shape, 0)
    kpos = ki * tk + lax.broadcasted_iota(jnp.int32, s.shape, 1)
    return jnp.where(kpos <= qpos, s, NEG)


def mla_fwd(qn, qr, kv, kr, H, *, tq=1024, tk=1024, sub=512):
    S = qn.shape[0]
    tq, tk = _tile(S, tq), _tile(S, tk)
    sub = _tile(tq, sub)
    nk = S // tk
    scale = MLA_SCALE
    c2 = scale * LOG2E

    def body(qn_ref, qr_ref, kn_ref, kr_ref, v_ref, o_ref, L_ref, m_s, l_s, acc):
        qi, ki = pl.program_id(1), pl.program_id(2)

        @pl.when(ki == 0)
        def _():
            m_s[...] = jnp.full_like(m_s, NEG)
            l_s[...] = jnp.zeros_like(l_s)
            acc[...] = jnp.zeros_like(acc)

        def step(masked):
            k = jnp.concatenate([kn_ref[...], kr_ref[...]], axis=1)
            v = v_ref[...]
            for r in range(tq // sub):
                rows = slice(r * sub, (r + 1) * sub)
                q = jnp.concatenate([qn_ref[rows, :], qr_ref[rows, :]], axis=1)
                s = _dot_nt(q, k)
                if masked:
                    s = _causal_mask(s, qi, ki, tq, tk, r * sub)
                m_prev = m_s[rows, :]
                m_new = jnp.maximum(m_prev, jnp.max(s, axis=1, keepdims=True))
                alpha = jnp.exp2((m_prev - m_new) * c2)
                p = jnp.exp2((s - jnp.tile(m_new, (1, tk // LANES))) * c2)
                l_s[rows, :] = alpha * l_s[rows, :] + jnp.sum(p, axis=1, keepdims=True)
                acc[rows, :] = alpha * acc[rows, :] + _dot(p.astype(BF), v)
                m_s[rows, :] = m_new

        full = ki * tk + tk - 1 <= qi * tq
        live = ki * tk <= qi * tq + tq - 1
        pl.when(full)(lambda: step(False))
        pl.when(live & jnp.logical_not(full))(lambda: step(True))

        @pl.when(ki == nk - 1)
        def _():
            o_ref[...] = (acc[...] / l_s[...]).astype(BF)
            L_ref[...] = m_s[...] * scale + jnp.log(l_s[...])

    kcl = lambda qi, ki: jnp.minimum(ki, (qi * tq + tq - 1) // tk)
    qs = pl.BlockSpec((tq, HEAD), lambda h, qi, ki: (qi, h))
    return pl.pallas_call(
        body, name="mla_fwd", grid=(H, S // tq, nk),
        in_specs=[qs, qs, pl.BlockSpec((tk, HEAD), lambda h, qi, ki: (kcl(qi, ki), h)),
                  pl.BlockSpec((tk, LANES), lambda h, qi, ki: (kcl(qi, ki), 0)),
                  pl.BlockSpec((tk, HEAD), lambda h, qi, ki: (kcl(qi, ki), H + h))],
        out_specs=[qs, qs],
        out_shape=[jax.ShapeDtypeStruct((S, H * HEAD), BF), jax.ShapeDtypeStruct((S, H * HEAD), F32)],
        scratch_shapes=[pltpu.VMEM((tq, LANES), F32), pltpu.VMEM((tq, LANES), F32), pltpu.VMEM((tq, HEAD), F32)],
        compiler_params=_params(("parallel", "parallel", "arbitrary")),
    )(qn, qr, kv, kr, kv)


def _mla_probs(s, L_rows, c2, width):
    return jnp.exp2(s * c2 - jnp.tile(L_rows * LOG2E, (1, width // LANES)))


def mla_bwd(qn, qr, kv, kr, do, L, delta, rope, H, prev=None, *, tq=1024, tk=1024, sub=512):
    S = qn.shape[0]
    tq, tk = _tile(S, tq), _tile(S, tk)
    sub = _tile(tq, sub)
    nq, nk = S // tq, S // tk
    scale = MLA_SCALE
    c2 = scale * LOG2E
    has_prev = prev is not None

    def body(*refs):
        (qn_ref, qr_ref, kn_ref, kr_ref, v_ref, do_ref, L_ref, dl_ref, cq_ref, sq_ref, ck_ref, sk_ref) = refs[:12]
        refs = refs[12:]
        if has_prev:
            pk_ref, pv_ref, pr_ref = refs[:3]
            refs = refs[3:]
        dqn_ref, dqr_ref, dk_ref, dv_ref, dr_ref, dq_full, dk_acc, dv_acc = refs
        h, ki, qi = pl.program_id(0), pl.program_id(1), pl.program_id(2)
        qrows = pl.ds(pl.multiple_of(qi * tq, tq), tq)
        krows = pl.ds(pl.multiple_of(ki * tk, tk), tk)

        @pl.when(qi == 0)
        def _():
            dk_acc[...] = jnp.zeros_like(dk_acc)
            dv_acc[...] = jnp.zeros_like(dv_acc)

        @pl.when((qi == 0) & (h == 0))
        def _():
            dr_ref[krows, :] = jnp.zeros((tk, LANES), F32)

        @pl.when(ki == 0)
        def _():
            dq_full[qrows, :] = jnp.zeros((tq, 2 * HEAD), F32)

        def step(masked):
            k = jnp.concatenate([kn_ref[...], kr_ref[...]], axis=1)
            v = v_ref[...]
            for r in range(tq // sub):
                rows = slice(r * sub, (r + 1) * sub)
                q = jnp.concatenate([qn_ref[rows, :], qr_ref[rows, :]], axis=1)
                s = _dot_nt(q, k)
                if masked:
                    s = _causal_mask(s, qi, ki, tq, tk, r * sub)
                p = _mla_probs(s, L_ref[rows, :], c2, tk)
                dov = do_ref[rows, :]
                dv_acc[...] += _dot_tn(p.astype(BF), dov)
                dp = _dot_nt(dov, v)
                ds = (p * (dp - jnp.tile(dl_ref[rows, :], (1, tk // LANES)))).astype(BF)
                dk_acc[...] += _dot_tn(ds, q)
                dq_full[pl.ds(pl.multiple_of(qi * tq + r * sub, sub), sub), :] += _dot(ds, k)

        full = ki * tk + tk - 1 <= qi * tq
        live = ki * tk <= qi * tq + tq - 1
        pl.when(full)(lambda: step(False))
        pl.when(live & jnp.logical_not(full))(lambda: step(True))

        @pl.when(ki == (qi * tq + tq - 1) // tk)
        def _():
            dq = dq_full[qrows, :] * scale
            dqn_ref[qrows, :] = dq[:, :HEAD].astype(BF)
            dqr_ref[qrows, :] = _rope_bwd(dq[:, HEAD:], cq_ref[...], sq_ref[...]).astype(BF)

        @pl.when(qi == nq - 1)
        def _():
            dk = dk_acc[...] * scale
            dkn, dv = dk[:, :HEAD], dv_acc[...]
            if has_prev:
                dkn, dv = dkn + pk_ref[...], dv + pv_ref[...]
            dk_ref[...] = dkn
            dv_ref[...] = dv
            dr_ref[krows, :] += dk[:, HEAD:]

        @pl.when((qi == nq - 1) & (h == H - 1))
        def _():
            dr = _rope_bwd(dr_ref[krows, :], ck_ref[...], sk_ref[...])
            dr_ref[krows, :] = dr + pr_ref[...] if has_prev else dr

    qcl = lambda ki, qi: jnp.maximum(qi, (ki * tk) // tq)
    qs = pl.BlockSpec((tq, HEAD), lambda h, ki, qi: (qcl(ki, qi), h))
    qt = pl.BlockSpec((tq, LANES), lambda h, ki, qi: (qcl(ki, qi), 0))
    kn = pl.BlockSpec((tk, HEAD), lambda h, ki, qi: (ki, h))
    vs = pl.BlockSpec((tk, HEAD), lambda h, ki, qi: (ki, H + h))
    k1 = pl.BlockSpec((tk, LANES), lambda h, ki, qi: (ki, 0))
    head = pl.BlockSpec((S, HEAD), lambda h, ki, qi: (0, h))
    in_specs = [qs, qs, kn, k1, vs, qs, qs, qs, qt, qt, k1, k1]
    args = [qn, qr, kv, kr, kv, do, L, delta, *rope, *rope]
    if has_prev:
        in_specs += [kn, vs, k1]
        args += [prev[0], prev[0], prev[1]]
    return pl.pallas_call(
        body, name="mla_bwd", grid=(H, nk, nq), in_specs=in_specs,
        out_specs=[head, head, kn, kn, pl.BlockSpec((S, LANES), lambda h, ki, qi: (0, 0))],
        out_shape=[jax.ShapeDtypeStruct((S, H * HEAD), BF), jax.ShapeDtypeStruct((S, H * HEAD), BF),
                   jax.ShapeDtypeStruct((S, H * HEAD), F32), jax.ShapeDtypeStruct((S, H * HEAD), F32),
                   jax.ShapeDtypeStruct((S, LANES), F32)],
        scratch_shapes=[pltpu.VMEM((S, 2 * HEAD), F32), pltpu.VMEM((tk, 2 * HEAD), F32), pltpu.VMEM((tk, HEAD), F32)],
        compiler_params=_params(("arbitrary", "arbitrary", "arbitrary")),
    )(*args)


def adamw(recvs, w, m, v, *, name="adamw"):
    R, C = w.shape
    n = len(recvs)
    R1 = R // n
    assert all(r.shape == (N_DEV, R1, C) for r in recvs)
    tr = 16
    while tr * 2 * C * 44 <= 6 * 1024 * 1024 and R1 % (tr * 2) == 0:
        tr *= 2
    tr = min(tr, R1)
    assert R1 % tr == 0
    nb = R1 // tr

    def body(*refs):
        r_refs = refs[:n]
        w_ref, m_ref, v_ref, g_ref, d_ref, mo_ref, vo_ref = refs[n:]
        for slab in range(n):
            @pl.when(pl.program_id(0) == slab)
            def _(r_ref=r_refs[slab]):
                g = r_ref[0].astype(F32)
                for s in range(1, N_DEV):
                    g = g + r_ref[s].astype(F32)
                m2 = ADAM_B1 * m_ref[...] + (1.0 - ADAM_B1) * g
                v2 = ADAM_B2 * v_ref[...] + (1.0 - ADAM_B2) * (g * g)
                m_hat = m2 / (1.0 - ADAM_B1 ** ADAM_STEP)
                v_hat = v2 / (1.0 - ADAM_B2 ** ADAM_STEP)
                g_ref[...] = g
                d_ref[...] = -ADAM_LR * (m_hat / (jnp.sqrt(v_hat) + ADAM_EPS) + ADAM_WD * w_ref[...])
                mo_ref[...] = m2
                vo_ref[...] = v2

    def recv_spec(slab):
        return pl.BlockSpec((N_DEV, tr, C), lambda l, i: (0, jnp.where(l == slab, i, jnp.where(l < slab, 0, nb - 1)), 0))

    row = pl.BlockSpec((tr, C), lambda l, i: (l * nb + i, 0))
    return pl.pallas_call(
        body, name=name, grid=(n, nb),
        in_specs=[recv_spec(slab) for slab in range(n)] + [row, row, row],
        out_specs=[row] * 4, out_shape=[jax.ShapeDtypeStruct((R, C), F32)] * 4,
        compiler_params=_params(("arbitrary", "arbitrary")),
    )(*recvs, w, m, v)


def _pack(ts, width, dtype, row_mult=16):
    flat = jnp.concatenate([t.astype(dtype).reshape(-1) for t in ts])
    n = flat.shape[0]
    rows = -(-n // (width * row_mult)) * row_mult
    return jnp.pad(flat, (0, rows * width - n)).reshape(rows, width)


def _unpack(buf, shapes):
    lead = buf.shape[:-2]
    flat = buf.reshape(lead + (-1,))
    out, off = [], 0
    for s in shapes:
        n = int(np.prod(s))
        out.append(flat[..., off:off + n].reshape(lead + tuple(s)))
        off += n
    return out


def _rope_pad(r):
    z = jnp.zeros(r.shape[:-1] + (ROPE_HALF,), r.dtype)
    return jnp.concatenate([r[..., :ROPE_HALF], z, r[..., ROPE_HALF:], z], axis=-1)


def _rope_unpad(r):
    return jnp.concatenate([r[..., :ROPE_HALF], r[..., 2 * ROPE_HALF:3 * ROPE_HALF]], axis=-1)


def _step(P):
    x0 = P['x'][0]
    target = P['loss_target'][0]
    S, D = x0.shape
    NL = P['ffn_norm1'].shape[0]
    NA = P['a_wqkv'].shape[0]
    Fs = P['ffn1_wg'].shape[2]
    H = D // HEAD
    KV = P['b_wdkv'].shape[1]
    QL = P['b_wdq'].shape[2]
    HW = H * HEAD

    ffn_seq = [(l, f) for l in range(NL) for f in (0, 1)]
    ffn_loc = {(l, f): [P[n][l].astype(BF) for n in (f'ffn{f + 1}_wg', f'ffn{f + 1}_wu', f'ffn{f + 1}_wd')]
               for l, f in ffn_seq}
    assert 1 <= NA < NL
    groups = {('A', l): [('a_wqkv', l), ('a_wo', l)] for l in range(NA)}
    groups['B'] = [(n, None) for n in MIX_W if n.startswith('b_')]
    pick = lambda pre, n, l: P[pre + n] if l is None else P[pre + n][l]
    pack_group = lambda gk, pre, dtype: _pack([pick(pre, n, l) for n, l in groups[gk]], D, dtype)
    group_shapes = {gk: [pick('', n, l).shape for n, l in members] for gk, members in groups.items()}
    group_loc = {gk: pack_group(gk, '', BF) for gk in groups}
    gather_at = {(0, 0): [('A', 0)]}
    for l in range(1, NA):
        gather_at.setdefault((l - 1, 1), []).append(('A', l))
    gather_at.setdefault((NA - 1, 0), []).append('B')
    send_at = {(l, 0): [('A', l)] for l in range(NA)}
    send_at.setdefault((NA - 1, 1), []).append('B')
    gathered = {}
    ffn_w = {ffn_seq[0]: all_gather(ffn_loc[ffn_seq[0]])}

    def ffn_forward(x, gain, key):
        nxt = ffn_seq.index(key) + 1
        gks = gather_at.get(key, [])
        carry = (ffn_loc[ffn_seq[nxt]] if nxt < len(ffn_seq) else []) + [group_loc[gk] for gk in gks]
        xo, h, G, U, *got = ffn_fwd(x, gain, *ffn_w[key], carry=carry)
        if nxt < len(ffn_seq):
            ffn_w[ffn_seq[nxt]] = got[:3]
        for gk, arr in zip(gks, got[len(got) - len(gks):]):
            gathered[gk] = {n: t for (n, _), t in zip(groups[gk], _unpack(arr, group_shapes[gk]))}
        return xo, h, G, U

    cols = lambda t: jnp.moveaxis(t, 0, -2).reshape(t.shape[1:-1] + (N_DEV * t.shape[-1],))
    rows = lambda t, lead: jnp.moveaxis(t, 0, lead).reshape(t.shape[1:1 + lead] + (N_DEV * t.shape[1 + lead],) + t.shape[2 + lead:])
    a_full = {}

    def a_weights(l):
        if l not in a_full:
            g = gathered[('A', l)]
            a_full[l] = (cols(g['a_wqkv']), rows(g['a_wo'], 0))
        return a_full[l]

    inv = 1.0 / (ROPE_THETA ** (jnp.arange(0, 2 * ROPE_HALF, 2, dtype=F32) / (2 * ROPE_HALF)))
    ang = jnp.arange(S, dtype=F32)[:, None] * inv[None, :]
    z = jnp.zeros((S, ROPE_HALF), F32)
    rope = (jnp.concatenate([jnp.cos(ang), z, jnp.cos(ang), z], axis=1),
            jnp.concatenate([-jnp.sin(ang), z, jnp.sin(ang), z], axis=1))
    biases = [dilated_bias(H, d) for d in BRANCH_DILATIONS]

    saved = []
    kvs = None
    x = x0
    for l in range(NL):
        st = {'x_in': x}
        if l == NA:
            g = gathered['B']
            b_wo = rows(g['b_wo'], 1)
            wdkv = rows(g['b_wdkv'], 0)
            wkr = _rope_pad(rows(g['b_wkr'], 0))
            wkv = jnp.concatenate([rows(g['b_wuk'], 0).reshape(KV, HW), rows(g['b_wuv'], 0).reshape(KV, HW)], axis=1)
            wdq = rows(g['b_wdq'], 1)
            wuq = rows(g['b_wuq'], 1)
            wuq_n = wuq[..., :HEAD].reshape(-1, QL, HW)
            wuq_r = _rope_pad(wuq[..., HEAD:]).reshape(-1, QL, HW)
            ckv_pre, hkv = norm_mm(x, P['kv_norm'], wdkv, out_dtype=F32, name="kv_down")
            kr = norm_mm(x, P['kv_norm'], wkr, out_dtype=BF, tn=LANES, rope=rope, write_h=False, name="kv_rope")
            kvm, ckv = norm_mm(ckv_pre, P['b_ckv_norm'], wkv, out_dtype=BF, name="kv_up")
            kvs = dict(x=x, ckv_pre=ckv_pre, hkv=hkv, kr=kr, kv=kvm, ckv=ckv)
        xa, h1, G1, U1 = ffn_forward(x, P['ffn_norm1'][l], (l, 0))
        st.update(h1=h1, G1=G1, U1=U1, xa=xa)
        if l < NA:
            w_qkv, w_o = a_weights(l)
            qkv, hm = norm_mm(xa, P['mix_norm'][l], w_qkv, out_dtype=BF, tn=3 * HW // 4, name="a_qkv")
            o, Lj = dil_attn_fwd(qkv, biases, H)
            xb = mm(o, w_o, add=xa, name="mix_out")
            st.update(qkv=qkv, hm=hm, o=o, L=Lj)
        else:
            jb = l - NA
            cq_pre, hm = norm_mm(xa, P['mix_norm'][l], wdq[jb], out_dtype=F32, name="q_down")
            qn, cq = norm_mm(cq_pre, P['b_cq_norm'][jb], wuq_n[jb], out_dtype=BF, name="q_up")
            qr = norm_mm(cq_pre, P['b_cq_norm'][jb], wuq_r[jb], out_dtype=BF, tn=LANES, rope=rope, write_h=False,
                         name="q_rope")
            o, Lj = mla_fwd(qn, qr, kvs['kv'], kvs['kr'], H)
            xb = mm(o, b_wo[jb], add=xa, name="mix_out")
            st.update(cq_pre=cq_pre, hm=hm, qn=qn, qr=qr, cq=cq, o=o, L=Lj)
        x, h2, G2, U2 = ffn_forward(xb, P['ffn_norm2'][l], (l, 1))
        st.update(xb=xb, h2=h2, G2=G2, U2=U2)
        saved.append(st)

    loss_part, dx, dg_final = loss_head(x, P['final_norm'], target)

    gw = {}
    gv = {'final_norm': dg_final}
    dkv_acc = None
    for n in ('ffn_norm1', 'mix_norm', 'ffn_norm2'):
        gv[n] = [None] * NL
    gv['b_cq_norm'] = [None] * (NL - NA)
    ffn_r = {}
    pending = []
    a_g = {'a_wqkv': [None] * NA, 'a_wo': [None] * NA}
    b_g = {n: [None] * (NL - NA) for n in ('b_wdq', 'b_wuq', 'b_wo')}

    split_cols = lambda t: jnp.moveaxis(t.reshape(t.shape[:-1] + (N_DEV, t.shape[-1] // N_DEV)), -2, 0)
    split_rows = lambda t, lead: jnp.moveaxis(
        t.reshape(t.shape[:lead] + (N_DEV, t.shape[lead] // N_DEV) + t.shape[lead + 1:]), lead, 0)
    group_r = {}

    def pack_group_grads(gk):
        if gk == 'B':
            for n in b_g:
                gw[n] = jnp.stack(b_g[n])
            lead = {'b_wdq': 1, 'b_wuq': 1, 'b_wo': 1}
            shards = [split_rows(gw[n], lead.get(n, 0)) for n, _ in groups[gk]]
        else:
            l = gk[1]
            shards = [split_cols(a_g['a_wqkv'][l]), split_rows(a_g['a_wo'][l], 0)]
        flat = jnp.concatenate([t.astype(BF).reshape(N_DEV, -1) for t in shards], axis=1)
        rows = group_loc[gk].shape[0]
        return jnp.pad(flat, ((0, 0), (0, rows * D - flat.shape[1]))).reshape(N_DEV, rows, D)

    def ffn_backward(dy, x_in, gain, key, h, G, U):
        sent, carry = pending.pop() if pending else (None, [])
        dxi, dout, dG, dU, dgain, got = ffn_bwd_x(dy, x_in, gain, G, U, *ffn_w[key], carry=carry)
        if carry:
            ffn_r[sent] = got
        gks = send_at.get(key, [])
        dws, got_w = ffn_bwd_w(h, dout, G, U, dG, dU, carry=[pack_group_grads(gk) for gk in gks])
        group_r.update(zip(gks, got_w))
        pending.append((key, dws))
        return dxi, dgain

    for l in reversed(range(NL)):
        st = saved[l]
        dxb, gv['ffn_norm2'][l] = ffn_backward(dx, st['xb'], P['ffn_norm2'][l], (l, 1), st['h2'], st['G2'], st['U2'])
        xa = st['xa']
        if l < NA:
            w_qkv, w_o = a_weights(l)
            do = mm(dxb, w_o, tb=True, out_dtype=BF, name="mix_out_dx")
            a_g['a_wo'][l] = mm(st['o'], dxb, ta=True, out_dtype=BF, tm=1024, tn=1024, name="mix_out_dw")
            delta = head_delta(do, st['o'])
            dqkv = jnp.concatenate(dil_attn_bwd(st['qkv'], do, st['L'], delta, biases, H), axis=1)
            dh = mm(dqkv, w_qkv, tb=True, name="a_qkv_dx")
            a_g['a_wqkv'][l] = mm(st['hm'], dqkv, ta=True, out_dtype=BF, tm=1024, tn=1024, name="a_qkv_dw")
            dxa, gv['mix_norm'][l] = norm_bwd(xa, P['mix_norm'][l], dh, dxb, name="mix_norm_bwd")
        else:
            jb = l - NA
            do = mm(dxb, b_wo[jb], tb=True, out_dtype=BF, name="mix_out_dx")
            b_g['b_wo'][jb] = mm(st['o'], dxb, ta=True, out_dtype=BF, tm=1024, tn=1024, name="mix_out_dw")
            delta = head_delta(do, st['o'])
            dqn, dqr, dkn, dv, dr = mla_bwd(st['qn'], st['qr'], kvs['kv'], kvs['kr'], do, st['L'], delta, rope, H,
                                            prev=dkv_acc)
            dkv_acc = (jnp.concatenate([dkn, dv], axis=1), dr)
            dcq = mm(dqn, wuq_n[jb], tb=True, name="q_up_dx")
            dcq = mm(dqr, wuq_r[jb], tb=True, add=dcq, name="q_up_dx_add")
            dwn = mm(st['cq'], dqn, ta=True, out_dtype=F32, name="q_up_dw")
            dwr = mm(st['cq'], dqr, ta=True, out_dtype=F32, name="q_up_dw")
            b_g['b_wuq'][jb] = jnp.concatenate(
                [dwn.reshape(QL, H, HEAD), _rope_unpad(dwr.reshape(QL, H, HEAD))], axis=-1)
            dcq_pre, gv['b_cq_norm'][jb] = norm_bwd(st['cq_pre'], P['b_cq_norm'][jb], dcq, name="cq_norm_bwd")
            dh = mm(dcq_pre, wdq[jb], tb=True, name="q_down_dx")
            b_g['b_wdq'][jb] = mm(st['hm'], dcq_pre, ta=True, out_dtype=F32, tm=1024, name="q_down_dw")
            dxa, gv['mix_norm'][l] = norm_bwd(xa, P['mix_norm'][l], dh, dxb, name="mix_norm_bwd")
        dx, gv['ffn_norm1'][l] = ffn_backward(dxa, st['x_in'], P['ffn_norm1'][l], (l, 0), st['h1'], st['G1'], st['U1'])
        if l == NA:
            dkvm, dr = dkv_acc
            dckv = mm(dkvm, wkv, tb=True, name="kv_up_dx")
            dwkv = mm(kvs['ckv'], dkvm, ta=True, out_dtype=F32, name="kv_up_dw")
            gw['b_wuk'] = dwkv[:, :HW].reshape(KV, H, HEAD)
            gw['b_wuv'] = dwkv[:, HW:].reshape(KV, H, HEAD)
            dckv_pre, gv['b_ckv_norm'] = norm_bwd(kvs['ckv_pre'], P['b_ckv_norm'], dckv, name="ckv_norm_bwd")
            dh = mm(dckv_pre, wdkv, tb=True, name="kv_down_dx")
            dh = mm(dr, wkr, tb=True, add=dh, name="kv_rope_dx")
            gw['b_wdkv'] = mm(kvs['hkv'], dckv_pre, ta=True, out_dtype=F32, tm=1024, name="kv_down_dw")
            gw['b_wkr'] = _rope_unpad(mm(kvs['hkv'], dr, ta=True, out_dtype=F32, tm=1024, name="kv_rope_dw"))
            dx, gv['kv_norm'] = norm_bwd(kvs['x'], P['kv_norm'], dh, dx, name="kv_norm_bwd")

    for n in ('ffn_norm1', 'mix_norm', 'ffn_norm2', 'b_cq_norm'):
        gv[n] = jnp.stack(gv[n])

    vec_parts = [gv[n] for n in VEC_W] + [jnp.full((LANES,), loss_part, F32)]
    gvec = _pack(vec_parts, LANES, F32, row_mult=8)
    last_key, last = pending.pop()
    *got, rv = exchange(last + [gvec], [False] * len(last) + [True])
    ffn_r[last_key] = got

    res = {}
    for f in (0, 1):
        for which, kind in enumerate(('wg', 'wu', 'wd')):
            n = f'ffn{f + 1}_{kind}'
            recvs = [ffn_r[(l, f)][which] for l in range(NL)]
            flat = lambda t: t.reshape((NL * t.shape[1], t.shape[2]))
            out = adamw(recvs, flat(P[n]), flat(P['m_' + n]), flat(P['v_' + n]),
                        name="adamw_row" if kind == 'wd' else "adamw_col")
            res[n] = [t.reshape(P[n].shape) for t in out]
    per_layer = {n: [None] * NA for n in ('a_wqkv', 'a_wo')}
    for gk, members in groups.items():
        out = adamw([group_r[gk]], *[pack_group(gk, pre, F32) for pre in ('', 'm_', 'v_')], name="adamw_mix")
        for (n, l), parts in zip(members, zip(*[_unpack(t, group_shapes[gk]) for t in out])):
            if l is None:
                res[n] = list(parts)
            else:
                per_layer[n][l] = parts
    for n, layers in per_layer.items():
        res[n] = [jnp.stack(field) for field in zip(*layers)]
    vec_shapes = [P[n].shape for n in VEC_W] + [(LANES,)]
    ones = jnp.ones((LANES,), F32)
    pv = lambda pre: _pack([P[pre + n] for n in VEC_W] + [ones], LANES, F32, row_mult=8)
    out = adamw([rv], pv(''), pv('m_'), pv('v_'), name="adamw_vec")
    unp = [_unpack(t, vec_shapes) for t in out]
    for idx, n in enumerate(VEC_W):
        res[n] = [u[idx] for u in unp]
    loss = unp[0][-1][0]

    outs = [loss, dx[None]]
    for field in range(4):
        outs += [res[n][field] for n in W_NAMES]
    return tuple(outs)


def kernel(x, ffn_norm1, ffn1_wg, ffn1_wu, ffn1_wd, mix_norm, ffn_norm2, ffn2_wg, ffn2_wu, ffn2_wd, a_wqkv, a_wo, kv_norm, b_wdkv, b_ckv_norm, b_wkr, b_wuk, b_wuv, b_wdq, b_cq_norm, b_wuq, b_wo, final_norm, loss_target, m_ffn_norm1, m_ffn1_wg, m_ffn1_wu, m_ffn1_wd, m_mix_norm, m_ffn_norm2, m_ffn2_wg, m_ffn2_wu, m_ffn2_wd, m_a_wqkv, m_a_wo, m_kv_norm, m_b_wdkv, m_b_ckv_norm, m_b_wkr, m_b_wuk, m_b_wuv, m_b_wdq, m_b_cq_norm, m_b_wuq, m_b_wo, m_final_norm, v_ffn_norm1, v_ffn1_wg, v_ffn1_wu, v_ffn1_wd, v_mix_norm, v_ffn_norm2, v_ffn2_wg, v_ffn2_wu, v_ffn2_wd, v_a_wqkv, v_a_wo, v_kv_norm, v_b_wdkv, v_b_ckv_norm, v_b_wkr, v_b_wuk, v_b_wuv, v_b_wdq, v_b_cq_norm, v_b_wuq, v_b_wo, v_final_norm):
    return _step(dict(locals()))
```

```python
import functools
import math

import numpy as np
import jax
import jax.numpy as jnp
from jax import lax
from jax.experimental import pallas as pl
from jax.experimental.pallas import tpu as pltpu

BF = jnp.bfloat16
F32 = jnp.float32
MESH = pl.DeviceIdType.MESH
ANY = pl.BlockSpec(memory_space=pl.ANY)

N_DEV = 8
LANES = 128
HEAD = 128
ROPE_HALF = 32
DIL_N = 128
BRANCH_DILATIONS = (1, 4, 16)
ROPE_THETA = 10000.0
MLA_SCALE = (HEAD + 2 * ROPE_HALF) ** -0.5
LOG2E = math.log2(math.e)
EPS = 1e-6
NEG = -1e30
VMEM_LIMIT = 58 * 1024 * 1024

ADAM_LR, ADAM_B1, ADAM_B2, ADAM_EPS, ADAM_WD, ADAM_STEP = 0.001, 0.9, 0.999, 1e-08, 0.01, 10

W_NAMES = ['ffn_norm1', 'ffn1_wg', 'ffn1_wu', 'ffn1_wd', 'mix_norm', 'ffn_norm2', 'ffn2_wg', 'ffn2_wu', 'ffn2_wd',
           'a_wqkv', 'a_wo', 'kv_norm', 'b_wdkv', 'b_ckv_norm', 'b_wkr', 'b_wuk', 'b_wuv', 'b_wdq', 'b_cq_norm',
           'b_wuq', 'b_wo', 'final_norm']
MIX_W = ['a_wqkv', 'a_wo', 'b_wdkv', 'b_wkr', 'b_wuk', 'b_wuv', 'b_wdq', 'b_wuq', 'b_wo']
VEC_W = ['ffn_norm1', 'mix_norm', 'ffn_norm2', 'kv_norm', 'b_ckv_norm', 'b_cq_norm', 'final_norm']


def _tile(n, pref):
    t = min(n, pref)
    assert n % t == 0, (n, pref)
    return t


def _params(sem):
    return pltpu.CompilerParams(dimension_semantics=sem, vmem_limit_bytes=VMEM_LIMIT)


def _dot(a, b):
    return jnp.dot(a, b, preferred_element_type=F32)


def _dot_nt(a, b):
    return lax.dot_general(a, b, (((1,), (1,)), ((), ())), preferred_element_type=F32)


def _dot_tn(a, b):
    return lax.dot_general(a, b, (((0,), (0,)), ((), ())), preferred_element_type=F32)


def _rms(x, g):
    r = lax.rsqrt(jnp.mean(x * x, axis=-1, keepdims=True) + EPS)
    return x * r * g, r


def _rms_bwd(x, g, dh):
    r = lax.rsqrt(jnp.mean(x * x, axis=-1, keepdims=True) + EPS)
    xhat = x * r
    gd = dh * g
    dx = r * (gd - xhat * jnp.mean(gd * xhat, axis=-1, keepdims=True))
    return dx, jnp.sum(dh * xhat, axis=0, keepdims=True)


def _rope(t, cos, sin):
    return t * cos + pltpu.roll(t, 2 * ROPE_HALF, 1) * sin


def _rope_bwd(g, cos, sin):
    return g * cos + pltpu.roll(g * sin, 2 * ROPE_HALF, 1)


def _coords():
    return lax.axis_index("x"), lax.axis_index("y"), lax.axis_index("c")


def _lin(px, py, pc):
    return 4 * px + 2 * py + pc


def _gather_phases(ins, outs, ssem, rsem, lsem):
    n = len(ins)
    x, y, c = _coords()
    me, sibling = (x, y, c), (x, y, 1 - c)
    chips = [(1 - x, y), (x, 1 - y), (1 - x, 1 - y)]

    def copy(t, k, block, to, src=None):
        slot = outs[t].at[_lin(*block)]
        return pltpu.make_async_remote_copy(
            src_ref=slot if src is None else src, dst_ref=slot,
            send_sem=ssem.at[7 * t + k], recv_sem=rsem.at[7 * t + k],
            device_id=to, device_id_type=MESH)

    def mine(t):
        return pltpu.make_async_copy(ins[t], outs[t].at[_lin(*me)], lsem.at[t])

    def first(t):
        return [copy(t, 0, me, sibling, src=ins[t])] + [copy(t, 1 + j, me, (*chip, c), src=ins[t])
                                                       for j, chip in enumerate(chips)]

    def passed(t):
        return [copy(t, 4 + j, (*chip, c), sibling) for j, chip in enumerate(chips)]

    def start():
        for t in range(n):
            mine(t).start()
            for cp in first(t):
                cp.start()

    def middle():
        for t in range(n):
            fw = passed(t)
            for j, chip in enumerate(chips):
                copy(t, 1 + j, (*chip, c), me).wait_recv()
                fw[j].start()

    def finish():
        for t in range(n):
            copy(t, 0, sibling, me).wait_recv()
            for j, chip in enumerate(chips):
                copy(t, 4 + j, (*chip, 1 - c), me).wait_recv()
        for t in range(n):
            for cp in first(t) + passed(t):
                cp.wait_send()
            mine(t).wait()

    return start, middle, finish


def _exchange_phases(ins, outs, bcast, ssem, rsem, lsem):
    n = len(ins)
    x, y, c = _coords()
    me = _lin(x, y, c)
    peers = []
    for k in range(1, N_DEV):
        kx, ky, kc = (k >> 2) & 1, (k >> 1) & 1, k & 1
        peers.append((k, (1 - x if kx else x, 1 - y if ky else y, 1 - c if kc else c)))

    def own(t):
        return pltpu.make_async_copy(ins[t] if bcast[t] else ins[t].at[me], outs[t].at[me], lsem.at[t])

    def send(t, k, peer):
        return pltpu.make_async_remote_copy(
            src_ref=ins[t] if bcast[t] else ins[t].at[_lin(*peer)], dst_ref=outs[t].at[me],
            send_sem=ssem.at[7 * t + k - 1], recv_sem=rsem.at[7 * t + k - 1],
            device_id=peer, device_id_type=MESH)

    def arrival(t, k, peer):
        slot = outs[t].at[_lin(*peer)]
        return pltpu.make_async_remote_copy(
            src_ref=slot, dst_ref=slot, send_sem=ssem.at[7 * t + k - 1], recv_sem=rsem.at[7 * t + k - 1],
            device_id=peer, device_id_type=MESH)

    def start():
        for t in range(n):
            own(t).start()
            for k, peer in peers:
                send(t, k, peer).start()

    def finish():
        for t in range(n):
            for k, peer in peers:
                arrival(t, k, peer).wait_recv()
        for t in range(n):
            for k, peer in peers:
                send(t, k, peer).wait_send()
            own(t).wait()

    return start, finish


def _comm_sems(n):
    return [pltpu.SemaphoreType.DMA((7 * n,)), pltpu.SemaphoreType.DMA((7 * n,)), pltpu.SemaphoreType.DMA((n,))]


def _gathered(xs):
    return [jax.ShapeDtypeStruct((N_DEV,) + a.shape, a.dtype) for a in xs]


def all_gather(xs):
    n = len(xs)

    def body(*refs):
        start, middle, finish = _gather_phases(refs[:n], refs[n:2 * n], *refs[2 * n:])
        start()
        middle()
        finish()

    return pl.pallas_call(
        body, name="all_gather", out_shape=_gathered(xs),
        in_specs=[ANY] * n, out_specs=[ANY] * n, scratch_shapes=_comm_sems(n),
    )(*xs)


def exchange(xs, bcast):
    n = len(xs)

    def body(*refs):
        start, finish = _exchange_phases(refs[:n], refs[n:2 * n], bcast, *refs[2 * n:])
        start()
        finish()

    out_shape = [jax.ShapeDtypeStruct(((N_DEV,) + a.shape) if b else a.shape, a.dtype) for a, b in zip(xs, bcast)]
    return pl.pallas_call(
        body, name="grad_exchange", out_shape=out_shape,
        in_specs=[ANY] * n, out_specs=[ANY] * n, scratch_shapes=_comm_sems(n),
    )(*xs)


def mm(a, b, *, ta=False, tb=False, add=None, out_dtype=F32, tm=512, tn=1024, tk=None, name="mm"):
    K, M = a.shape if ta else a.shape[::-1]
    N = b.shape[0] if tb else b.shape[1]
    assert (b.shape[1] if tb else b.shape[0]) == K and not (ta and tb)
    if tk is None:
        tk = 512 if ta else 2048
    tm, tn, tk = _tile(M, tm), _tile(N, tn), _tile(K, tk)
    nk = K // tk
    has_add = add is not None

    def body(*refs):
        if has_add:
            a_ref, b_ref, add_ref, o_ref, acc = refs
        else:
            a_ref, b_ref, o_ref, acc = refs
        k = pl.program_id(2)

        @pl.when(k == 0)
        def _():
            acc[...] = jnp.zeros_like(acc)

        av = a_ref[...].astype(BF)
        bv = b_ref[...].astype(BF)
        if ta:
            acc[...] += _dot_tn(av, bv)
        elif tb:
            acc[...] += _dot_nt(av, bv)
        else:
            acc[...] += _dot(av, bv)

        @pl.when(k == nk - 1)
        def _():
            r = acc[...]
            if has_add:
                r = r + add_ref[...]
            o_ref[...] = r.astype(out_dtype)

    a_spec = pl.BlockSpec((tk, tm), lambda i, j, k: (k, i)) if ta else pl.BlockSpec((tm, tk), lambda i, j, k: (i, k))
    b_spec = pl.BlockSpec((tn, tk), lambda i, j, k: (j, k)) if tb else pl.BlockSpec((tk, tn), lambda i, j, k: (k, j))
    o_spec = pl.BlockSpec((tm, tn), lambda i, j, k: (i, j))
    in_specs = [a_spec, b_spec] + ([o_spec] if has_add else [])
    args = [a, b] + ([add] if has_add else [])
    return pl.pallas_call(
        body, name=name, grid=(M // tm, N // tn, nk),
        in_specs=in_specs, out_specs=o_spec,
        out_shape=jax.ShapeDtypeStruct((M, N), out_dtype),
        scratch_shapes=[pltpu.VMEM((tm, tn), F32)],
        compiler_params=_params(("parallel", "parallel", "arbitrary")),
    )(*args)


def norm_mm(x, gain, w, *, out_dtype, tn=512, rope=None, write_h=True, tm=512, name="norm_mm"):
    S, K = x.shape
    N = w.shape[1]
    tm, tn = _tile(S, tm), _tile(N, tn)
    if rope is not None:
        assert tn == LANES
    gain = gain.reshape(1, K)

    def body(*refs):
        refs = list(refs)
        x_ref, g_ref, w_ref = refs[:3]
        refs = refs[3:]
        if rope is not None:
            cos_ref, sin_ref = refs[:2]
            refs = refs[2:]
        y_ref = refs[0]
        h_ref = refs[1] if write_h else None
        hs = refs[-1]
        j = pl.program_id(1)

        @pl.when(j == 0)
        def _():
            h, _ = _rms(x_ref[...], g_ref[...])
            hb = h.astype(BF)
            hs[...] = hb
            if write_h:
                h_ref[...] = hb

        y = _dot(hs[...], w_ref[...])
        if rope is not None:
            y = _rope(y, cos_ref[...], sin_ref[...])
        y_ref[...] = y.astype(out_dtype)

    in_specs = [pl.BlockSpec((tm, K), lambda i, j: (i, 0)), pl.BlockSpec((1, K), lambda i, j: (0, 0)),
                pl.BlockSpec((K, tn), lambda i, j: (0, j))]
    args = [x, gain, w]
    if rope is not None:
        in_specs += [pl.BlockSpec((tm, LANES), lambda i, j: (i, 0))] * 2
        args += list(rope)
    out_specs = [pl.BlockSpec((tm, tn), lambda i, j: (i, j))]
    out_shape = [jax.ShapeDtypeStruct((S, N), out_dtype)]
    if write_h:
        out_specs.append(pl.BlockSpec((tm, K), lambda i, j: (i, 0)))
        out_shape.append(jax.ShapeDtypeStruct((S, K), BF))
    res = pl.pallas_call(
        body, name=name, grid=(S // tm, N // tn), in_specs=in_specs, out_specs=out_specs, out_shape=out_shape,
        scratch_shapes=[pltpu.VMEM((tm, K), BF)],
        compiler_params=_params(("parallel", "arbitrary")),
    )(*args)
    return res if write_h else res[0]


def norm_bwd(x, gain, dh, dres=None, *, tm=512, name="norm_bwd"):
    S, K = x.shape
    tm = _tile(S, tm)
    gain = gain.reshape(1, K)
    has_res = dres is not None

    def body(*refs):
        if has_res:
            x_ref, g_ref, dh_ref, dr_ref, dx_ref, dg_ref = refs
        else:
            x_ref, g_ref, dh_ref, dx_ref, dg_ref = refs

        @pl.when(pl.program_id(0) == 0)
        def _():
            dg_ref[...] = jnp.zeros_like(dg_ref)

        dx, dg = _rms_bwd(x_ref[...], g_ref[...], dh_ref[...].astype(F32))
        if has_res:
            dx = dx + dr_ref[...]
        dx_ref[...] = dx
        dg_ref[...] += jnp.broadcast_to(dg, dg_ref.shape)

    row = pl.BlockSpec((tm, K), lambda i: (i, 0))
    in_specs = [row, pl.BlockSpec((1, K), lambda i: (0, 0)), row] + ([row] if has_res else [])
    args = [x, gain, dh] + ([dres] if has_res else [])
    dx, dg = pl.pallas_call(
        body, name=name, grid=(S // tm,), in_specs=in_specs,
        out_specs=[row, pl.BlockSpec((8, K), lambda i: (0, 0))],
        out_shape=[jax.ShapeDtypeStruct((S, K), F32), jax.ShapeDtypeStruct((8, K), F32)],
        compiler_params=_params(("arbitrary",)),
    )(*args)
    return dx, dg[0]


def loss_head(x, gain, target, *, tm=512):
    S, K = x.shape
    tm = _tile(S, tm)
    gain = gain.reshape(1, K)

    def body(x_ref, g_ref, t_ref, dx_ref, dg_ref, ls_ref):
        @pl.when(pl.program_id(0) == 0)
        def _():
            dg_ref[...] = jnp.zeros_like(dg_ref)
            ls_ref[...] = jnp.zeros_like(ls_ref)

        xv, g = x_ref[...], g_ref[...]
        y, _ = _rms(xv, g)
        e = y - t_ref[...]
        part = jnp.sum(jnp.mean(e * e, axis=-1, keepdims=True), axis=0, keepdims=True)
        ls_ref[...] += jnp.broadcast_to(0.5 * part, ls_ref.shape)
        dx, dg = _rms_bwd(xv, g, e / K)
        dx_ref[...] = dx
        dg_ref[...] += jnp.broadcast_to(dg, dg_ref.shape)

    row = pl.BlockSpec((tm, K), lambda i: (i, 0))
    dx, dg, ls = pl.pallas_call(
        body, name="loss_head", grid=(S // tm,),
        in_specs=[row, pl.BlockSpec((1, K), lambda i: (0, 0)), row],
        out_specs=[row, pl.BlockSpec((8, K), lambda i: (0, 0)), pl.BlockSpec((8, LANES), lambda i: (0, 0))],
        out_shape=[jax.ShapeDtypeStruct((S, K), F32), jax.ShapeDtypeStruct((8, K), F32),
                   jax.ShapeDtypeStruct((8, LANES), F32)],
        compiler_params=_params(("arbitrary",)),
    )(x, gain, target)
    return ls[0, 0], dx, dg[0]


GATHER_TAIL = 3


def _once(shape, index_map):
    return pl.BlockSpec(shape, index_map, pipeline_mode=pl.Buffered(1))


def ffn_fwd(x, gain, wg, wu, wd, carry=(), *, tm=512):
    S, D = x.shape
    NB, _, Fs = wg.shape
    tm = _tile(S, tm)
    ni = S // tm
    nc = len(carry)
    gain = gain.reshape(1, D)

    def body(*refs):
        x_ref, g_ref, wg_ref, wu_ref, wd_ref = refs[:5]
        c_in = refs[5:5 + nc]
        xo_ref, h_ref, G_ref, U_ref = refs[5 + nc:9 + nc]
        c_out = refs[9 + nc:9 + 2 * nc]
        hs, acc = refs[9 + 2 * nc:11 + 2 * nc]
        i, j = pl.program_id(0), pl.program_id(1)
        if nc:
            start, middle, finish = _gather_phases(c_in, c_out, *refs[11 + 2 * nc:])
            pl.when((i == 0) & (j == 0))(start)
            pl.when((i == max(ni - GATHER_TAIL, ni // 2)) & (j == 0))(middle)

        @pl.when(j == 0)
        def _():
            h, _ = _rms(x_ref[...], g_ref[...])
            hb = h.astype(BF)
            hs[...] = hb
            h_ref[...] = hb
            acc[...] = jnp.zeros_like(acc)

        h = hs[...]
        g = _dot(h, wg_ref[...])
        u = _dot(h, wu_ref[...])
        G_ref[...] = g.astype(BF)
        U_ref[...] = u.astype(BF)
        a = (g * jax.nn.sigmoid(g) * u).astype(BF)
        acc[...] += _dot(a, wd_ref[...])

        @pl.when(j == NB - 1)
        def _():
            xo_ref[...] = x_ref[...] + 0.5 * acc[...]

        if nc:
            pl.when((i == ni - 1) & (j == NB - 1))(finish)

    row = lambda i, j: (i, 0)
    blk = lambda i, j: (j, 0, 0)
    hid = pl.BlockSpec((None, tm, Fs), lambda i, j: (j, i, 0))
    return pl.pallas_call(
        body, name="ffn_fwd_gather" if nc else "ffn_fwd", grid=(ni, NB),
        in_specs=[pl.BlockSpec((tm, D), row), pl.BlockSpec((1, D), lambda i, j: (0, 0)),
                  pl.BlockSpec((None, D, Fs), blk), pl.BlockSpec((None, D, Fs), blk), pl.BlockSpec((None, Fs, D), blk)]
                 + [ANY] * nc,
        out_specs=[pl.BlockSpec((tm, D), row), pl.BlockSpec((tm, D), row), hid, hid] + [ANY] * nc,
        out_shape=[jax.ShapeDtypeStruct((S, D), F32), jax.ShapeDtypeStruct((S, D), BF),
                   jax.ShapeDtypeStruct((NB, S, Fs), BF), jax.ShapeDtypeStruct((NB, S, Fs), BF)] + _gathered(carry),
        scratch_shapes=[pltpu.VMEM((tm, D), BF), pltpu.VMEM((tm, D), F32)] + (_comm_sems(nc) if nc else []),
        compiler_params=_params(("arbitrary", "arbitrary")),
    )(x, gain, wg, wu, wd, *carry)


def ffn_bwd_x(dy, x, gain, G, U, wg, wu, wd, carry=(), *, tm=512, sub=256):
    S, D = x.shape
    NB, _, Fs = wg.shape
    tm = _tile(S, tm)
    sub = _tile(tm, sub)
    ni = S // tm
    nc = len(carry)
    gain = gain.reshape(1, D)

    def body(*refs):
        dy_ref, x_ref, g_ref, G_ref, U_ref, wg_ref, wu_ref, wd_ref = refs[:8]
        c_in = refs[8:8 + nc]
        dx_ref, do_ref, dG_ref, dU_ref, dg_ref = refs[8 + nc:13 + nc]
        c_out = refs[13 + nc:13 + 2 * nc]
        dob, acc = refs[13 + 2 * nc:15 + 2 * nc]
        i, j = pl.program_id(0), pl.program_id(1)
        if nc:
            start, finish = _exchange_phases(c_in, c_out, [False] * nc, *refs[15 + 2 * nc:])
            pl.when((i == 0) & (j == 0))(start)

        @pl.when(j == 0)
        def _():
            d = (0.5 * dy_ref[...]).astype(BF)
            dob[...] = d
            do_ref[...] = d
            acc[...] = jnp.zeros_like(acc)

        @pl.when((i == 0) & (j == 0))
        def _():
            dg_ref[...] = jnp.zeros_like(dg_ref)

        for r in range(tm // sub):
            rows = slice(r * sub, (r + 1) * sub)
            dA = _dot_nt(dob[rows, :], wd_ref[...])
            g = G_ref[rows, :].astype(F32)
            u = U_ref[rows, :].astype(F32)
            sig = jax.nn.sigmoid(g)
            dG = (dA * u * (sig * (1.0 + g * (1.0 - sig)))).astype(BF)
            dU = (dA * (g * sig)).astype(BF)
            dG_ref[rows, :] = dG
            dU_ref[rows, :] = dU
            acc[rows, :] += _dot_nt(dG, wg_ref[...]) + _dot_nt(dU, wu_ref[...])

        @pl.when(j == NB - 1)
        def _():
            dxn, dg = _rms_bwd(x_ref[...], g_ref[...], acc[...])
            dx_ref[...] = dy_ref[...] + dxn
            dg_ref[...] += jnp.broadcast_to(dg, dg_ref.shape)

        if nc:
            pl.when((i == ni - 1) & (j == NB - 1))(finish)

    row = lambda i, j: (i, 0)
    blk = lambda i, j: (j, 0, 0)
    hid = pl.BlockSpec((None, tm, Fs), lambda i, j: (j, i, 0))
    dx, dout, dG, dU, dg, *got = pl.pallas_call(
        body, name="ffn_bwd_x_exchange" if nc else "ffn_bwd_x", grid=(ni, NB),
        in_specs=[_once((tm, D), row), _once((tm, D), row), pl.BlockSpec((1, D), lambda i, j: (0, 0)), hid, hid,
                  pl.BlockSpec((None, D, Fs), blk), pl.BlockSpec((None, D, Fs), blk), pl.BlockSpec((None, Fs, D), blk)]
                 + [ANY] * nc,
        out_specs=[_once((tm, D), row), _once((tm, D), row), hid, hid, pl.BlockSpec((8, D), lambda i, j: (0, 0))]
                  + [ANY] * nc,
        out_shape=[jax.ShapeDtypeStruct((S, D), F32), jax.ShapeDtypeStruct((S, D), BF),
                   jax.ShapeDtypeStruct((NB, S, Fs), BF), jax.ShapeDtypeStruct((NB, S, Fs), BF),
                   jax.ShapeDtypeStruct((8, D), F32)] + [jax.ShapeDtypeStruct(a.shape, a.dtype) for a in carry],
        scratch_shapes=[pltpu.VMEM((tm, D), BF), pltpu.VMEM((tm, D), F32)] + (_comm_sems(nc) if nc else []),
        compiler_params=_params(("arbitrary", "arbitrary")),
    )(dy, x, gain, G, U, wg, wu, wd, *carry)
    return dx, dout, dG, dU, dg[0], got


def ffn_bwd_w(h, dout, G, U, dG, dU, carry=(), *, tm=512):
    S, D = h.shape
    NB, _, Fs = G.shape
    tm = _tile(S, tm)
    ni = S // tm
    nc = len(carry)

    def body(*refs):
        h_ref, do_ref, G_ref, U_ref, dG_ref, dU_ref = refs[:6]
        c_in = refs[6:6 + nc]
        wg_ref, wu_ref, wd_ref = refs[6 + nc:9 + nc]
        c_out = refs[9 + nc:9 + 2 * nc]
        ag, au, ad = refs[9 + 2 * nc:12 + 2 * nc]
        j, i = pl.program_id(0), pl.program_id(1)
        if nc:
            start, finish = _exchange_phases(c_in, c_out, [False] * nc, *refs[12 + 2 * nc:])
            pl.when((j == 0) & (i == 0))(start)

        @pl.when(i == 0)
        def _():
            ag[...] = jnp.zeros_like(ag)
            au[...] = jnp.zeros_like(au)
            ad[...] = jnp.zeros_like(ad)

        h = h_ref[...]
        ag[...] += _dot_tn(h, dG_ref[...])
        au[...] += _dot_tn(h, dU_ref[...])
        g = G_ref[...].astype(F32)
        a = (g * jax.nn.sigmoid(g) * U_ref[...].astype(F32)).astype(BF)
        ad[...] += _dot_tn(a, do_ref[...])

        @pl.when(i == ni - 1)
        def _():
            wg_ref[...] = ag[...].astype(BF)
            wu_ref[...] = au[...].astype(BF)
            wd_ref[...] = ad[...].astype(BF)

        if nc:
            pl.when((j == NB - 1) & (i == ni - 1))(finish)

    row = pl.BlockSpec((tm, D), lambda j, i: (i, 0))
    hid = pl.BlockSpec((None, tm, Fs), lambda j, i: (j, i, 0))
    blk = lambda j, i: (j, 0, 0)
    dwg, dwu, dwd, *got = pl.pallas_call(
        body, name="ffn_bwd_w_exchange" if nc else "ffn_bwd_w", grid=(NB, ni),
        in_specs=[row, row, hid, hid, hid, hid] + [ANY] * nc,
        out_specs=[_once((None, D, Fs), blk), _once((None, D, Fs), blk), _once((None, Fs, D), blk)] + [ANY] * nc,
        out_shape=[jax.ShapeDtypeStruct((NB, D, Fs), BF), jax.ShapeDtypeStruct((NB, D, Fs), BF),
                   jax.ShapeDtypeStruct((NB, Fs, D), BF)] + [jax.ShapeDtypeStruct(a.shape, a.dtype) for a in carry],
        scratch_shapes=[pltpu.VMEM((D, Fs), F32), pltpu.VMEM((D, Fs), F32), pltpu.VMEM((Fs, D), F32)]
                       + (_comm_sems(nc) if nc else []),
        compiler_params=_params(("arbitrary", "arbitrary")),
    )(h, dout, G, U, dG, dU, *carry)
    return [dwg, dwu, dwd], got


def dilated_bias(H, d):
    n = DIL_N
    slopes = 2.0 ** (-8.0 * (np.arange(H) + 1) / H)
    i = np.arange(n)[:, None]
    j = np.arange(2 * n)[None, :]
    steps = n + i - j
    band = (steps >= 0) & (steps <= n)
    first = band & (j >= n)
    bias = -slopes[:, None, None] * (d * steps).astype(np.float64)[None]
    out = np.stack([np.where(band[None], bias, NEG), np.where(first[None], bias, NEG)], axis=1)
    return jnp.asarray(out, dtype=F32)


def head_delta(do, o, *, tm=512):
    S, W = o.shape
    tm = _tile(S, tm)

    def body(do_ref, o_ref, d_ref):
        for h in range(W // HEAD):
            cols = slice(h * HEAD, (h + 1) * HEAD)
            s = jnp.sum(do_ref[:, cols].astype(F32) * o_ref[:, cols].astype(F32), axis=1, keepdims=True)
            d_ref[:, cols] = jnp.broadcast_to(s, (tm, HEAD))

    blk = pl.BlockSpec((tm, W), lambda i: (i, 0))
    return pl.pallas_call(
        body, name="head_delta", grid=(S // tm,), in_specs=[blk, blk], out_specs=blk,
        out_shape=jax.ShapeDtypeStruct((S, W), F32),
        compiler_params=_params(("parallel",)),
    )(do, o)


DIL_CHUNK = DIL_N * max(BRANCH_DILATIONS)
DIL_UNROLL = 4


def _rows(start, size, d):
    return pl.ds(pl.multiple_of(start, DIL_N), size) if d == 1 else pl.ds(start, size, stride=d)


def dil_attn_fwd(qkv, biases, H):
    S = qkv.shape[0]
    C = DIL_CHUNK
    assert S % C == 0
    scale = HEAD ** -0.5
    nbr = len(BRANCH_DILATIONS)

    def body(*refs):
        q_ref, kc_ref, kp_ref, vc_ref, vp_ref = refs[:5]
        b_refs = refs[5:5 + nbr]
        o_ref, L_ref, qf, kf, vf = refs[5 + nbr:10 + nbr]
        ogs = refs[10 + nbr:10 + 2 * nbr]
        lgs = refs[10 + 2 * nbr:10 + 3 * nbr]
        c = pl.program_id(1)
        qf[...] = q_ref[...].astype(F32)
        kf[0:C, :] = kp_ref[...].astype(F32)
        kf[C:2 * C, :] = kc_ref[...].astype(F32)
        vf[0:C, :] = vp_ref[...].astype(F32)
        vf[C:2 * C, :] = vc_ref[...].astype(F32)
        for d, b_ref, og, lg in zip(BRANCH_DILATIONS, b_refs, ogs, lgs):
            span = DIL_N * d

            def block(t, carry, d=d, b_ref=b_ref, og=og, lg=lg, span=span):
                r, b = t % d, t // d
                q0 = b * span + r
                q = qf[_rows(q0, DIL_N, d), :].astype(BF)
                k2 = kf[_rows(C + q0 - span, 2 * DIL_N, d), :].astype(BF)
                v2 = vf[_rows(C + q0 - span, 2 * DIL_N, d), :].astype(BF)
                bias = jnp.where((c == 0) & (b == 0), b_ref[1], b_ref[0])
                s = _dot_nt(q, k2) * scale + bias
                m = jnp.max(s, axis=1, keepdims=True)
                p = jnp.exp(s - m)
                l = jnp.sum(p, axis=1, keepdims=True)
                og[_rows(q0, DIL_N, d), :] = _dot(p.astype(BF), v2) / l
                lg[_rows(q0, DIL_N, d), :] = jnp.broadcast_to(m + jnp.log(l), (DIL_N, LANES))
                return carry

            lax.fori_loop(0, C // DIL_N, block, 0, unroll=DIL_UNROLL)
        ls = [lg[...] for lg in lgs]
        m = functools.reduce(jnp.maximum, ls)
        es = [jnp.exp(l - m) for l in ls]
        z = functools.reduce(jnp.add, es)
        mix = functools.reduce(jnp.add, [e * og[...] for e, og in zip(es, ogs)])
        o_ref[...] = (mix / z).astype(BF)
        L_ref[...] = m + jnp.log(z)

    cur = lambda off: pl.BlockSpec((C, HEAD), lambda h, c: (c, off * H + h))
    prv = lambda off: pl.BlockSpec((C, HEAD), lambda h, c: (jnp.maximum(c - 1, 0), off * H + h))
    tab = pl.BlockSpec((None, 2, DIL_N, 2 * DIL_N), lambda h, c: (h, 0, 0, 0))
    out = pl.BlockSpec((C, HEAD), lambda h, c: (c, h))
    return pl.pallas_call(
        body, name="dil_attn_fwd", grid=(H, S // C),
        in_specs=[cur(0), cur(1), prv(1), cur(2), prv(2)] + [tab] * nbr,
        out_specs=[out, out],
        out_shape=[jax.ShapeDtypeStruct((S, H * HEAD), BF), jax.ShapeDtypeStruct((S, H * HEAD), F32)],
        scratch_shapes=[pltpu.VMEM((C, HEAD), F32), pltpu.VMEM((2 * C, HEAD), F32), pltpu.VMEM((2 * C, HEAD), F32)]
                       + [pltpu.VMEM((C, HEAD), F32)] * (2 * nbr),
        compiler_params=_params(("parallel", "arbitrary")),
    )(qkv, qkv, qkv, qkv, qkv, *biases)


def dil_attn_bwd(qkv, do, L, delta, biases, H):
    S = qkv.shape[0]
    C = DIL_CHUNK
    assert S % C == 0
    nc = S // C
    scale = HEAD ** -0.5
    nbr = len(BRANCH_DILATIONS)

    def body(*refs):
        (qc_ref, qn_ref, kc_ref, kp_ref, vc_ref, vp_ref, doc_ref, don_ref, Lc_ref, Ln_ref, dlc_ref, dln_ref) = refs[:12]
        b_refs = refs[12:12 + nbr]
        (dq_ref, dk_ref, dv_ref, qf, dof, Lf, dlf, kf, vf, dq_acc, dk_acc, dv_acc,
         dq_g, dka_g, dva_g, dkb_g, dvb_g) = refs[12 + nbr:]
        c = pl.program_id(1)
        for buf, a_ref, b_ref in ((qf, qc_ref, qn_ref), (dof, doc_ref, don_ref), (Lf, Lc_ref, Ln_ref),
                                  (dlf, dlc_ref, dln_ref), (kf, kp_ref, kc_ref), (vf, vp_ref, vc_ref)):
            buf[0:C, :] = a_ref[...].astype(F32)
            buf[C:2 * C, :] = b_ref[...].astype(F32)
        for g_idx, (d, b_ref) in enumerate(zip(BRANCH_DILATIONS, b_refs)):
            span = DIL_N * d
            dkb_g[C:2 * C, :] = jnp.zeros((C, HEAD), F32)
            dvb_g[C:2 * C, :] = jnp.zeros((C, HEAD), F32)

            def block(t, carry, d=d, b_ref=b_ref, span=span):
                r, b = t % d, t // d
                q0 = b * span + r
                qrows = _rows(q0, DIL_N, d)
                krows = _rows(C + q0 - span, 2 * DIL_N, d)
                q, dov = qf[qrows, :].astype(BF), dof[qrows, :].astype(BF)
                k2, v2 = kf[krows, :].astype(BF), vf[krows, :].astype(BF)
                Lq, dl = Lf[qrows, :], dlf[qrows, :]
                bias = jnp.where((c == 0) & (b == 0), b_ref[1], b_ref[0])
                s = _dot_nt(q, k2) * scale + bias
                p = jnp.exp(s - jnp.concatenate([Lq, Lq], axis=1))
                ds = p * (_dot_nt(dov, v2) - jnp.concatenate([dl, dl], axis=1))
                dsb, pb = ds.astype(BF), p.astype(BF)
                dq_g[qrows, :] = _dot(dsb, k2) * scale
                dk2 = _dot_tn(dsb, q) * scale
                dv2 = _dot_tn(pb, dov)
                dka_g[qrows, :] = dk2[DIL_N:]
                dva_g[qrows, :] = dv2[DIL_N:]
                prows = _rows(C + q0 - span, DIL_N, d)
                dkb_g[prows, :] = dk2[:DIL_N]
                dvb_g[prows, :] = dv2[:DIL_N]
                return carry

            lax.fori_loop(0, C // DIL_N, block, 0, unroll=DIL_UNROLL)

            @pl.when(c < nc - 1)
            def _(d=d, b_ref=b_ref, span=span):
                def nxt(r, carry):
                    qrows = _rows(C + r, DIL_N, d)
                    krows = _rows(2 * C - span + r, DIL_N, d)
                    q, dov = qf[qrows, :].astype(BF), dof[qrows, :].astype(BF)
                    k1, v1 = kf[krows, :].astype(BF), vf[krows, :].astype(BF)
                    s = _dot_nt(q, k1) * scale + b_ref[0][:, :DIL_N]
                    p = jnp.exp(s - Lf[qrows, :])
                    ds = p * (_dot_nt(dov, v1) - dlf[qrows, :])
                    dkb_g[krows, :] = _dot_tn(ds.astype(BF), q) * scale
                    dvb_g[krows, :] = _dot_tn(p.astype(BF), dov)
                    return carry

                lax.fori_loop(0, d, nxt, 0, unroll=min(d, DIL_UNROLL))

            if g_idx == 0:
                dq_acc[...] = dq_g[...]
                dk_acc[...] = dka_g[...] + dkb_g[C:2 * C, :]
                dv_acc[...] = dva_g[...] + dvb_g[C:2 * C, :]
            else:
                dq_acc[...] += dq_g[...]
                dk_acc[...] += dka_g[...] + dkb_g[C:2 * C, :]
                dv_acc[...] += dva_g[...] + dvb_g[C:2 * C, :]

        dq_ref[...] = dq_acc[...].astype(BF)
        dk_ref[...] = dk_acc[...].astype(BF)
        dv_ref[...] = dv_acc[...].astype(BF)

    cur3 = lambda off: pl.BlockSpec((C, HEAD), lambda h, c: (c, off * H + h))
    prv3 = lambda off: pl.BlockSpec((C, HEAD), lambda h, c: (jnp.maximum(c - 1, 0), off * H + h))
    nxt3 = lambda off: pl.BlockSpec((C, HEAD), lambda h, c: (jnp.minimum(c + 1, nc - 1), off * H + h))
    cur1 = pl.BlockSpec((C, HEAD), lambda h, c: (c, h))
    nxt1 = pl.BlockSpec((C, HEAD), lambda h, c: (jnp.minimum(c + 1, nc - 1), h))
    tab = pl.BlockSpec((None, 2, DIL_N, 2 * DIL_N), lambda h, c: (h, 0, 0, 0))
    return pl.pallas_call(
        body, name="dil_attn_bwd", grid=(H, nc),
        in_specs=[cur3(0), nxt3(0), cur3(1), prv3(1), cur3(2), prv3(2), cur1, nxt1, cur1, nxt1, cur1, nxt1] + [tab] * nbr,
        out_specs=[cur1] * 3,
        out_shape=[jax.ShapeDtypeStruct((S, H * HEAD), BF)] * 3,
        scratch_shapes=[pltpu.VMEM((2 * C, HEAD), F32)] * 6 + [pltpu.VMEM((C, HEAD), F32)] * 6
                       + [pltpu.VMEM((2 * C, HEAD), F32)] * 2,
        compiler_params=_params(("parallel", "arbitrary")),
    )(qkv, qkv, qkv, qkv, qkv, qkv, do, do, L, L, delta, delta, *biases)


def _causal_mask(s, qi, ki, tq, tk, row0=0):
    qpos = qi * tq + row0 + lax.broadcasted_iota(jnp.int32, s.shape, 0)
    kpos = ki * tk + lax.broadcasted_iota(jnp.int32, s.shape, 1)
    return jnp.where(kpos <= qpos, s, NEG)


def mla_fwd(qn, qr, kv, kr, H, *, tq=1024, tk=1024, sub=512):
    S = qn.shape[0]
    tq, tk = _tile(S, tq), _tile(S, tk)
    sub = _tile(tq, sub)
    nk = S // tk
    scale = MLA_SCALE
    c2 = scale * LOG2E

    def body(qn_ref, qr_ref, kn_ref, kr_ref, v_ref, o_ref, L_ref, m_s, l_s, acc):
        qi, ki = pl.program_id(1), pl.program_id(2)

        @pl.when(ki == 0)
        def _():
            m_s[...] = jnp.full_like(m_s, NEG)
            l_s[...] = jnp.zeros_like(l_s)
            acc[...] = jnp.zeros_like(acc)

        def step(masked):
            k = jnp.concatenate([kn_ref[...], kr_ref[...]], axis=1)
            v = v_ref[...]
            for r in range(tq // sub):
                rows = slice(r * sub, (r + 1) * sub)
                q = jnp.concatenate([qn_ref[rows, :], qr_ref[rows, :]], axis=1)
                s = _dot_nt(q, k)
                if masked:
                    s = _causal_mask(s, qi, ki, tq, tk, r * sub)
                m_prev = m_s[rows, :]
                m_new = jnp.maximum(m_prev, jnp.max(s, axis=1, keepdims=True))
                alpha = jnp.exp2((m_prev - m_new) * c2)
                p = jnp.exp2((s - jnp.tile(m_new, (1, tk // LANES))) * c2)
                l_s[rows, :] = alpha * l_s[rows, :] + jnp.sum(p, axis=1, keepdims=True)
                acc[rows, :] = alpha * acc[rows, :] + _dot(p.astype(BF), v)
                m_s[rows, :] = m_new

        full = ki * tk + tk - 1 <= qi * tq
        live = ki * tk <= qi * tq + tq - 1
        pl.when(full)(lambda: step(False))
        pl.when(live & jnp.logical_not(full))(lambda: step(True))

        @pl.when(ki == nk - 1)
        def _():
            o_ref[...] = (acc[...] / l_s[...]).astype(BF)
            L_ref[...] = m_s[...] * scale + jnp.log(l_s[...])

    kcl = lambda qi, ki: jnp.minimum(ki, (qi * tq + tq - 1) // tk)
    qs = pl.BlockSpec((tq, HEAD), lambda h, qi, ki: (qi, h))
    return pl.pallas_call(
        body, name="mla_fwd", grid=(H, S // tq, nk),
        in_specs=[qs, qs, pl.BlockSpec((tk, HEAD), lambda h, qi, ki: (kcl(qi, ki), h)),
                  pl.BlockSpec((tk, LANES), lambda h, qi, ki: (kcl(qi, ki), 0)),
                  pl.BlockSpec((tk, HEAD), lambda h, qi, ki: (kcl(qi, ki), H + h))],
        out_specs=[qs, qs],
        out_shape=[jax.ShapeDtypeStruct((S, H * HEAD), BF), jax.ShapeDtypeStruct((S, H * HEAD), F32)],
        scratch_shapes=[pltpu.VMEM((tq, LANES), F32), pltpu.VMEM((tq, LANES), F32), pltpu.VMEM((tq, HEAD), F32)],
        compiler_params=_params(("parallel", "parallel", "arbitrary")),
    )(qn, qr, kv, kr, kv)


def _mla_probs(s, L_rows, c2, width):
    return jnp.exp2(s * c2 - jnp.tile(L_rows * LOG2E, (1, width // LANES)))


def mla_bwd(qn, qr, kv, kr, do, L, delta, rope, H, prev=None, *, tq=1024, tk=1024, sub=512):
    S = qn.shape[0]
    tq, tk = _tile(S, tq), _tile(S, tk)
    sub = _tile(tq, sub)
    nq, nk = S // tq, S // tk
    scale = MLA_SCALE
    c2 = scale * LOG2E
    has_prev = prev is not None

    def body(*refs):
        (qn_ref, qr_ref, kn_ref, kr_ref, v_ref, do_ref, L_ref, dl_ref, cq_ref, sq_ref, ck_ref, sk_ref) = refs[:12]
        refs = refs[12:]
        if has_prev:
            pk_ref, pv_ref, pr_ref = refs[:3]
            refs = refs[3:]
        dqn_ref, dqr_ref, dk_ref, dv_ref, dr_ref, dq_full, dk_acc, dv_acc = refs
        h, ki, qi = pl.program_id(0), pl.program_id(1), pl.program_id(2)
        qrows = pl.ds(pl.multiple_of(qi * tq, tq), tq)
        krows = pl.ds(pl.multiple_of(ki * tk, tk), tk)

        @pl.when(qi == 0)
        def _():
            dk_acc[...] = jnp.zeros_like(dk_acc)
            dv_acc[...] = jnp.zeros_like(dv_acc)

        @pl.when((qi == 0) & (h == 0))
        def _():
            dr_ref[krows, :] = jnp.zeros((tk, LANES), F32)

        @pl.when(ki == 0)
        def _():
            dq_full[qrows, :] = jnp.zeros((tq, 2 * HEAD), F32)

        def step(masked):
            k = jnp.concatenate([kn_ref[...], kr_ref[...]], axis=1)
            v = v_ref[...]
            for r in range(tq // sub):
                rows = slice(r * sub, (r + 1) * sub)
                q = jnp.concatenate([qn_ref[rows, :], qr_ref[rows, :]], axis=1)
                s = _dot_nt(q, k)
                if masked:
                    s = _causal_mask(s, qi, ki, tq, tk, r * sub)
                p = _mla_probs(s, L_ref[rows, :], c2, tk)
                dov = do_ref[rows, :]
                dv_acc[...] += _dot_tn(p.astype(BF), dov)
                dp = _dot_nt(dov, v)
                ds = (p * (dp - jnp.tile(dl_ref[rows, :], (1, tk // LANES)))).astype(BF)
                dk_acc[...] += _dot_tn(ds, q)
                dq_full[pl.ds(pl.multiple_of(qi * tq + r * sub, sub), sub), :] += _dot(ds, k)

        full = ki * tk + tk - 1 <= qi * tq
        live = ki * tk <= qi * tq + tq - 1
        pl.when(full)(lambda: step(False))
        pl.when(live & jnp.logical_not(full))(lambda: step(True))

        @pl.when(ki == (qi * tq + tq - 1) // tk)
        def _():
            dq = dq_full[qrows, :] * scale
            dqn_ref[qrows, :] = dq[:, :HEAD].astype(BF)
            dqr_ref[qrows, :] = _rope_bwd(dq[:, HEAD:], cq_ref[...], sq_ref[...]).astype(BF)

        @pl.when(qi == nq - 1)
        def _():
            dk = dk_acc[...] * scale
            dkn, dv = dk[:, :HEAD], dv_acc[...]
            if has_prev:
                dkn, dv = dkn + pk_ref[...], dv + pv_ref[...]
            dk_ref[...] = dkn
            dv_ref[...] = dv
            dr_ref[krows, :] += dk[:, HEAD:]

        @pl.when((qi == nq - 1) & (h == H - 1))
        def _():
            dr = _rope_bwd(dr_ref[krows, :], ck_ref[...], sk_ref[...])
            dr_ref[krows, :] = dr + pr_ref[...] if has_prev else dr

    qcl = lambda ki, qi: jnp.maximum(qi, (ki * tk) // tq)
    qs = pl.BlockSpec((tq, HEAD), lambda h, ki, qi: (qcl(ki, qi), h))
    qt = pl.BlockSpec((tq, LANES), lambda h, ki, qi: (qcl(ki, qi), 0))
    kn = pl.BlockSpec((tk, HEAD), lambda h, ki, qi: (ki, h))
    vs = pl.BlockSpec((tk, HEAD), lambda h, ki, qi: (ki, H + h))
    k1 = pl.BlockSpec((tk, LANES), lambda h, ki, qi: (ki, 0))
    head = pl.BlockSpec((S, HEAD), lambda h, ki, qi: (0, h))
    in_specs = [qs, qs, kn, k1, vs, qs, qs, qs, qt, qt, k1, k1]
    args = [qn, qr, kv, kr, kv, do, L, delta, *rope, *rope]
    if has_prev:
        in_specs += [kn, vs, k1]
        args += [prev[0], prev[0], prev[1]]
    return pl.pallas_call(
        body, name="mla_bwd", grid=(H, nk, nq), in_specs=in_specs,
        out_specs=[head, head, kn, kn, pl.BlockSpec((S, LANES), lambda h, ki, qi: (0, 0))],
        out_shape=[jax.ShapeDtypeStruct((S, H * HEAD), BF), jax.ShapeDtypeStruct((S, H * HEAD), BF),
                   jax.ShapeDtypeStruct((S, H * HEAD), F32), jax.ShapeDtypeStruct((S, H * HEAD), F32),
                   jax.ShapeDtypeStruct((S, LANES), F32)],
        scratch_shapes=[pltpu.VMEM((S, 2 * HEAD), F32), pltpu.VMEM((tk, 2 * HEAD), F32), pltpu.VMEM((tk, HEAD), F32)],
        compiler_params=_params(("arbitrary", "arbitrary", "arbitrary")),
    )(*args)


def adamw(recvs, w, m, v, *, name="adamw"):
    R, C = w.shape
    n = len(recvs)
    R1 = R // n
    assert all(r.shape == (N_DEV, R1, C) for r in recvs)
    tr = 16
    while tr * 2 * C * 44 <= 6 * 1024 * 1024 and R1 % (tr * 2) == 0:
        tr *= 2
    tr = min(tr, R1)
    assert R1 % tr == 0
    nb = R1 // tr

    def body(*refs):
        r_refs = refs[:n]
        w_ref, m_ref, v_ref, g_ref, d_ref, mo_ref, vo_ref = refs[n:]
        for slab in range(n):
            @pl.when(pl.program_id(0) == slab)
            def _(r_ref=r_refs[slab]):
                g = r_ref[0].astype(F32)
                for s in range(1, N_DEV):
                    g = g + r_ref[s].astype(F32)
                m2 = ADAM_B1 * m_ref[...] + (1.0 - ADAM_B1) * g
                v2 = ADAM_B2 * v_ref[...] + (1.0 - ADAM_B2) * (g * g)
                m_hat = m2 / (1.0 - ADAM_B1 ** ADAM_STEP)
                v_hat = v2 / (1.0 - ADAM_B2 ** ADAM_STEP)
                g_ref[...] = g
                d_ref[...] = -ADAM_LR * (m_hat / (jnp.sqrt(v_hat) + ADAM_EPS) + ADAM_WD * w_ref[...])
                mo_ref[...] = m2
                vo_ref[...] = v2

    def recv_spec(slab):
        return pl.BlockSpec((N_DEV, tr, C), lambda l, i: (0, jnp.where(l == slab, i, jnp.where(l < slab, 0, nb - 1)), 0))

    row = pl.BlockSpec((tr, C), lambda l, i: (l * nb + i, 0))
    return pl.pallas_call(
        body, name=name, grid=(n, nb),
        in_specs=[recv_spec(slab) for slab in range(n)] + [row, row, row],
        out_specs=[row] * 4, out_shape=[jax.ShapeDtypeStruct((R, C), F32)] * 4,
        compiler_params=_params(("arbitrary", "arbitrary")),
    )(*recvs, w, m, v)


def _pack(ts, width, dtype, row_mult=16):
    flat = jnp.concatenate([t.astype(dtype).reshape(-1) for t in ts])
    n = flat.shape[0]
    rows = -(-n // (width * row_mult)) * row_mult
    return jnp.pad(flat, (0, rows * width - n)).reshape(rows, width)


def _unpack(buf, shapes):
    lead = buf.shape[:-2]
    flat = buf.reshape(lead + (-1,))
    out, off = [], 0
    for s in shapes:
        n = int(np.prod(s))
        out.append(flat[..., off:off + n].reshape(lead + tuple(s)))
        off += n
    return out


def _rope_pad(r):
    z = jnp.zeros(r.shape[:-1] + (ROPE_HALF,), r.dtype)
    return jnp.concatenate([r[..., :ROPE_HALF], z, r[..., ROPE_HALF:], z], axis=-1)


def _rope_unpad(r):
    return jnp.concatenate([r[..., :ROPE_HALF], r[..., 2 * ROPE_HALF:3 * ROPE_HALF]], axis=-1)


def _step(P):
    x0 = P['x'][0]
    target = P['loss_target'][0]
    S, D = x0.shape
    NL = P['ffn_norm1'].shape[0]
    NA = P['a_wqkv'].shape[0]
    Fs = P['ffn1_wg'].shape[2]
    H = D // HEAD
    KV = P['b_wdkv'].shape[1]
    QL = P['b_wdq'].shape[2]
    HW = H * HEAD

    ffn_seq = [(l, f) for l in range(NL) for f in (0, 1)]
    ffn_loc = {(l, f): [P[n][l].astype(BF) for n in (f'ffn{f + 1}_wg', f'ffn{f + 1}_wu', f'ffn{f + 1}_wd')]
               for l, f in ffn_seq}
    assert 1 <= NA < NL
    groups = {('A', l): [('a_wqkv', l), ('a_wo', l)] for l in range(NA)}
    groups['B'] = [(n, None) for n in MIX_W if n.startswith('b_')]
    pick = lambda pre, n, l: P[pre + n] if l is None else P[pre + n][l]
    pack_group = lambda gk, pre, dtype: _pack([pick(pre, n, l) for n, l in groups[gk]], D, dtype)
    group_shapes = {gk: [pick('', n, l).shape for n, l in members] for gk, members in groups.items()}
    group_loc = {gk: pack_group(gk, '', BF) for gk in groups}
    gather_at = {(0, 0): [('A', 0)]}
    for l in range(1, NA):
        gather_at.setdefault((l - 1, 1), []).append(('A', l))
    gather_at.setdefault((NA - 1, 0), []).append('B')
    send_at = {(l, 0): [('A', l)] for l in range(NA)}
    send_at.setdefault((NA - 1, 1), []).append('B')
    gathered = {}
    ffn_w = {ffn_seq[0]: all_gather(ffn_loc[ffn_seq[0]])}

    def ffn_forward(x, gain, key):
        nxt = ffn_seq.index(key) + 1
        gks = gather_at.get(key, [])
        carry = (ffn_loc[ffn_seq[nxt]] if nxt < len(ffn_seq) else []) + [group_loc[gk] for gk in gks]
        xo, h, G, U, *got = ffn_fwd(x, gain, *ffn_w[key], carry=carry)
        if nxt < len(ffn_seq):
            ffn_w[ffn_seq[nxt]] = got[:3]
        for gk, arr in zip(gks, got[len(got) - len(gks):]):
            gathered[gk] = {n: t for (n, _), t in zip(groups[gk], _unpack(arr, group_shapes[gk]))}
        return xo, h, G, U

    cols = lambda t: jnp.moveaxis(t, 0, -2).reshape(t.shape[1:-1] + (N_DEV * t.shape[-1],))
    rows = lambda t, lead: jnp.moveaxis(t, 0, lead).reshape(t.shape[1:1 + lead] + (N_DEV * t.shape[1 + lead],) + t.shape[2 + lead:])
    a_full = {}

    def a_weights(l):
        if l not in a_full:
            g = gathered[('A', l)]
            a_full[l] = (cols(g['a_wqkv']), rows(g['a_wo'], 0))
        return a_full[l]

    inv = 1.0 / (ROPE_THETA ** (jnp.arange(0, 2 * ROPE_HALF, 2, dtype=F32) / (2 * ROPE_HALF)))
    ang = jnp.arange(S, dtype=F32)[:, None] * inv[None, :]
    z = jnp.zeros((S, ROPE_HALF), F32)
    rope = (jnp.concatenate([jnp.cos(ang), z, jnp.cos(ang), z], axis=1),
            jnp.concatenate([-jnp.sin(ang), z, jnp.sin(ang), z], axis=1))
    biases = [dilated_bias(H, d) for d in BRANCH_DILATIONS]

    saved = []
    kvs = None
    x = x0
    for l in range(NL):
        st = {'x_in': x}
        if l == NA:
            g = gathered['B']
            b_wo = rows(g['b_wo'], 1)
            wdkv = rows(g['b_wdkv'], 0)
            wkr = _rope_pad(rows(g['b_wkr'], 0))
            wkv = jnp.concatenate([rows(g['b_wuk'], 0).reshape(KV, HW), rows(g['b_wuv'], 0).reshape(KV, HW)], axis=1)
            wdq = rows(g['b_wdq'], 1)
            wuq = rows(g['b_wuq'], 1)
            wuq_n = wuq[..., :HEAD].reshape(-1, QL, HW)
            wuq_r = _rope_pad(wuq[..., HEAD:]).reshape(-1, QL, HW)
            ckv_pre, hkv = norm_mm(x, P['kv_norm'], wdkv, out_dtype=F32, name="kv_down")
            kr = norm_mm(x, P['kv_norm'], wkr, out_dtype=BF, tn=LANES, rope=rope, write_h=False, name="kv_rope")
            kvm, ckv = norm_mm(ckv_pre, P['b_ckv_norm'], wkv, out_dtype=BF, name="kv_up")
            kvs = dict(x=x, ckv_pre=ckv_pre, hkv=hkv, kr=kr, kv=kvm, ckv=ckv)
        xa, h1, G1, U1 = ffn_forward(x, P['ffn_norm1'][l], (l, 0))
        st.update(h1=h1, G1=G1, U1=U1, xa=xa)
        if l < NA:
            w_qkv, w_o = a_weights(l)
            qkv, hm = norm_mm(xa, P['mix_norm'][l], w_qkv, out_dtype=BF, tn=3 * HW // 4, name="a_qkv")
            o, Lj = dil_attn_fwd(qkv, biases, H)
            xb = mm(o, w_o, add=xa, name="mix_out")
            st.update(qkv=qkv, hm=hm, o=o, L=Lj)
        else:
            jb = l - NA
            cq_pre, hm = norm_mm(xa, P['mix_norm'][l], wdq[jb], out_dtype=F32, name="q_down")
            qn, cq = norm_mm(cq_pre, P['b_cq_norm'][jb], wuq_n[jb], out_dtype=BF, name="q_up")
            qr = norm_mm(cq_pre, P['b_cq_norm'][jb], wuq_r[jb], out_dtype=BF, tn=LANES, rope=rope, write_h=False,
                         name="q_rope")
            o, Lj = mla_fwd(qn, qr, kvs['kv'], kvs['kr'], H)
            xb = mm(o, b_wo[jb], add=xa, name="mix_out")
            st.update(cq_pre=cq_pre, hm=hm, qn=qn, qr=qr, cq=cq, o=o, L=Lj)
        x, h2, G2, U2 = ffn_forward(xb, P['ffn_norm2'][l], (l, 1))
        st.update(xb=xb, h2=h2, G2=G2, U2=U2)
        saved.append(st)

    loss_part, dx, dg_final = loss_head(x, P['final_norm'], target)

    gw = {}
    gv = {'final_norm': dg_final}
    dkv_acc = None
    for n in ('ffn_norm1', 'mix_norm', 'ffn_norm2'):
        gv[n] = [None] * NL
    gv['b_cq_norm'] = [None] * (NL - NA)
    ffn_r = {}
    pending = []
    a_g = {'a_wqkv': [None] * NA, 'a_wo': [None] * NA}
    b_g = {n: [None] * (NL - NA) for n in ('b_wdq', 'b_wuq', 'b_wo')}

    split_cols = lambda t: jnp.moveaxis(t.reshape(t.shape[:-1] + (N_DEV, t.shape[-1] // N_DEV)), -2, 0)
    split_rows = lambda t, lead: jnp.moveaxis(
        t.reshape(t.shape[:lead] + (N_DEV, t.shape[lead] // N_DEV) + t.shape[lead + 1:]), lead, 0)
    group_r = {}

    def pack_group_grads(gk):
        if gk == 'B':
            for n in b_g:
                gw[n] = jnp.stack(b_g[n])
            lead = {'b_wdq': 1, 'b_wuq': 1, 'b_wo': 1}
            shards = [split_rows(gw[n], lead.get(n, 0)) for n, _ in groups[gk]]
        else:
            l = gk[1]
            shards = [split_cols(a_g['a_wqkv'][l]), split_rows(a_g['a_wo'][l], 0)]
        flat = jnp.concatenate([t.astype(BF).reshape(N_DEV, -1) for t in shards], axis=1)
        rows = group_loc[gk].shape[0]
        return jnp.pad(flat, ((0, 0), (0, rows * D - flat.shape[1]))).reshape(N_DEV, rows, D)

    def ffn_backward(dy, x_in, gain, key, h, G, U):
        sent, carry = pending.pop() if pending else (None, [])
        dxi, dout, dG, dU, dgain, got = ffn_bwd_x(dy, x_in, gain, G, U, *ffn_w[key], carry=carry)
        if carry:
            ffn_r[sent] = got
        gks = send_at.get(key, [])
        dws, got_w = ffn_bwd_w(h, dout, G, U, dG, dU, carry=[pack_group_grads(gk) for gk in gks])
        group_r.update(zip(gks, got_w))
        pending.append((key, dws))
        return dxi, dgain

    for l in reversed(range(NL)):
        st = saved[l]
        dxb, gv['ffn_norm2'][l] = ffn_backward(dx, st['xb'], P['ffn_norm2'][l], (l, 1), st['h2'], st['G2'], st['U2'])
        xa = st['xa']
        if l < NA:
            w_qkv, w_o = a_weights(l)
            do = mm(dxb, w_o, tb=True, out_dtype=BF, name="mix_out_dx")
            a_g['a_wo'][l] = mm(st['o'], dxb, ta=True, out_dtype=BF, tm=1024, tn=1024, name="mix_out_dw")
            delta = head_delta(do, st['o'])
            dqkv = jnp.concatenate(dil_attn_bwd(st['qkv'], do, st['L'], delta, biases, H), axis=1)
            dh = mm(dqkv, w_qkv, tb=True, name="a_qkv_dx")
            a_g['a_wqkv'][l] = mm(st['hm'], dqkv, ta=True, out_dtype=BF, tm=1024, tn=1024, name="a_qkv_dw")
            dxa, gv['mix_norm'][l] = norm_bwd(xa, P['mix_norm'][l], dh, dxb, name="mix_norm_bwd")
        else:
            jb = l - NA
            do = mm(dxb, b_wo[jb], tb=True, out_dtype=BF, name="mix_out_dx")
            b_g['b_wo'][jb] = mm(st['o'], dxb, ta=True, out_dtype=BF, tm=1024, tn=1024, name="mix_out_dw")
            delta = head_delta(do, st['o'])
            dqn, dqr, dkn, dv, dr = mla_bwd(st['qn'], st['qr'], kvs['kv'], kvs['kr'], do, st['L'], delta, rope, H,
                                            prev=dkv_acc)
            dkv_acc = (jnp.concatenate([dkn, dv], axis=1), dr)
            dcq = mm(dqn, wuq_n[jb], tb=True, name="q_up_dx")
            dcq = mm(dqr, wuq_r[jb], tb=True, add=dcq, name="q_up_dx_add")
            dwn = mm(st['cq'], dqn, ta=True, out_dtype=F32, name="q_up_dw")
            dwr = mm(st['cq'], dqr, ta=True, out_dtype=F32, name="q_up_dw")
            b_g['b_wuq'][jb] = jnp.concatenate(
                [dwn.reshape(QL, H, HEAD), _rope_unpad(dwr.reshape(QL, H, HEAD))], axis=-1)
            dcq_pre, gv['b_cq_norm'][jb] = norm_bwd(st['cq_pre'], P['b_cq_norm'][jb], dcq, name="cq_norm_bwd")
            dh = mm(dcq_pre, wdq[jb], tb=True, name="q_down_dx")
            b_g['b_wdq'][jb] = mm(st['hm'], dcq_pre, ta=True, out_dtype=F32, tm=1024, name="q_down_dw")
            dxa, gv['mix_norm'][l] = norm_bwd(xa, P['mix_norm'][l], dh, dxb, name="mix_norm_bwd")
        dx, gv['ffn_norm1'][l] = ffn_backward(dxa, st['x_in'], P['ffn_norm1'][l], (l, 0), st['h1'], st['G1'], st['U1'])
        if l == NA:
            dkvm, dr = dkv_acc
            dckv = mm(dkvm, wkv, tb=True, name="kv_up_dx")
            dwkv = mm(kvs['ckv'], dkvm, ta=True, out_dtype=F32, name="kv_up_dw")
            gw['b_wuk'] = dwkv[:, :HW].reshape(KV, H, HEAD)
            gw['b_wuv'] = dwkv[:, HW:].reshape(KV, H, HEAD)
            dckv_pre, gv['b_ckv_norm'] = norm_bwd(kvs['ckv_pre'], P['b_ckv_norm'], dckv, name="ckv_norm_bwd")
            dh = mm(dckv_pre, wdkv, tb=True, name="kv_down_dx")
            dh = mm(dr, wkr, tb=True, add=dh, name="kv_rope_dx")
            gw['b_wdkv'] = mm(kvs['hkv'], dckv_pre, ta=True, out_dtype=F32, tm=1024, name="kv_down_dw")
            gw['b_wkr'] = _rope_unpad(mm(kvs['hkv'], dr, ta=True, out_dtype=F32, tm=1024, name="kv_rope_dw"))
            dx, gv['kv_norm'] = norm_bwd(kvs['x'], P['kv_norm'], dh, dx, name="kv_norm_bwd")

    for n in ('ffn_norm1', 'mix_norm', 'ffn_norm2', 'b_cq_norm'):
        gv[n] = jnp.stack(gv[n])

    vec_parts = [gv[n] for n in VEC_W] + [jnp.full((LANES,), loss_part, F32)]
    gvec = _pack(vec_parts, LANES, F32, row_mult=8)
    last_key, last = pending.pop()
    *got, rv = exchange(last + [gvec], [False] * len(last) + [True])
    ffn_r[last_key] = got

    res = {}
    for f in (0, 1):
        for which, kind in enumerate(('wg', 'wu', 'wd')):
            n = f'ffn{f + 1}_{kind}'
            recvs = [ffn_r[(l, f)][which] for l in range(NL)]
            flat = lambda t: t.reshape((NL * t.shape[1], t.shape[2]))
            out = adamw(recvs, flat(P[n]), flat(P['m_' + n]), flat(P['v_' + n]),
                        name="adamw_row" if kind == 'wd' else "adamw_col")
            res[n] = [t.reshape(P[n].shape) for t in out]
    per_layer = {n: [None] * NA for n in ('a_wqkv', 'a_wo')}
    for gk, members in groups.items():
        out = adamw([group_r[gk]], *[pack_group(gk, pre, F32) for pre in ('', 'm_', 'v_')], name="adamw_mix")
        for (n, l), parts in zip(members, zip(*[_unpack(t, group_shapes[gk]) for t in out])):
            if l is None:
                res[n] = list(parts)
            else:
                per_layer[n][l] = parts
    for n, layers in per_layer.items():
        res[n] = [jnp.stack(field) for field in zip(*layers)]
    vec_shapes = [P[n].shape for n in VEC_W] + [(LANES,)]
    ones = jnp.ones((LANES,), F32)
    pv = lambda pre: _pack([P[pre + n] for n in VEC_W] + [ones], LANES, F32, row_mult=8)
    out = adamw([rv], pv(''), pv('m_'), pv('v_'), name="adamw_vec")
    unp = [_unpack(t, vec_shapes) for t in out]
    for idx, n in enumerate(VEC_W):
        res[n] = [u[idx] for u in unp]
    loss = unp[0][-1][0]

    outs = [loss, dx[None]]
    for field in range(4):
        outs += [res[n][field] for n in W_NAMES]
    return tuple(outs)


def kernel(x, ffn_norm1, ffn1_wg, ffn1_wu, ffn1_wd, mix_norm, ffn_norm2, ffn2_wg, ffn2_wu, ffn2_wd, a_wqkv, a_wo, kv_norm, b_wdkv, b_ckv_norm, b_wkr, b_wuk, b_wuv, b_wdq, b_cq_norm, b_wuq, b_wo, final_norm, loss_target, m_ffn_norm1, m_ffn1_wg, m_ffn1_wu, m_ffn1_wd, m_mix_norm, m_ffn_norm2, m_ffn2_wg, m_ffn2_wu, m_ffn2_wd, m_a_wqkv, m_a_wo, m_kv_norm, m_b_wdkv, m_b_ckv_norm, m_b_wkr, m_b_wuk, m_b_wuv, m_b_wdq, m_b_cq_norm, m_b_wuq, m_b_wo, m_final_norm, v_ffn_norm1, v_ffn1_wg, v_ffn1_wu, v_ffn1_wd, v_mix_norm, v_ffn_norm2, v_ffn2_wg, v_ffn2_wu, v_ffn2_wd, v_a_wqkv, v_a_wo, v_kv_norm, v_b_wdkv, v_b_ckv_norm, v_b_wkr, v_b_wuk, v_b_wuv, v_b_wdq, v_b_cq_norm, v_b_wuq, v_b_wo, v_final_norm):
    return _step(dict(locals()))
```

```python
import functools
import math

import numpy as np
import jax
import jax.numpy as jnp
from jax import lax
from jax.experimental import pallas as pl
from jax.experimental.pallas import tpu as pltpu

BF = jnp.bfloat16
F32 = jnp.float32
MESH = pl.DeviceIdType.MESH
ANY = pl.BlockSpec(memory_space=pl.ANY)

N_DEV = 8
LANES = 128
HEAD = 128
ROPE_HALF = 32
DIL_N = 128
BRANCH_DILATIONS = (1, 4, 16)
ROPE_THETA = 10000.0
MLA_SCALE = (HEAD + 2 * ROPE_HALF) ** -0.5
LOG2E = math.log2(math.e)
EPS = 1e-6
NEG = -1e30
VMEM_LIMIT = 58 * 1024 * 1024

ADAM_LR, ADAM_B1, ADAM_B2, ADAM_EPS, ADAM_WD, ADAM_STEP = 0.001, 0.9, 0.999, 1e-08, 0.01, 10

W_NAMES = ['ffn_norm1', 'ffn1_wg', 'ffn1_wu', 'ffn1_wd', 'mix_norm', 'ffn_norm2', 'ffn2_wg', 'ffn2_wu', 'ffn2_wd',
           'a_wqkv', 'a_wo', 'kv_norm', 'b_wdkv', 'b_ckv_norm', 'b_wkr', 'b_wuk', 'b_wuv', 'b_wdq', 'b_cq_norm',
           'b_wuq', 'b_wo', 'final_norm']
MIX_W = ['a_wqkv', 'a_wo', 'b_wdkv', 'b_wkr', 'b_wuk', 'b_wuv', 'b_wdq', 'b_wuq', 'b_wo']
VEC_W = ['ffn_norm1', 'mix_norm', 'ffn_norm2', 'kv_norm', 'b_ckv_norm', 'b_cq_norm', 'final_norm']


def _tile(n, pref):
    t = min(n, pref)
    assert n % t == 0, (n, pref)
    return t


def _params(sem):
    return pltpu.CompilerParams(dimension_semantics=sem, vmem_limit_bytes=VMEM_LIMIT)


def _dot(a, b):
    return jnp.dot(a, b, preferred_element_type=F32)


def _dot_nt(a, b):
    return lax.dot_general(a, b, (((1,), (1,)), ((), ())), preferred_element_type=F32)


def _dot_tn(a, b):
    return lax.dot_general(a, b, (((0,), (0,)), ((), ())), preferred_element_type=F32)


def _rms(x, g):
    r = lax.rsqrt(jnp.mean(x * x, axis=-1, keepdims=True) + EPS)
    return x * r * g, r


def _rms_bwd(x, g, dh):
    r = lax.rsqrt(jnp.mean(x * x, axis=-1, keepdims=True) + EPS)
    xhat = x * r
    gd = dh * g
    dx = r * (gd - xhat * jnp.mean(gd * xhat, axis=-1, keepdims=True))
    return dx, jnp.sum(dh * xhat, axis=0, keepdims=True)


def _rope(t, cos, sin):
    return t * cos + pltpu.roll(t, 2 * ROPE_HALF, 1) * sin


def _rope_bwd(g, cos, sin):
    return g * cos + pltpu.roll(g * sin, 2 * ROPE_HALF, 1)


def _coords():
    return lax.axis_index("x"), lax.axis_index("y"), lax.axis_index("c")


def _lin(px, py, pc):
    return 4 * px + 2 * py + pc


def _gather_phases(ins, outs, ssem, rsem, lsem):
    n = len(ins)
    x, y, c = _coords()
    me, sibling = (x, y, c), (x, y, 1 - c)
    chips = [(1 - x, y), (x, 1 - y), (1 - x, 1 - y)]

    def copy(t, k, block, to, src=None):
        slot = outs[t].at[_lin(*block)]
        return pltpu.make_async_remote_copy(
            src_ref=slot if src is None else src, dst_ref=slot,
            send_sem=ssem.at[7 * t + k], recv_sem=rsem.at[7 * t + k],
            device_id=to, device_id_type=MESH)

    def mine(t):
        return pltpu.make_async_copy(ins[t], outs[t].at[_lin(*me)], lsem.at[t])

    def first(t):
        return [copy(t, 0, me, sibling, src=ins[t])] + [copy(t, 1 + j, me, (*chip, c), src=ins[t])
                                                       for j, chip in enumerate(chips)]

    def passed(t):
        return [copy(t, 4 + j, (*chip, c), sibling) for j, chip in enumerate(chips)]

    def start():
        for t in range(n):
            mine(t).start()
            for cp in first(t):
                cp.start()

    def middle():
        for t in range(n):
            fw = passed(t)
            for j, chip in enumerate(chips):
                copy(t, 1 + j, (*chip, c), me).wait_recv()
                fw[j].start()

    def finish():
        for t in range(n):
            copy(t, 0, sibling, me).wait_recv()
            for j, chip in enumerate(chips):
                copy(t, 4 + j, (*chip, 1 - c), me).wait_recv()
        for t in range(n):
            for cp in first(t) + passed(t):
                cp.wait_send()
            mine(t).wait()

    return start, middle, finish


def _exchange_phases(ins, outs, bcast, ssem, rsem, lsem):
    n = len(ins)
    x, y, c = _coords()
    me = _lin(x, y, c)
    peers = []
    for k in range(1, N_DEV):
        kx, ky, kc = (k >> 2) & 1, (k >> 1) & 1, k & 1
        peers.append((k, (1 - x if kx else x, 1 - y if ky else y, 1 - c if kc else c)))

    def own(t):
        return pltpu.make_async_copy(ins[t] if bcast[t] else ins[t].at[me], outs[t].at[me], lsem.at[t])

    def send(t, k, peer):
        return pltpu.make_async_remote_copy(
            src_ref=ins[t] if bcast[t] else ins[t].at[_lin(*peer)], dst_ref=outs[t].at[me],
            send_sem=ssem.at[7 * t + k - 1], recv_sem=rsem.at[7 * t + k - 1],
            device_id=peer, device_id_type=MESH)

    def arrival(t, k, peer):
        slot = outs[t].at[_lin(*peer)]
        return pltpu.make_async_remote_copy(
            src_ref=slot, dst_ref=slot, send_sem=ssem.at[7 * t + k - 1], recv_sem=rsem.at[7 * t + k - 1],
            device_id=peer, device_id_type=MESH)

    def start():
        for t in range(n):
            own(t).start()
            for k, peer in peers:
                send(t, k, peer).start()

    def finish():
        for t in range(n):
            for k, peer in peers:
                arrival(t, k, peer).wait_recv()
        for t in range(n):
            for k, peer in peers:
                send(t, k, peer).wait_send()
            own(t).wait()

    return start, finish


def _comm_sems(n):
    return [pltpu.SemaphoreType.DMA((7 * n,)), pltpu.SemaphoreType.DMA((7 * n,)), pltpu.SemaphoreType.DMA((n,))]


def _gathered(xs):
    return [jax.ShapeDtypeStruct((N_DEV,) + a.shape, a.dtype) for a in xs]


def all_gather(xs):
    n = len(xs)

    def body(*refs):
        start, middle, finish = _gather_phases(refs[:n], refs[n:2 * n], *refs[2 * n:])
        start()
        middle()
        finish()

    return pl.pallas_call(
        body, name="all_gather", out_shape=_gathered(xs),
        in_specs=[ANY] * n, out_specs=[ANY] * n, scratch_shapes=_comm_sems(n),
    )(*xs)


def exchange(xs, bcast):
    n = len(xs)

    def body(*refs):
        start, finish = _exchange_phases(refs[:n], refs[n:2 * n], bcast, *refs[2 * n:])
        start()
        finish()

    out_shape = [jax.ShapeDtypeStruct(((N_DEV,) + a.shape) if b else a.shape, a.dtype) for a, b in zip(xs, bcast)]
    return pl.pallas_call(
        body, name="grad_exchange", out_shape=out_shape,
        in_specs=[ANY] * n, out_specs=[ANY] * n, scratch_shapes=_comm_sems(n),
    )(*xs)


def mm(a, b, *, ta=False, tb=False, add=None, out_dtype=F32, tm=512, tn=1024, tk=None, name="mm"):
    K, M = a.shape if ta else a.shape[::-1]
    N = b.shape[0] if tb else b.shape[1]
    assert (b.shape[1] if tb else b.shape[0]) == K and not (ta and tb)
    if tk is None:
        tk = 512 if ta else 2048
    tm, tn, tk = _tile(M, tm), _tile(N, tn), _tile(K, tk)
    nk = K // tk
    has_add = add is not None

    def body(*refs):
        if has_add:
            a_ref, b_ref, add_ref, o_ref, acc = refs
        else:
            a_ref, b_ref, o_ref, acc = refs
        k = pl.program_id(2)

        @pl.when(k == 0)
        def _():
            acc[...] = jnp.zeros_like(acc)

        av = a_ref[...].astype(BF)
        bv = b_ref[...].astype(BF)
        if ta:
            acc[...] += _dot_tn(av, bv)
        elif tb:
            acc[...] += _dot_nt(av, bv)
        else:
            acc[...] += _dot(av, bv)

        @pl.when(k == nk - 1)
        def _():
            r = acc[...]
            if has_add:
                r = r + add_ref[...]
            o_ref[...] = r.astype(out_dtype)

    a_spec = pl.BlockSpec((tk, tm), lambda i, j, k: (k, i)) if ta else pl.BlockSpec((tm, tk), lambda i, j, k: (i, k))
    b_spec = pl.BlockSpec((tn, tk), lambda i, j, k: (j, k)) if tb else pl.BlockSpec((tk, tn), lambda i, j, k: (k, j))
    o_spec = pl.BlockSpec((tm, tn), lambda i, j, k: (i, j))
    in_specs = [a_spec, b_spec] + ([o_spec] if has_add else [])
    args = [a, b] + ([add] if has_add else [])
    return pl.pallas_call(
        body, name=name, grid=(M // tm, N // tn, nk),
        in_specs=in_specs, out_specs=o_spec,
        out_shape=jax.ShapeDtypeStruct((M, N), out_dtype),
        scratch_shapes=[pltpu.VMEM((tm, tn), F32)],
        compiler_params=_params(("parallel", "parallel", "arbitrary")),
    )(*args)


def norm_mm(x, gain, w, *, out_dtype, tn=512, rope=None, write_h=True, tm=512, name="norm_mm"):
    S, K = x.shape
    N = w.shape[1]
    tm, tn = _tile(S, tm), _tile(N, tn)
    if rope is not None:
        assert tn == LANES
    gain = gain.reshape(1, K)

    def body(*refs):
        refs = list(refs)
        x_ref, g_ref, w_ref = refs[:3]
        refs = refs[3:]
        if rope is not None:
            cos_ref, sin_ref = refs[:2]
            refs = refs[2:]
        y_ref = refs[0]
        h_ref = refs[1] if write_h else None
        hs = refs[-1]
        j = pl.program_id(1)

        @pl.when(j == 0)
        def _():
            h, _ = _rms(x_ref[...], g_ref[...])
            hb = h.astype(BF)
            hs[...] = hb
            if write_h:
                h_ref[...] = hb

        y = _dot(hs[...], w_ref[...])
        if rope is not None:
            y = _rope(y, cos_ref[...], sin_ref[...])
        y_ref[...] = y.astype(out_dtype)

    in_specs = [pl.BlockSpec((tm, K), lambda i, j: (i, 0)), pl.BlockSpec((1, K), lambda i, j: (0, 0)),
                pl.BlockSpec((K, tn), lambda i, j: (0, j))]
    args = [x, gain, w]
    if rope is not None:
        in_specs += [pl.BlockSpec((tm, LANES), lambda i, j: (i, 0))] * 2
        args += list(rope)
    out_specs = [pl.BlockSpec((tm, tn), lambda i, j: (i, j))]
    out_shape = [jax.ShapeDtypeStruct((S, N), out_dtype)]
    if write_h:
        out_specs.append(pl.BlockSpec((tm, K), lambda i, j: (i, 0)))
        out_shape.append(jax.ShapeDtypeStruct((S, K), BF))
    res = pl.pallas_call(
        body, name=name, grid=(S // tm, N // tn), in_specs=in_specs, out_specs=out_specs, out_shape=out_shape,
        scratch_shapes=[pltpu.VMEM((tm, K), BF)],
        compiler_params=_params(("parallel", "arbitrary")),
    )(*args)
    return res if write_h else res[0]


def norm_bwd(x, gain, dh, dres=None, *, tm=512, name="norm_bwd"):
    S, K = x.shape
    tm = _tile(S, tm)
    gain = gain.reshape(1, K)
    has_res = dres is not None

    def body(*refs):
        if has_res:
            x_ref, g_ref, dh_ref, dr_ref, dx_ref, dg_ref = refs
        else:
            x_ref, g_ref, dh_ref, dx_ref, dg_ref = refs

        @pl.when(pl.program_id(0) == 0)
        def _():
            dg_ref[...] = jnp.zeros_like(dg_ref)

        dx, dg = _rms_bwd(x_ref[...], g_ref[...], dh_ref[...].astype(F32))
        if has_res:
            dx = dx + dr_ref[...]
        dx_ref[...] = dx
        dg_ref[...] += jnp.broadcast_to(dg, dg_ref.shape)

    row = pl.BlockSpec((tm, K), lambda i: (i, 0))
    in_specs = [row, pl.BlockSpec((1, K), lambda i: (0, 0)), row] + ([row] if has_res else [])
    args = [x, gain, dh] + ([dres] if has_res else [])
    dx, dg = pl.pallas_call(
        body, name=name, grid=(S // tm,), in_specs=in_specs,
        out_specs=[row, pl.BlockSpec((8, K), lambda i: (0, 0))],
        out_shape=[jax.ShapeDtypeStruct((S, K), F32), jax.ShapeDtypeStruct((8, K), F32)],
        compiler_params=_params(("arbitrary",)),
    )(*args)
    return dx, dg[0]


def loss_head(x, gain, target, *, tm=512):
    S, K = x.shape
    tm = _tile(S, tm)
    gain = gain.reshape(1, K)

    def body(x_ref, g_ref, t_ref, dx_ref, dg_ref, ls_ref):
        @pl.when(pl.program_id(0) == 0)
        def _():
            dg_ref[...] = jnp.zeros_like(dg_ref)
            ls_ref[...] = jnp.zeros_like(ls_ref)

        xv, g = x_ref[...], g_ref[...]
        y, _ = _rms(xv, g)
        e = y - t_ref[...]
        part = jnp.sum(jnp.mean(e * e, axis=-1, keepdims=True), axis=0, keepdims=True)
        ls_ref[...] += jnp.broadcast_to(0.5 * part, ls_ref.shape)
        dx, dg = _rms_bwd(xv, g, e / K)
        dx_ref[...] = dx
        dg_ref[...] += jnp.broadcast_to(dg, dg_ref.shape)

    row = pl.BlockSpec((tm, K), lambda i: (i, 0))
    dx, dg, ls = pl.pallas_call(
        body, name="loss_head", grid=(S // tm,),
        in_specs=[row, pl.BlockSpec((1, K), lambda i: (0, 0)), row],
        out_specs=[row, pl.BlockSpec((8, K), lambda i: (0, 0)), pl.BlockSpec((8, LANES), lambda i: (0, 0))],
        out_shape=[jax.ShapeDtypeStruct((S, K), F32), jax.ShapeDtypeStruct((8, K), F32),
                   jax.ShapeDtypeStruct((8, LANES), F32)],
        compiler_params=_params(("arbitrary",)),
    )(x, gain, target)
    return ls[0, 0], dx, dg[0]


GATHER_TAIL = 3


def _once(shape, index_map):
    return pl.BlockSpec(shape, index_map, pipeline_mode=pl.Buffered(1))


def ffn_fwd(x, gain, wg, wu, wd, carry=(), *, tm=512):
    S, D = x.shape
    NB, _, Fs = wg.shape
    tm = _tile(S, tm)
    ni = S // tm
    nc = len(carry)
    gain = gain.reshape(1, D)

    def body(*refs):
        x_ref, g_ref, wg_ref, wu_ref, wd_ref = refs[:5]
        c_in = refs[5:5 + nc]
        xo_ref, h_ref, G_ref, U_ref = refs[5 + nc:9 + nc]
        c_out = refs[9 + nc:9 + 2 * nc]
        hs, acc = refs[9 + 2 * nc:11 + 2 * nc]
        i, j = pl.program_id(0), pl.program_id(1)
        if nc:
            start, middle, finish = _gather_phases(c_in, c_out, *refs[11 + 2 * nc:])
            pl.when((i == 0) & (j == 0))(start)
            pl.when((i == max(ni - GATHER_TAIL, ni // 2)) & (j == 0))(middle)

        @pl.when(j == 0)
        def _():
            h, _ = _rms(x_ref[...], g_ref[...])
            hb = h.astype(BF)
            hs[...] = hb
            h_ref[...] = hb
            acc[...] = jnp.zeros_like(acc)

        h = hs[...]
        g = _dot(h, wg_ref[...])
        u = _dot(h, wu_ref[...])
        G_ref[...] = g.astype(BF)
        U_ref[...] = u.astype(BF)
        a = (g * jax.nn.sigmoid(g) * u).astype(BF)
        acc[...] += _dot(a, wd_ref[...])

        @pl.when(j == NB - 1)
        def _():
            xo_ref[...] = x_ref[...] + 0.5 * acc[...]

        if nc:
            pl.when((i == ni - 1) & (j == NB - 1))(finish)

    row = lambda i, j: (i, 0)
    blk = lambda i, j: (j, 0, 0)
    hid = pl.BlockSpec((None, tm, Fs), lambda i, j: (j, i, 0))
    return pl.pallas_call(
        body, name="ffn_fwd_gather" if nc else "ffn_fwd", grid=(ni, NB),
        in_specs=[pl.BlockSpec((tm, D), row), pl.BlockSpec((1, D), lambda i, j: (0, 0)),
                  pl.BlockSpec((None, D, Fs), blk), pl.BlockSpec((None, D, Fs), blk), pl.BlockSpec((None, Fs, D), blk)]
                 + [ANY] * nc,
        out_specs=[pl.BlockSpec((tm, D), row), pl.BlockSpec((tm, D), row), hid, hid] + [ANY] * nc,
        out_shape=[jax.ShapeDtypeStruct((S, D), F32), jax.ShapeDtypeStruct((S, D), BF),
                   jax.ShapeDtypeStruct((NB, S, Fs), BF), jax.ShapeDtypeStruct((NB, S, Fs), BF)] + _gathered(carry),
        scratch_shapes=[pltpu.VMEM((tm, D), BF), pltpu.VMEM((tm, D), F32)] + (_comm_sems(nc) if nc else []),
        compiler_params=_params(("arbitrary", "arbitrary")),
    )(x, gain, wg, wu, wd, *carry)


def ffn_bwd_x(dy, x, gain, G, U, wg, wu, wd, carry=(), *, tm=512, sub=256):
    S, D = x.shape
    NB, _, Fs = wg.shape
    tm = _tile(S, tm)
    sub = _tile(tm, sub)
    ni = S // tm
    nc = len(carry)
    gain = gain.reshape(1, D)

    def body(*refs):
        dy_ref, x_ref, g_ref, G_ref, U_ref, wg_ref, wu_ref, wd_ref = refs[:8]
        c_in = refs[8:8 + nc]
        dx_ref, do_ref, dG_ref, dU_ref, dg_ref = refs[8 + nc:13 + nc]
        c_out = refs[13 + nc:13 + 2 * nc]
        dob, acc = refs[13 + 2 * nc:15 + 2 * nc]
        i, j = pl.program_id(0), pl.program_id(1)
        if nc:
            start, finish = _exchange_phases(c_in, c_out, [False] * nc, *refs[15 + 2 * nc:])
            pl.when((i == 0) & (j == 0))(start)

        @pl.when(j == 0)
        def _():
            d = (0.5 * dy_ref[...]).astype(BF)
            dob[...] = d
            do_ref[...] = d
            acc[...] = jnp.zeros_like(acc)

        @pl.when((i == 0) & (j == 0))
        def _():
            dg_ref[...] = jnp.zeros_like(dg_ref)

        for r in range(tm // sub):
            rows = slice(r * sub, (r + 1) * sub)
            dA = _dot_nt(dob[rows, :], wd_ref[...])
            g = G_ref[rows, :].astype(F32)
            u = U_ref[rows, :].astype(F32)
            sig = jax.nn.sigmoid(g)
            dG = (dA * u * (sig * (1.0 + g * (1.0 - sig)))).astype(BF)
            dU = (dA * (g * sig)).astype(BF)
            dG_ref[rows, :] = dG
            dU_ref[rows, :] = dU
            acc[rows, :] += _dot_nt(dG, wg_ref[...]) + _dot_nt(dU, wu_ref[...])

        @pl.when(j == NB - 1)
        def _():
            dxn, dg = _rms_bwd(x_ref[...], g_ref[...], acc[...])
            dx_ref[...] = dy_ref[...] + dxn
            dg_ref[...] += jnp.broadcast_to(dg, dg_ref.shape)

        if nc:
            pl.when((i == ni - 1) & (j == NB - 1))(finish)

    row = lambda i, j: (i, 0)
    blk = lambda i, j: (j, 0, 0)
    hid = pl.BlockSpec((None, tm, Fs), lambda i, j: (j, i, 0))
    dx, dout, dG, dU, dg, *got = pl.pallas_call(
        body, name="ffn_bwd_x_exchange" if nc else "ffn_bwd_x", grid=(ni, NB),
        in_specs=[_once((tm, D), row), _once((tm, D), row), pl.BlockSpec((1, D), lambda i, j: (0, 0)), hid, hid,
                  pl.BlockSpec((None, D, Fs), blk), pl.BlockSpec((None, D, Fs), blk), pl.BlockSpec((None, Fs, D), blk)]
                 + [ANY] * nc,
        out_specs=[_once((tm, D), row), _once((tm, D), row), hid, hid, pl.BlockSpec((8, D), lambda i, j: (0, 0))]
                  + [ANY] * nc,
        out_shape=[jax.ShapeDtypeStruct((S, D), F32), jax.ShapeDtypeStruct((S, D), BF),
                   jax.ShapeDtypeStruct((NB, S, Fs), BF), jax.ShapeDtypeStruct((NB, S, Fs), BF),
                   jax.ShapeDtypeStruct((8, D), F32)] + [jax.ShapeDtypeStruct(a.shape, a.dtype) for a in carry],
        scratch_shapes=[pltpu.VMEM((tm, D), BF), pltpu.VMEM((tm, D), F32)] + (_comm_sems(nc) if nc else []),
        compiler_params=_params(("arbitrary", "arbitrary")),
    )(dy, x, gain, G, U, wg, wu, wd, *carry)
    return dx, dout, dG, dU, dg[0], got


def ffn_bwd_w(h, dout, G, U, dG, dU, carry=(), *, tm=512):
    S, D = h.shape
    NB, _, Fs = G.shape
    tm = _tile(S, tm)
    ni = S // tm
    nc = len(carry)

    def body(*refs):
        h_ref, do_ref, G_ref, U_ref, dG_ref, dU_ref = refs[:6]
        c_in = refs[6:6 + nc]
        wg_ref, wu_ref, wd_ref = refs[6 + nc:9 + nc]
        c_out = refs[9 + nc:9 + 2 * nc]
        ag, au, ad = refs[9 + 2 * nc:12 + 2 * nc]
        j, i = pl.program_id(0), pl.program_id(1)
        if nc:
            start, finish = _exchange_phases(c_in, c_out, [False] * nc, *refs[12 + 2 * nc:])
            pl.when((j == 0) & (i == 0))(start)

        @pl.when(i == 0)
        def _():
            ag[...] = jnp.zeros_like(ag)
            au[...] = jnp.zeros_like(au)
            ad[...] = jnp.zeros_like(ad)

        h = h_ref[...]
        ag[...] += _dot_tn(h, dG_ref[...])
        au[...] += _dot_tn(h, dU_ref[...])
        g = G_ref[...].astype(F32)
        a = (g * jax.nn.sigmoid(g) * U_ref[...].astype(F32)).astype(BF)
        ad[...] += _dot_tn(a, do_ref[...])

        @pl.when(i == ni - 1)
        def _():
            wg_ref[...] = ag[...].astype(BF)
            wu_ref[...] = au[...].astype(BF)
            wd_ref[...] = ad[...].astype(BF)

        if nc:
            pl.when((j == NB - 1) & (i == ni - 1))(finish)

    row = pl.BlockSpec((tm, D), lambda j, i: (i, 0))
    hid = pl.BlockSpec((None, tm, Fs), lambda j, i: (j, i, 0))
    blk = lambda j, i: (j, 0, 0)
    dwg, dwu, dwd, *got = pl.pallas_call(
        body, name="ffn_bwd_w_exchange" if nc else "ffn_bwd_w", grid=(NB, ni),
        in_specs=[row, row, hid, hid, hid, hid] + [ANY] * nc,
        out_specs=[_once((None, D, Fs), blk), _once((None, D, Fs), blk), _once((None, Fs, D), blk)] + [ANY] * nc,
        out_shape=[jax.ShapeDtypeStruct((NB, D, Fs), BF), jax.ShapeDtypeStruct((NB, D, Fs), BF),
                   jax.ShapeDtypeStruct((NB, Fs, D), BF)] + [jax.ShapeDtypeStruct(a.shape, a.dtype) for a in carry],
        scratch_shapes=[pltpu.VMEM((D, Fs), F32), pltpu.VMEM((D, Fs), F32), pltpu.VMEM((Fs, D), F32)]
                       + (_comm_sems(nc) if nc else []),
        compiler_params=_params(("arbitrary", "arbitrary")),
    )(h, dout, G, U, dG, dU, *carry)
    return [dwg, dwu, dwd], got


def dilated_bias(H, d):
    n = DIL_N
    slopes = 2.0 ** (-8.0 * (np.arange(H) + 1) / H)
    i = np.arange(n)[:, None]
    j = np.arange(2 * n)[None, :]
    steps = n + i - j
    band = (steps >= 0) & (steps <= n)
    first = band & (j >= n)
    bias = -slopes[:, None, None] * (d * steps).astype(np.float64)[None]
    out = np.stack([np.where(band[None], bias, NEG), np.where(first[None], bias, NEG)], axis=1)
    return jnp.asarray(out, dtype=F32)


def head_delta(do, o, *, tm=512):
    S, W = o.shape
    tm = _tile(S, tm)

    def body(do_ref, o_ref, d_ref):
        for h in range(W // HEAD):
            cols = slice(h * HEAD, (h + 1) * HEAD)
            s = jnp.sum(do_ref[:, cols].astype(F32) * o_ref[:, cols].astype(F32), axis=1, keepdims=True)
            d_ref[:, cols] = jnp.broadcast_to(s, (tm, HEAD))

    blk = pl.BlockSpec((tm, W), lambda i: (i, 0))
    return pl.pallas_call(
        body, name="head_delta", grid=(S // tm,), in_specs=[blk, blk], out_specs=blk,
        out_shape=jax.ShapeDtypeStruct((S, W), F32),
        compiler_params=_params(("parallel",)),
    )(do, o)


DIL_CHUNK = DIL_N * max(BRANCH_DILATIONS)
DIL_UNROLL = 16


def _rows(start, size, d):
    return pl.ds(pl.multiple_of(start, DIL_N), size) if d == 1 else pl.ds(start, size, stride=d)


def dil_attn_fwd(qkv, biases, H):
    S = qkv.shape[0]
    C = DIL_CHUNK
    assert S % C == 0
    scale = HEAD ** -0.5
    nbr = len(BRANCH_DILATIONS)

    def body(*refs):
        q_ref, kc_ref, kp_ref, vc_ref, vp_ref = refs[:5]
        b_refs = refs[5:5 + nbr]
        o_ref, L_ref, qf, kf, vf = refs[5 + nbr:10 + nbr]
        ogs = refs[10 + nbr:10 + 2 * nbr]
        lgs = refs[10 + 2 * nbr:10 + 3 * nbr]
        c = pl.program_id(1)
        qf[...] = q_ref[...].astype(F32)
        kf[0:C, :] = kp_ref[...].astype(F32)
        kf[C:2 * C, :] = kc_ref[...].astype(F32)
        vf[0:C, :] = vp_ref[...].astype(F32)
        vf[C:2 * C, :] = vc_ref[...].astype(F32)
        for d, b_ref, og, lg in zip(BRANCH_DILATIONS, b_refs, ogs, lgs):
            span = DIL_N * d

            def block(t, carry, d=d, b_ref=b_ref, og=og, lg=lg, span=span):
                r, b = t % d, t // d
                q0 = b * span + r
                q = qf[_rows(q0, DIL_N, d), :].astype(BF)
                k2 = kf[_rows(C + q0 - span, 2 * DIL_N, d), :].astype(BF)
                v2 = vf[_rows(C + q0 - span, 2 * DIL_N, d), :].astype(BF)
                bias = jnp.where((c == 0) & (b == 0), b_ref[1], b_ref[0])
                s = _dot_nt(q, k2) * scale + bias
                m = jnp.max(s, axis=1, keepdims=True)
                p = jnp.exp(s - m)
                l = jnp.sum(p, axis=1, keepdims=True)
                og[_rows(q0, DIL_N, d), :] = _dot(p.astype(BF), v2) / l
                lg[_rows(q0, DIL_N, d), :] = jnp.broadcast_to(m + jnp.log(l), (DIL_N, LANES))
                return carry

            lax.fori_loop(0, C // DIL_N, block, 0, unroll=DIL_UNROLL)
        ls = [lg[...] for lg in lgs]
        m = functools.reduce(jnp.maximum, ls)
        es = [jnp.exp(l - m) for l in ls]
        z = functools.reduce(jnp.add, es)
        mix = functools.reduce(jnp.add, [e * og[...] for e, og in zip(es, ogs)])
        o_ref[...] = (mix / z).astype(BF)
        L_ref[...] = m + jnp.log(z)

    cur = lambda off: pl.BlockSpec((C, HEAD), lambda h, c: (c, off * H + h))
    prv = lambda off: pl.BlockSpec((C, HEAD), lambda h, c: (jnp.maximum(c - 1, 0), off * H + h))
    tab = pl.BlockSpec((None, 2, DIL_N, 2 * DIL_N), lambda h, c: (h, 0, 0, 0))
    out = pl.BlockSpec((C, HEAD), lambda h, c: (c, h))
    return pl.pallas_call(
        body, name="dil_attn_fwd", grid=(H, S // C),
        in_specs=[cur(0), cur(1), prv(1), cur(2), prv(2)] + [tab] * nbr,
        out_specs=[out, out],
        out_shape=[jax.ShapeDtypeStruct((S, H * HEAD), BF), jax.ShapeDtypeStruct((S, H * HEAD), F32)],
        scratch_shapes=[pltpu.VMEM((C, HEAD), F32), pltpu.VMEM((2 * C, HEAD), F32), pltpu.VMEM((2 * C, HEAD), F32)]
                       + [pltpu.VMEM((C, HEAD), F32)] * (2 * nbr),
        compiler_params=_params(("parallel", "arbitrary")),
    )(qkv, qkv, qkv, qkv, qkv, *biases)


def dil_attn_bwd(qkv, do, L, delta, biases, H):
    S = qkv.shape[0]
    C = DIL_CHUNK
    assert S % C == 0
    nc = S // C
    scale = HEAD ** -0.5
    nbr = len(BRANCH_DILATIONS)

    def body(*refs):
        (qc_ref, qn_ref, kc_ref, kp_ref, vc_ref, vp_ref, doc_ref, don_ref, Lc_ref, Ln_ref, dlc_ref, dln_ref) = refs[:12]
        b_refs = refs[12:12 + nbr]
        (dq_ref, dk_ref, dv_ref, qf, dof, Lf, dlf, kf, vf, dq_acc, dk_acc, dv_acc,
         dq_g, dka_g, dva_g, dkb_g, dvb_g) = refs[12 + nbr:]
        c = pl.program_id(1)
        for buf, a_ref, b_ref in ((qf, qc_ref, qn_ref), (dof, doc_ref, don_ref), (Lf, Lc_ref, Ln_ref),
                                  (dlf, dlc_ref, dln_ref), (kf, kp_ref, kc_ref), (vf, vp_ref, vc_ref)):
            buf[0:C, :] = a_ref[...].astype(F32)
            buf[C:2 * C, :] = b_ref[...].astype(F32)
        for g_idx, (d, b_ref) in enumerate(zip(BRANCH_DILATIONS, b_refs)):
            span = DIL_N * d
            dkb_g[C:2 * C, :] = jnp.zeros((C, HEAD), F32)
            dvb_g[C:2 * C, :] = jnp.zeros((C, HEAD), F32)

            def block(t, carry, d=d, b_ref=b_ref, span=span):
                r, b = t % d, t // d
                q0 = b * span + r
                qrows = _rows(q0, DIL_N, d)
                krows = _rows(C + q0 - span, 2 * DIL_N, d)
                q, dov = qf[qrows, :].astype(BF), dof[qrows, :].astype(BF)
                k2, v2 = kf[krows, :].astype(BF), vf[krows, :].astype(BF)
                Lq, dl = Lf[qrows, :], dlf[qrows, :]
                bias = jnp.where((c == 0) & (b == 0), b_ref[1], b_ref[0])
                s = _dot_nt(q, k2) * scale + bias
                p = jnp.exp(s - jnp.concatenate([Lq, Lq], axis=1))
                ds = p * (_dot_nt(dov, v2) - jnp.concatenate([dl, dl], axis=1))
                dsb, pb = ds.astype(BF), p.astype(BF)
                dq_g[qrows, :] = _dot(dsb, k2) * scale
                dk2 = _dot_tn(dsb, q) * scale
                dv2 = _dot_tn(pb, dov)
                dka_g[qrows, :] = dk2[DIL_N:]
                dva_g[qrows, :] = dv2[DIL_N:]
                prows = _rows(C + q0 - span, DIL_N, d)
                dkb_g[prows, :] = dk2[:DIL_N]
                dvb_g[prows, :] = dv2[:DIL_N]
                return carry

            lax.fori_loop(0, C // DIL_N, block, 0, unroll=DIL_UNROLL)

            @pl.when(c < nc - 1)
            def _(d=d, b_ref=b_ref, span=span):
                def nxt(r, carry):
                    qrows = _rows(C + r, DIL_N, d)
                    krows = _rows(2 * C - span + r, DIL_N, d)
                    q, dov = qf[qrows, :].astype(BF), dof[qrows, :].astype(BF)
                    k1, v1 = kf[krows, :].astype(BF), vf[krows, :].astype(BF)
                    s = _dot_nt(q, k1) * scale + b_ref[0][:, :DIL_N]
                    p = jnp.exp(s - Lf[qrows, :])
                    ds = p * (_dot_nt(dov, v1) - dlf[qrows, :])
                    dkb_g[krows, :] = _dot_tn(ds.astype(BF), q) * scale
                    dvb_g[krows, :] = _dot_tn(p.astype(BF), dov)
                    return carry

                lax.fori_loop(0, d, nxt, 0, unroll=min(d, DIL_UNROLL))

            if g_idx == 0:
                dq_acc[...] = dq_g[...]
                dk_acc[...] = dka_g[...] + dkb_g[C:2 * C, :]
                dv_acc[...] = dva_g[...] + dvb_g[C:2 * C, :]
            else:
                dq_acc[...] += dq_g[...]
                dk_acc[...] += dka_g[...] + dkb_g[C:2 * C, :]
                dv_acc[...] += dva_g[...] + dvb_g[C:2 * C, :]

        dq_ref[...] = dq_acc[...].astype(BF)
        dk_ref[...] = dk_acc[...].astype(BF)
        dv_ref[...] = dv_acc[...].astype(BF)

    cur3 = lambda off: pl.BlockSpec((C, HEAD), lambda h, c: (c, off * H + h))
    prv3 = lambda off: pl.BlockSpec((C, HEAD), lambda h, c: (jnp.maximum(c - 1, 0), off * H + h))
    nxt3 = lambda off: pl.BlockSpec((C, HEAD), lambda h, c: (jnp.minimum(c + 1, nc - 1), off * H + h))
    cur1 = pl.BlockSpec((C, HEAD), lambda h, c: (c, h))
    nxt1 = pl.BlockSpec((C, HEAD), lambda h, c: (jnp.minimum(c + 1, nc - 1), h))
    tab = pl.BlockSpec((None, 2, DIL_N, 2 * DIL_N), lambda h, c: (h, 0, 0, 0))
    return pl.pallas_call(
        body, name="dil_attn_bwd", grid=(H, nc),
        in_specs=[cur3(0), nxt3(0), cur3(1), prv3(1), cur3(2), prv3(2), cur1, nxt1, cur1, nxt1, cur1, nxt1] + [tab] * nbr,
        out_specs=[cur1] * 3,
        out_shape=[jax.ShapeDtypeStruct((S, H * HEAD), BF)] * 3,
        scratch_shapes=[pltpu.VMEM((2 * C, HEAD), F32)] * 6 + [pltpu.VMEM((C, HEAD), F32)] * 6
                       + [pltpu.VMEM((2 * C, HEAD), F32)] * 2,
        compiler_params=_params(("parallel", "arbitrary")),
    )(qkv, qkv, qkv, qkv, qkv, qkv, do, do, L, L, delta, delta, *biases)


def _causal_mask(s, qi, ki, tq, tk, row0=0):
    qpos = qi * tq + row0 + lax.broadcasted_iota(jnp.int32, s.shape, 0)
    kpos = ki * tk + lax.broadcasted_iota(jnp.int32, s.shape, 1)
    return jnp.where(kpos <= qpos, s, NEG)


def mla_fwd(qn, qr, kv, kr, H, *, tq=1024, tk=1024, sub=512):
    S = qn.shape[0]
    tq, tk = _tile(S, tq), _tile(S, tk)
    sub = _tile(tq, sub)
    nk = S // tk
    scale = MLA_SCALE
    c2 = scale * LOG2E

    def body(qn_ref, qr_ref, kn_ref, kr_ref, v_ref, o_ref, L_ref, m_s, l_s, acc):
        qi, ki = pl.program_id(1), pl.program_id(2)

        @pl.when(ki == 0)
        def _():
            m_s[...] = jnp.full_like(m_s, NEG)
            l_s[...] = jnp.zeros_like(l_s)
            acc[...] = jnp.zeros_like(acc)

        def step(masked):
            k = jnp.concatenate([kn_ref[...], kr_ref[...]], axis=1)
            v = v_ref[...]
            for r in range(tq // sub):
                rows = slice(r * sub, (r + 1) * sub)
                q = jnp.concatenate([qn_ref[rows, :], qr_ref[rows, :]], axis=1)
                s = _dot_nt(q, k)
                if masked:
                    s = _causal_mask(s, qi, ki, tq, tk, r * sub)
                m_prev = m_s[rows, :]
                m_new = jnp.maximum(m_prev, jnp.max(s, axis=1, keepdims=True))
                alpha = jnp.exp2((m_prev - m_new) * c2)
                p = jnp.exp2((s - jnp.tile(m_new, (1, tk // LANES))) * c2)
                l_s[rows, :] = alpha * l_s[rows, :] + jnp.sum(p, axis=1, keepdims=True)
                acc[rows, :] = alpha * acc[rows, :] + _dot(p.astype(BF), v)
                m_s[rows, :] = m_new

        full = ki * tk + tk - 1 <= qi * tq
        live = ki * tk <= qi * tq + tq - 1
        pl.when(full)(lambda: step(False))
        pl.when(live & jnp.logical_not(full))(lambda: step(True))

        @pl.when(ki == nk - 1)
        def _():
            o_ref[...] = (acc[...] / l_s[...]).astype(BF)
            L_ref[...] = m_s[...] * scale + jnp.log(l_s[...])

    kcl = lambda qi, ki: jnp.minimum(ki, (qi * tq + tq - 1) // tk)
    qs = pl.BlockSpec((tq, HEAD), lambda h, qi, ki: (qi, h))
    return pl.pallas_call(
        body, name="mla_fwd", grid=(H, S // tq, nk),
        in_specs=[qs, qs, pl.BlockSpec((tk, HEAD), lambda h, qi, ki: (kcl(qi, ki), h)),
                  pl.BlockSpec((tk, LANES), lambda h, qi, ki: (kcl(qi, ki), 0)),
                  pl.BlockSpec((tk, HEAD), lambda h, qi, ki: (kcl(qi, ki), H + h))],
        out_specs=[qs, qs],
        out_shape=[jax.ShapeDtypeStruct((S, H * HEAD), BF), jax.ShapeDtypeStruct((S, H * HEAD), F32)],
        scratch_shapes=[pltpu.VMEM((tq, LANES), F32), pltpu.VMEM((tq, LANES), F32), pltpu.VMEM((tq, HEAD), F32)],
        compiler_params=_params(("parallel", "parallel", "arbitrary")),
    )(qn, qr, kv, kr, kv)


def _mla_probs(s, L_rows, c2, width):
    return jnp.exp2(s * c2 - jnp.tile(L_rows * LOG2E, (1, width // LANES)))


def mla_bwd(qn, qr, kv, kr, do, L, delta, rope, H, prev=None, *, tq=1024, tk=1024, sub=512):
    S = qn.shape[0]
    tq, tk = _tile(S, tq), _tile(S, tk)
    sub = _tile(tq, sub)
    nq, nk = S // tq, S // tk
    scale = MLA_SCALE
    c2 = scale * LOG2E
    has_prev = prev is not None

    def body(*refs):
        (qn_ref, qr_ref, kn_ref, kr_ref, v_ref, do_ref, L_ref, dl_ref, cq_ref, sq_ref, ck_ref, sk_ref) = refs[:12]
        refs = refs[12:]
        if has_prev:
            pk_ref, pv_ref, pr_ref = refs[:3]
            refs = refs[3:]
        dqn_ref, dqr_ref, dk_ref, dv_ref, dr_ref, dq_full, dk_acc, dv_acc = refs
        h, ki, qi = pl.program_id(0), pl.program_id(1), pl.program_id(2)
        qrows = pl.ds(pl.multiple_of(qi * tq, tq), tq)
        krows = pl.ds(pl.multiple_of(ki * tk, tk), tk)

        @pl.when(qi == 0)
        def _():
            dk_acc[...] = jnp.zeros_like(dk_acc)
            dv_acc[...] = jnp.zeros_like(dv_acc)

        @pl.when((qi == 0) & (h == 0))
        def _():
            dr_ref[krows, :] = jnp.zeros((tk, LANES), F32)

        @pl.when(ki == 0)
        def _():
            dq_full[qrows, :] = jnp.zeros((tq, 2 * HEAD), F32)

        def step(masked):
            k = jnp.concatenate([kn_ref[...], kr_ref[...]], axis=1)
            v = v_ref[...]
            for r in range(tq // sub):
                rows = slice(r * sub, (r + 1) * sub)
                q = jnp.concatenate([qn_ref[rows, :], qr_ref[rows, :]], axis=1)
                s = _dot_nt(q, k)
                if masked:
                    s = _causal_mask(s, qi, ki, tq, tk, r * sub)
                p = _mla_probs(s, L_ref[rows, :], c2, tk)
                dov = do_ref[rows, :]
                dv_acc[...] += _dot_tn(p.astype(BF), dov)
                dp = _dot_nt(dov, v)
                ds = (p * (dp - jnp.tile(dl_ref[rows, :], (1, tk // LANES)))).astype(BF)
                dk_acc[...] += _dot_tn(ds, q)
                dq_full[pl.ds(pl.multiple_of(qi * tq + r * sub, sub), sub), :] += _dot(ds, k)

        full = ki * tk + tk - 1 <= qi * tq
        live = ki * tk <= qi * tq + tq - 1
        pl.when(full)(lambda: step(False))
        pl.when(live & jnp.logical_not(full))(lambda: step(True))

        @pl.when(ki == (qi * tq + tq - 1) // tk)
        def _():
            dq = dq_full[qrows, :] * scale
            dqn_ref[qrows, :] = dq[:, :HEAD].astype(BF)
            dqr_ref[qrows, :] = _rope_bwd(dq[:, HEAD:], cq_ref[...], sq_ref[...]).astype(BF)

        @pl.when(qi == nq - 1)
        def _():
            dk = dk_acc[...] * scale
            dkn, dv = dk[:, :HEAD], dv_acc[...]
            if has_prev:
                dkn, dv = dkn + pk_ref[...], dv + pv_ref[...]
            dk_ref[...] = dkn
            dv_ref[...] = dv
            dr_ref[krows, :] += dk[:, HEAD:]

        @pl.when((qi == nq - 1) & (h == H - 1))
        def _():
            dr = _rope_bwd(dr_ref[krows, :], ck_ref[...], sk_ref[...])
            dr_ref[krows, :] = dr + pr_ref[...] if has_prev else dr

    qcl = lambda ki, qi: jnp.maximum(qi, (ki * tk) // tq)
    qs = pl.BlockSpec((tq, HEAD), lambda h, ki, qi: (qcl(ki, qi), h))
    qt = pl.BlockSpec((tq, LANES), lambda h, ki, qi: (qcl(ki, qi), 0))
    kn = pl.BlockSpec((tk, HEAD), lambda h, ki, qi: (ki, h))
    vs = pl.BlockSpec((tk, HEAD), lambda h, ki, qi: (ki, H + h))
    k1 = pl.BlockSpec((tk, LANES), lambda h, ki, qi: (ki, 0))
    head = pl.BlockSpec((S, HEAD), lambda h, ki, qi: (0, h))
    in_specs = [qs, qs, kn, k1, vs, qs, qs, qs, qt, qt, k1, k1]
    args = [qn, qr, kv, kr, kv, do, L, delta, *rope, *rope]
    if has_prev:
        in_specs += [kn, vs, k1]
        args += [prev[0], prev[0], prev[1]]
    return pl.pallas_call(
        body, name="mla_bwd", grid=(H, nk, nq), in_specs=in_specs,
        out_specs=[head, head, kn, kn, pl.BlockSpec((S, LANES), lambda h, ki, qi: (0, 0))],
        out_shape=[jax.ShapeDtypeStruct((S, H * HEAD), BF), jax.ShapeDtypeStruct((S, H * HEAD), BF),
                   jax.ShapeDtypeStruct((S, H * HEAD), F32), jax.ShapeDtypeStruct((S, H * HEAD), F32),
                   jax.ShapeDtypeStruct((S, LANES), F32)],
        scratch_shapes=[pltpu.VMEM((S, 2 * HEAD), F32), pltpu.VMEM((tk, 2 * HEAD), F32), pltpu.VMEM((tk, HEAD), F32)],
        compiler_params=_params(("arbitrary", "arbitrary", "arbitrary")),
    )(*args)


def adamw(recvs, w, m, v, *, name="adamw"):
    R, C = w.shape
    n = len(recvs)
    R1 = R // n
    assert all(r.shape == (N_DEV, R1, C) for r in recvs)
    tr = 16
    while tr * 2 * C * 44 <= 6 * 1024 * 1024 and R1 % (tr * 2) == 0:
        tr *= 2
    tr = min(tr, R1)
    assert R1 % tr == 0
    nb = R1 // tr

    def body(*refs):
        r_refs = refs[:n]
        w_ref, m_ref, v_ref, g_ref, d_ref, mo_ref, vo_ref = refs[n:]
        for slab in range(n):
            @pl.when(pl.program_id(0) == slab)
            def _(r_ref=r_refs[slab]):
                g = r_ref[0].astype(F32)
                for s in range(1, N_DEV):
                    g = g + r_ref[s].astype(F32)
                m2 = ADAM_B1 * m_ref[...] + (1.0 - ADAM_B1) * g
                v2 = ADAM_B2 * v_ref[...] + (1.0 - ADAM_B2) * (g * g)
                m_hat = m2 / (1.0 - ADAM_B1 ** ADAM_STEP)
                v_hat = v2 / (1.0 - ADAM_B2 ** ADAM_STEP)
                g_ref[...] = g
                d_ref[...] = -ADAM_LR * (m_hat / (jnp.sqrt(v_hat) + ADAM_EPS) + ADAM_WD * w_ref[...])
                mo_ref[...] = m2
                vo_ref[...] = v2

    def recv_spec(slab):
        return pl.BlockSpec((N_DEV, tr, C), lambda l, i: (0, jnp.where(l == slab, i, jnp.where(l < slab, 0, nb - 1)), 0))

    row = pl.BlockSpec((tr, C), lambda l, i: (l * nb + i, 0))
    return pl.pallas_call(
        body, name=name, grid=(n, nb),
        in_specs=[recv_spec(slab) for slab in range(n)] + [row, row, row],
        out_specs=[row] * 4, out_shape=[jax.ShapeDtypeStruct((R, C), F32)] * 4,
        compiler_params=_params(("arbitrary", "arbitrary")),
    )(*recvs, w, m, v)


def _pack(ts, width, dtype, row_mult=16):
    flat = jnp.concatenate([t.astype(dtype).reshape(-1) for t in ts])
    n = flat.shape[0]
    rows = -(-n // (width * row_mult)) * row_mult
    return jnp.pad(flat, (0, rows * width - n)).reshape(rows, width)


def _unpack(buf, shapes):
    lead = buf.shape[:-2]
    flat = buf.reshape(lead + (-1,))
    out, off = [], 0
    for s in shapes:
        n = int(np.prod(s))
        out.append(flat[..., off:off + n].reshape(lead + tuple(s)))
        off += n
    return out


def _rope_pad(r):
    z = jnp.zeros(r.shape[:-1] + (ROPE_HALF,), r.dtype)
    return jnp.concatenate([r[..., :ROPE_HALF], z, r[..., ROPE_HALF:], z], axis=-1)


def _rope_unpad(r):
    return jnp.concatenate([r[..., :ROPE_HALF], r[..., 2 * ROPE_HALF:3 * ROPE_HALF]], axis=-1)


def _step(P):
    x0 = P['x'][0]
    target = P['loss_target'][0]
    S, D = x0.shape
    NL = P['ffn_norm1'].shape[0]
    NA = P['a_wqkv'].shape[0]
    Fs = P['ffn1_wg'].shape[2]
    H = D // HEAD
    KV = P['b_wdkv'].shape[1]
    QL = P['b_wdq'].shape[2]
    HW = H * HEAD

    ffn_seq = [(l, f) for l in range(NL) for f in (0, 1)]
    ffn_loc = {(l, f): [P[n][l].astype(BF) for n in (f'ffn{f + 1}_wg', f'ffn{f + 1}_wu', f'ffn{f + 1}_wd')]
               for l, f in ffn_seq}
    assert 1 <= NA < NL
    groups = {('A', l): [('a_wqkv', l), ('a_wo', l)] for l in range(NA)}
    groups['B'] = [(n, None) for n in MIX_W if n.startswith('b_')]
    pick = lambda pre, n, l: P[pre + n] if l is None else P[pre + n][l]
    pack_group = lambda gk, pre, dtype: _pack([pick(pre, n, l) for n, l in groups[gk]], D, dtype)
    group_shapes = {gk: [pick('', n, l).shape for n, l in members] for gk, members in groups.items()}
    group_loc = {gk: pack_group(gk, '', BF) for gk in groups}
    gather_at = {(0, 0): [('A', 0)]}
    for l in range(1, NA):
        gather_at.setdefault((l - 1, 1), []).append(('A', l))
    gather_at.setdefault((NA - 1, 0), []).append('B')
    send_at = {(l, 0): [('A', l)] for l in range(NA)}
    send_at.setdefault((NA - 1, 1), []).append('B')
    gathered = {}
    ffn_w = {ffn_seq[0]: all_gather(ffn_loc[ffn_seq[0]])}

    def ffn_forward(x, gain, key):
        nxt = ffn_seq.index(key) + 1
        gks = gather_at.get(key, [])
        carry = (ffn_loc[ffn_seq[nxt]] if nxt < len(ffn_seq) else []) + [group_loc[gk] for gk in gks]
        xo, h, G, U, *got = ffn_fwd(x, gain, *ffn_w[key], carry=carry)
        if nxt < len(ffn_seq):
            ffn_w[ffn_seq[nxt]] = got[:3]
        for gk, arr in zip(gks, got[len(got) - len(gks):]):
            gathered[gk] = {n: t for (n, _), t in zip(groups[gk], _unpack(arr, group_shapes[gk]))}
        return xo, h, G, U

    cols = lambda t: jnp.moveaxis(t, 0, -2).reshape(t.shape[1:-1] + (N_DEV * t.shape[-1],))
    rows = lambda t, lead: jnp.moveaxis(t, 0, lead).reshape(t.shape[1:1 + lead] + (N_DEV * t.shape[1 + lead],) + t.shape[2 + lead:])
    a_full = {}

    def a_weights(l):
        if l not in a_full:
            g = gathered[('A', l)]
            a_full[l] = (cols(g['a_wqkv']), rows(g['a_wo'], 0))
        return a_full[l]

    inv = 1.0 / (ROPE_THETA ** (jnp.arange(0, 2 * ROPE_HALF, 2, dtype=F32) / (2 * ROPE_HALF)))
    ang = jnp.arange(S, dtype=F32)[:, None] * inv[None, :]
    z = jnp.zeros((S, ROPE_HALF), F32)
    rope = (jnp.concatenate([jnp.cos(ang), z, jnp.cos(ang), z], axis=1),
            jnp.concatenate([-jnp.sin(ang), z, jnp.sin(ang), z], axis=1))
    biases = [dilated_bias(H, d) for d in BRANCH_DILATIONS]

    saved = []
    kvs = None
    x = x0
    for l in range(NL):
        st = {'x_in': x}
        if l == NA:
            g = gathered['B']
            b_wo = rows(g['b_wo'], 1)
            wdkv = rows(g['b_wdkv'], 0)
            wkr = _rope_pad(rows(g['b_wkr'], 0))
            wkv = jnp.concatenate([rows(g['b_wuk'], 0).reshape(KV, HW), rows(g['b_wuv'], 0).reshape(KV, HW)], axis=1)
            wdq = rows(g['b_wdq'], 1)
            wuq = rows(g['b_wuq'], 1)
            wuq_n = wuq[..., :HEAD].reshape(-1, QL, HW)
            wuq_r = _rope_pad(wuq[..., HEAD:]).reshape(-1, QL, HW)
            ckv_pre, hkv = norm_mm(x, P['kv_norm'], wdkv, out_dtype=F32, name="kv_down")
            kr = norm_mm(x, P['kv_norm'], wkr, out_dtype=BF, tn=LANES, rope=rope, write_h=False, name="kv_rope")
            kvm, ckv = norm_mm(ckv_pre, P['b_ckv_norm'], wkv, out_dtype=BF, name="kv_up")
            kvs = dict(x=x, ckv_pre=ckv_pre, hkv=hkv, kr=kr, kv=kvm, ckv=ckv)
        xa, h1, G1, U1 = ffn_forward(x, P['ffn_norm1'][l], (l, 0))
        st.update(h1=h1, G1=G1, U1=U1, xa=xa)
        if l < NA:
            w_qkv, w_o = a_weights(l)
            qkv, hm = norm_mm(xa, P['mix_norm'][l], w_qkv, out_dtype=BF, tn=3 * HW // 4, name="a_qkv")
            o, Lj = dil_attn_fwd(qkv, biases, H)
            xb = mm(o, w_o, add=xa, name="mix_out")
            st.update(qkv=qkv, hm=hm, o=o, L=Lj)
        else:
            jb = l - NA
            cq_pre, hm = norm_mm(xa, P['mix_norm'][l], wdq[jb], out_dtype=F32, name="q_down")
            qn, cq = norm_mm(cq_pre, P['b_cq_norm'][jb], wuq_n[jb], out_dtype=BF, name="q_up")
            qr = norm_mm(cq_pre, P['b_cq_norm'][jb], wuq_r[jb], out_dtype=BF, tn=LANES, rope=rope, write_h=False,
                         name="q_rope")
            o, Lj = mla_fwd(qn, qr, kvs['kv'], kvs['kr'], H)
            xb = mm(o, b_wo[jb], add=xa, name="mix_out")
            st.update(cq_pre=cq_pre, hm=hm, qn=qn, qr=qr, cq=cq, o=o, L=Lj)
        x, h2, G2, U2 = ffn_forward(xb, P['ffn_norm2'][l], (l, 1))
        st.update(xb=xb, h2=h2, G2=G2, U2=U2)
        saved.append(st)

    loss_part, dx, dg_final = loss_head(x, P['final_norm'], target)

    gw = {}
    gv = {'final_norm': dg_final}
    dkv_acc = None
    for n in ('ffn_norm1', 'mix_norm', 'ffn_norm2'):
        gv[n] = [None] * NL
    gv['b_cq_norm'] = [None] * (NL - NA)
    ffn_r = {}
    pending = []
    a_g = {'a_wqkv': [None] * NA, 'a_wo': [None] * NA}
    b_g = {n: [None] * (NL - NA) for n in ('b_wdq', 'b_wuq', 'b_wo')}

    split_cols = lambda t: jnp.moveaxis(t.reshape(t.shape[:-1] + (N_DEV, t.shape[-1] // N_DEV)), -2, 0)
    split_rows = lambda t, lead: jnp.moveaxis(
        t.reshape(t.shape[:lead] + (N_DEV, t.shape[lead] // N_DEV) + t.shape[lead + 1:]), lead, 0)
    group_r = {}

    def pack_group_grads(gk):
        if gk == 'B':
            for n in b_g:
                gw[n] = jnp.stack(b_g[n])
            lead = {'b_wdq': 1, 'b_wuq': 1, 'b_wo': 1}
            shards = [split_rows(gw[n], lead.get(n, 0)) for n, _ in groups[gk]]
        else:
            l = gk[1]
            shards = [split_cols(a_g['a_wqkv'][l]), split_rows(a_g['a_wo'][l], 0)]
        flat = jnp.concatenate([t.astype(BF).reshape(N_DEV, -1) for t in shards], axis=1)
        rows = group_loc[gk].shape[0]
        return jnp.pad(flat, ((0, 0), (0, rows * D - flat.shape[1]))).reshape(N_DEV, rows, D)

    def ffn_backward(dy, x_in, gain, key, h, G, U):
        sent, carry = pending.pop() if pending else (None, [])
        dxi, dout, dG, dU, dgain, got = ffn_bwd_x(dy, x_in, gain, G, U, *ffn_w[key], carry=carry)
        if carry:
            ffn_r[sent] = got
        gks = send_at.get(key, [])
        dws, got_w = ffn_bwd_w(h, dout, G, U, dG, dU, carry=[pack_group_grads(gk) for gk in gks])
        group_r.update(zip(gks, got_w))
        pending.append((key, dws))
        return dxi, dgain

    for l in reversed(range(NL)):
        st = saved[l]
        dxb, gv['ffn_norm2'][l] = ffn_backward(dx, st['xb'], P['ffn_norm2'][l], (l, 1), st['h2'], st['G2'], st['U2'])
        xa = st['xa']
        if l < NA:
            w_qkv, w_o = a_weights(l)
            do = mm(dxb, w_o, tb=True, out_dtype=BF, name="mix_out_dx")
            a_g['a_wo'][l] = mm(st['o'], dxb, ta=True, out_dtype=BF, tm=1024, tn=1024, name="mix_out_dw")
            delta = head_delta(do, st['o'])
            dqkv = jnp.concatenate(dil_attn_bwd(st['qkv'], do, st['L'], delta, biases, H), axis=1)
            dh = mm(dqkv, w_qkv, tb=True, name="a_qkv_dx")
            a_g['a_wqkv'][l] = mm(st['hm'], dqkv, ta=True, out_dtype=BF, tm=1024, tn=1024, name="a_qkv_dw")
            dxa, gv['mix_norm'][l] = norm_bwd(xa, P['mix_norm'][l], dh, dxb, name="mix_norm_bwd")
        else:
            jb = l - NA
            do = mm(dxb, b_wo[jb], tb=True, out_dtype=BF, name="mix_out_dx")
            b_g['b_wo'][jb] = mm(st['o'], dxb, ta=True, out_dtype=BF, tm=1024, tn=1024, name="mix_out_dw")
            delta = head_delta(do, st['o'])
            dqn, dqr, dkn, dv, dr = mla_bwd(st['qn'], st['qr'], kvs['kv'], kvs['kr'], do, st['L'], delta, rope, H,
                                            prev=dkv_acc)
            dkv_acc = (jnp.concatenate([dkn, dv], axis=1), dr)
            dcq = mm(dqn, wuq_n[jb], tb=True, name="q_up_dx")
            dcq = mm(dqr, wuq_r[jb], tb=True, add=dcq, name="q_up_dx_add")
            dwn = mm(st['cq'], dqn, ta=True, out_dtype=F32, name="q_up_dw")
            dwr = mm(st['cq'], dqr, ta=True, out_dtype=F32, name="q_up_dw")
            b_g['b_wuq'][jb] = jnp.concatenate(
                [dwn.reshape(QL, H, HEAD), _rope_unpad(dwr.reshape(QL, H, HEAD))], axis=-1)
            dcq_pre, gv['b_cq_norm'][jb] = norm_bwd(st['cq_pre'], P['b_cq_norm'][jb], dcq, name="cq_norm_bwd")
            dh = mm(dcq_pre, wdq[jb], tb=True, name="q_down_dx")
            b_g['b_wdq'][jb] = mm(st['hm'], dcq_pre, ta=True, out_dtype=F32, tm=1024, name="q_down_dw")
            dxa, gv['mix_norm'][l] = norm_bwd(xa, P['mix_norm'][l], dh, dxb, name="mix_norm_bwd")
        dx, gv['ffn_norm1'][l] = ffn_backward(dxa, st['x_in'], P['ffn_norm1'][l], (l, 0), st['h1'], st['G1'], st['U1'])
        if l == NA:
            dkvm, dr = dkv_acc
            dckv = mm(dkvm, wkv, tb=True, name="kv_up_dx")
            dwkv = mm(kvs['ckv'], dkvm, ta=True, out_dtype=F32, name="kv_up_dw")
            gw['b_wuk'] = dwkv[:, :HW].reshape(KV, H, HEAD)
            gw['b_wuv'] = dwkv[:, HW:].reshape(KV, H, HEAD)
            dckv_pre, gv['b_ckv_norm'] = norm_bwd(kvs['ckv_pre'], P['b_ckv_norm'], dckv, name="ckv_norm_bwd")
            dh = mm(dckv_pre, wdkv, tb=True, name="kv_down_dx")
            dh = mm(dr, wkr, tb=True, add=dh, name="kv_rope_dx")
            gw['b_wdkv'] = mm(kvs['hkv'], dckv_pre, ta=True, out_dtype=F32, tm=1024, name="kv_down_dw")
            gw['b_wkr'] = _rope_unpad(mm(kvs['hkv'], dr, ta=True, out_dtype=F32, tm=1024, name="kv_rope_dw"))
            dx, gv['kv_norm'] = norm_bwd(kvs['x'], P['kv_norm'], dh, dx, name="kv_norm_bwd")

    for n in ('ffn_norm1', 'mix_norm', 'ffn_norm2', 'b_cq_norm'):
        gv[n] = jnp.stack(gv[n])

    vec_parts = [gv[n] for n in VEC_W] + [jnp.full((LANES,), loss_part, F32)]
    gvec = _pack(vec_parts, LANES, F32, row_mult=8)
    last_key, last = pending.pop()
    *got, rv = exchange(last + [gvec], [False] * len(last) + [True])
    ffn_r[last_key] = got

    res = {}
    for f in (0, 1):
        for which, kind in enumerate(('wg', 'wu', 'wd')):
            n = f'ffn{f + 1}_{kind}'
            recvs = [ffn_r[(l, f)][which] for l in range(NL)]
            flat = lambda t: t.reshape((NL * t.shape[1], t.shape[2]))
            out = adamw(recvs, flat(P[n]), flat(P['m_' + n]), flat(P['v_' + n]),
                        name="adamw_row" if kind == 'wd' else "adamw_col")
            res[n] = [t.reshape(P[n].shape) for t in out]
    per_layer = {n: [None] * NA for n in ('a_wqkv', 'a_wo')}
    for gk, members in groups.items():
        out = adamw([group_r[gk]], *[pack_group(gk, pre, F32) for pre in ('', 'm_', 'v_')], name="adamw_mix")
        for (n, l), parts in zip(members, zip(*[_unpack(t, group_shapes[gk]) for t in out])):
            if l is None:
                res[n] = list(parts)
            else:
                per_layer[n][l] = parts
    for n, layers in per_layer.items():
        res[n] = [jnp.stack(field) for field in zip(*layers)]
    vec_shapes = [P[n].shape for n in VEC_W] + [(LANES,)]
    ones = jnp.ones((LANES,), F32)
    pv = lambda pre: _pack([P[pre + n] for n in VEC_W] + [ones], LANES, F32, row_mult=8)
    out = adamw([rv], pv(''), pv('m_'), pv('v_'), name="adamw_vec")
    unp = [_unpack(t, vec_shapes) for t in out]
    for idx, n in enumerate(VEC_W):
        res[n] = [u[idx] for u in unp]
    loss = unp[0][-1][0]

    outs = [loss, dx[None]]
    for field in range(4):
        outs += [res[n][field] for n in W_NAMES]
    return tuple(outs)


def kernel(x, ffn_norm1, ffn1_wg, ffn1_wu, ffn1_wd, mix_norm, ffn_norm2, ffn2_wg, ffn2_wu, ffn2_wd, a_wqkv, a_wo, kv_norm, b_wdkv, b_ckv_norm, b_wkr, b_wuk, b_wuv, b_wdq, b_cq_norm, b_wuq, b_wo, final_norm, loss_target, m_ffn_norm1, m_ffn1_wg, m_ffn1_wu, m_ffn1_wd, m_mix_norm, m_ffn_norm2, m_ffn2_wg, m_ffn2_wu, m_ffn2_wd, m_a_wqkv, m_a_wo, m_kv_norm, m_b_wdkv, m_b_ckv_norm, m_b_wkr, m_b_wuk, m_b_wuv, m_b_wdq, m_b_cq_norm, m_b_wuq, m_b_wo, m_final_norm, v_ffn_norm1, v_ffn1_wg, v_ffn1_wu, v_ffn1_wd, v_mix_norm, v_ffn_norm2, v_ffn2_wg, v_ffn2_wu, v_ffn2_wd, v_a_wqkv, v_a_wo, v_kv_norm, v_b_wdkv, v_b_ckv_norm, v_b_wkr, v_b_wuk, v_b_wuv, v_b_wdq, v_b_cq_norm, v_b_wuq, v_b_wo, v_final_norm):
    return _step(dict(locals()))
```

```python
import functools
import math

import numpy as np
import jax
import jax.numpy as jnp
from jax import lax
from jax.experimental import pallas as pl
from jax.experimental.pallas import tpu as pltpu

BF = jnp.bfloat16
F32 = jnp.float32
MESH = pl.DeviceIdType.MESH
ANY = pl.BlockSpec(memory_space=pl.ANY)

N_DEV = 8
LANES = 128
HEAD = 128
ROPE_HALF = 32
DIL_N = 128
BRANCH_DILATIONS = (1, 4, 16)
ROPE_THETA = 10000.0
MLA_SCALE = (HEAD + 2 * ROPE_HALF) ** -0.5
LOG2E = math.log2(math.e)
EPS = 1e-6
NEG = -1e30
VMEM_LIMIT = 58 * 1024 * 1024

ADAM_LR, ADAM_B1, ADAM_B2, ADAM_EPS, ADAM_WD, ADAM_STEP = 0.001, 0.9, 0.999, 1e-08, 0.01, 10

W_NAMES = ['ffn_norm1', 'ffn1_wg', 'ffn1_wu', 'ffn1_wd', 'mix_norm', 'ffn_norm2', 'ffn2_wg', 'ffn2_wu', 'ffn2_wd',
           'a_wqkv', 'a_wo', 'kv_norm', 'b_wdkv', 'b_ckv_norm', 'b_wkr', 'b_wuk', 'b_wuv', 'b_wdq', 'b_cq_norm',
           'b_wuq', 'b_wo', 'final_norm']
MIX_W = ['a_wqkv', 'a_wo', 'b_wdkv', 'b_wkr', 'b_wuk', 'b_wuv', 'b_wdq', 'b_wuq', 'b_wo']
VEC_W = ['ffn_norm1', 'mix_norm', 'ffn_norm2', 'kv_norm', 'b_ckv_norm', 'b_cq_norm', 'final_norm']


def _tile(n, pref):
    t = min(n, pref)
    assert n % t == 0, (n, pref)
    return t


def _params(sem):
    return pltpu.CompilerParams(dimension_semantics=sem, vmem_limit_bytes=VMEM_LIMIT)


def _dot(a, b):
    return jnp.dot(a, b, preferred_element_type=F32)


def _dot_nt(a, b):
    return lax.dot_general(a, b, (((1,), (1,)), ((), ())), preferred_element_type=F32)


def _dot_tn(a, b):
    return lax.dot_general(a, b, (((0,), (0,)), ((), ())), preferred_element_type=F32)


def _rms(x, g):
    r = lax.rsqrt(jnp.mean(x * x, axis=-1, keepdims=True) + EPS)
    return x * r * g, r


def _rms_bwd(x, g, dh):
    r = lax.rsqrt(jnp.mean(x * x, axis=-1, keepdims=True) + EPS)
    xhat = x * r
    gd = dh * g
    dx = r * (gd - xhat * jnp.mean(gd * xhat, axis=-1, keepdims=True))
    return dx, jnp.sum(dh * xhat, axis=0, keepdims=True)


def _rope(t, cos, sin):
    return t * cos + pltpu.roll(t, 2 * ROPE_HALF, 1) * sin


def _rope_bwd(g, cos, sin):
    return g * cos + pltpu.roll(g * sin, 2 * ROPE_HALF, 1)


def _coords():
    return lax.axis_index("x"), lax.axis_index("y"), lax.axis_index("c")


def _lin(px, py, pc):
    return 4 * px + 2 * py + pc


def _gather_phases(ins, outs, ssem, rsem, lsem):
    n = len(ins)
    x, y, c = _coords()
    me, sibling = (x, y, c), (x, y, 1 - c)
    chips = [(1 - x, y), (x, 1 - y), (1 - x, 1 - y)]

    def copy(t, k, block, to, src=None):
        slot = outs[t].at[_lin(*block)]
        return pltpu.make_async_remote_copy(
            src_ref=slot if src is None else src, dst_ref=slot,
            send_sem=ssem.at[7 * t + k], recv_sem=rsem.at[7 * t + k],
            device_id=to, device_id_type=MESH)

    def mine(t):
        return pltpu.make_async_copy(ins[t], outs[t].at[_lin(*me)], lsem.at[t])

    def first(t):
        return [copy(t, 0, me, sibling, src=ins[t])] + [copy(t, 1 + j, me, (*chip, c), src=ins[t])
                                                       for j, chip in enumerate(chips)]

    def passed(t):
        return [copy(t, 4 + j, (*chip, c), sibling) for j, chip in enumerate(chips)]

    def start():
        for t in range(n):
            mine(t).start()
            for cp in first(t):
                cp.start()

    def middle():
        for t in range(n):
            fw = passed(t)
            for j, chip in enumerate(chips):
                copy(t, 1 + j, (*chip, c), me).wait_recv()
                fw[j].start()

    def finish():
        for t in range(n):
            copy(t, 0, sibling, me).wait_recv()
            for j, chip in enumerate(chips):
                copy(t, 4 + j, (*chip, 1 - c), me).wait_recv()
        for t in range(n):
            for cp in first(t) + passed(t):
                cp.wait_send()
            mine(t).wait()

    return start, middle, finish


def _exchange_phases(ins, outs, bcast, ssem, rsem, lsem):
    n = len(ins)
    x, y, c = _coords()
    me = _lin(x, y, c)
    peers = []
    for k in range(1, N_DEV):
        kx, ky, kc = (k >> 2) & 1, (k >> 1) & 1, k & 1
        peers.append((k, (1 - x if kx else x, 1 - y if ky else y, 1 - c if kc else c)))

    def own(t):
        return pltpu.make_async_copy(ins[t] if bcast[t] else ins[t].at[me], outs[t].at[me], lsem.at[t])

    def send(t, k, peer):
        return pltpu.make_async_remote_copy(
            src_ref=ins[t] if bcast[t] else ins[t].at[_lin(*peer)], dst_ref=outs[t].at[me],
            send_sem=ssem.at[7 * t + k - 1], recv_sem=rsem.at[7 * t + k - 1],
            device_id=peer, device_id_type=MESH)

    def arrival(t, k, peer):
        slot = outs[t].at[_lin(*peer)]
        return pltpu.make_async_remote_copy(
            src_ref=slot, dst_ref=slot, send_sem=ssem.at[7 * t + k - 1], recv_sem=rsem.at[7 * t + k - 1],
            device_id=peer, device_id_type=MESH)

    def start():
        for t in range(n):
            own(t).start()
            for k, peer in peers:
                send(t, k, peer).start()

    def finish():
        for t in range(n):
            for k, peer in peers:
                arrival(t, k, peer).wait_recv()
        for t in range(n):
            for k, peer in peers:
                send(t, k, peer).wait_send()
            own(t).wait()

    return start, finish


def _comm_sems(n):
    return [pltpu.SemaphoreType.DMA((7 * n,)), pltpu.SemaphoreType.DMA((7 * n,)), pltpu.SemaphoreType.DMA((n,))]


def _gathered(xs):
    return [jax.ShapeDtypeStruct((N_DEV,) + a.shape, a.dtype) for a in xs]


def all_gather(xs):
    n = len(xs)

    def body(*refs):
        start, middle, finish = _gather_phases(refs[:n], refs[n:2 * n], *refs[2 * n:])
        start()
        middle()
        finish()

    return pl.pallas_call(
        body, name="all_gather", out_shape=_gathered(xs),
        in_specs=[ANY] * n, out_specs=[ANY] * n, scratch_shapes=_comm_sems(n),
    )(*xs)


def exchange(xs, bcast):
    n = len(xs)

    def body(*refs):
        start, finish = _exchange_phases(refs[:n], refs[n:2 * n], bcast, *refs[2 * n:])
        start()
        finish()

    out_shape = [jax.ShapeDtypeStruct(((N_DEV,) + a.shape) if b else a.shape, a.dtype) for a, b in zip(xs, bcast)]
    return pl.pallas_call(
        body, name="grad_exchange", out_shape=out_shape,
        in_specs=[ANY] * n, out_specs=[ANY] * n, scratch_shapes=_comm_sems(n),
    )(*xs)


def mm(a, b, *, ta=False, tb=False, add=None, out_dtype=F32, tm=512, tn=1024, tk=None, name="mm"):
    K, M = a.shape if ta else a.shape[::-1]
    N = b.shape[0] if tb else b.shape[1]
    assert (b.shape[1] if tb else b.shape[0]) == K and not (ta and tb)
    if tk is None:
        tk = 512 if ta else 2048
    tm, tn, tk = _tile(M, tm), _tile(N, tn), _tile(K, tk)
    nk = K // tk
    has_add = add is not None

    def body(*refs):
        if has_add:
            a_ref, b_ref, add_ref, o_ref, acc = refs
        else:
            a_ref, b_ref, o_ref, acc = refs
        k = pl.program_id(2)

        @pl.when(k == 0)
        def _():
            acc[...] = jnp.zeros_like(acc)

        av = a_ref[...].astype(BF)
        bv = b_ref[...].astype(BF)
        if ta:
            acc[...] += _dot_tn(av, bv)
        elif tb:
            acc[...] += _dot_nt(av, bv)
        else:
            acc[...] += _dot(av, bv)

        @pl.when(k == nk - 1)
        def _():
            r = acc[...]
            if has_add:
                r = r + add_ref[...]
            o_ref[...] = r.astype(out_dtype)

    a_spec = pl.BlockSpec((tk, tm), lambda i, j, k: (k, i)) if ta else pl.BlockSpec((tm, tk), lambda i, j, k: (i, k))
    b_spec = pl.BlockSpec((tn, tk), lambda i, j, k: (j, k)) if tb else pl.BlockSpec((tk, tn), lambda i, j, k: (k, j))
    o_spec = pl.BlockSpec((tm, tn), lambda i, j, k: (i, j))
    in_specs = [a_spec, b_spec] + ([o_spec] if has_add else [])
    args = [a, b] + ([add] if has_add else [])
    return pl.pallas_call(
        body, name=name, grid=(M // tm, N // tn, nk),
        in_specs=in_specs, out_specs=o_spec,
        out_shape=jax.ShapeDtypeStruct((M, N), out_dtype),
        scratch_shapes=[pltpu.VMEM((tm, tn), F32)],
        compiler_params=_params(("parallel", "parallel", "arbitrary")),
    )(*args)


def norm_mm(x, gain, w, *, out_dtype, tn=512, rope=None, write_h=True, tm=512, name="norm_mm"):
    S, K = x.shape
    N = w.shape[1]
    tm, tn = _tile(S, tm), _tile(N, tn)
    if rope is not None:
        assert tn == LANES
    gain = gain.reshape(1, K)

    def body(*refs):
        refs = list(refs)
        x_ref, g_ref, w_ref = refs[:3]
        refs = refs[3:]
        if rope is not None:
            cos_ref, sin_ref = refs[:2]
            refs = refs[2:]
        y_ref = refs[0]
        h_ref = refs[1] if write_h else None
        hs = refs[-1]
        j = pl.program_id(1)

        @pl.when(j == 0)
        def _():
            h, _ = _rms(x_ref[...], g_ref[...])
            hb = h.astype(BF)
            hs[...] = hb
            if write_h:
                h_ref[...] = hb

        y = _dot(hs[...], w_ref[...])
        if rope is not None:
            y = _rope(y, cos_ref[...], sin_ref[...])
        y_ref[...] = y.astype(out_dtype)

    in_specs = [pl.BlockSpec((tm, K), lambda i, j: (i, 0)), pl.BlockSpec((1, K), lambda i, j: (0, 0)),
                pl.BlockSpec((K, tn), lambda i, j: (0, j))]
    args = [x, gain, w]
    if rope is not None:
        in_specs += [pl.BlockSpec((tm, LANES), lambda i, j: (i, 0))] * 2
        args += list(rope)
    out_specs = [pl.BlockSpec((tm, tn), lambda i, j: (i, j))]
    out_shape = [jax.ShapeDtypeStruct((S, N), out_dtype)]
    if write_h:
        out_specs.append(pl.BlockSpec((tm, K), lambda i, j: (i, 0)))
        out_shape.append(jax.ShapeDtypeStruct((S, K), BF))
    res = pl.pallas_call(
        body, name=name, grid=(S // tm, N // tn), in_specs=in_specs, out_specs=out_specs, out_shape=out_shape,
        scratch_shapes=[pltpu.VMEM((tm, K), BF)],
        compiler_params=_params(("parallel", "arbitrary")),
    )(*args)
    return res if write_h else res[0]


def norm_bwd(x, gain, dh, dres=None, *, tm=512, name="norm_bwd"):
    S, K = x.shape
    tm = _tile(S, tm)
    gain = gain.reshape(1, K)
    has_res = dres is not None

    def body(*refs):
        if has_res:
            x_ref, g_ref, dh_ref, dr_ref, dx_ref, dg_ref = refs
        else:
            x_ref, g_ref, dh_ref, dx_ref, dg_ref = refs

        @pl.when(pl.program_id(0) == 0)
        def _():
            dg_ref[...] = jnp.zeros_like(dg_ref)

        dx, dg = _rms_bwd(x_ref[...], g_ref[...], dh_ref[...].astype(F32))
        if has_res:
            dx = dx + dr_ref[...]
        dx_ref[...] = dx
        dg_ref[...] += jnp.broadcast_to(dg, dg_ref.shape)

    row = pl.BlockSpec((tm, K), lambda i: (i, 0))
    in_specs = [row, pl.BlockSpec((1, K), lambda i: (0, 0)), row] + ([row] if has_res else [])
    args = [x, gain, dh] + ([dres] if has_res else [])
    dx, dg = pl.pallas_call(
        body, name=name, grid=(S // tm,), in_specs=in_specs,
        out_specs=[row, pl.BlockSpec((8, K), lambda i: (0, 0))],
        out_shape=[jax.ShapeDtypeStruct((S, K), F32), jax.ShapeDtypeStruct((8, K), F32)],
        compiler_params=_params(("arbitrary",)),
    )(*args)
    return dx, dg[0]


def loss_head(x, gain, target, *, tm=512):
    S, K = x.shape
    tm = _tile(S, tm)
    gain = gain.reshape(1, K)

    def body(x_ref, g_ref, t_ref, dx_ref, dg_ref, ls_ref):
        @pl.when(pl.program_id(0) == 0)
        def _():
            dg_ref[...] = jnp.zeros_like(dg_ref)
            ls_ref[...] = jnp.zeros_like(ls_ref)

        xv, g = x_ref[...], g_ref[...]
        y, _ = _rms(xv, g)
        e = y - t_ref[...]
        part = jnp.sum(jnp.mean(e * e, axis=-1, keepdims=True), axis=0, keepdims=True)
        ls_ref[...] += jnp.broadcast_to(0.5 * part, ls_ref.shape)
        dx, dg = _rms_bwd(xv, g, e / K)
        dx_ref[...] = dx
        dg_ref[...] += jnp.broadcast_to(dg, dg_ref.shape)

    row = pl.BlockSpec((tm, K), lambda i: (i, 0))
    dx, dg, ls = pl.pallas_call(
        body, name="loss_head", grid=(S // tm,),
        in_specs=[row, pl.BlockSpec((1, K), lambda i: (0, 0)), row],
        out_specs=[row, pl.BlockSpec((8, K), lambda i: (0, 0)), pl.BlockSpec((8, LANES), lambda i: (0, 0))],
        out_shape=[jax.ShapeDtypeStruct((S, K), F32), jax.ShapeDtypeStruct((8, K), F32),
                   jax.ShapeDtypeStruct((8, LANES), F32)],
        compiler_params=_params(("arbitrary",)),
    )(x, gain, target)
    return ls[0, 0], dx, dg[0]


GATHER_TAIL = 3


def _once(shape, index_map):
    return pl.BlockSpec(shape, index_map, pipeline_mode=pl.Buffered(1))


def ffn_fwd(x, gain, wg, wu, wd, carry=(), *, tm=512):
    S, D = x.shape
    NB, _, Fs = wg.shape
    tm = _tile(S, tm)
    ni = S // tm
    nc = len(carry)
    gain = gain.reshape(1, D)

    def body(*refs):
        x_ref, g_ref, wg_ref, wu_ref, wd_ref = refs[:5]
        c_in = refs[5:5 + nc]
        xo_ref, h_ref, G_ref, U_ref = refs[5 + nc:9 + nc]
        c_out = refs[9 + nc:9 + 2 * nc]
        hs, acc = refs[9 + 2 * nc:11 + 2 * nc]
        i, j = pl.program_id(0), pl.program_id(1)
        if nc:
            start, middle, finish = _gather_phases(c_in, c_out, *refs[11 + 2 * nc:])
            pl.when((i == 0) & (j == 0))(start)
            pl.when((i == max(ni - GATHER_TAIL, ni // 2)) & (j == 0))(middle)

        @pl.when(j == 0)
        def _():
            h, _ = _rms(x_ref[...], g_ref[...])
            hb = h.astype(BF)
            hs[...] = hb
            h_ref[...] = hb
            acc[...] = jnp.zeros_like(acc)

        h = hs[...]
        g = _dot(h, wg_ref[...])
        u = _dot(h, wu_ref[...])
        G_ref[...] = g.astype(BF)
        U_ref[...] = u.astype(BF)
        a = (g * jax.nn.sigmoid(g) * u).astype(BF)
        acc[...] += _dot(a, wd_ref[...])

        @pl.when(j == NB - 1)
        def _():
            xo_ref[...] = x_ref[...] + 0.5 * acc[...]

        if nc:
            pl.when((i == ni - 1) & (j == NB - 1))(finish)

    row = lambda i, j: (i, 0)
    blk = lambda i, j: (j, 0, 0)
    hid = pl.BlockSpec((None, tm, Fs), lambda i, j: (j, i, 0))
    return pl.pallas_call(
        body, name="ffn_fwd_gather" if nc else "ffn_fwd", grid=(ni, NB),
        in_specs=[pl.BlockSpec((tm, D), row), pl.BlockSpec((1, D), lambda i, j: (0, 0)),
                  pl.BlockSpec((None, D, Fs), blk), pl.BlockSpec((None, D, Fs), blk), pl.BlockSpec((None, Fs, D), blk)]
                 + [ANY] * nc,
        out_specs=[pl.BlockSpec((tm, D), row), pl.BlockSpec((tm, D), row), hid, hid] + [ANY] * nc,
        out_shape=[jax.ShapeDtypeStruct((S, D), F32), jax.ShapeDtypeStruct((S, D), BF),
                   jax.ShapeDtypeStruct((NB, S, Fs), BF), jax.ShapeDtypeStruct((NB, S, Fs), BF)] + _gathered(carry),
        scratch_shapes=[pltpu.VMEM((tm, D), BF), pltpu.VMEM((tm, D), F32)] + (_comm_sems(nc) if nc else []),
        compiler_params=_params(("arbitrary", "arbitrary")),
    )(x, gain, wg, wu, wd, *carry)


def ffn_bwd_x(dy, x, gain, G, U, wg, wu, wd, carry=(), *, tm=512, sub=256):
    S, D = x.shape
    NB, _, Fs = wg.shape
    tm = _tile(S, tm)
    sub = _tile(tm, sub)
    ni = S // tm
    nc = len(carry)
    gain = gain.reshape(1, D)

    def body(*refs):
        dy_ref, x_ref, g_ref, G_ref, U_ref, wg_ref, wu_ref, wd_ref = refs[:8]
        c_in = refs[8:8 + nc]
        dx_ref, do_ref, dG_ref, dU_ref, dg_ref = refs[8 + nc:13 + nc]
        c_out = refs[13 + nc:13 + 2 * nc]
        dob, acc = refs[13 + 2 * nc:15 + 2 * nc]
        i, j = pl.program_id(0), pl.program_id(1)
        if nc:
            start, finish = _exchange_phases(c_in, c_out, [False] * nc, *refs[15 + 2 * nc:])
            pl.when((i == 0) & (j == 0))(start)

        @pl.when(j == 0)
        def _():
            d = (0.5 * dy_ref[...]).astype(BF)
            dob[...] = d
            do_ref[...] = d
            acc[...] = jnp.zeros_like(acc)

        @pl.when((i == 0) & (j == 0))
        def _():
            dg_ref[...] = jnp.zeros_like(dg_ref)

        for r in range(tm // sub):
            rows = slice(r * sub, (r + 1) * sub)
            dA = _dot_nt(dob[rows, :], wd_ref[...])
            g = G_ref[rows, :].astype(F32)
            u = U_ref[rows, :].astype(F32)
            sig = jax.nn.sigmoid(g)
            dG = (dA * u * (sig * (1.0 + g * (1.0 - sig)))).astype(BF)
            dU = (dA * (g * sig)).astype(BF)
            dG_ref[rows, :] = dG
            dU_ref[rows, :] = dU
            acc[rows, :] += _dot_nt(dG, wg_ref[...]) + _dot_nt(dU, wu_ref[...])

        @pl.when(j == NB - 1)
        def _():
            dxn, dg = _rms_bwd(x_ref[...], g_ref[...], acc[...])
            dx_ref[...] = dy_ref[...] + dxn
            dg_ref[...] += jnp.broadcast_to(dg, dg_ref.shape)

        if nc:
            pl.when((i == ni - 1) & (j == NB - 1))(finish)

    row = lambda i, j: (i, 0)
    blk = lambda i, j: (j, 0, 0)
    hid = pl.BlockSpec((None, tm, Fs), lambda i, j: (j, i, 0))
    dx, dout, dG, dU, dg, *got = pl.pallas_call(
        body, name="ffn_bwd_x_exchange" if nc else "ffn_bwd_x", grid=(ni, NB),
        in_specs=[_once((tm, D), row), _once((tm, D), row), pl.BlockSpec((1, D), lambda i, j: (0, 0)), hid, hid,
                  pl.BlockSpec((None, D, Fs), blk), pl.BlockSpec((None, D, Fs), blk), pl.BlockSpec((None, Fs, D), blk)]
                 + [ANY] * nc,
        out_specs=[_once((tm, D), row), _once((tm, D), row), hid, hid, pl.BlockSpec((8, D), lambda i, j: (0, 0))]
                  + [ANY] * nc,
        out_shape=[jax.ShapeDtypeStruct((S, D), F32), jax.ShapeDtypeStruct((S, D), BF),
                   jax.ShapeDtypeStruct((NB, S, Fs), BF), jax.ShapeDtypeStruct((NB, S, Fs), BF),
                   jax.ShapeDtypeStruct((8, D), F32)] + [jax.ShapeDtypeStruct(a.shape, a.dtype) for a in carry],
        scratch_shapes=[pltpu.VMEM((tm, D), BF), pltpu.VMEM((tm, D), F32)] + (_comm_sems(nc) if nc else []),
        compiler_params=_params(("arbitrary", "arbitrary")),
    )(dy, x, gain, G, U, wg, wu, wd, *carry)
    return dx, dout, dG, dU, dg[0], got


def ffn_bwd_w(h, dout, G, U, dG, dU, carry=(), *, tm=512):
    S, D = h.shape
    NB, _, Fs = G.shape
    tm = _tile(S, tm)
    ni = S // tm
    nc = len(carry)

    def body(*refs):
        h_ref, do_ref, G_ref, U_ref, dG_ref, dU_ref = refs[:6]
        c_in = refs[6:6 + nc]
        wg_ref, wu_ref, wd_ref = refs[6 + nc:9 + nc]
        c_out = refs[9 + nc:9 + 2 * nc]
        ag, au, ad = refs[9 + 2 * nc:12 + 2 * nc]
        j, i = pl.program_id(0), pl.program_id(1)
        if nc:
            start, finish = _exchange_phases(c_in, c_out, [False] * nc, *refs[12 + 2 * nc:])
            pl.when((j == 0) & (i == 0))(start)

        @pl.when(i == 0)
        def _():
            ag[...] = jnp.zeros_like(ag)
            au[...] = jnp.zeros_like(au)
            ad[...] = jnp.zeros_like(ad)

        h = h_ref[...]
        ag[...] += _dot_tn(h, dG_ref[...])
        au[...] += _dot_tn(h, dU_ref[...])
        g = G_ref[...].astype(F32)
        a = (g * jax.nn.sigmoid(g) * U_ref[...].astype(F32)).astype(BF)
        ad[...] += _dot_tn(a, do_ref[...])

        @pl.when(i == ni - 1)
        def _():
            wg_ref[...] = ag[...].astype(BF)
            wu_ref[...] = au[...].astype(BF)
            wd_ref[...] = ad[...].astype(BF)

        if nc:
            pl.when((j == NB - 1) & (i == ni - 1))(finish)

    row = pl.BlockSpec((tm, D), lambda j, i: (i, 0))
    hid = pl.BlockSpec((None, tm, Fs), lambda j, i: (j, i, 0))
    blk = lambda j, i: (j, 0, 0)
    dwg, dwu, dwd, *got = pl.pallas_call(
        body, name="ffn_bwd_w_exchange" if nc else "ffn_bwd_w", grid=(NB, ni),
        in_specs=[row, row, hid, hid, hid, hid] + [ANY] * nc,
        out_specs=[pl.BlockSpec((None, D, Fs), blk), pl.BlockSpec((None, D, Fs), blk), pl.BlockSpec((None, Fs, D), blk)]
                  + [ANY] * nc,
        out_shape=[jax.ShapeDtypeStruct((NB, D, Fs), BF), jax.ShapeDtypeStruct((NB, D, Fs), BF),
                   jax.ShapeDtypeStruct((NB, Fs, D), BF)] + [jax.ShapeDtypeStruct(a.shape, a.dtype) for a in carry],
        scratch_shapes=[pltpu.VMEM((D, Fs), F32), pltpu.VMEM((D, Fs), F32), pltpu.VMEM((Fs, D), F32)]
                       + (_comm_sems(nc) if nc else []),
        compiler_params=_params(("arbitrary", "arbitrary")),
    )(h, dout, G, U, dG, dU, *carry)
    return [dwg, dwu, dwd], got


def dilated_bias(H, d):
    n = DIL_N
    slopes = 2.0 ** (-8.0 * (np.arange(H) + 1) / H)
    i = np.arange(n)[:, None]
    j = np.arange(2 * n)[None, :]
    steps = n + i - j
    band = (steps >= 0) & (steps <= n)
    first = band & (j >= n)
    bias = -slopes[:, None, None] * (d * steps).astype(np.float64)[None]
    out = np.stack([np.where(band[None], bias, NEG), np.where(first[None], bias, NEG)], axis=1)
    return jnp.asarray(out, dtype=F32)


def head_delta(do, o, *, tm=512):
    S, W = o.shape
    tm = _tile(S, tm)

    def body(do_ref, o_ref, d_ref):
        for h in range(W // HEAD):
            cols = slice(h * HEAD, (h + 1) * HEAD)
            s = jnp.sum(do_ref[:, cols].astype(F32) * o_ref[:, cols].astype(F32), axis=1, keepdims=True)
            d_ref[:, cols] = jnp.broadcast_to(s, (tm, HEAD))

    blk = pl.BlockSpec((tm, W), lambda i: (i, 0))
    return pl.pallas_call(
        body, name="head_delta", grid=(S // tm,), in_specs=[blk, blk], out_specs=blk,
        out_shape=jax.ShapeDtypeStruct((S, W), F32),
        compiler_params=_params(("parallel",)),
    )(do, o)


DIL_CHUNK = DIL_N * max(BRANCH_DILATIONS)
DIL_UNROLL = 16


def _rows(start, size, d):
    return pl.ds(pl.multiple_of(start, DIL_N), size) if d == 1 else pl.ds(start, size, stride=d)


def dil_attn_fwd(qkv, biases, H):
    S = qkv.shape[0]
    C = DIL_CHUNK
    assert S % C == 0
    scale = HEAD ** -0.5
    nbr = len(BRANCH_DILATIONS)

    def body(*refs):
        q_ref, kc_ref, kp_ref, vc_ref, vp_ref = refs[:5]
        b_refs = refs[5:5 + nbr]
        o_ref, L_ref, qf, kf, vf = refs[5 + nbr:10 + nbr]
        ogs = refs[10 + nbr:10 + 2 * nbr]
        lgs = refs[10 + 2 * nbr:10 + 3 * nbr]
        c = pl.program_id(1)
        qf[...] = q_ref[...].astype(F32)
        kf[0:C, :] = kp_ref[...].astype(F32)
        kf[C:2 * C, :] = kc_ref[...].astype(F32)
        vf[0:C, :] = vp_ref[...].astype(F32)
        vf[C:2 * C, :] = vc_ref[...].astype(F32)
        for d, b_ref, og, lg in zip(BRANCH_DILATIONS, b_refs, ogs, lgs):
            span = DIL_N * d

            def block(t, carry, d=d, b_ref=b_ref, og=og, lg=lg, span=span):
                r, b = t % d, t // d
                q0 = b * span + r
                q = qf[_rows(q0, DIL_N, d), :].astype(BF)
                k2 = kf[_rows(C + q0 - span, 2 * DIL_N, d), :].astype(BF)
                v2 = vf[_rows(C + q0 - span, 2 * DIL_N, d), :].astype(BF)
                bias = jnp.where((c == 0) & (b == 0), b_ref[1], b_ref[0])
                s = _dot_nt(q, k2) * scale + bias
                m = jnp.max(s, axis=1, keepdims=True)
                p = jnp.exp(s - m)
                l = jnp.sum(p, axis=1, keepdims=True)
                og[_rows(q0, DIL_N, d), :] = _dot(p.astype(BF), v2) / l
                lg[_rows(q0, DIL_N, d), :] = jnp.broadcast_to(m + jnp.log(l), (DIL_N, LANES))
                return carry

            lax.fori_loop(0, C // DIL_N, block, 0, unroll=DIL_UNROLL)
        ls = [lg[...] for lg in lgs]
        m = functools.reduce(jnp.maximum, ls)
        es = [jnp.exp(l - m) for l in ls]
        z = functools.reduce(jnp.add, es)
        mix = functools.reduce(jnp.add, [e * og[...] for e, og in zip(es, ogs)])
        o_ref[...] = (mix / z).astype(BF)
        L_ref[...] = m + jnp.log(z)

    cur = lambda off: pl.BlockSpec((C, HEAD), lambda h, c: (c, off * H + h))
    prv = lambda off: pl.BlockSpec((C, HEAD), lambda h, c: (jnp.maximum(c - 1, 0), off * H + h))
    tab = pl.BlockSpec((None, 2, DIL_N, 2 * DIL_N), lambda h, c: (h, 0, 0, 0))
    out = pl.BlockSpec((C, HEAD), lambda h, c: (c, h))
    return pl.pallas_call(
        body, name="dil_attn_fwd", grid=(H, S // C),
        in_specs=[cur(0), cur(1), prv(1), cur(2), prv(2)] + [tab] * nbr,
        out_specs=[out, out],
        out_shape=[jax.ShapeDtypeStruct((S, H * HEAD), BF), jax.ShapeDtypeStruct((S, H * HEAD), F32)],
        scratch_shapes=[pltpu.VMEM((C, HEAD), F32), pltpu.VMEM((2 * C, HEAD), F32), pltpu.VMEM((2 * C, HEAD), F32)]
                       + [pltpu.VMEM((C, HEAD), F32)] * (2 * nbr),
        compiler_params=_params(("parallel", "arbitrary")),
    )(qkv, qkv, qkv, qkv, qkv, *biases)


def dil_attn_bwd(qkv, do, L, delta, biases, H):
    S = qkv.shape[0]
    C = DIL_CHUNK
    assert S % C == 0
    nc = S // C
    scale = HEAD ** -0.5
    nbr = len(BRANCH_DILATIONS)

    def body(*refs):
        (qc_ref, qn_ref, kc_ref, kp_ref, vc_ref, vp_ref, doc_ref, don_ref, Lc_ref, Ln_ref, dlc_ref, dln_ref) = refs[:12]
        b_refs = refs[12:12 + nbr]
        (dq_ref, dk_ref, dv_ref, qf, dof, Lf, dlf, kf, vf, dq_acc, dk_acc, dv_acc,
         dq_g, dka_g, dva_g, dkb_g, dvb_g) = refs[12 + nbr:]
        c = pl.program_id(1)
        for buf, a_ref, b_ref in ((qf, qc_ref, qn_ref), (dof, doc_ref, don_ref), (Lf, Lc_ref, Ln_ref),
                                  (dlf, dlc_ref, dln_ref), (kf, kp_ref, kc_ref), (vf, vp_ref, vc_ref)):
            buf[0:C, :] = a_ref[...].astype(F32)
            buf[C:2 * C, :] = b_ref[...].astype(F32)
        for g_idx, (d, b_ref) in enumerate(zip(BRANCH_DILATIONS, b_refs)):
            span = DIL_N * d
            dkb_g[C:2 * C, :] = jnp.zeros((C, HEAD), F32)
            dvb_g[C:2 * C, :] = jnp.zeros((C, HEAD), F32)

            def block(t, carry, d=d, b_ref=b_ref, span=span):
                r, b = t % d, t // d
                q0 = b * span + r
                qrows = _rows(q0, DIL_N, d)
                krows = _rows(C + q0 - span, 2 * DIL_N, d)
                q, dov = qf[qrows, :].astype(BF), dof[qrows, :].astype(BF)
                k2, v2 = kf[krows, :].astype(BF), vf[krows, :].astype(BF)
                Lq, dl = Lf[qrows, :], dlf[qrows, :]
                bias = jnp.where((c == 0) & (b == 0), b_ref[1], b_ref[0])
                s = _dot_nt(q, k2) * scale + bias
                p = jnp.exp(s - jnp.concatenate([Lq, Lq], axis=1))
                ds = p * (_dot_nt(dov, v2) - jnp.concatenate([dl, dl], axis=1))
                dsb, pb = ds.astype(BF), p.astype(BF)
                dq_g[qrows, :] = _dot(dsb, k2) * scale
                dk2 = _dot_tn(dsb, q) * scale
                dv2 = _dot_tn(pb, dov)
                dka_g[qrows, :] = dk2[DIL_N:]
                dva_g[qrows, :] = dv2[DIL_N:]
                prows = _rows(C + q0 - span, DIL_N, d)
                dkb_g[prows, :] = dk2[:DIL_N]
                dvb_g[prows, :] = dv2[:DIL_N]
                return carry

            lax.fori_loop(0, C // DIL_N, block, 0, unroll=DIL_UNROLL)

            @pl.when(c < nc - 1)
            def _(d=d, b_ref=b_ref, span=span):
                def nxt(r, carry):
                    qrows = _rows(C + r, DIL_N, d)
                    krows = _rows(2 * C - span + r, DIL_N, d)
                    q, dov = qf[qrows, :].astype(BF), dof[qrows, :].astype(BF)
                    k1, v1 = kf[krows, :].astype(BF), vf[krows, :].astype(BF)
                    s = _dot_nt(q, k1) * scale + b_ref[0][:, :DIL_N]
                    p = jnp.exp(s - Lf[qrows, :])
                    ds = p * (_dot_nt(dov, v1) - dlf[qrows, :])
                    dkb_g[krows, :] = _dot_tn(ds.astype(BF), q) * scale
                    dvb_g[krows, :] = _dot_tn(p.astype(BF), dov)
                    return carry

                lax.fori_loop(0, d, nxt, 0, unroll=min(d, DIL_UNROLL))

            if g_idx == 0:
                dq_acc[...] = dq_g[...]
                dk_acc[...] = dka_g[...] + dkb_g[C:2 * C, :]
                dv_acc[...] = dva_g[...] + dvb_g[C:2 * C, :]
            else:
                dq_acc[...] += dq_g[...]
                dk_acc[...] += dka_g[...] + dkb_g[C:2 * C, :]
                dv_acc[...] += dva_g[...] + dvb_g[C:2 * C, :]

        dq_ref[...] = dq_acc[...].astype(BF)
        dk_ref[...] = dk_acc[...].astype(BF)
        dv_ref[...] = dv_acc[...].astype(BF)

    cur3 = lambda off: pl.BlockSpec((C, HEAD), lambda h, c: (c, off * H + h))
    prv3 = lambda off: pl.BlockSpec((C, HEAD), lambda h, c: (jnp.maximum(c - 1, 0), off * H + h))
    nxt3 = lambda off: pl.BlockSpec((C, HEAD), lambda h, c: (jnp.minimum(c + 1, nc - 1), off * H + h))
    cur1 = pl.BlockSpec((C, HEAD), lambda h, c: (c, h))
    nxt1 = pl.BlockSpec((C, HEAD), lambda h, c: (jnp.minimum(c + 1, nc - 1), h))
    tab = pl.BlockSpec((None, 2, DIL_N, 2 * DIL_N), lambda h, c: (h, 0, 0, 0))
    return pl.pallas_call(
        body, name="dil_attn_bwd", grid=(H, nc),
        in_specs=[cur3(0), nxt3(0), cur3(1), prv3(1), cur3(2), prv3(2), cur1, nxt1, cur1, nxt1, cur1, nxt1] + [tab] * nbr,
        out_specs=[cur1] * 3,
        out_shape=[jax.ShapeDtypeStruct((S, H * HEAD), BF)] * 3,
        scratch_shapes=[pltpu.VMEM((2 * C, HEAD), F32)] * 6 + [pltpu.VMEM((C, HEAD), F32)] * 6
                       + [pltpu.VMEM((2 * C, HEAD), F32)] * 2,
        compiler_params=_params(("parallel", "arbitrary")),
    )(qkv, qkv, qkv, qkv, qkv, qkv, do, do, L, L, delta, delta, *biases)


def _causal_mask(s, qi, ki, tq, tk, row0=0):
    qpos = qi * tq + row0 + lax.broadcasted_iota(jnp.int32, s.shape, 0)
    kpos = ki * tk + lax.broadcasted_iota(jnp.int32, s.shape, 1)
    return jnp.where(kpos <= qpos, s, NEG)


def mla_fwd(qn, qr, kv, kr, H, *, tq=1024, tk=1024, sub=512):
    S = qn.shape[0]
    tq, tk = _tile(S, tq), _tile(S, tk)
    sub = _tile(tq, sub)
    nk = S // tk
    scale = MLA_SCALE
    c2 = scale * LOG2E

    def body(qn_ref, qr_ref, kn_ref, kr_ref, v_ref, o_ref, L_ref, m_s, l_s, acc):
        qi, ki = pl.program_id(1), pl.program_id(2)

        @pl.when(ki == 0)
        def _():
            m_s[...] = jnp.full_like(m_s, NEG)
            l_s[...] = jnp.zeros_like(l_s)
            acc[...] = jnp.zeros_like(acc)

        def step(masked):
            k = jnp.concatenate([kn_ref[...], kr_ref[...]], axis=1)
            v = v_ref[...]
            for r in range(tq // sub):
                rows = slice(r * sub, (r + 1) * sub)
                q = jnp.concatenate([qn_ref[rows, :], qr_ref[rows, :]], axis=1)
                s = _dot_nt(q, k)
                if masked:
                    s = _causal_mask(s, qi, ki, tq, tk, r * sub)
                m_prev = m_s[rows, :]
                m_new = jnp.maximum(m_prev, jnp.max(s, axis=1, keepdims=True))
                alpha = jnp.exp2((m_prev - m_new) * c2)
                p = jnp.exp2((s - jnp.tile(m_new, (1, tk // LANES))) * c2)
                l_s[rows, :] = alpha * l_s[rows, :] + jnp.sum(p, axis=1, keepdims=True)
                acc[rows, :] = alpha * acc[rows, :] + _dot(p.astype(BF), v)
                m_s[rows, :] = m_new

        full = ki * tk + tk - 1 <= qi * tq
        live = ki * tk <= qi * tq + tq - 1
        pl.when(full)(lambda: step(False))
        pl.when(live & jnp.logical_not(full))(lambda: step(True))

        @pl.when(ki == nk - 1)
        def _():
            o_ref[...] = (acc[...] / l_s[...]).astype(BF)
            L_ref[...] = m_s[...] * scale + jnp.log(l_s[...])

    kcl = lambda qi, ki: jnp.minimum(ki, (qi * tq + tq - 1) // tk)
    qs = pl.BlockSpec((tq, HEAD), lambda h, qi, ki: (qi, h))
    return pl.pallas_call(
        body, name="mla_fwd", grid=(H, S // tq, nk),
        in_specs=[qs, qs, pl.BlockSpec((tk, HEAD), lambda h, qi, ki: (kcl(qi, ki), h)),
                  pl.BlockSpec((tk, LANES), lambda h, qi, ki: (kcl(qi, ki), 0)),
                  pl.BlockSpec((tk, HEAD), lambda h, qi, ki: (kcl(qi, ki), H + h))],
        out_specs=[qs, qs],
        out_shape=[jax.ShapeDtypeStruct((S, H * HEAD), BF), jax.ShapeDtypeStruct((S, H * HEAD), F32)],
        scratch_shapes=[pltpu.VMEM((tq, LANES), F32), pltpu.VMEM((tq, LANES), F32), pltpu.VMEM((tq, HEAD), F32)],
        compiler_params=_params(("parallel", "parallel", "arbitrary")),
    )(qn, qr, kv, kr, kv)


def _mla_probs(s, L_rows, c2, width):
    return jnp.exp2(s * c2 - jnp.tile(L_rows * LOG2E, (1, width // LANES)))


def mla_bwd(qn, qr, kv, kr, do, L, delta, rope, H, prev=None, *, tq=1024, tk=1024, sub=512):
    S = qn.shape[0]
    tq, tk = _tile(S, tq), _tile(S, tk)
    sub = _tile(tq, sub)
    nq, nk = S // tq, S // tk
    scale = MLA_SCALE
    c2 = scale * LOG2E
    has_prev = prev is not None

    def body(*refs):
        (qn_ref, qr_ref, kn_ref, kr_ref, v_ref, do_ref, L_ref, dl_ref, cq_ref, sq_ref, ck_ref, sk_ref) = refs[:12]
        refs = refs[12:]
        if has_prev:
            pk_ref, pv_ref, pr_ref = refs[:3]
            refs = refs[3:]
        dqn_ref, dqr_ref, dk_ref, dv_ref, dr_ref, dq_full, dk_acc, dv_acc = refs
        h, ki, qi = pl.program_id(0), pl.program_id(1), pl.program_id(2)
        qrows = pl.ds(pl.multiple_of(qi * tq, tq), tq)
        krows = pl.ds(pl.multiple_of(ki * tk, tk), tk)

        @pl.when(qi == 0)
        def _():
            dk_acc[...] = jnp.zeros_like(dk_acc)
            dv_acc[...] = jnp.zeros_like(dv_acc)

        @pl.when((qi == 0) & (h == 0))
        def _():
            dr_ref[krows, :] = jnp.zeros((tk, LANES), F32)

        @pl.when(ki == 0)
        def _():
            dq_full[qrows, :] = jnp.zeros((tq, 2 * HEAD), F32)

        def step(masked):
            k = jnp.concatenate([kn_ref[...], kr_ref[...]], axis=1)
            v = v_ref[...]
            for r in range(tq // sub):
                rows = slice(r * sub, (r + 1) * sub)
                q = jnp.concatenate([qn_ref[rows, :], qr_ref[rows, :]], axis=1)
                s = _dot_nt(q, k)
                if masked:
                    s = _causal_mask(s, qi, ki, tq, tk, r * sub)
                p = _mla_probs(s, L_ref[rows, :], c2, tk)
                dov = do_ref[rows, :]
                dv_acc[...] += _dot_tn(p.astype(BF), dov)
                dp = _dot_nt(dov, v)
                ds = (p * (dp - jnp.tile(dl_ref[rows, :], (1, tk // LANES)))).astype(BF)
                dk_acc[...] += _dot_tn(ds, q)
                dq_full[pl.ds(pl.multiple_of(qi * tq + r * sub, sub), sub), :] += _dot(ds, k)

        full = ki * tk + tk - 1 <= qi * tq
        live = ki * tk <= qi * tq + tq - 1
        pl.when(full)(lambda: step(False))
        pl.when(live & jnp.logical_not(full))(lambda: step(True))

        @pl.when(ki == (qi * tq + tq - 1) // tk)
        def _():
            dq = dq_full[qrows, :] * scale
            dqn_ref[qrows, :] = dq[:, :HEAD].astype(BF)
            dqr_ref[qrows, :] = _rope_bwd(dq[:, HEAD:], cq_ref[...], sq_ref[...]).astype(BF)

        @pl.when(qi == nq - 1)
        def _():
            dk = dk_acc[...] * scale
            dkn, dv = dk[:, :HEAD], dv_acc[...]
            if has_prev:
                dkn, dv = dkn + pk_ref[...], dv + pv_ref[...]
            dk_ref[...] = dkn
            dv_ref[...] = dv
            dr_ref[krows, :] += dk[:, HEAD:]

        @pl.when((qi == nq - 1) & (h == H - 1))
        def _():
            dr = _rope_bwd(dr_ref[krows, :], ck_ref[...], sk_ref[...])
            dr_ref[krows, :] = dr + pr_ref[...] if has_prev else dr

    qcl = lambda ki, qi: jnp.maximum(qi, (ki * tk) // tq)
    qs = pl.BlockSpec((tq, HEAD), lambda h, ki, qi: (qcl(ki, qi), h))
    qt = pl.BlockSpec((tq, LANES), lambda h, ki, qi: (qcl(ki, qi), 0))
    kn = pl.BlockSpec((tk, HEAD), lambda h, ki, qi: (ki, h))
    vs = pl.BlockSpec((tk, HEAD), lambda h, ki, qi: (ki, H + h))
    k1 = pl.BlockSpec((tk, LANES), lambda h, ki, qi: (ki, 0))
    head = pl.BlockSpec((S, HEAD), lambda h, ki, qi: (0, h))
    in_specs = [qs, qs, kn, k1, vs, qs, qs, qs, qt, qt, k1, k1]
    args = [qn, qr, kv, kr, kv, do, L, delta, *rope, *rope]
    if has_prev:
        in_specs += [kn, vs, k1]
        args += [prev[0], prev[0], prev[1]]
    return pl.pallas_call(
        body, name="mla_bwd", grid=(H, nk, nq), in_specs=in_specs,
        out_specs=[head, head, kn, kn, pl.BlockSpec((S, LANES), lambda h, ki, qi: (0, 0))],
        out_shape=[jax.ShapeDtypeStruct((S, H * HEAD), BF), jax.ShapeDtypeStruct((S, H * HEAD), BF),
                   jax.ShapeDtypeStruct((S, H * HEAD), F32), jax.ShapeDtypeStruct((S, H * HEAD), F32),
                   jax.ShapeDtypeStruct((S, LANES), F32)],
        scratch_shapes=[pltpu.VMEM((S, 2 * HEAD), F32), pltpu.VMEM((tk, 2 * HEAD), F32), pltpu.VMEM((tk, HEAD), F32)],
        compiler_params=_params(("arbitrary", "arbitrary", "arbitrary")),
    )(*args)


def adamw(recvs, w, m, v, *, name="adamw"):
    R, C = w.shape
    n = len(recvs)
    R1 = R // n
    assert all(r.shape == (N_DEV, R1, C) for r in recvs)
    tr = 16
    while tr * 2 * C * 44 <= 6 * 1024 * 1024 and R1 % (tr * 2) == 0:
        tr *= 2
    tr = min(tr, R1)
    assert R1 % tr == 0
    nb = R1 // tr

    def body(*refs):
        r_refs = refs[:n]
        w_ref, m_ref, v_ref, g_ref, d_ref, mo_ref, vo_ref = refs[n:]
        for slab in range(n):
            @pl.when(pl.program_id(0) == slab)
            def _(r_ref=r_refs[slab]):
                g = r_ref[0].astype(F32)
                for s in range(1, N_DEV):
                    g = g + r_ref[s].astype(F32)
                m2 = ADAM_B1 * m_ref[...] + (1.0 - ADAM_B1) * g
                v2 = ADAM_B2 * v_ref[...] + (1.0 - ADAM_B2) * (g * g)
                m_hat = m2 / (1.0 - ADAM_B1 ** ADAM_STEP)
                v_hat = v2 / (1.0 - ADAM_B2 ** ADAM_STEP)
                g_ref[...] = g
                d_ref[...] = -ADAM_LR * (m_hat / (jnp.sqrt(v_hat) + ADAM_EPS) + ADAM_WD * w_ref[...])
                mo_ref[...] = m2
                vo_ref[...] = v2

    def recv_spec(slab):
        return pl.BlockSpec((N_DEV, tr, C), lambda l, i: (0, jnp.where(l == slab, i, jnp.where(l < slab, 0, nb - 1)), 0))

    row = pl.BlockSpec((tr, C), lambda l, i: (l * nb + i, 0))
    return pl.pallas_call(
        body, name=name, grid=(n, nb),
        in_specs=[recv_spec(slab) for slab in range(n)] + [row, row, row],
        out_specs=[row] * 4, out_shape=[jax.ShapeDtypeStruct((R, C), F32)] * 4,
        compiler_params=_params(("arbitrary", "arbitrary")),
    )(*recvs, w, m, v)


def _pack(ts, width, dtype, row_mult=16):
    flat = jnp.concatenate([t.astype(dtype).reshape(-1) for t in ts])
    n = flat.shape[0]
    rows = -(-n // (width * row_mult)) * row_mult
    return jnp.pad(flat, (0, rows * width - n)).reshape(rows, width)


def _unpack(buf, shapes):
    lead = buf.shape[:-2]
    flat = buf.reshape(lead + (-1,))
    out, off = [], 0
    for s in shapes:
        n = int(np.prod(s))
        out.append(flat[..., off:off + n].reshape(lead + tuple(s)))
        off += n
    return out


def _rope_pad(r):
    z = jnp.zeros(r.shape[:-1] + (ROPE_HALF,), r.dtype)
    return jnp.concatenate([r[..., :ROPE_HALF], z, r[..., ROPE_HALF:], z], axis=-1)


def _rope_unpad(r):
    return jnp.concatenate([r[..., :ROPE_HALF], r[..., 2 * ROPE_HALF:3 * ROPE_HALF]], axis=-1)


def _step(P):
    x0 = P['x'][0]
    target = P['loss_target'][0]
    S, D = x0.shape
    NL = P['ffn_norm1'].shape[0]
    NA = P['a_wqkv'].shape[0]
    Fs = P['ffn1_wg'].shape[2]
    H = D // HEAD
    KV = P['b_wdkv'].shape[1]
    QL = P['b_wdq'].shape[2]
    HW = H * HEAD

    ffn_seq = [(l, f) for l in range(NL) for f in (0, 1)]
    ffn_loc = {(l, f): [P[n][l].astype(BF) for n in (f'ffn{f + 1}_wg', f'ffn{f + 1}_wu', f'ffn{f + 1}_wd')]
               for l, f in ffn_seq}
    assert 1 <= NA < NL
    groups = {('A', l): [('a_wqkv', l), ('a_wo', l)] for l in range(NA)}
    groups['B'] = [(n, None) for n in MIX_W if n.startswith('b_')]
    pick = lambda pre, n, l: P[pre + n] if l is None else P[pre + n][l]
    pack_group = lambda gk, pre, dtype: _pack([pick(pre, n, l) for n, l in groups[gk]], D, dtype)
    group_shapes = {gk: [pick('', n, l).shape for n, l in members] for gk, members in groups.items()}
    group_loc = {gk: pack_group(gk, '', BF) for gk in groups}
    gather_at = {(0, 0): [('A', 0)]}
    for l in range(1, NA):
        gather_at.setdefault((l - 1, 1), []).append(('A', l))
    gather_at.setdefault((NA - 1, 0), []).append('B')
    send_at = {(l, 0): [('A', l)] for l in range(NA)}
    send_at.setdefault((NA - 1, 1), []).append('B')
    gathered = {}
    ffn_w = {ffn_seq[0]: all_gather(ffn_loc[ffn_seq[0]])}

    def ffn_forward(x, gain, key):
        nxt = ffn_seq.index(key) + 1
        gks = gather_at.get(key, [])
        carry = (ffn_loc[ffn_seq[nxt]] if nxt < len(ffn_seq) else []) + [group_loc[gk] for gk in gks]
        xo, h, G, U, *got = ffn_fwd(x, gain, *ffn_w[key], carry=carry)
        if nxt < len(ffn_seq):
            ffn_w[ffn_seq[nxt]] = got[:3]
        for gk, arr in zip(gks, got[len(got) - len(gks):]):
            gathered[gk] = {n: t for (n, _), t in zip(groups[gk], _unpack(arr, group_shapes[gk]))}
        return xo, h, G, U

    cols = lambda t: jnp.moveaxis(t, 0, -2).reshape(t.shape[1:-1] + (N_DEV * t.shape[-1],))
    rows = lambda t, lead: jnp.moveaxis(t, 0, lead).reshape(t.shape[1:1 + lead] + (N_DEV * t.shape[1 + lead],) + t.shape[2 + lead:])
    a_full = {}

    def a_weights(l):
        if l not in a_full:
            g = gathered[('A', l)]
            a_full[l] = (cols(g['a_wqkv']), rows(g['a_wo'], 0))
        return a_full[l]

    inv = 1.0 / (ROPE_THETA ** (jnp.arange(0, 2 * ROPE_HALF, 2, dtype=F32) / (2 * ROPE_HALF)))
    ang = jnp.arange(S, dtype=F32)[:, None] * inv[None, :]
    z = jnp.zeros((S, ROPE_HALF), F32)
    rope = (jnp.concatenate([jnp.cos(ang), z, jnp.cos(ang), z], axis=1),
            jnp.concatenate([-jnp.sin(ang), z, jnp.sin(ang), z], axis=1))
    biases = [dilated_bias(H, d) for d in BRANCH_DILATIONS]

    saved = []
    kvs = None
    x = x0
    for l in range(NL):
        st = {'x_in': x}
        if l == NA:
            g = gathered['B']
            b_wo = rows(g['b_wo'], 1)
            wdkv = rows(g['b_wdkv'], 0)
            wkr = _rope_pad(rows(g['b_wkr'], 0))
            wkv = jnp.concatenate([rows(g['b_wuk'], 0).reshape(KV, HW), rows(g['b_wuv'], 0).reshape(KV, HW)], axis=1)
            wdq = rows(g['b_wdq'], 1)
            wuq = rows(g['b_wuq'], 1)
            wuq_n = wuq[..., :HEAD].reshape(-1, QL, HW)
            wuq_r = _rope_pad(wuq[..., HEAD:]).reshape(-1, QL, HW)
            ckv_pre, hkv = norm_mm(x, P['kv_norm'], wdkv, out_dtype=F32, name="kv_down")
            kr = norm_mm(x, P['kv_norm'], wkr, out_dtype=BF, tn=LANES, rope=rope, write_h=False, name="kv_rope")
            kvm, ckv = norm_mm(ckv_pre, P['b_ckv_norm'], wkv, out_dtype=BF, name="kv_up")
            kvs = dict(x=x, ckv_pre=ckv_pre, hkv=hkv, kr=kr, kv=kvm, ckv=ckv)
        xa, h1, G1, U1 = ffn_forward(x, P['ffn_norm1'][l], (l, 0))
        st.update(h1=h1, G1=G1, U1=U1, xa=xa)
        if l < NA:
            w_qkv, w_o = a_weights(l)
            qkv, hm = norm_mm(xa, P['mix_norm'][l], w_qkv, out_dtype=BF, tn=3 * HW // 4, name="a_qkv")
            o, Lj = dil_attn_fwd(qkv, biases, H)
            xb = mm(o, w_o, add=xa, name="mix_out")
            st.update(qkv=qkv, hm=hm, o=o, L=Lj)
        else:
            jb = l - NA
            cq_pre, hm = norm_mm(xa, P['mix_norm'][l], wdq[jb], out_dtype=F32, name="q_down")
            qn, cq = norm_mm(cq_pre, P['b_cq_norm'][jb], wuq_n[jb], out_dtype=BF, name="q_up")
            qr = norm_mm(cq_pre, P['b_cq_norm'][jb], wuq_r[jb], out_dtype=BF, tn=LANES, rope=rope, write_h=False,
                         name="q_rope")
            o, Lj = mla_fwd(qn, qr, kvs['kv'], kvs['kr'], H)
            xb = mm(o, b_wo[jb], add=xa, name="mix_out")
            st.update(cq_pre=cq_pre, hm=hm, qn=qn, qr=qr, cq=cq, o=o, L=Lj)
        x, h2, G2, U2 = ffn_forward(xb, P['ffn_norm2'][l], (l, 1))
        st.update(xb=xb, h2=h2, G2=G2, U2=U2)
        saved.append(st)

    loss_part, dx, dg_final = loss_head(x, P['final_norm'], target)

    gw = {}
    gv = {'final_norm': dg_final}
    dkv_acc = None
    for n in ('ffn_norm1', 'mix_norm', 'ffn_norm2'):
        gv[n] = [None] * NL
    gv['b_cq_norm'] = [None] * (NL - NA)
    ffn_r = {}
    pending = []
    a_g = {'a_wqkv': [None] * NA, 'a_wo': [None] * NA}
    b_g = {n: [None] * (NL - NA) for n in ('b_wdq', 'b_wuq', 'b_wo')}

    split_cols = lambda t: jnp.moveaxis(t.reshape(t.shape[:-1] + (N_DEV, t.shape[-1] // N_DEV)), -2, 0)
    split_rows = lambda t, lead: jnp.moveaxis(
        t.reshape(t.shape[:lead] + (N_DEV, t.shape[lead] // N_DEV) + t.shape[lead + 1:]), lead, 0)
    group_r = {}

    def pack_group_grads(gk):
        if gk == 'B':
            for n in b_g:
                gw[n] = jnp.stack(b_g[n])
            lead = {'b_wdq': 1, 'b_wuq': 1, 'b_wo': 1}
            shards = [split_rows(gw[n], lead.get(n, 0)) for n, _ in groups[gk]]
        else:
            l = gk[1]
            shards = [split_cols(a_g['a_wqkv'][l]), split_rows(a_g['a_wo'][l], 0)]
        flat = jnp.concatenate([t.astype(BF).reshape(N_DEV, -1) for t in shards], axis=1)
        rows = group_loc[gk].shape[0]
        return jnp.pad(flat, ((0, 0), (0, rows * D - flat.shape[1]))).reshape(N_DEV, rows, D)

    def ffn_backward(dy, x_in, gain, key, h, G, U):
        sent, carry = pending.pop() if pending else (None, [])
        dxi, dout, dG, dU, dgain, got = ffn_bwd_x(dy, x_in, gain, G, U, *ffn_w[key], carry=carry)
        if carry:
            ffn_r[sent] = got
        gks = send_at.get(key, [])
        dws, got_w = ffn_bwd_w(h, dout, G, U, dG, dU, carry=[pack_group_grads(gk) for gk in gks])
        group_r.update(zip(gks, got_w))
        pending.append((key, dws))
        return dxi, dgain

    for l in reversed(range(NL)):
        st = saved[l]
        dxb, gv['ffn_norm2'][l] = ffn_backward(dx, st['xb'], P['ffn_norm2'][l], (l, 1), st['h2'], st['G2'], st['U2'])
        xa = st['xa']
        if l < NA:
            w_qkv, w_o = a_weights(l)
            do = mm(dxb, w_o, tb=True, out_dtype=BF, name="mix_out_dx")
            a_g['a_wo'][l] = mm(st['o'], dxb, ta=True, out_dtype=BF, tm=1024, tn=1024, name="mix_out_dw")
            delta = head_delta(do, st['o'])
            dqkv = jnp.concatenate(dil_attn_bwd(st['qkv'], do, st['L'], delta, biases, H), axis=1)
            dh = mm(dqkv, w_qkv, tb=True, name="a_qkv_dx")
            a_g['a_wqkv'][l] = mm(st['hm'], dqkv, ta=True, out_dtype=BF, tm=1024, tn=1024, name="a_qkv_dw")
            dxa, gv['mix_norm'][l] = norm_bwd(xa, P['mix_norm'][l], dh, dxb, name="mix_norm_bwd")
        else:
            jb = l - NA
            do = mm(dxb, b_wo[jb], tb=True, out_dtype=BF, name="mix_out_dx")
            b_g['b_wo'][jb] = mm(st['o'], dxb, ta=True, out_dtype=BF, tm=1024, tn=1024, name="mix_out_dw")
            delta = head_delta(do, st['o'])
            dqn, dqr, dkn, dv, dr = mla_bwd(st['qn'], st['qr'], kvs['kv'], kvs['kr'], do, st['L'], delta, rope, H,
                                            prev=dkv_acc)
            dkv_acc = (jnp.concatenate([dkn, dv], axis=1), dr)
            dcq = mm(dqn, wuq_n[jb], tb=True, name="q_up_dx")
            dcq = mm(dqr, wuq_r[jb], tb=True, add=dcq, name="q_up_dx_add")
            dwn = mm(st['cq'], dqn, ta=True, out_dtype=F32, name="q_up_dw")
            dwr = mm(st['cq'], dqr, ta=True, out_dtype=F32, name="q_up_dw")
            b_g['b_wuq'][jb] = jnp.concatenate(
                [dwn.reshape(QL, H, HEAD), _rope_unpad(dwr.reshape(QL, H, HEAD))], axis=-1)
            dcq_pre, gv['b_cq_norm'][jb] = norm_bwd(st['cq_pre'], P['b_cq_norm'][jb], dcq, name="cq_norm_bwd")
            dh = mm(dcq_pre, wdq[jb], tb=True, name="q_down_dx")
            b_g['b_wdq'][jb] = mm(st['hm'], dcq_pre, ta=True, out_dtype=F32, tm=1024, name="q_down_dw")
            dxa, gv['mix_norm'][l] = norm_bwd(xa, P['mix_norm'][l], dh, dxb, name="mix_norm_bwd")
        dx, gv['ffn_norm1'][l] = ffn_backward(dxa, st['x_in'], P['ffn_norm1'][l], (l, 0), st['h1'], st['G1'], st['U1'])
        if l == NA:
            dkvm, dr = dkv_acc
            dckv = mm(dkvm, wkv, tb=True, name="kv_up_dx")
            dwkv = mm(kvs['ckv'], dkvm, ta=True, out_dtype=F32, name="kv_up_dw")
            gw['b_wuk'] = dwkv[:, :HW].reshape(KV, H, HEAD)
            gw['b_wuv'] = dwkv[:, HW:].reshape(KV, H, HEAD)
            dckv_pre, gv['b_ckv_norm'] = norm_bwd(kvs['ckv_pre'], P['b_ckv_norm'], dckv, name="ckv_norm_bwd")
            dh = mm(dckv_pre, wdkv, tb=True, name="kv_down_dx")
            dh = mm(dr, wkr, tb=True, add=dh, name="kv_rope_dx")
            gw['b_wdkv'] = mm(kvs['hkv'], dckv_pre, ta=True, out_dtype=F32, tm=1024, name="kv_down_dw")
            gw['b_wkr'] = _rope_unpad(mm(kvs['hkv'], dr, ta=True, out_dtype=F32, tm=1024, name="kv_rope_dw"))
            dx, gv['kv_norm'] = norm_bwd(kvs['x'], P['kv_norm'], dh, dx, name="kv_norm_bwd")

    for n in ('ffn_norm1', 'mix_norm', 'ffn_norm2', 'b_cq_norm'):
        gv[n] = jnp.stack(gv[n])

    vec_parts = [gv[n] for n in VEC_W] + [jnp.full((LANES,), loss_part, F32)]
    gvec = _pack(vec_parts, LANES, F32, row_mult=8)
    last_key, last = pending.pop()
    *got, rv = exchange(last + [gvec], [False] * len(last) + [True])
    ffn_r[last_key] = got

    res = {}
    for f in (0, 1):
        for which, kind in enumerate(('wg', 'wu', 'wd')):
            n = f'ffn{f + 1}_{kind}'
            recvs = [ffn_r[(l, f)][which] for l in range(NL)]
            flat = lambda t: t.reshape((NL * t.shape[1], t.shape[2]))
            out = adamw(recvs, flat(P[n]), flat(P['m_' + n]), flat(P['v_' + n]),
                        name="adamw_row" if kind == 'wd' else "adamw_col")
            res[n] = [t.reshape(P[n].shape) for t in out]
    per_layer = {n: [None] * NA for n in ('a_wqkv', 'a_wo')}
    for gk, members in groups.items():
        out = adamw([group_r[gk]], *[pack_group(gk, pre, F32) for pre in ('', 'm_', 'v_')], name="adamw_mix")
        for (n, l), parts in zip(members, zip(*[_unpack(t, group_shapes[gk]) for t in out])):
            if l is None:
                res[n] = list(parts)
            else:
                per_layer[n][l] = parts
    for n, layers in per_layer.items():
        res[n] = [jnp.stack(field) for field in zip(*layers)]
    vec_shapes = [P[n].shape for n in VEC_W] + [(LANES,)]
    ones = jnp.ones((LANES,), F32)
    pv = lambda pre: _pack([P[pre + n] for n in VEC_W] + [ones], LANES, F32, row_mult=8)
    out = adamw([rv], pv(''), pv('m_'), pv('v_'), name="adamw_vec")
    unp = [_unpack(t, vec_shapes) for t in out]
    for idx, n in enumerate(VEC_W):
        res[n] = [u[idx] for u in unp]
    loss = unp[0][-1][0]

    outs = [loss, dx[None]]
    for field in range(4):
        outs += [res[n][field] for n in W_NAMES]
    return tuple(outs)


def kernel(x, ffn_norm1, ffn1_wg, ffn1_wu, ffn1_wd, mix_norm, ffn_norm2, ffn2_wg, ffn2_wu, ffn2_wd, a_wqkv, a_wo, kv_norm, b_wdkv, b_ckv_norm, b_wkr, b_wuk, b_wuv, b_wdq, b_cq_norm, b_wuq, b_wo, final_norm, loss_target, m_ffn_norm1, m_ffn1_wg, m_ffn1_wu, m_ffn1_wd, m_mix_norm, m_ffn_norm2, m_ffn2_wg, m_ffn2_wu, m_ffn2_wd, m_a_wqkv, m_a_wo, m_kv_norm, m_b_wdkv, m_b_ckv_norm, m_b_wkr, m_b_wuk, m_b_wuv, m_b_wdq, m_b_cq_norm, m_b_wuq, m_b_wo, m_final_norm, v_ffn_norm1, v_ffn1_wg, v_ffn1_wu, v_ffn1_wd, v_mix_norm, v_ffn_norm2, v_ffn2_wg, v_ffn2_wu, v_ffn2_wd, v_a_wqkv, v_a_wo, v_kv_norm, v_b_wdkv, v_b_ckv_norm, v_b_wkr, v_b_wuk, v_b_wuv, v_b_wdq, v_b_cq_norm, v_b_wuq, v_b_wo, v_final_norm):
    return _step(dict(locals()))
```
